```python
import math
import jax
import jax.numpy as jnp
from jax import lax
import numpy as np

D_MODEL = 1024
BATCH = 32
SEQ = 256
DEPTH = 4
DEC_BATCH = 2
DEC_SEQ = 1024
PAST_LEN = 512

GRID_W = 64
N_MIXERS = 3
N_CONV_LAYERS = (DEPTH + 2) // 3
N_MLA_LAYERS = (DEPTH + 1) // 3
N_GQA_LAYERS = DEPTH // 3
N_DENSE_LAYERS = (DEPTH + 1) // 2
N_MOE_LAYERS = DEPTH // 2

CONV_WIDTH = 3

MLA_HEADS = 8
MLA_NOPE = 128
MLA_ROPE = 64
MLA_V = 128
MLA_Q_RANK = 384
MLA_KV_RANK = 256

GQA_HEADS = 8
GQA_KV_HEADS = 2
GQA_HEAD_DIM = 128

D_FF = 2816
N_EXPERTS = 8
TOP_K = 2
MOE_FF = 2048

ROPE_THETA = 10000.0
EPS = 1e-6
Q_BLOCK = 128

kernel_name = "hybrid_dit_conv_mla_gqa_moe_step"


def rms_norm(x, g):
    xf = x.astype(jnp.float32)
    y = xf * lax.rsqrt(jnp.mean(xf * xf, axis=-1, keepdims=True) + EPS)
    return (y * g.astype(jnp.float32)).astype(x.dtype)


def adaln_params(cond, w, b):
    m = jax.nn.silu(cond) @ w + b
    return jnp.split(m[:, None, :], 6, axis=-1)


def rope_1d(x, pos):
    half = x.shape[-1] // 2
    freqs = ROPE_THETA ** (-jnp.arange(half, dtype=jnp.float32) / half)
    ang = pos.astype(jnp.float32)[:, None] * freqs[None, :]
    cos = jnp.cos(ang)[None, :, None, :]
    sin = jnp.sin(ang)[None, :, None, :]
    xf = x.astype(jnp.float32)
    x1, x2 = xf[..., :half], xf[..., half:]
    return jnp.concatenate([x1 * cos - x2 * sin, x1 * sin + x2 * cos], axis=-1).astype(x.dtype)


def rope_2d(x):
    n_tokens = x.shape[1]
    rows = n_tokens // GRID_W
    row = jnp.repeat(jnp.arange(rows), GRID_W)
    col = jnp.tile(jnp.arange(GRID_W), rows)
    d = x.shape[-1] // 2
    return jnp.concatenate([rope_1d(x[..., :d], row), rope_1d(x[..., d:], col)], axis=-1)


def attention(q, k, v):
    B, Sq, H, dk = q.shape
    G = k.shape[2]
    R = H // G
    dv = v.shape[-1]
    scale = dk ** -0.5
    qb = q.reshape(B, Sq // Q_BLOCK, Q_BLOCK, G, R, dk).transpose(1, 0, 2, 3, 4, 5)

    def block(qi):
        s = jnp.einsum('bqgrd,bkgd->bgrqk', qi, k, preferred_element_type=jnp.float32) * scale
        p = jax.nn.softmax(s, axis=-1).astype(v.dtype)
        return jnp.einsum('bgrqk,bkgd->bqgrd', p, v)

    o = lax.map(block, qb)
    return o.transpose(1, 0, 2, 3, 4, 5).reshape(B, Sq, H, dv)


def merge_heads(o, w_o):
    return o.reshape(o.shape[0], o.shape[1], -1) @ w_o


def short_conv_mixer(h, w_in, conv_k, w_out):
    D = h.shape[-1]
    b_gate, c_gate, xin = jnp.split(h @ w_in, 3, axis=-1)
    u = c_gate * xin
    conv = lax.conv_general_dilated(
        u, conv_k[:, None, :].astype(u.dtype), window_strides=(1,),
        padding=((CONV_WIDTH // 2, CONV_WIDTH // 2),),
        dimension_numbers=('NWC', 'WIO', 'NWC'), feature_group_count=D)
    return (b_gate * conv) @ w_out


def mla_project(h, w_down, q_norm_g, kv_norm_g, w_uq):
    B, S, _ = h.shape
    down = h @ w_down
    cq = down[..., :MLA_Q_RANK]
    ckv = down[..., MLA_Q_RANK:MLA_Q_RANK + MLA_KV_RANK]
    kpe = down[..., MLA_Q_RANK + MLA_KV_RANK:]
    q = (rms_norm(cq, q_norm_g) @ w_uq).reshape(B, S, MLA_HEADS, MLA_NOPE + MLA_ROPE)
    return q, rms_norm(ckv, kv_norm_g), kpe


def mla_expand(ckv, kpe, w_ukv):
    B, S, _ = ckv.shape
    kv = (ckv @ w_ukv).reshape(B, S, MLA_HEADS, MLA_NOPE + MLA_V)
    k_pe = jnp.broadcast_to(kpe[:, :, None, :], (B, S, MLA_HEADS, MLA_ROPE))
    k = jnp.concatenate([kv[..., :MLA_NOPE], k_pe], axis=-1)
    return k, kv[..., MLA_NOPE:]


def gqa_project(h, w_qkv, q_g, k_g):
    B, S, _ = h.shape
    qkv = h @ w_qkv
    nq = GQA_HEADS * GQA_HEAD_DIM
    nk = GQA_KV_HEADS * GQA_HEAD_DIM
    q = qkv[..., :nq].reshape(B, S, GQA_HEADS, GQA_HEAD_DIM)
    k = qkv[..., nq:nq + nk].reshape(B, S, GQA_KV_HEADS, GQA_HEAD_DIM)
    v = qkv[..., nq + nk:].reshape(B, S, GQA_KV_HEADS, GQA_HEAD_DIM)
    return rms_norm(q, q_g), rms_norm(k, k_g), v


def swiglu(h, w_gate, w_up, w_down):
    return (jax.nn.silu(h @ w_gate) * (h @ w_up)) @ w_down


def moe_swiglu(h, w_router, w_gate, w_up, w_down):
    B, S, D = h.shape
    t = h.reshape(-1, D)
    logits = (t @ w_router).astype(jnp.float32)
    top_v, top_i = lax.top_k(logits, TOP_K)
    top_w = jax.nn.softmax(top_v, axis=-1)
    gates = jnp.sum(jax.nn.one_hot(top_i, N_EXPERTS, dtype=jnp.float32) * top_w[..., None], axis=1)
    g = jnp.einsum('nd,edf->nef', t, w_gate)
    u = jnp.einsum('nd,edf->nef', t, w_up)
    act = jax.nn.silu(g) * u * gates.astype(t.dtype)[..., None]
    return jnp.einsum('nef,efd->nd', act, w_down).reshape(B, S, D)


def setup_inputs(seed: int = 0) -> dict:
    key = jax.random.key(seed)
    keys = jax.random.split(key, 48)
    counter = [0]

    def nxt():
        k = keys[counter[0]]
        counter[0] += 1
        return k

    def nrm(shape, fan_in=None, scale=1.0):
        w = jax.random.normal(nxt(), shape, jnp.float32)
        if fan_in is not None:
            w = w * (fan_in ** -0.5)
        return w * scale

    def gain(shape):
        return 1.0 + 0.1 * jax.random.normal(nxt(), shape, jnp.float32)

    D = D_MODEL
    return {
        "x_prompt": nrm((BATCH, SEQ, D)),
        "x_sample": nrm((DEC_BATCH, DEC_SEQ, D)),
        "cache_mla_ckv": nrm((DEC_BATCH, N_MLA_LAYERS, PAST_LEN, MLA_KV_RANK)),
        "cache_mla_kpe": nrm((DEC_BATCH, N_MLA_LAYERS, PAST_LEN, MLA_ROPE)),
        "cache_gqa_k": nrm((DEC_BATCH, N_GQA_LAYERS, PAST_LEN, GQA_KV_HEADS, GQA_HEAD_DIM)),
        "cache_gqa_v": nrm((DEC_BATCH, N_GQA_LAYERS, PAST_LEN, GQA_KV_HEADS, GQA_HEAD_DIM)),
        "c": nrm((DEC_BATCH, D)),
        "c_ctx": nrm((D,)),
        "w_ada": nrm((DEPTH, D, 6 * D), D, 0.5),
        "b_ada": nrm((DEPTH, 6 * D), None, 0.02),
        "norm1_g": gain((DEPTH, D)),
        "norm2_g": gain((DEPTH, D)),
        "conv_w_in": nrm((N_CONV_LAYERS, D, 3 * D), D),
        "conv_k": nrm((N_CONV_LAYERS, CONV_WIDTH, D), CONV_WIDTH),
        "conv_w_out": nrm((N_CONV_LAYERS, D, D), D),
        "mla_w_down": nrm((N_MLA_LAYERS, D, MLA_Q_RANK + MLA_KV_RANK + MLA_ROPE), D),
        "mla_q_norm_g": gain((N_MLA_LAYERS, MLA_Q_RANK)),
        "mla_kv_norm_g": gain((N_MLA_LAYERS, MLA_KV_RANK)),
        "mla_w_uq": nrm((N_MLA_LAYERS, MLA_Q_RANK, MLA_HEADS * (MLA_NOPE + MLA_ROPE)), MLA_Q_RANK),
        "mla_w_ukv": nrm((N_MLA_LAYERS, MLA_KV_RANK, MLA_HEADS * (MLA_NOPE + MLA_V)), MLA_KV_RANK),
        "mla_w_o": nrm((N_MLA_LAYERS, MLA_HEADS * MLA_V, D), MLA_HEADS * MLA_V),
        "gqa_w_qkv": nrm((N_GQA_LAYERS, D, (GQA_HEADS + 2 * GQA_KV_HEADS) * GQA_HEAD_DIM), D),
        "gqa_q_norm_g": gain((N_GQA_LAYERS, GQA_HEAD_DIM)),
        "gqa_k_norm_g": gain((N_GQA_LAYERS, GQA_HEAD_DIM)),
        "gqa_w_o": nrm((N_GQA_LAYERS, GQA_HEADS * GQA_HEAD_DIM, D), GQA_HEADS * GQA_HEAD_DIM),
        "ffn_w_gate": nrm((N_DENSE_LAYERS, D, D_FF), D),
        "ffn_w_up": nrm((N_DENSE_LAYERS, D, D_FF), D),
        "ffn_w_down": nrm((N_DENSE_LAYERS, D_FF, D), D_FF),
        "moe_w_router": nrm((N_MOE_LAYERS, D, N_EXPERTS), D),
        "moe_w_gate": nrm((N_MOE_LAYERS, N_EXPERTS, D, MOE_FF), D),
        "moe_w_up": nrm((N_MOE_LAYERS, N_EXPERTS, D, MOE_FF), D),
        "moe_w_down": nrm((N_MOE_LAYERS, N_EXPERTS, MOE_FF, D), MOE_FF),
        "final_norm_g": gain((D,)),
    }


def reference(x_prompt, x_sample, cache_mla_ckv, cache_mla_kpe, cache_gqa_k, cache_gqa_v,
              c, c_ctx, w_ada, b_ada, norm1_g, norm2_g,
              conv_w_in, conv_k, conv_w_out,
              mla_w_down, mla_q_norm_g, mla_kv_norm_g, mla_w_uq, mla_w_ukv, mla_w_o,
              gqa_w_qkv, gqa_q_norm_g, gqa_k_norm_g, gqa_w_o,
              ffn_w_gate, ffn_w_up, ffn_w_down,
              moe_w_router, moe_w_gate, moe_w_up, moe_w_down,
              final_norm_g):
    xp = x_prompt
    xs = x_sample
    cond_ctx = c_ctx[None, :]
    new_mla_ckv, new_mla_kpe, new_gqa_k, new_gqa_v = [], [], [], []

    for i in range(DEPTH):
        sh_ap, sc_ap, g_ap, sh_fp, sc_fp, g_fp = adaln_params(cond_ctx, w_ada[i], b_ada[i])
        sh_as, sc_as, g_as, sh_fs, sc_fs, g_fs = adaln_params(c, w_ada[i], b_ada[i])
        hp = rms_norm(xp, norm1_g[i]) * (1.0 + sc_ap) + sh_ap
        hs = rms_norm(xs, norm1_g[i]) * (1.0 + sc_as) + sh_as
        j = i // N_MIXERS
        kind = i % N_MIXERS

        if kind == 0:
            op = short_conv_mixer(hp, conv_w_in[j], conv_k[j], conv_w_out[j])
            os_ = short_conv_mixer(hs, conv_w_in[j], conv_k[j], conv_w_out[j])
        elif kind == 1:
            q_p, ckv_p, kpe_p = mla_project(hp, mla_w_down[j], mla_q_norm_g[j], mla_kv_norm_g[j], mla_w_uq[j])
            k_p, v_p = mla_expand(ckv_p, kpe_p, mla_w_ukv[j])
            op = merge_heads(attention(q_p, k_p, v_p), mla_w_o[j])
            new_mla_ckv.append(ckv_p)
            new_mla_kpe.append(kpe_p)
            q_s, ckv_s, kpe_s = mla_project(hs, mla_w_down[j], mla_q_norm_g[j], mla_kv_norm_g[j], mla_w_uq[j])
            q_s = jnp.concatenate([q_s[..., :MLA_NOPE], rope_2d(q_s[..., MLA_NOPE:])], axis=-1)
            kpe_s = rope_2d(kpe_s[:, :, None, :])[:, :, 0, :]
            k_s, v_s = mla_expand(jnp.concatenate([ckv_s, cache_mla_ckv[:, j]], axis=1),
                                  jnp.concatenate([kpe_s, cache_mla_kpe[:, j]], axis=1),
                                  mla_w_ukv[j])
            os_ = merge_heads(attention(q_s, k_s, v_s), mla_w_o[j])
        else:
            q_p, k_p, v_p = gqa_project(hp, gqa_w_qkv[j], gqa_q_norm_g[j], gqa_k_norm_g[j])
            op = merge_heads(attention(q_p, k_p, v_p), gqa_w_o[j])
            new_gqa_k.append(k_p)
            new_gqa_v.append(v_p)
            q_s, k_s, v_s = gqa_project(hs, gqa_w_qkv[j], gqa_q_norm_g[j], gqa_k_norm_g[j])
            q_s = rope_2d(q_s)
            k_s = jnp.concatenate([rope_2d(k_s), cache_gqa_k[:, j]], axis=1)
            v_s = jnp.concatenate([v_s, cache_gqa_v[:, j]], axis=1)
            os_ = merge_heads(attention(q_s, k_s, v_s), gqa_w_o[j])

        xp = xp + g_ap * op
        xs = xs + g_as * os_

        hp = rms_norm(xp, norm2_g[i]) * (1.0 + sc_fp) + sh_fp
        hs = rms_norm(xs, norm2_g[i]) * (1.0 + sc_fs) + sh_fs
        m = i // 2
        if i % 2 == 0:
            fp = swiglu(hp, ffn_w_gate[m], ffn_w_up[m], ffn_w_down[m])
            fs = swiglu(hs, ffn_w_gate[m], ffn_w_up[m], ffn_w_down[m])
        else:
            fp = moe_swiglu(hp, moe_w_router[m], moe_w_gate[m], moe_w_up[m], moe_w_down[m])
            fs = moe_swiglu(hs, moe_w_router[m], moe_w_gate[m], moe_w_up[m], moe_w_down[m])
        xp = xp + g_fp * fp
        xs = xs + g_fs * fs

    y_prompt = rms_norm(xp, final_norm_g)
    y_sample = rms_norm(xs, final_norm_g)
    out_mla_ckv = jnp.stack(new_mla_ckv, axis=1)
    out_mla_kpe = jnp.stack(new_mla_kpe, axis=1)
    out_gqa_k = jnp.stack(new_gqa_k, axis=1)
    out_gqa_v = jnp.stack(new_gqa_v, axis=1)
    return (y_prompt, y_sample, out_mla_ckv, out_mla_kpe, out_gqa_k, out_gqa_v)
```

```python
import functools

import numpy as np
import jax
import jax.numpy as jnp
from jax import lax
from jax.experimental import pallas as pl
from jax.experimental.pallas import tpu as pltpu

D_MODEL = 1024
BATCH = 32
SEQ = 256
DEPTH = 4
DEC_BATCH = 2
DEC_SEQ = 1024
PAST_LEN = 512
GRID_W = 64
N_MIXERS = 3
CONV_WIDTH = 3
MLA_HEADS = 8
MLA_NOPE = 128
MLA_ROPE = 64
MLA_V = 128
MLA_Q_RANK = 384
MLA_KV_RANK = 256
GQA_HEADS = 8
GQA_KV_HEADS = 2
GQA_HEAD_DIM = 128
D_FF = 2816
N_EXPERTS = 8
MOE_FF = 2048
ROPE_THETA = 10000.0
EPS = 1e-6

N_CTX = BATCH * SEQ
N_LAT = DEC_BATCH * DEC_SEQ
N_TOK = N_CTX + N_LAT
N_COND = 8
LANES = 128
Q_TILE = 256
VMEM_LIMIT = 56 * 1024 * 1024

F32 = jnp.float32
BF16 = jnp.bfloat16


def _dot(a, b):
    return jnp.dot(a, b, preferred_element_type=F32)


def _dot_t(a, b):
    return lax.dot_general(a, b, (((1,), (1,)), ((), ())), preferred_element_type=F32)


def _rms(x):
    return x * lax.rsqrt(jnp.mean(x * x, axis=-1, keepdims=True) + EPS)


def _modulate(x, g, shift, scale):
    return _rms(x) * g * (1.0 + scale) + shift


def _silu(x):
    return x * jax.nn.sigmoid(x)


def _cond_of_tile(i, tm):
    start = i * tm
    return jnp.where(start < N_CTX, 0, 1 + (start - N_CTX) // DEC_SEQ)


def _resident(shape):
    return pl.BlockSpec(shape, lambda *_: (0,) * len(shape), pipeline_mode=pl.Buffered(1))


def _params(*sem):
    return pltpu.CompilerParams(dimension_semantics=sem, vmem_limit_bytes=VMEM_LIMIT)


def _adaln_kernel(cond_ref, w_ref, b_ref, o_ref):
    s = _silu(cond_ref[...]).astype(BF16)
    o_ref[...] = _dot(s, w_ref[...].astype(BF16)) + b_ref[...]


def _adaln_table(cond, w_ada, b_ada):
    tn = 1536
    out = pl.pallas_call(
        _adaln_kernel,
        grid=(DEPTH, 6 * D_MODEL // tn),
        in_specs=[
            pl.BlockSpec((N_COND, D_MODEL), lambda l, j: (0, 0)),
            pl.BlockSpec((None, D_MODEL, tn), lambda l, j: (l, 0, j)),
            pl.BlockSpec((None, 1, tn), lambda l, j: (l, 0, j)),
        ],
        out_specs=pl.BlockSpec((None, N_COND, tn), lambda l, j: (l, 0, j)),
        out_shape=jax.ShapeDtypeStruct((DEPTH, N_COND, 6 * D_MODEL), F32),
        compiler_params=_params("arbitrary", "arbitrary"),
        name="adaln_table",
    )(cond, w_ada, b_ada.reshape(DEPTH, 1, 6 * D_MODEL))
    return out.reshape(DEPTH, N_COND, 6, D_MODEL)


def _ada_spec(layer, tm):
    return pl.BlockSpec((None, None, 6, D_MODEL), lambda i: (layer, _cond_of_tile(i, tm), 0, 0))


def _ada_spec_at(layer, cond_fn):
    return pl.BlockSpec((None, None, 6, D_MODEL), lambda i: (layer, cond_fn(i), 0, 0))


CONV_TM = 1024
CONV_CC = 256


def _conv_kernel(x_ref, ada_ref, g1_ref, win_ref, ck_ref, wout_ref, o_ref, v_scr):
    tm = x_ref.shape[0]
    i = pl.program_id(0)
    x = x_ref[...]
    h = _modulate(x, g1_ref[...], ada_ref[0:1, :], ada_ref[1:2, :]).astype(BF16)
    period = jnp.where(i * tm < N_CTX, SEQ, DEC_SEQ)
    pos = lax.broadcasted_iota(jnp.int32, (tm, 1), 0) & (period - 1)
    first = pos == 0
    last = pos == period - 1
    for j in range(D_MODEL // CONV_CC):
        lo = j * CONV_CC
        b_gate = _dot(h, win_ref[:, lo:lo + CONV_CC])
        c_gate = _dot(h, win_ref[:, D_MODEL + lo:D_MODEL + lo + CONV_CC])
        x_in = _dot(h, win_ref[:, 2 * D_MODEL + lo:2 * D_MODEL + lo + CONV_CC])
        u = c_gate * x_in
        u_prev = jnp.where(first, 0.0, pltpu.roll(u, 1, 0))
        u_next = jnp.where(last, 0.0, pltpu.roll(u, tm - 1, 0))
        conv = (ck_ref[0:1, lo:lo + CONV_CC] * u_prev + ck_ref[1:2, lo:lo + CONV_CC] * u
                + ck_ref[2:3, lo:lo + CONV_CC] * u_next)
        v_scr[:, lo:lo + CONV_CC] = (b_gate * conv).astype(BF16)
    o_ref[...] = x + ada_ref[2:3, :] * _dot(v_scr[...], wout_ref[...])


def _conv_layer(x, ada, layer, g1, w_in, conv_k, w_out):
    tm = CONV_TM
    return pl.pallas_call(
        _conv_kernel,
        grid=(N_TOK // tm,),
        in_specs=[
            pl.BlockSpec((tm, D_MODEL), lambda i: (i, 0)),
            _ada_spec(layer, tm),
            _resident((1, D_MODEL)),
            _resident((D_MODEL, 3 * D_MODEL)),
            _resident((CONV_WIDTH, D_MODEL)),
            _resident((D_MODEL, D_MODEL)),
        ],
        out_specs=pl.BlockSpec((tm, D_MODEL), lambda i: (i, 0)),
        out_shape=jax.ShapeDtypeStruct((N_TOK, D_MODEL), F32),
        scratch_shapes=[pltpu.VMEM((tm, D_MODEL), BF16)],
        compiler_params=_params("arbitrary"),
        name=f"conv_mixer_{layer}",
    )(x, ada, g1, w_in, conv_k, w_out)


FFN_TM = 512
FFN_CHUNKS = ((0, 1536), (1536, D_FF))


def _ffn_kernel(x_ref, ada_ref, g2_ref, wg_ref, wu_ref, wd_ref, o_ref):
    x = x_ref[...]
    h = _modulate(x, g2_ref[...], ada_ref[3:4, :], ada_ref[4:5, :]).astype(BF16)
    f = None
    for lo, hi in FFN_CHUNKS:
        g = _dot(h, wg_ref[:, lo:hi])
        u = _dot(h, wu_ref[:, lo:hi])
        part = _dot((_silu(g) * u).astype(BF16), wd_ref[lo:hi, :])
        f = part if f is None else f + part
    o_ref[...] = x + ada_ref[5:6, :] * f


def _ffn_layer(x, ada, layer, g2, w_gate, w_up, w_down):
    tm = FFN_TM
    return pl.pallas_call(
        _ffn_kernel,
        grid=(N_TOK // tm,),
        in_specs=[
            pl.BlockSpec((tm, D_MODEL), lambda i: (i, 0)),
            _ada_spec(layer, tm),
            _resident((1, D_MODEL)),
            _resident((D_MODEL, D_FF)),
            _resident((D_MODEL, D_FF)),
            _resident((D_FF, D_MODEL)),
        ],
        out_specs=pl.BlockSpec((tm, D_MODEL), lambda i: (i, 0)),
        out_shape=jax.ShapeDtypeStruct((N_TOK, D_MODEL), F32),
        compiler_params=_params("arbitrary"),
        name=f"dense_ffn_{layer}",
    )(x, ada, g2, w_gate, w_up, w_down)


def _attention(qk_parts, v_ref, heads, kv_of_head, dv, scale, o_scr, finish):
    def q_block(rows):
        for h in range(heads):
            g = kv_of_head(h)
            s = None
            for q_ref, k_ref, width, k_shared in qk_parts:
                kb = 0 if k_shared else g
                part = _dot_t(q_ref[rows, h * width:(h + 1) * width],
                              k_ref[:, kb * width:(kb + 1) * width])
                s = part if s is None else s + part
            s = s * scale
            e = jnp.exp(s - jnp.max(s, axis=-1, keepdims=True))
            denom = jnp.sum(e, axis=-1, keepdims=True)
            o = _dot(e.astype(BF16), v_ref[:, g * dv:(g + 1) * dv]) / denom
            o_scr[rows, h * dv:(h + 1) * dv] = o.astype(BF16)
        finish(rows)

    _for_row_blocks(o_scr.shape[0], q_block)


def _for_row_blocks(n_rows, fn):
    if n_rows == Q_TILE:
        fn(pl.ds(0, Q_TILE))
    else:
        def body(b, carry):
            fn(pl.ds(pl.multiple_of(b * Q_TILE, Q_TILE), Q_TILE))
            return carry
        lax.fori_loop(0, n_rows // Q_TILE, body, 0)


MLA_QK = MLA_NOPE + MLA_ROPE
MLA_DOWN_COLS = MLA_Q_RANK + MLA_KV_RANK + 2 * MLA_ROPE
MLA_Q_NOPE_COLS = MLA_HEADS * MLA_NOPE
MLA_Q_ROPE_COLS = MLA_HEADS * MLA_ROPE


def _mla_kernel(*refs, latent):
    if latent:
        (x_ref, ada_ref, g1_ref, wdown_ref, gq_ref, gkv_ref, wuq_ref, wukv_ref, wo_ref,
         cosq_ref, sinq_ref, cosk_ref, sink_ref, cckv_ref, ckpe_ref,
         o_ref, qn_scr, qr_scr, kn_scr, kr_scr, v_scr, o_scr) = refs
    else:
        (x_ref, ada_ref, g1_ref, wdown_ref, gq_ref, gkv_ref, wuq_ref, wukv_ref, wo_ref,
         o_ref, ckv_out_ref, kpe_out_ref, qn_scr, qr_scr, kn_scr, kr_scr, v_scr, o_scr) = refs
    tm = x_ref.shape[0]
    kpe_lo = MLA_Q_RANK + MLA_KV_RANK

    def project(rows):
        h = _modulate(x_ref[rows, :], g1_ref[...], ada_ref[0:1, :], ada_ref[1:2, :]).astype(BF16)
        down = _dot(h, wdown_ref[...])
        c_q = down[:, :MLA_Q_RANK]
        c_kv = down[:, MLA_Q_RANK:kpe_lo]
        kpe = down[:, kpe_lo:kpe_lo + MLA_ROPE]
        q = _dot((_rms(c_q) * gq_ref[...]).astype(BF16), wuq_ref[...])
        ckv_n = _rms(c_kv) * gkv_ref[...]
        kv = _dot(ckv_n.astype(BF16), wukv_ref[...])
        qn_scr[rows, :] = q[:, :MLA_Q_NOPE_COLS].astype(BF16)
        q_rope = q[:, MLA_Q_NOPE_COLS:MLA_Q_NOPE_COLS + MLA_Q_ROPE_COLS]
        kn_scr[rows, :] = kv[:, :MLA_Q_NOPE_COLS].astype(BF16)
        v_scr[rows, :] = kv[:, MLA_Q_NOPE_COLS:].astype(BF16)
        if latent:
            q_swap = q[:, MLA_Q_NOPE_COLS + MLA_Q_ROPE_COLS:]
            q_rope = q_rope * cosq_ref[rows, :] + q_swap * sinq_ref[rows, :]
            kpe_swap = down[:, kpe_lo + MLA_ROPE:kpe_lo + 2 * MLA_ROPE]
            kpe = kpe * cosk_ref[rows, :] + kpe_swap * sink_ref[rows, :]
        else:
            ckv_out_ref[rows, :] = ckv_n
            kpe_out_ref[rows, :] = kpe
        qr_scr[rows, :] = q_rope.astype(BF16)
        kr_scr[rows, :] = kpe.astype(BF16)

    _for_row_blocks(tm, project)

    if latent:
        def expand_cache(rows):
            kv_c = _dot(cckv_ref[rows, :].astype(BF16), wukv_ref[...])
            dst = pl.ds(pl.multiple_of(tm + rows.start, Q_TILE), Q_TILE)
            kn_scr[dst, :] = kv_c[:, :MLA_Q_NOPE_COLS].astype(BF16)
            v_scr[dst, :] = kv_c[:, MLA_Q_NOPE_COLS:].astype(BF16)
            kr_scr[dst, :] = ckpe_ref[rows, :].astype(BF16)

        _for_row_blocks(PAST_LEN, expand_cache)

    def finish(rows):
        o_ref[rows, :] = x_ref[rows, :] + ada_ref[2:3, :] * _dot(o_scr[rows, :], wo_ref[...])

    _attention([(qn_scr, kn_scr, MLA_NOPE, False), (qr_scr, kr_scr, MLA_ROPE, True)],
               v_scr, MLA_HEADS, lambda hd: hd, MLA_V, MLA_QK ** -0.5, o_scr, finish)


def _mla_scratch(tm, sk):
    return [
        pltpu.VMEM((tm, MLA_Q_NOPE_COLS), BF16),
        pltpu.VMEM((tm, MLA_Q_ROPE_COLS), BF16),
        pltpu.VMEM((sk, MLA_Q_NOPE_COLS), BF16),
        pltpu.VMEM((sk, MLA_ROPE), BF16),
        pltpu.VMEM((sk, MLA_HEADS * MLA_V), BF16),
        pltpu.VMEM((tm, MLA_HEADS * MLA_V), BF16),
    ]


def _mla_layer(x, ada, layer, g1, w_down, g_q, g_kv, w_uq, w_ukv, w_o,
               cache_ckv, cache_kpe, rope):
    n_uq = w_uq.shape[1]
    weights = [
        _resident((1, D_MODEL)),
        _resident((D_MODEL, MLA_DOWN_COLS)),
        _resident((1, MLA_Q_RANK)),
        _resident((1, MLA_KV_RANK)),
        _resident((MLA_Q_RANK, n_uq)),
        _resident((MLA_KV_RANK, 2 * MLA_Q_NOPE_COLS)),
        _resident((MLA_HEADS * MLA_V, D_MODEL)),
    ]
    weight_args = (g1, w_down, g_q, g_kv, w_uq, w_ukv, w_o)
    ctx_out, new_ckv, new_kpe = pl.pallas_call(
        functools.partial(_mla_kernel, latent=False),
        grid=(BATCH,),
        in_specs=[pl.BlockSpec((SEQ, D_MODEL), lambda i: (i, 0)),
                  _ada_spec_at(layer, lambda i: 0)] + weights,
        out_specs=[
            pl.BlockSpec((SEQ, D_MODEL), lambda i: (i, 0)),
            pl.BlockSpec((SEQ, MLA_KV_RANK), lambda i: (i, 0)),
            pl.BlockSpec((SEQ, MLA_ROPE), lambda i: (i, 0)),
        ],
        out_shape=[
            jax.ShapeDtypeStruct((N_CTX, D_MODEL), F32),
            jax.ShapeDtypeStruct((N_CTX, MLA_KV_RANK), F32),
            jax.ShapeDtypeStruct((N_CTX, MLA_ROPE), F32),
        ],
        scratch_shapes=_mla_scratch(SEQ, SEQ),
        compiler_params=_params("arbitrary"),
        name=f"mla_ctx_{layer}",
    )(x, ada, *weight_args)

    cos_q, sin_q, cos_k, sin_k = rope
    ctx_blocks = N_CTX // DEC_SEQ
    lat_out = pl.pallas_call(
        functools.partial(_mla_kernel, latent=True),
        grid=(DEC_BATCH,),
        in_specs=[pl.BlockSpec((DEC_SEQ, D_MODEL), lambda b: (ctx_blocks + b, 0)),
                  _ada_spec_at(layer, lambda b: 1 + b)] + weights + [
            _resident((DEC_SEQ, MLA_Q_ROPE_COLS)),
            _resident((DEC_SEQ, MLA_Q_ROPE_COLS)),
            _resident((DEC_SEQ, MLA_ROPE)),
            _resident((DEC_SEQ, MLA_ROPE)),
            pl.BlockSpec((None, PAST_LEN, MLA_KV_RANK), lambda b: (b, 0, 0)),
            pl.BlockSpec((None, PAST_LEN, MLA_ROPE), lambda b: (b, 0, 0)),
        ],
        out_specs=pl.BlockSpec((DEC_SEQ, D_MODEL), lambda b: (b, 0)),
        out_shape=jax.ShapeDtypeStruct((N_LAT, D_MODEL), F32),
        scratch_shapes=_mla_scratch(DEC_SEQ, DEC_SEQ + PAST_LEN),
        compiler_params=_params("arbitrary"),
        name=f"mla_latent_{layer}",
    )(x, ada, *weight_args, cos_q, sin_q, cos_k, sin_k, cache_ckv, cache_kpe)
    return ctx_out, lat_out, new_ckv, new_kpe


GQA_Q_COLS = GQA_HEADS * GQA_HEAD_DIM
GQA_KV_COLS = GQA_KV_HEADS * GQA_HEAD_DIM
GQA_REP = GQA_HEADS // GQA_KV_HEADS


def _head_rms(t, n_heads):
    cols = []
    for hd in range(n_heads):
        th = t[:, hd * GQA_HEAD_DIM:(hd + 1) * GQA_HEAD_DIM]
        r = lax.rsqrt(jnp.mean(th * th, axis=-1, keepdims=True) + EPS)
        cols.append(jnp.broadcast_to(r, th.shape))
    return jnp.concatenate(cols, axis=-1)


def _gqa_kernel(*refs, latent):
    if latent:
        (x_ref, ada_ref, g1_ref, wqkv_ref, gq_ref, gk_ref, wo_ref,
         gqs_ref, gks_ref, cosq_ref, sinq_ref, cosk_ref, sink_ref, ck_ref, cv_ref,
         o_ref, q_scr, k_scr, v_scr, o_scr) = refs
    else:
        (x_ref, ada_ref, g1_ref, wqkv_ref, gq_ref, gk_ref, wo_ref,
         o_ref, k_out_ref, v_out_ref, q_scr, k_scr, v_scr, o_scr) = refs
    tm = x_ref.shape[0]

    def project(rows):
        h = _modulate(x_ref[rows, :], g1_ref[...], ada_ref[0:1, :], ada_ref[1:2, :]).astype(BF16)
        qkv = _dot(h, wqkv_ref[...])
        q_raw = qkv[:, :GQA_Q_COLS]
        k_raw = qkv[:, GQA_Q_COLS:GQA_Q_COLS + GQA_KV_COLS]
        v = qkv[:, GQA_Q_COLS + GQA_KV_COLS:GQA_Q_COLS + 2 * GQA_KV_COLS]
        q_r = _head_rms(q_raw, GQA_HEADS)
        k_r = _head_rms(k_raw, GQA_KV_HEADS)
        q = q_raw * q_r * gq_ref[...]
        k = k_raw * k_r * gk_ref[...]
        if latent:
            lo = GQA_Q_COLS + 2 * GQA_KV_COLS
            q_swap = qkv[:, lo:lo + GQA_Q_COLS] * q_r * gqs_ref[...]
            k_swap = qkv[:, lo + GQA_Q_COLS:lo + GQA_Q_COLS + GQA_KV_COLS] * k_r * gks_ref[...]
            q = q * cosq_ref[rows, :] + q_swap * sinq_ref[rows, :]
            k = k * cosk_ref[rows, :] + k_swap * sink_ref[rows, :]
        else:
            k_out_ref[rows, :] = k
            v_out_ref[rows, :] = v
        q_scr[rows, :] = q.astype(BF16)
        k_scr[rows, :] = k.astype(BF16)
        v_scr[rows, :] = v.astype(BF16)

    _for_row_blocks(tm, project)

    if latent:
        k_scr[tm:, :] = ck_ref[...].astype(BF16)
        v_scr[tm:, :] = cv_ref[...].astype(BF16)

    def finish(rows):
        o_ref[rows, :] = x_ref[rows, :] + ada_ref[2:3, :] * _dot(o_scr[rows, :], wo_ref[...])

    _attention([(q_scr, k_scr, GQA_HEAD_DIM, False)], v_scr, GQA_HEADS,
               lambda hd: hd // GQA_REP, GQA_HEAD_DIM, GQA_HEAD_DIM ** -0.5, o_scr, finish)


def _gqa_scratch(tm, sk):
    return [
        pltpu.VMEM((tm, GQA_Q_COLS), BF16),
        pltpu.VMEM((sk, GQA_KV_COLS), BF16),
        pltpu.VMEM((sk, GQA_KV_COLS), BF16),
        pltpu.VMEM((tm, GQA_Q_COLS), BF16),
    ]


def _gqa_layer(x, ada, layer, g1, w_qkv_ctx, w_qkv_lat, g_q, g_k, g_q_swap, g_k_swap, w_o,
               cache_k, cache_v, rope):
    ctx_out, new_k, new_v = pl.pallas_call(
        functools.partial(_gqa_kernel, latent=False),
        grid=(BATCH,),
        in_specs=[pl.BlockSpec((SEQ, D_MODEL), lambda i: (i, 0)),
                  _ada_spec_at(layer, lambda i: 0),
                  _resident((1, D_MODEL)),
                  _resident(w_qkv_ctx.shape),
                  _resident((1, GQA_Q_COLS)),
                  _resident((1, GQA_KV_COLS)),
                  _resident((GQA_Q_COLS, D_MODEL))],
        out_specs=[
            pl.BlockSpec((SEQ, D_MODEL), lambda i: (i, 0)),
            pl.BlockSpec((SEQ, GQA_KV_COLS), lambda i: (i, 0)),
            pl.BlockSpec((SEQ, GQA_KV_COLS), lambda i: (i, 0)),
        ],
        out_shape=[
            jax.ShapeDtypeStruct((N_CTX, D_MODEL), F32),
            jax.ShapeDtypeStruct((N_CTX, GQA_KV_COLS), F32),
            jax.ShapeDtypeStruct((N_CTX, GQA_KV_COLS), F32),
        ],
        scratch_shapes=_gqa_scratch(SEQ, SEQ),
        compiler_params=_params("arbitrary"),
        name=f"gqa_ctx_{layer}",
    )(x, ada, g1, w_qkv_ctx, g_q, g_k, w_o)

    cos_q, sin_q, cos_k, sin_k = rope
    ctx_blocks = N_CTX // DEC_SEQ
    lat_out = pl.pallas_call(
        functools.partial(_gqa_kernel, latent=True),
        grid=(DEC_BATCH,),
        in_specs=[pl.BlockSpec((DEC_SEQ, D_MODEL), lambda b: (ctx_blocks + b, 0)),
                  _ada_spec_at(layer, lambda b: 1 + b),
                  _resident((1, D_MODEL)),
                  _resident(w_qkv_lat.shape),
                  _resident((1, GQA_Q_COLS)),
                  _resident((1, GQA_KV_COLS)),
                  _resident((GQA_Q_COLS, D_MODEL)),
                  _resident((1, GQA_Q_COLS)),
                  _resident((1, GQA_KV_COLS)),
                  _resident((DEC_SEQ, GQA_Q_COLS)),
                  _resident((DEC_SEQ, GQA_Q_COLS)),
                  _resident((DEC_SEQ, GQA_KV_COLS)),
                  _resident((DEC_SEQ, GQA_KV_COLS)),
                  pl.BlockSpec((None, PAST_LEN, GQA_KV_COLS), lambda b: (b, 0, 0)),
                  pl.BlockSpec((None, PAST_LEN, GQA_KV_COLS), lambda b: (b, 0, 0))],
        out_specs=pl.BlockSpec((DEC_SEQ, D_MODEL), lambda b: (b, 0)),
        out_shape=jax.ShapeDtypeStruct((N_LAT, D_MODEL), F32),
        scratch_shapes=_gqa_scratch(DEC_SEQ, DEC_SEQ + PAST_LEN),
        compiler_params=_params("arbitrary"),
        name=f"gqa_latent_{layer}",
    )(x, ada, g1, w_qkv_lat, g_q, g_k, w_o, g_q_swap, g_k_swap,
      cos_q, sin_q, cos_k, sin_k, cache_k, cache_v)
    return ctx_out, lat_out, new_k, new_v


MOE_TM = 512
MOE_FC = 1024


def _top2_gates(logits):
    lane = lax.broadcasted_iota(jnp.int32, logits.shape, 1)
    neg = jnp.float32(-jnp.inf)
    l1 = jnp.where(lane < N_EXPERTS, logits, neg)
    m1 = jnp.max(l1, axis=-1, keepdims=True)
    i1 = jnp.min(jnp.where(l1 == m1, lane, LANES), axis=-1, keepdims=True)
    l2 = jnp.where(lane == i1, neg, l1)
    m2 = jnp.max(l2, axis=-1, keepdims=True)
    i2 = jnp.min(jnp.where(l2 == m2, lane, LANES), axis=-1, keepdims=True)
    e2 = jnp.exp(m2 - m1)
    w1 = 1.0 / (1.0 + e2)
    w2 = e2 / (1.0 + e2)
    return jnp.where(lane == i1, w1, 0.0) + jnp.where(lane == i2, w2, 0.0)


def _moe_kernel(x_ref, ada_ref, g2_ref, wr_ref, wg_ref, wu_ref, wd_ref, o_ref,
                h_scr, gate_scr, acc_scr):
    e = pl.program_id(1)
    f = pl.program_id(2)

    @pl.when((e == 0) & (f == 0))
    def _():
        h = _modulate(x_ref[...], g2_ref[...], ada_ref[3:4, :], ada_ref[4:5, :])
        h_scr[...] = h.astype(BF16)
        logits = jnp.dot(h, wr_ref[...], precision=lax.Precision.HIGHEST,
                         preferred_element_type=F32)
        gate_scr[...] = _top2_gates(logits)
        acc_scr[...] = jnp.zeros_like(acc_scr)

    h = h_scr[...]
    g = _dot(h, wg_ref[...])
    u = _dot(h, wu_ref[...])
    lane = lax.broadcasted_iota(jnp.int32, gate_scr.shape, 1)
    gate = jnp.sum(jnp.where(lane == e, gate_scr[...], 0.0), axis=-1, keepdims=True)
    acc_scr[...] += _dot((_silu(g) * u * gate).astype(BF16), wd_ref[...])

    @pl.when((e == N_EXPERTS - 1) & (f == pl.num_programs(2) - 1))
    def _():
        o_ref[...] = x_ref[...] + ada_ref[5:6, :] * acc_scr[...]


def _moe_layer(x, ada, layer, g2, w_router, w_gate, w_up, w_down):
    tm = MOE_TM
    return pl.pallas_call(
        _moe_kernel,
        grid=(N_TOK // tm, N_EXPERTS, MOE_FF // MOE_FC),
        in_specs=[
            pl.BlockSpec((tm, D_MODEL), lambda i, e, f: (i, 0)),
            pl.BlockSpec((None, None, 6, D_MODEL),
                         lambda i, e, f: (layer, _cond_of_tile(i, tm), 0, 0)),
            _resident((1, D_MODEL)),
            _resident((D_MODEL, LANES)),
            pl.BlockSpec((None, D_MODEL, MOE_FC), lambda i, e, f: (e, 0, f)),
            pl.BlockSpec((None, D_MODEL, MOE_FC), lambda i, e, f: (e, 0, f)),
            pl.BlockSpec((None, MOE_FC, D_MODEL), lambda i, e, f: (e, f, 0)),
        ],
        out_specs=pl.BlockSpec((tm, D_MODEL), lambda i, e, f: (i, 0)),
        out_shape=jax.ShapeDtypeStruct((N_TOK, D_MODEL), F32),
        scratch_shapes=[pltpu.VMEM((tm, D_MODEL), BF16),
                        pltpu.VMEM((tm, LANES), F32),
                        pltpu.VMEM((tm, D_MODEL), F32)],
        compiler_params=_params("arbitrary", "arbitrary", "arbitrary"),
        name=f"moe_{layer}",
    )(x, ada, g2, w_router, w_gate, w_up, w_down)


def _final_kernel(x_ref, g_ref, o_ref):
    o_ref[...] = _rms(x_ref[...]) * g_ref[...]


def _final_norm(x, g, row0, rows):
    tm = 1024
    blk0 = row0 // tm
    return pl.pallas_call(
        _final_kernel,
        grid=(rows // tm,),
        in_specs=[pl.BlockSpec((tm, D_MODEL), lambda i: (blk0 + i, 0)),
                  _resident((1, D_MODEL))],
        out_specs=pl.BlockSpec((tm, D_MODEL), lambda i: (i, 0)),
        out_shape=jax.ShapeDtypeStruct((rows, D_MODEL), F32),
        compiler_params=_params("arbitrary"),
        name="final_norm",
    )(x, g)


def _rope_tables(dim):
    half = dim // 2
    quarter = half // 2
    t = np.arange(DEC_SEQ)
    pos = np.stack([t // GRID_W, t % GRID_W], axis=1).astype(np.float32)
    freqs = (ROPE_THETA ** (-np.arange(quarter, dtype=np.float32) / quarter)).astype(np.float32)
    ang = pos[:, :, None] * freqs[None, None, :]
    cos = np.cos(ang.astype(np.float64))
    sin = np.sin(ang.astype(np.float64))
    cos_t = np.concatenate([cos, cos], axis=-1).reshape(DEC_SEQ, dim)
    sin_t = np.concatenate([-sin, sin], axis=-1).reshape(DEC_SEQ, dim)
    lane = np.arange(dim)
    swap = np.where(lane % half < quarter, lane + quarter, lane - quarter)
    return cos_t.astype(np.float32), sin_t.astype(np.float32), swap


def kernel(x_prompt, x_sample, cache_mla_ckv, cache_mla_kpe, cache_gqa_k, cache_gqa_v, c, c_ctx, w_ada, b_ada, norm1_g, norm2_g, conv_w_in, conv_k, conv_w_out, mla_w_down, mla_q_norm_g, mla_kv_norm_g, mla_w_uq, mla_w_ukv, mla_w_o, gqa_w_qkv, gqa_q_norm_g, gqa_k_norm_g, gqa_w_o, ffn_w_gate, ffn_w_up, ffn_w_down, moe_w_router, moe_w_gate, moe_w_up, moe_w_down, final_norm_g):
    x = jnp.concatenate([x_prompt.reshape(N_CTX, D_MODEL), x_sample.reshape(N_LAT, D_MODEL)], axis=0)
    cond = jnp.concatenate(
        [c_ctx[None, :], c, jnp.zeros((N_COND - 1 - DEC_BATCH, D_MODEL), F32)], axis=0)
    ada = _adaln_table(cond, w_ada, b_ada)

    new_ckv, new_kpe, new_k, new_v = [], [], [], []
    for i in range(DEPTH):
        j = i // N_MIXERS
        kind = i % N_MIXERS
        g1 = norm1_g[i][None, :]
        if kind == 0:
            x = _conv_layer(x, ada, i, g1, conv_w_in[j].astype(BF16), conv_k[j],
                            conv_w_out[j].astype(BF16))
        elif kind == 1:
            cos64, sin64, swap64 = _rope_tables(MLA_ROPE)
            wd = mla_w_down[j]
            kpe_cols = wd[:, MLA_Q_RANK + MLA_KV_RANK:]
            w_down = jnp.concatenate([wd, kpe_cols[:, swap64]], axis=1).astype(BF16)
            wq = mla_w_uq[j].reshape(MLA_Q_RANK, MLA_HEADS, MLA_QK)
            wq_nope = wq[:, :, :MLA_NOPE].reshape(MLA_Q_RANK, MLA_Q_NOPE_COLS)
            wq_rope = wq[:, :, MLA_NOPE:]
            w_uq = jnp.concatenate(
                [wq_nope, wq_rope.reshape(MLA_Q_RANK, MLA_Q_ROPE_COLS),
                 wq_rope[:, :, swap64].reshape(MLA_Q_RANK, MLA_Q_ROPE_COLS)], axis=1).astype(BF16)
            wkv = mla_w_ukv[j].reshape(MLA_KV_RANK, MLA_HEADS, MLA_NOPE + MLA_V)
            w_ukv = jnp.concatenate(
                [wkv[:, :, :MLA_NOPE].reshape(MLA_KV_RANK, MLA_Q_NOPE_COLS),
                 wkv[:, :, MLA_NOPE:].reshape(MLA_KV_RANK, MLA_HEADS * MLA_V)], axis=1).astype(BF16)
            rope = (jnp.asarray(np.tile(cos64, (1, MLA_HEADS))), jnp.asarray(np.tile(sin64, (1, MLA_HEADS))),
                    jnp.asarray(cos64), jnp.asarray(sin64))
            xc, xl, ckv_p, kpe_p = _mla_layer(
                x, ada, i, g1, w_down, mla_q_norm_g[j][None, :], mla_kv_norm_g[j][None, :],
                w_uq, w_ukv, mla_w_o[j].astype(BF16), cache_mla_ckv[:, j], cache_mla_kpe[:, j], rope)
            x = jnp.concatenate([xc, xl], axis=0)
            new_ckv.append(ckv_p.reshape(BATCH, SEQ, MLA_KV_RANK))
            new_kpe.append(kpe_p.reshape(BATCH, SEQ, MLA_ROPE))
        else:
            cos128, sin128, swap128 = _rope_tables(GQA_HEAD_DIM)
            wqkv = gqa_w_qkv[j]
            wq = wqkv[:, :GQA_Q_COLS].reshape(D_MODEL, GQA_HEADS, GQA_HEAD_DIM)
            wk = wqkv[:, GQA_Q_COLS:GQA_Q_COLS + GQA_KV_COLS].reshape(D_MODEL, GQA_KV_HEADS, GQA_HEAD_DIM)
            w_qkv_ctx = wqkv.astype(BF16)
            w_qkv_lat = jnp.concatenate(
                [wqkv, wq[:, :, swap128].reshape(D_MODEL, GQA_Q_COLS),
                 wk[:, :, swap128].reshape(D_MODEL, GQA_KV_COLS)], axis=1).astype(BF16)
            gq = gqa_q_norm_g[j]
            gk = gqa_k_norm_g[j]
            rope = (jnp.asarray(np.tile(cos128, (1, GQA_HEADS))), jnp.asarray(np.tile(sin128, (1, GQA_HEADS))),
                    jnp.asarray(np.tile(cos128, (1, GQA_KV_HEADS))), jnp.asarray(np.tile(sin128, (1, GQA_KV_HEADS))))
            xc, xl, k_p, v_p = _gqa_layer(
                x, ada, i, g1, w_qkv_ctx, w_qkv_lat,
                jnp.tile(gq, GQA_HEADS)[None, :], jnp.tile(gk, GQA_KV_HEADS)[None, :],
                jnp.tile(gq[swap128], GQA_HEADS)[None, :], jnp.tile(gk[swap128], GQA_KV_HEADS)[None, :],
                gqa_w_o[j].astype(BF16),
                cache_gqa_k[:, j].reshape(DEC_BATCH, PAST_LEN, GQA_KV_COLS),
                cache_gqa_v[:, j].reshape(DEC_BATCH, PAST_LEN, GQA_KV_COLS), rope)
            x = jnp.concatenate([xc, xl], axis=0)
            new_k.append(k_p.reshape(BATCH, SEQ, GQA_KV_HEADS, GQA_HEAD_DIM))
            new_v.append(v_p.reshape(BATCH, SEQ, GQA_KV_HEADS, GQA_HEAD_DIM))

        m = i // 2
        g2 = norm2_g[i][None, :]
        if i % 2 == 0:
            x = _ffn_layer(x, ada, i, g2, ffn_w_gate[m].astype(BF16), ffn_w_up[m].astype(BF16),
                           ffn_w_down[m].astype(BF16))
        else:
            w_router = jnp.pad(moe_w_router[m], ((0, 0), (0, LANES - N_EXPERTS)))
            x = _moe_layer(x, ada, i, g2, w_router, moe_w_gate[m].astype(BF16),
                           moe_w_up[m].astype(BF16), moe_w_down[m].astype(BF16))

    g_final = final_norm_g[None, :]
    y_prompt = _final_norm(x, g_final, 0, N_CTX).reshape(BATCH, SEQ, D_MODEL)
    y_sample = _final_norm(x, g_final, N_CTX, N_LAT).reshape(DEC_BATCH, DEC_SEQ, D_MODEL)
    return (y_prompt, y_sample,
            jnp.stack(new_ckv, axis=1), jnp.stack(new_kpe, axis=1),
            jnp.stack(new_k, axis=1), jnp.stack(new_v, axis=1))
```

```python
import functools

import numpy as np
import jax
import jax.numpy as jnp
from jax import lax
from jax.experimental import pallas as pl
from jax.experimental.pallas import tpu as pltpu

D_MODEL = 1024
BATCH = 32
SEQ = 256
DEPTH = 4
DEC_BATCH = 2
DEC_SEQ = 1024
PAST_LEN = 512
GRID_W = 64
N_MIXERS = 3
CONV_WIDTH = 3
MLA_HEADS = 8
MLA_NOPE = 128
MLA_ROPE = 64
MLA_V = 128
MLA_Q_RANK = 384
MLA_KV_RANK = 256
GQA_HEADS = 8
GQA_KV_HEADS = 2
GQA_HEAD_DIM = 128
D_FF = 2816
N_EXPERTS = 8
MOE_FF = 2048
ROPE_THETA = 10000.0
EPS = 1e-6

N_CTX = BATCH * SEQ
N_LAT = DEC_BATCH * DEC_SEQ
N_TOK = N_CTX + N_LAT
N_COND = 8
LANES = 128
Q_TILE = 256
VMEM_LIMIT = 56 * 1024 * 1024

F32 = jnp.float32
BF16 = jnp.bfloat16


def _dot(a, b):
    return jnp.dot(a, b, preferred_element_type=F32)


def _dot_t(a, b):
    return lax.dot_general(a, b, (((1,), (1,)), ((), ())), preferred_element_type=F32)


def _rms(x):
    return x * lax.rsqrt(jnp.mean(x * x, axis=-1, keepdims=True) + EPS)


def _modulate(x, g, shift, scale):
    return _rms(x) * g * (1.0 + scale) + shift


def _silu(x):
    return x * jax.nn.sigmoid(x)


def _cond_of_tile(i, tm):
    start = i * tm
    return jnp.where(start < N_CTX, 0, 1 + (start - N_CTX) // DEC_SEQ)


def _resident(shape):
    return pl.BlockSpec(shape, lambda *_: (0,) * len(shape), pipeline_mode=pl.Buffered(1))


def _params(*sem):
    return pltpu.CompilerParams(dimension_semantics=sem, vmem_limit_bytes=VMEM_LIMIT)


def _adaln_kernel(cond_ref, w_ref, b_ref, o_ref):
    s = _silu(cond_ref[...]).astype(BF16)
    o_ref[...] = _dot(s, w_ref[...].astype(BF16)) + b_ref[...]


def _adaln_table(cond, w_ada, b_ada):
    tn = 1536
    out = pl.pallas_call(
        _adaln_kernel,
        grid=(DEPTH, 6 * D_MODEL // tn),
        in_specs=[
            pl.BlockSpec((N_COND, D_MODEL), lambda l, j: (0, 0)),
            pl.BlockSpec((None, D_MODEL, tn), lambda l, j: (l, 0, j)),
            pl.BlockSpec((None, 1, tn), lambda l, j: (l, 0, j)),
        ],
        out_specs=pl.BlockSpec((None, N_COND, tn), lambda l, j: (l, 0, j)),
        out_shape=jax.ShapeDtypeStruct((DEPTH, N_COND, 6 * D_MODEL), F32),
        compiler_params=_params("arbitrary", "arbitrary"),
        name="adaln_table",
    )(cond, w_ada, b_ada.reshape(DEPTH, 1, 6 * D_MODEL))
    return out.reshape(DEPTH, N_COND, 6, D_MODEL)


def _ada_spec(layer, tm):
    return pl.BlockSpec((None, None, 6, D_MODEL), lambda i: (layer, _cond_of_tile(i, tm), 0, 0))


def _ada_spec_at(layer, cond_fn):
    return pl.BlockSpec((None, None, 6, D_MODEL), lambda i: (layer, cond_fn(i), 0, 0))


CONV_TM = 1024
CONV_CC = 256


def _conv_kernel(x_ref, ada_ref, g1_ref, win_ref, ck_ref, wout_ref, o_ref, v_scr):
    tm = x_ref.shape[0]
    i = pl.program_id(0)
    x = x_ref[...]
    h = _modulate(x, g1_ref[...], ada_ref[0:1, :], ada_ref[1:2, :]).astype(BF16)
    period = jnp.where(i * tm < N_CTX, SEQ, DEC_SEQ)
    pos = lax.broadcasted_iota(jnp.int32, (tm, 1), 0) & (period - 1)
    first = pos == 0
    last = pos == period - 1
    for j in range(D_MODEL // CONV_CC):
        lo = j * CONV_CC
        b_gate = _dot(h, win_ref[:, lo:lo + CONV_CC])
        c_gate = _dot(h, win_ref[:, D_MODEL + lo:D_MODEL + lo + CONV_CC])
        x_in = _dot(h, win_ref[:, 2 * D_MODEL + lo:2 * D_MODEL + lo + CONV_CC])
        u = c_gate * x_in
        u_prev = jnp.where(first, 0.0, pltpu.roll(u, 1, 0))
        u_next = jnp.where(last, 0.0, pltpu.roll(u, tm - 1, 0))
        conv = (ck_ref[0:1, lo:lo + CONV_CC] * u_prev + ck_ref[1:2, lo:lo + CONV_CC] * u
                + ck_ref[2:3, lo:lo + CONV_CC] * u_next)
        v_scr[:, lo:lo + CONV_CC] = (b_gate * conv).astype(BF16)
    o_ref[...] = x + ada_ref[2:3, :] * _dot(v_scr[...], wout_ref[...])


def _conv_layer(x, ada, layer, g1, w_in, conv_k, w_out):
    tm = CONV_TM
    return pl.pallas_call(
        _conv_kernel,
        grid=(N_TOK // tm,),
        in_specs=[
            pl.BlockSpec((tm, D_MODEL), lambda i: (i, 0)),
            _ada_spec(layer, tm),
            _resident((1, D_MODEL)),
            _resident((D_MODEL, 3 * D_MODEL)),
            _resident((CONV_WIDTH, D_MODEL)),
            _resident((D_MODEL, D_MODEL)),
        ],
        out_specs=pl.BlockSpec((tm, D_MODEL), lambda i: (i, 0)),
        out_shape=jax.ShapeDtypeStruct((N_TOK, D_MODEL), F32),
        scratch_shapes=[pltpu.VMEM((tm, D_MODEL), BF16)],
        compiler_params=_params("arbitrary"),
        name=f"conv_mixer_{layer}",
    )(x, ada, g1, w_in, conv_k, w_out)


FFN_TM = 512
FFN_CHUNKS = ((0, 1536), (1536, D_FF))


def _ffn_kernel(x_ref, ada_ref, g2_ref, wg_ref, wu_ref, wd_ref, o_ref):
    x = x_ref[...]
    h = _modulate(x, g2_ref[...], ada_ref[3:4, :], ada_ref[4:5, :]).astype(BF16)
    f = None
    for lo, hi in FFN_CHUNKS:
        g = _dot(h, wg_ref[:, lo:hi])
        u = _dot(h, wu_ref[:, lo:hi])
        part = _dot((_silu(g) * u).astype(BF16), wd_ref[lo:hi, :])
        f = part if f is None else f + part
    o_ref[...] = x + ada_ref[5:6, :] * f


def _ffn_layer(x, ada, layer, g2, w_gate, w_up, w_down):
    tm = FFN_TM
    return pl.pallas_call(
        _ffn_kernel,
        grid=(N_TOK // tm,),
        in_specs=[
            pl.BlockSpec((tm, D_MODEL), lambda i: (i, 0)),
            _ada_spec(layer, tm),
            _resident((1, D_MODEL)),
            _resident((D_MODEL, D_FF)),
            _resident((D_MODEL, D_FF)),
            _resident((D_FF, D_MODEL)),
        ],
        out_specs=pl.BlockSpec((tm, D_MODEL), lambda i: (i, 0)),
        out_shape=jax.ShapeDtypeStruct((N_TOK, D_MODEL), F32),
        compiler_params=_params("arbitrary"),
        name=f"dense_ffn_{layer}",
    )(x, ada, g2, w_gate, w_up, w_down)


def _attention(qk_parts, v_ref, heads, kv_of_head, dv, scale, o_scr, finish):
    def q_block(rows):
        for h in range(heads):
            g = kv_of_head(h)
            s = None
            for q_ref, k_ref, width, k_shared in qk_parts:
                kb = 0 if k_shared else g
                part = _dot_t(q_ref[rows, h * width:(h + 1) * width],
                              k_ref[:, kb * width:(kb + 1) * width])
                s = part if s is None else s + part
            s = s * scale
            e = jnp.exp(s - jnp.max(s, axis=-1, keepdims=True))
            denom = jnp.sum(e, axis=-1, keepdims=True)
            o = _dot(e.astype(BF16), v_ref[:, g * dv:(g + 1) * dv]) / denom
            o_scr[rows, h * dv:(h + 1) * dv] = o.astype(BF16)
        finish(rows)

    _for_row_blocks(o_scr.shape[0], q_block)


def _for_row_blocks(n_rows, fn):
    if n_rows == Q_TILE:
        fn(pl.ds(0, Q_TILE))
    else:
        def body(b, carry):
            fn(pl.ds(pl.multiple_of(b * Q_TILE, Q_TILE), Q_TILE))
            return carry
        lax.fori_loop(0, n_rows // Q_TILE, body, 0)


MLA_QK = MLA_NOPE + MLA_ROPE
MLA_DOWN_COLS = MLA_Q_RANK + MLA_KV_RANK + 2 * MLA_ROPE
MLA_Q_NOPE_COLS = MLA_HEADS * MLA_NOPE
MLA_Q_ROPE_COLS = MLA_HEADS * MLA_ROPE


def _mla_kernel(*refs, latent):
    if latent:
        (x_ref, ada_ref, g1_ref, wdown_ref, gq_ref, gkv_ref, wuq_ref, wukv_ref, wo_ref,
         cosq_ref, sinq_ref, cosk_ref, sink_ref, cckv_ref, ckpe_ref,
         o_ref, qn_scr, qr_scr, kn_scr, kr_scr, v_scr, o_scr) = refs
    else:
        (x_ref, ada_ref, g1_ref, wdown_ref, gq_ref, gkv_ref, wuq_ref, wukv_ref, wo_ref,
         o_ref, ckv_out_ref, kpe_out_ref, qn_scr, qr_scr, kn_scr, kr_scr, v_scr, o_scr) = refs
    tm = x_ref.shape[0]
    kpe_lo = MLA_Q_RANK + MLA_KV_RANK

    def project(rows):
        h = _modulate(x_ref[rows, :], g1_ref[...], ada_ref[0:1, :], ada_ref[1:2, :]).astype(BF16)
        down = _dot(h, wdown_ref[...])
        c_q = down[:, :MLA_Q_RANK]
        c_kv = down[:, MLA_Q_RANK:kpe_lo]
        kpe = down[:, kpe_lo:kpe_lo + MLA_ROPE]
        q = _dot((_rms(c_q) * gq_ref[...]).astype(BF16), wuq_ref[...])
        ckv_n = _rms(c_kv) * gkv_ref[...]
        kv = _dot(ckv_n.astype(BF16), wukv_ref[...])
        qn_scr[rows, :] = q[:, :MLA_Q_NOPE_COLS].astype(BF16)
        q_rope = q[:, MLA_Q_NOPE_COLS:MLA_Q_NOPE_COLS + MLA_Q_ROPE_COLS]
        kn_scr[rows, :] = kv[:, :MLA_Q_NOPE_COLS].astype(BF16)
        v_scr[rows, :] = kv[:, MLA_Q_NOPE_COLS:].astype(BF16)
        if latent:
            q_swap = q[:, MLA_Q_NOPE_COLS + MLA_Q_ROPE_COLS:]
            q_rope = q_rope * cosq_ref[rows, :] + q_swap * sinq_ref[rows, :]
            kpe_swap = down[:, kpe_lo + MLA_ROPE:kpe_lo + 2 * MLA_ROPE]
            kpe = kpe * cosk_ref[rows, :] + kpe_swap * sink_ref[rows, :]
        else:
            ckv_out_ref[rows, :] = ckv_n
            kpe_out_ref[rows, :] = kpe
        qr_scr[rows, :] = q_rope.astype(BF16)
        kr_scr[rows, :] = kpe.astype(BF16)

    _for_row_blocks(tm, project)

    if latent:
        def expand_cache(rows):
            kv_c = _dot(cckv_ref[rows, :].astype(BF16), wukv_ref[...])
            dst = pl.ds(pl.multiple_of(tm + rows.start, Q_TILE), Q_TILE)
            kn_scr[dst, :] = kv_c[:, :MLA_Q_NOPE_COLS].astype(BF16)
            v_scr[dst, :] = kv_c[:, MLA_Q_NOPE_COLS:].astype(BF16)
            kr_scr[dst, :] = ckpe_ref[rows, :].astype(BF16)

        _for_row_blocks(PAST_LEN, expand_cache)

    def finish(rows):
        o_ref[rows, :] = x_ref[rows, :] + ada_ref[2:3, :] * _dot(o_scr[rows, :], wo_ref[...])

    _attention([(qn_scr, kn_scr, MLA_NOPE, False), (qr_scr, kr_scr, MLA_ROPE, True)],
               v_scr, MLA_HEADS, lambda hd: hd, MLA_V, MLA_QK ** -0.5, o_scr, finish)


def _mla_scratch(tm, sk):
    return [
        pltpu.VMEM((tm, MLA_Q_NOPE_COLS), BF16),
        pltpu.VMEM((tm, MLA_Q_ROPE_COLS), BF16),
        pltpu.VMEM((sk, MLA_Q_NOPE_COLS), BF16),
        pltpu.VMEM((sk, MLA_ROPE), BF16),
        pltpu.VMEM((sk, MLA_HEADS * MLA_V), BF16),
        pltpu.VMEM((tm, MLA_HEADS * MLA_V), BF16),
    ]


def _mla_layer(x, ada, layer, g1, w_down, g_q, g_kv, w_uq, w_ukv, w_o,
               cache_ckv, cache_kpe, rope):
    n_uq = w_uq.shape[1]
    weights = [
        _resident((1, D_MODEL)),
        _resident((D_MODEL, MLA_DOWN_COLS)),
        _resident((1, MLA_Q_RANK)),
        _resident((1, MLA_KV_RANK)),
        _resident((MLA_Q_RANK, n_uq)),
        _resident((MLA_KV_RANK, 2 * MLA_Q_NOPE_COLS)),
        _resident((MLA_HEADS * MLA_V, D_MODEL)),
    ]
    weight_args = (g1, w_down, g_q, g_kv, w_uq, w_ukv, w_o)
    ctx_out, new_ckv, new_kpe = pl.pallas_call(
        functools.partial(_mla_kernel, latent=False),
        grid=(BATCH,),
        in_specs=[pl.BlockSpec((SEQ, D_MODEL), lambda i: (i, 0)),
                  _ada_spec_at(layer, lambda i: 0)] + weights,
        out_specs=[
            pl.BlockSpec((SEQ, D_MODEL), lambda i: (i, 0)),
            pl.BlockSpec((SEQ, MLA_KV_RANK), lambda i: (i, 0)),
            pl.BlockSpec((SEQ, MLA_ROPE), lambda i: (i, 0)),
        ],
        out_shape=[
            jax.ShapeDtypeStruct((N_CTX, D_MODEL), F32),
            jax.ShapeDtypeStruct((N_CTX, MLA_KV_RANK), F32),
            jax.ShapeDtypeStruct((N_CTX, MLA_ROPE), F32),
        ],
        scratch_shapes=_mla_scratch(SEQ, SEQ),
        compiler_params=_params("arbitrary"),
        name=f"mla_ctx_{layer}",
    )(x, ada, *weight_args)

    cos_q, sin_q, cos_k, sin_k = rope
    ctx_blocks = N_CTX // DEC_SEQ
    lat_out = pl.pallas_call(
        functools.partial(_mla_kernel, latent=True),
        grid=(DEC_BATCH,),
        in_specs=[pl.BlockSpec((DEC_SEQ, D_MODEL), lambda b: (ctx_blocks + b, 0)),
                  _ada_spec_at(layer, lambda b: 1 + b)] + weights + [
            _resident((DEC_SEQ, MLA_Q_ROPE_COLS)),
            _resident((DEC_SEQ, MLA_Q_ROPE_COLS)),
            _resident((DEC_SEQ, MLA_ROPE)),
            _resident((DEC_SEQ, MLA_ROPE)),
            pl.BlockSpec((None, PAST_LEN, MLA_KV_RANK), lambda b: (b, 0, 0)),
            pl.BlockSpec((None, PAST_LEN, MLA_ROPE), lambda b: (b, 0, 0)),
        ],
        out_specs=pl.BlockSpec((DEC_SEQ, D_MODEL), lambda b: (b, 0)),
        out_shape=jax.ShapeDtypeStruct((N_LAT, D_MODEL), F32),
        scratch_shapes=_mla_scratch(DEC_SEQ, DEC_SEQ + PAST_LEN),
        compiler_params=_params("arbitrary"),
        name=f"mla_latent_{layer}",
    )(x, ada, *weight_args, cos_q, sin_q, cos_k, sin_k, cache_ckv, cache_kpe)
    return ctx_out, lat_out, new_ckv, new_kpe


GQA_Q_COLS = GQA_HEADS * GQA_HEAD_DIM
GQA_KV_COLS = GQA_KV_HEADS * GQA_HEAD_DIM
GQA_REP = GQA_HEADS // GQA_KV_HEADS


def _head_rms(t, n_heads):
    cols = []
    for hd in range(n_heads):
        th = t[:, hd * GQA_HEAD_DIM:(hd + 1) * GQA_HEAD_DIM]
        r = lax.rsqrt(jnp.mean(th * th, axis=-1, keepdims=True) + EPS)
        cols.append(jnp.broadcast_to(r, th.shape))
    return jnp.concatenate(cols, axis=-1)


def _gqa_kernel(*refs, latent):
    if latent:
        (x_ref, ada_ref, g1_ref, wqkv_ref, gq_ref, gk_ref, wo_ref,
         gqs_ref, gks_ref, cosq_ref, sinq_ref, cosk_ref, sink_ref, ck_ref, cv_ref,
         o_ref, q_scr, k_scr, v_scr, o_scr) = refs
    else:
        (x_ref, ada_ref, g1_ref, wqkv_ref, gq_ref, gk_ref, wo_ref,
         o_ref, k_out_ref, v_out_ref, q_scr, k_scr, v_scr, o_scr) = refs
    tm = x_ref.shape[0]

    def project(rows):
        h = _modulate(x_ref[rows, :], g1_ref[...], ada_ref[0:1, :], ada_ref[1:2, :]).astype(BF16)
        qkv = _dot(h, wqkv_ref[...])
        q_raw = qkv[:, :GQA_Q_COLS]
        k_raw = qkv[:, GQA_Q_COLS:GQA_Q_COLS + GQA_KV_COLS]
        v = qkv[:, GQA_Q_COLS + GQA_KV_COLS:GQA_Q_COLS + 2 * GQA_KV_COLS]
        q_r = _head_rms(q_raw, GQA_HEADS)
        k_r = _head_rms(k_raw, GQA_KV_HEADS)
        q = q_raw * q_r * gq_ref[...]
        k = k_raw * k_r * gk_ref[...]
        if latent:
            lo = GQA_Q_COLS + 2 * GQA_KV_COLS
            q_swap = qkv[:, lo:lo + GQA_Q_COLS] * q_r * gqs_ref[...]
            k_swap = qkv[:, lo + GQA_Q_COLS:lo + GQA_Q_COLS + GQA_KV_COLS] * k_r * gks_ref[...]
            q = q * cosq_ref[rows, :] + q_swap * sinq_ref[rows, :]
            k = k * cosk_ref[rows, :] + k_swap * sink_ref[rows, :]
        else:
            k_out_ref[rows, :] = k
            v_out_ref[rows, :] = v
        q_scr[rows, :] = q.astype(BF16)
        k_scr[rows, :] = k.astype(BF16)
        v_scr[rows, :] = v.astype(BF16)

    _for_row_blocks(tm, project)

    if latent:
        k_scr[tm:, :] = ck_ref[...].astype(BF16)
        v_scr[tm:, :] = cv_ref[...].astype(BF16)

    def finish(rows):
        o_ref[rows, :] = x_ref[rows, :] + ada_ref[2:3, :] * _dot(o_scr[rows, :], wo_ref[...])

    _attention([(q_scr, k_scr, GQA_HEAD_DIM, False)], v_scr, GQA_HEADS,
               lambda hd: hd // GQA_REP, GQA_HEAD_DIM, GQA_HEAD_DIM ** -0.5, o_scr, finish)


def _gqa_scratch(tm, sk):
    return [
        pltpu.VMEM((tm, GQA_Q_COLS), BF16),
        pltpu.VMEM((sk, GQA_KV_COLS), BF16),
        pltpu.VMEM((sk, GQA_KV_COLS), BF16),
        pltpu.VMEM((tm, GQA_Q_COLS), BF16),
    ]


def _gqa_layer(x, ada, layer, g1, w_qkv_ctx, w_qkv_lat, g_q, g_k, g_q_swap, g_k_swap, w_o,
               cache_k, cache_v, rope):
    ctx_out, new_k, new_v = pl.pallas_call(
        functools.partial(_gqa_kernel, latent=False),
        grid=(BATCH,),
        in_specs=[pl.BlockSpec((SEQ, D_MODEL), lambda i: (i, 0)),
                  _ada_spec_at(layer, lambda i: 0),
                  _resident((1, D_MODEL)),
                  _resident(w_qkv_ctx.shape),
                  _resident((1, GQA_Q_COLS)),
                  _resident((1, GQA_KV_COLS)),
                  _resident((GQA_Q_COLS, D_MODEL))],
        out_specs=[
            pl.BlockSpec((SEQ, D_MODEL), lambda i: (i, 0)),
            pl.BlockSpec((SEQ, GQA_KV_COLS), lambda i: (i, 0)),
            pl.BlockSpec((SEQ, GQA_KV_COLS), lambda i: (i, 0)),
        ],
        out_shape=[
            jax.ShapeDtypeStruct((N_CTX, D_MODEL), F32),
            jax.ShapeDtypeStruct((N_CTX, GQA_KV_COLS), F32),
            jax.ShapeDtypeStruct((N_CTX, GQA_KV_COLS), F32),
        ],
        scratch_shapes=_gqa_scratch(SEQ, SEQ),
        compiler_params=_params("arbitrary"),
        name=f"gqa_ctx_{layer}",
    )(x, ada, g1, w_qkv_ctx, g_q, g_k, w_o)

    cos_q, sin_q, cos_k, sin_k = rope
    ctx_blocks = N_CTX // DEC_SEQ
    lat_out = pl.pallas_call(
        functools.partial(_gqa_kernel, latent=True),
        grid=(DEC_BATCH,),
        in_specs=[pl.BlockSpec((DEC_SEQ, D_MODEL), lambda b: (ctx_blocks + b, 0)),
                  _ada_spec_at(layer, lambda b: 1 + b),
                  _resident((1, D_MODEL)),
                  _resident(w_qkv_lat.shape),
                  _resident((1, GQA_Q_COLS)),
                  _resident((1, GQA_KV_COLS)),
                  _resident((GQA_Q_COLS, D_MODEL)),
                  _resident((1, GQA_Q_COLS)),
                  _resident((1, GQA_KV_COLS)),
                  _resident((DEC_SEQ, GQA_Q_COLS)),
                  _resident((DEC_SEQ, GQA_Q_COLS)),
                  _resident((DEC_SEQ, GQA_KV_COLS)),
                  _resident((DEC_SEQ, GQA_KV_COLS)),
                  pl.BlockSpec((None, PAST_LEN, GQA_KV_COLS), lambda b: (b, 0, 0)),
                  pl.BlockSpec((None, PAST_LEN, GQA_KV_COLS), lambda b: (b, 0, 0))],
        out_specs=pl.BlockSpec((DEC_SEQ, D_MODEL), lambda b: (b, 0)),
        out_shape=jax.ShapeDtypeStruct((N_LAT, D_MODEL), F32),
        scratch_shapes=_gqa_scratch(DEC_SEQ, DEC_SEQ + PAST_LEN),
        compiler_params=_params("arbitrary"),
        name=f"gqa_latent_{layer}",
    )(x, ada, g1, w_qkv_lat, g_q, g_k, w_o, g_q_swap, g_k_swap,
      cos_q, sin_q, cos_k, sin_k, cache_k, cache_v)
    return ctx_out, lat_out, new_k, new_v


ROUTE_TM = 512
MOE_TG = 256
MOE_MAX_TILES = 2 * N_TOK // MOE_TG + N_EXPERTS
MOE_ROWS = MOE_MAX_TILES * MOE_TG
MOE_FILLS = 2 * N_EXPERTS
MOE_FC = 1024
MOVE_TM = 256


def _route_kernel(x_ref, ada_ref, g2_ref, wr_ref, h_ref, dest_ref, gatew_ref, count_ref, carry_scr):
    tm = x_ref.shape[0]

    @pl.when(pl.program_id(0) == 0)
    def _():
        carry_scr[...] = jnp.zeros_like(carry_scr)

    h = _modulate(x_ref[...], g2_ref[...], ada_ref[3:4, :], ada_ref[4:5, :])
    h_ref[...] = h.reshape(tm, 1, D_MODEL)
    logits = jnp.dot(h, wr_ref[...], precision=lax.Precision.HIGHEST, preferred_element_type=F32)
    lane = lax.broadcasted_iota(jnp.int32, logits.shape, 1)
    neg = jnp.float32(-jnp.inf)
    l1 = jnp.where(lane < N_EXPERTS, logits, neg)
    m1 = jnp.max(l1, axis=-1, keepdims=True)
    i1 = jnp.min(jnp.where(l1 == m1, lane, LANES), axis=-1, keepdims=True)
    l2 = jnp.where(lane == i1, neg, l1)
    m2 = jnp.max(l2, axis=-1, keepdims=True)
    i2 = jnp.min(jnp.where(l2 == m2, lane, LANES), axis=-1, keepdims=True)
    e2 = jnp.exp(m2 - m1)
    w1 = 1.0 / (1.0 + e2)
    w2 = e2 / (1.0 + e2)
    member = jnp.where((lane == i1) | (lane == i2), 1.0, 0.0)
    r = lax.broadcasted_iota(jnp.int32, (tm, tm), 0)
    c = lax.broadcasted_iota(jnp.int32, (tm, tm), 1)
    tri = jnp.where(c < r, 1.0, 0.0).astype(BF16)
    before = _dot(tri, member.astype(BF16)) + carry_scr[0:1, :]
    rank1 = jnp.sum(jnp.where(lane == i1, before, 0.0), axis=-1, keepdims=True).astype(jnp.int32)
    rank2 = jnp.sum(jnp.where(lane == i2, before, 0.0), axis=-1, keepdims=True).astype(jnp.int32)
    dest_ref[...] = jnp.where(lane == 0, i1, jnp.where(lane == 1, i2, jnp.where(
        lane == 2, rank1, jnp.where(lane == 3, rank2, 0))))
    gatew_ref[...] = jnp.where(lane == 0, w1, jnp.where(lane == 1, w2, 0.0))
    carry_scr[...] = carry_scr[...] + jnp.sum(member, axis=0, keepdims=True)
    count_ref[...] = carry_scr[...]


def _route(x, ada, layer, g2, w_router):
    tm = ROUTE_TM
    return pl.pallas_call(
        _route_kernel,
        grid=(N_TOK // tm,),
        in_specs=[
            pl.BlockSpec((tm, D_MODEL), lambda i: (i, 0)),
            _ada_spec(layer, tm),
            _resident((1, D_MODEL)),
            _resident((D_MODEL, LANES)),
        ],
        out_specs=[
            pl.BlockSpec((tm, 1, D_MODEL), lambda i: (i, 0, 0)),
            pl.BlockSpec((tm, LANES), lambda i: (i, 0)),
            pl.BlockSpec((tm, LANES), lambda i: (i, 0)),
            pl.BlockSpec((8, LANES), lambda i: (0, 0)),
        ],
        out_shape=[
            jax.ShapeDtypeStruct((N_TOK, 1, D_MODEL), F32),
            jax.ShapeDtypeStruct((N_TOK, LANES), jnp.int32),
            jax.ShapeDtypeStruct((N_TOK, LANES), F32),
            jax.ShapeDtypeStruct((8, LANES), F32),
        ],
        scratch_shapes=[pltpu.VMEM((8, LANES), F32)],
        compiler_params=_params("arbitrary"),
        name=f"moe_route_{layer}",
    )(x, ada, g2, w_router)


def _dispatch_kernel(d1_ref, d2_ref, fill_start_ref, fill_rows_ref, h_ref, hs_ref, zero_scr, sem):
    i = pl.program_id(0)
    tm = h_ref.shape[0]
    base = i * tm

    def fill_copy(k):
        n = fill_rows_ref[k]
        return pltpu.make_async_copy(zero_scr.at[pl.ds(0, n)],
                                     hs_ref.at[pl.ds(fill_start_ref[k], n)], sem.at[1])

    @pl.when(i == 0)
    def _():
        zero_scr[...] = jnp.zeros_like(zero_scr)
        for k in range(MOE_FILLS):
            @pl.when(fill_rows_ref[k] > 0)
            def _():
                fill_copy(k).start()
        for k in range(MOE_FILLS):
            @pl.when(fill_rows_ref[k] > 0)
            def _():
                fill_copy(k).wait()

    def row_copy(r, d_ref):
        return pltpu.make_async_copy(h_ref.at[r], hs_ref.at[d_ref[base + r]], sem.at[0])

    def issue(r, carry):
        row_copy(r, d1_ref).start()
        row_copy(r, d2_ref).start()
        return carry

    def drain(r, carry):
        row_copy(r, d1_ref).wait()
        row_copy(r, d2_ref).wait()
        return carry

    lax.fori_loop(0, tm, issue, 0)
    lax.fori_loop(0, tm, drain, 0)


def _dispatch(d1, d2, fill_start, fill_rows, h_rows):
    tm = MOVE_TM
    return pl.pallas_call(
        _dispatch_kernel,
        grid_spec=pltpu.PrefetchScalarGridSpec(
            num_scalar_prefetch=4,
            grid=(N_TOK // tm,),
            in_specs=[pl.BlockSpec((tm, 1, D_MODEL), lambda i, *_: (i, 0, 0))],
            out_specs=pl.BlockSpec(memory_space=pl.ANY),
            scratch_shapes=[pltpu.VMEM((MOE_TG, 1, D_MODEL), F32),
                            pltpu.SemaphoreType.DMA((2,))],
        ),
        out_shape=jax.ShapeDtypeStruct((MOE_ROWS, 1, D_MODEL), F32),
        compiler_params=_params("arbitrary"),
        name="moe_dispatch",
    )(d1, d2, fill_start, fill_rows, h_rows)


def _experts_kernel(te_ref, nt_ref, hs_ref, wg_ref, wu_ref, wd_ref, ys_ref, h2d_scr):
    used = pl.program_id(0) < nt_ref[0]

    @pl.when(jnp.logical_not(used))
    def _():
        ys_ref[...] = jnp.zeros_like(ys_ref)

    @pl.when(used)
    def _():
        h2d_scr[...] = hs_ref[...].reshape(MOE_TG, D_MODEL)
        h = h2d_scr[...].astype(BF16)
        y = None
        for lo in range(0, MOE_FF, MOE_FC):
            g = _dot(h, wg_ref[:, lo:lo + MOE_FC])
            u = _dot(h, wu_ref[:, lo:lo + MOE_FC])
            part = _dot((_silu(g) * u).astype(BF16), wd_ref[lo:lo + MOE_FC, :])
            y = part if y is None else y + part
        ys_ref[...] = y.reshape(MOE_TG, 1, D_MODEL)


def _experts(tile_expert, n_tiles, hs, w_gate, w_up, w_down):
    rows = pl.BlockSpec((MOE_TG, 1, D_MODEL), lambda t, te, nt: (t, 0, 0))
    return pl.pallas_call(
        _experts_kernel,
        grid_spec=pltpu.PrefetchScalarGridSpec(
            num_scalar_prefetch=2,
            grid=(MOE_MAX_TILES,),
            in_specs=[
                rows,
                pl.BlockSpec((None, D_MODEL, MOE_FF), lambda t, te, nt: (te[t], 0, 0)),
                pl.BlockSpec((None, D_MODEL, MOE_FF), lambda t, te, nt: (te[t], 0, 0)),
                pl.BlockSpec((None, MOE_FF, D_MODEL), lambda t, te, nt: (te[t], 0, 0)),
            ],
            out_specs=rows,
            scratch_shapes=[pltpu.VMEM((MOE_TG, D_MODEL), F32)],
        ),
        out_shape=jax.ShapeDtypeStruct((MOE_ROWS, 1, D_MODEL), F32),
        compiler_params=_params("arbitrary"),
        name="moe_experts",
    )(tile_expert, n_tiles, hs, w_gate, w_up, w_down)


def _combine_kernel(d1_ref, d2_ref, x_ref, ada_ref, gw_ref, ys_ref, o_ref,
                    ya0, yb0, ya1, yb1, ya2d, yb2d, sem):
    i = pl.program_id(0)
    tm = x_ref.shape[0]

    def row_copies(step, r, buf_a, buf_b, slot):
        t = step * tm + r
        return (pltpu.make_async_copy(ys_ref.at[d1_ref[t]], buf_a.at[r], sem.at[slot]),
                pltpu.make_async_copy(ys_ref.at[d2_ref[t]], buf_b.at[r], sem.at[slot]))

    def start_tile(step, buf_a, buf_b, slot):
        def body(r, carry):
            for cp in row_copies(step, r, buf_a, buf_b, slot):
                cp.start()
            return carry
        lax.fori_loop(0, tm, body, 0)

    def wait_tile(step, buf_a, buf_b, slot):
        def body(r, carry):
            for cp in row_copies(step, r, buf_a, buf_b, slot):
                cp.wait()
            return carry
        lax.fori_loop(0, tm, body, 0)

    @pl.when(i == 0)
    def _():
        start_tile(0, ya0, yb0, 0)

    def run(cur, nxt):
        @pl.when(i + 1 < pl.num_programs(0))
        def _():
            start_tile(i + 1, *nxt)

        wait_tile(i, *cur)
        ya2d[...] = cur[0][...].reshape(tm, D_MODEL)
        yb2d[...] = cur[1][...].reshape(tm, D_MODEL)
        mix = gw_ref[:, 0:1] * ya2d[...] + gw_ref[:, 1:2] * yb2d[...]
        o_ref[...] = x_ref[...] + ada_ref[5:6, :] * mix

    @pl.when(i % 2 == 0)
    def _():
        run((ya0, yb0, 0), (ya1, yb1, 1))

    @pl.when(i % 2 == 1)
    def _():
        run((ya1, yb1, 1), (ya0, yb0, 0))


def _combine(d1, d2, x, ada, layer, gate_w, ys):
    tm = MOVE_TM
    row_buf = pltpu.VMEM((tm, 1, D_MODEL), F32)
    return pl.pallas_call(
        _combine_kernel,
        grid_spec=pltpu.PrefetchScalarGridSpec(
            num_scalar_prefetch=2,
            grid=(N_TOK // tm,),
            in_specs=[
                pl.BlockSpec((tm, D_MODEL), lambda i, *_: (i, 0)),
                pl.BlockSpec((None, None, 6, D_MODEL),
                             lambda i, *_: (layer, _cond_of_tile(i, tm), 0, 0)),
                pl.BlockSpec((tm, LANES), lambda i, *_: (i, 0)),
                pl.BlockSpec(memory_space=pl.ANY),
            ],
            out_specs=pl.BlockSpec((tm, D_MODEL), lambda i, *_: (i, 0)),
            scratch_shapes=[row_buf, row_buf, row_buf, row_buf,
                            pltpu.VMEM((tm, D_MODEL), F32), pltpu.VMEM((tm, D_MODEL), F32),
                            pltpu.SemaphoreType.DMA((2,))],
        ),
        out_shape=jax.ShapeDtypeStruct((N_TOK, D_MODEL), F32),
        compiler_params=_params("arbitrary"),
        name="moe_combine",
    )(d1, d2, x, ada, gate_w, ys)


def _moe_layer(x, ada, layer, g2, w_router, w_gate, w_up, w_down):
    h_rows, route, gate_w, counts = _route(x, ada, layer, g2, w_router)
    cnt = counts[0, :N_EXPERTS].astype(jnp.int32)
    tiles_per_expert = (cnt + MOE_TG - 1) // MOE_TG
    tile_end = jnp.cumsum(tiles_per_expert)
    row_start = (tile_end - tiles_per_expert) * MOE_TG
    n_tiles = tile_end[-1:]
    t = jnp.minimum(jnp.arange(MOE_MAX_TILES, dtype=jnp.int32), n_tiles - 1)
    tile_expert = jnp.sum((t[:, None] >= tile_end[None, :]).astype(jnp.int32), axis=1)
    experts = jnp.arange(N_EXPERTS, dtype=jnp.int32)
    d1 = jnp.sum(jnp.where(route[:, 0:1] == experts, row_start, 0), axis=1) + route[:, 2]
    d2 = jnp.sum(jnp.where(route[:, 1:2] == experts, row_start, 0), axis=1) + route[:, 3]
    spare_tile = jnp.minimum(n_tiles + experts, MOE_MAX_TILES - 1)
    fill_start = jnp.concatenate([row_start + cnt, spare_tile * MOE_TG])
    fill_rows = jnp.concatenate([tiles_per_expert * MOE_TG - cnt,
                                 jnp.where(n_tiles + experts < MOE_MAX_TILES, MOE_TG, 0)])
    hs = _dispatch(d1, d2, fill_start, fill_rows, h_rows)
    ys = _experts(tile_expert, n_tiles, hs, w_gate, w_up, w_down)
    return _combine(d1, d2, x, ada, layer, gate_w, ys)


def _final_kernel(x_ref, g_ref, o_ref):
    o_ref[...] = _rms(x_ref[...]) * g_ref[...]


def _final_norm(x, g, row0, rows):
    tm = 1024
    blk0 = row0 // tm
    return pl.pallas_call(
        _final_kernel,
        grid=(rows // tm,),
        in_specs=[pl.BlockSpec((tm, D_MODEL), lambda i: (blk0 + i, 0)),
                  _resident((1, D_MODEL))],
        out_specs=pl.BlockSpec((tm, D_MODEL), lambda i: (i, 0)),
        out_shape=jax.ShapeDtypeStruct((rows, D_MODEL), F32),
        compiler_params=_params("arbitrary"),
        name="final_norm",
    )(x, g)


def _rope_tables(dim):
    half = dim // 2
    quarter = half // 2
    t = np.arange(DEC_SEQ)
    pos = np.stack([t // GRID_W, t % GRID_W], axis=1).astype(np.float32)
    freqs = (ROPE_THETA ** (-np.arange(quarter, dtype=np.float32) / quarter)).astype(np.float32)
    ang = pos[:, :, None] * freqs[None, None, :]
    cos = np.cos(ang.astype(np.float64))
    sin = np.sin(ang.astype(np.float64))
    cos_t = np.concatenate([cos, cos], axis=-1).reshape(DEC_SEQ, dim)
    sin_t = np.concatenate([-sin, sin], axis=-1).reshape(DEC_SEQ, dim)
    lane = np.arange(dim)
    swap = np.where(lane % half < quarter, lane + quarter, lane - quarter)
    return cos_t.astype(np.float32), sin_t.astype(np.float32), swap


def kernel(x_prompt, x_sample, cache_mla_ckv, cache_mla_kpe, cache_gqa_k, cache_gqa_v, c, c_ctx, w_ada, b_ada, norm1_g, norm2_g, conv_w_in, conv_k, conv_w_out, mla_w_down, mla_q_norm_g, mla_kv_norm_g, mla_w_uq, mla_w_ukv, mla_w_o, gqa_w_qkv, gqa_q_norm_g, gqa_k_norm_g, gqa_w_o, ffn_w_gate, ffn_w_up, ffn_w_down, moe_w_router, moe_w_gate, moe_w_up, moe_w_down, final_norm_g):
    x = jnp.concatenate([x_prompt.reshape(N_CTX, D_MODEL), x_sample.reshape(N_LAT, D_MODEL)], axis=0)
    cond = jnp.concatenate(
        [c_ctx[None, :], c, jnp.zeros((N_COND - 1 - DEC_BATCH, D_MODEL), F32)], axis=0)
    ada = _adaln_table(cond, w_ada, b_ada)

    new_ckv, new_kpe, new_k, new_v = [], [], [], []
    for i in range(DEPTH):
        j = i // N_MIXERS
        kind = i % N_MIXERS
        g1 = norm1_g[i][None, :]
        if kind == 0:
            x = _conv_layer(x, ada, i, g1, conv_w_in[j].astype(BF16), conv_k[j],
                            conv_w_out[j].astype(BF16))
        elif kind == 1:
            cos64, sin64, swap64 = _rope_tables(MLA_ROPE)
            wd = mla_w_down[j]
            kpe_cols = wd[:, MLA_Q_RANK + MLA_KV_RANK:]
            w_down = jnp.concatenate([wd, kpe_cols[:, swap64]], axis=1).astype(BF16)
            wq = mla_w_uq[j].reshape(MLA_Q_RANK, MLA_HEADS, MLA_QK)
            wq_nope = wq[:, :, :MLA_NOPE].reshape(MLA_Q_RANK, MLA_Q_NOPE_COLS)
            wq_rope = wq[:, :, MLA_NOPE:]
            w_uq = jnp.concatenate(
                [wq_nope, wq_rope.reshape(MLA_Q_RANK, MLA_Q_ROPE_COLS),
                 wq_rope[:, :, swap64].reshape(MLA_Q_RANK, MLA_Q_ROPE_COLS)], axis=1).astype(BF16)
            wkv = mla_w_ukv[j].reshape(MLA_KV_RANK, MLA_HEADS, MLA_NOPE + MLA_V)
            w_ukv = jnp.concatenate(
                [wkv[:, :, :MLA_NOPE].reshape(MLA_KV_RANK, MLA_Q_NOPE_COLS),
                 wkv[:, :, MLA_NOPE:].reshape(MLA_KV_RANK, MLA_HEADS * MLA_V)], axis=1).astype(BF16)
            rope = (jnp.asarray(np.tile(cos64, (1, MLA_HEADS))), jnp.asarray(np.tile(sin64, (1, MLA_HEADS))),
                    jnp.asarray(cos64), jnp.asarray(sin64))
            xc, xl, ckv_p, kpe_p = _mla_layer(
                x, ada, i, g1, w_down, mla_q_norm_g[j][None, :], mla_kv_norm_g[j][None, :],
                w_uq, w_ukv, mla_w_o[j].astype(BF16), cache_mla_ckv[:, j], cache_mla_kpe[:, j], rope)
            x = jnp.concatenate([xc, xl], axis=0)
            new_ckv.append(ckv_p.reshape(BATCH, SEQ, MLA_KV_RANK))
            new_kpe.append(kpe_p.reshape(BATCH, SEQ, MLA_ROPE))
        else:
            cos128, sin128, swap128 = _rope_tables(GQA_HEAD_DIM)
            wqkv = gqa_w_qkv[j]
            wq = wqkv[:, :GQA_Q_COLS].reshape(D_MODEL, GQA_HEADS, GQA_HEAD_DIM)
            wk = wqkv[:, GQA_Q_COLS:GQA_Q_COLS + GQA_KV_COLS].reshape(D_MODEL, GQA_KV_HEADS, GQA_HEAD_DIM)
            w_qkv_ctx = wqkv.astype(BF16)
            w_qkv_lat = jnp.concatenate(
                [wqkv, wq[:, :, swap128].reshape(D_MODEL, GQA_Q_COLS),
                 wk[:, :, swap128].reshape(D_MODEL, GQA_KV_COLS)], axis=1).astype(BF16)
            gq = gqa_q_norm_g[j]
            gk = gqa_k_norm_g[j]
            rope = (jnp.asarray(np.tile(cos128, (1, GQA_HEADS))), jnp.asarray(np.tile(sin128, (1, GQA_HEADS))),
                    jnp.asarray(np.tile(cos128, (1, GQA_KV_HEADS))), jnp.asarray(np.tile(sin128, (1, GQA_KV_HEADS))))
            xc, xl, k_p, v_p = _gqa_layer(
                x, ada, i, g1, w_qkv_ctx, w_qkv_lat,
                jnp.tile(gq, GQA_HEADS)[None, :], jnp.tile(gk, GQA_KV_HEADS)[None, :],
                jnp.tile(gq[swap128], GQA_HEADS)[None, :], jnp.tile(gk[swap128], GQA_KV_HEADS)[None, :],
                gqa_w_o[j].astype(BF16),
                cache_gqa_k[:, j].reshape(DEC_BATCH, PAST_LEN, GQA_KV_COLS),
                cache_gqa_v[:, j].reshape(DEC_BATCH, PAST_LEN, GQA_KV_COLS), rope)
            x = jnp.concatenate([xc, xl], axis=0)
            new_k.append(k_p.reshape(BATCH, SEQ, GQA_KV_HEADS, GQA_HEAD_DIM))
            new_v.append(v_p.reshape(BATCH, SEQ, GQA_KV_HEADS, GQA_HEAD_DIM))

        m = i // 2
        g2 = norm2_g[i][None, :]
        if i % 2 == 0:
            x = _ffn_layer(x, ada, i, g2, ffn_w_gate[m].astype(BF16), ffn_w_up[m].astype(BF16),
                           ffn_w_down[m].astype(BF16))
        else:
            w_router = jnp.pad(moe_w_router[m], ((0, 0), (0, LANES - N_EXPERTS)))
            x = _moe_layer(x, ada, i, g2, w_router, moe_w_gate[m].astype(BF16),
                           moe_w_up[m].astype(BF16), moe_w_down[m].astype(BF16))

    g_final = final_norm_g[None, :]
    y_prompt = _final_norm(x, g_final, 0, N_CTX).reshape(BATCH, SEQ, D_MODEL)
    y_sample = _final_norm(x, g_final, N_CTX, N_LAT).reshape(DEC_BATCH, DEC_SEQ, D_MODEL)
    return (y_prompt, y_sample,
            jnp.stack(new_ckv, axis=1), jnp.stack(new_kpe, axis=1),
            jnp.stack(new_k, axis=1), jnp.stack(new_v, axis=1))
```

```python
import functools

import numpy as np
import jax
import jax.numpy as jnp
from jax import lax
from jax.experimental import pallas as pl
from jax.experimental.pallas import tpu as pltpu

D_MODEL = 1024
BATCH = 32
SEQ = 256
DEPTH = 4
DEC_BATCH = 2
DEC_SEQ = 1024
PAST_LEN = 512
GRID_W = 64
N_MIXERS = 3
CONV_WIDTH = 3
MLA_HEADS = 8
MLA_NOPE = 128
MLA_ROPE = 64
MLA_V = 128
MLA_Q_RANK = 384
MLA_KV_RANK = 256
GQA_HEADS = 8
GQA_KV_HEADS = 2
GQA_HEAD_DIM = 128
D_FF = 2816
N_EXPERTS = 8
MOE_FF = 2048
ROPE_THETA = 10000.0
EPS = 1e-6

N_CTX = BATCH * SEQ
N_LAT = DEC_BATCH * DEC_SEQ
N_TOK = N_CTX + N_LAT
N_COND = 8
LANES = 128
Q_TILE = 256
VMEM_LIMIT = 56 * 1024 * 1024

F32 = jnp.float32
BF16 = jnp.bfloat16


def _dot(a, b):
    return jnp.dot(a, b, preferred_element_type=F32)


def _dot_t(a, b):
    return lax.dot_general(a, b, (((1,), (1,)), ((), ())), preferred_element_type=F32)


def _rms(x):
    return x * lax.rsqrt(jnp.mean(x * x, axis=-1, keepdims=True) + EPS)


def _modulate(x, g, shift, scale):
    return _rms(x) * g * (1.0 + scale) + shift


def _silu(x):
    return x * jax.nn.sigmoid(x)


def _cond_of_tile(i, tm):
    start = i * tm
    return jnp.where(start < N_CTX, 0, 1 + (start - N_CTX) // DEC_SEQ)


def _resident(shape):
    return pl.BlockSpec(shape, lambda *_: (0,) * len(shape), pipeline_mode=pl.Buffered(1))


def _resident_at(index, shape):
    return pl.BlockSpec((None,) + tuple(shape), lambda *_: (index,) + (0,) * len(shape),
                        pipeline_mode=pl.Buffered(1))


def _params(*sem):
    return pltpu.CompilerParams(dimension_semantics=sem, vmem_limit_bytes=VMEM_LIMIT)


def _adaln_kernel(cond_ref, w_ref, b_ref, o_ref):
    s = _silu(cond_ref[...]).astype(BF16)
    o_ref[...] = _dot(s, w_ref[...].astype(BF16)) + b_ref[...]


def _adaln_table(cond, w_ada, b_ada):
    tn = 1536
    out = pl.pallas_call(
        _adaln_kernel,
        grid=(DEPTH, 6 * D_MODEL // tn),
        in_specs=[
            pl.BlockSpec((N_COND, D_MODEL), lambda l, j: (0, 0)),
            pl.BlockSpec((None, D_MODEL, tn), lambda l, j: (l, 0, j)),
            pl.BlockSpec((None, 1, tn), lambda l, j: (l, 0, j)),
        ],
        out_specs=pl.BlockSpec((None, N_COND, tn), lambda l, j: (l, 0, j)),
        out_shape=jax.ShapeDtypeStruct((DEPTH, N_COND, 6 * D_MODEL), F32),
        compiler_params=_params("arbitrary", "arbitrary"),
        name="adaln_table",
    )(cond, w_ada, b_ada.reshape(DEPTH, 1, 6 * D_MODEL))
    return out.reshape(DEPTH, N_COND, 6, D_MODEL)


def _ada_spec(layer, tm):
    return pl.BlockSpec((None, None, 6, D_MODEL), lambda i: (layer, _cond_of_tile(i, tm), 0, 0))


def _ada_spec_at(layer, cond_fn):
    return pl.BlockSpec((None, None, 6, D_MODEL), lambda i: (layer, cond_fn(i), 0, 0))


CONV_TM = 1024
CONV_CC = 256


def _conv_kernel(*refs, split):
    if split:
        xp_ref, x_ref, ada_ref, g1_ref, win_ref, ck_ref, wout_ref, o_ref, v_scr = refs
    else:
        x_ref, ada_ref, g1_ref, win_ref, ck_ref, wout_ref, o_ref, v_scr = refs
    tm = x_ref.shape[0]
    i = pl.program_id(0)
    x = x_ref[...]
    if split:
        x = jnp.where(i * tm < N_CTX, xp_ref[...], x)
    h = _modulate(x, g1_ref[...], ada_ref[0:1, :], ada_ref[1:2, :]).astype(BF16)
    period = jnp.where(i * tm < N_CTX, SEQ, DEC_SEQ)
    pos = lax.broadcasted_iota(jnp.int32, (tm, 1), 0) & (period - 1)
    first = pos == 0
    last = pos == period - 1
    for j in range(D_MODEL // CONV_CC):
        lo = j * CONV_CC
        b_gate = _dot(h, win_ref[:, lo:lo + CONV_CC])
        c_gate = _dot(h, win_ref[:, D_MODEL + lo:D_MODEL + lo + CONV_CC])
        x_in = _dot(h, win_ref[:, 2 * D_MODEL + lo:2 * D_MODEL + lo + CONV_CC])
        u = c_gate * x_in
        u_prev = jnp.where(first, 0.0, pltpu.roll(u, 1, 0))
        u_next = jnp.where(last, 0.0, pltpu.roll(u, tm - 1, 0))
        conv = (ck_ref[0:1, lo:lo + CONV_CC] * u_prev + ck_ref[1:2, lo:lo + CONV_CC] * u
                + ck_ref[2:3, lo:lo + CONV_CC] * u_next)
        v_scr[:, lo:lo + CONV_CC] = (b_gate * conv).astype(BF16)
    o_ref[...] = x + ada_ref[2:3, :] * _dot(v_scr[...], wout_ref[...])


def _conv_layer(xs, ada, layer, j, g1, w_in, conv_k, w_out):
    tm = CONV_TM
    ctx_tiles = N_CTX // tm
    if len(xs) == 1:
        x_specs = [pl.BlockSpec((tm, D_MODEL), lambda i: (i, 0))]
    else:
        x_specs = [pl.BlockSpec((tm, D_MODEL), lambda i: (jnp.minimum(i, ctx_tiles - 1), 0)),
                   pl.BlockSpec((tm, D_MODEL), lambda i: (jnp.maximum(i - ctx_tiles, 0), 0))]
    return pl.pallas_call(
        functools.partial(_conv_kernel, split=len(xs) == 2),
        grid=(N_TOK // tm,),
        in_specs=x_specs + [
            _ada_spec(layer, tm),
            _resident((1, D_MODEL)),
            _resident_at(j, (D_MODEL, 3 * D_MODEL)),
            _resident_at(j, (CONV_WIDTH, D_MODEL)),
            _resident_at(j, (D_MODEL, D_MODEL)),
        ],
        out_specs=pl.BlockSpec((tm, D_MODEL), lambda i: (i, 0)),
        out_shape=jax.ShapeDtypeStruct((N_TOK, D_MODEL), F32),
        scratch_shapes=[pltpu.VMEM((tm, D_MODEL), BF16)],
        compiler_params=_params("arbitrary"),
        name=f"conv_mixer_{layer}",
    )(*xs, ada, g1, w_in, conv_k, w_out)


FFN_TM = 512
FFN_CHUNKS = ((0, 1536), (1536, D_FF))


def _ffn_kernel(x_ref, ada_ref, g2_ref, wg_ref, wu_ref, wd_ref, o_ref):
    x = x_ref[...]
    h = _modulate(x, g2_ref[...], ada_ref[3:4, :], ada_ref[4:5, :]).astype(BF16)
    f = None
    for lo, hi in FFN_CHUNKS:
        g = _dot(h, wg_ref[:, lo:hi])
        u = _dot(h, wu_ref[:, lo:hi])
        part = _dot((_silu(g) * u).astype(BF16), wd_ref[lo:hi, :])
        f = part if f is None else f + part
    o_ref[...] = x + ada_ref[5:6, :] * f


def _ffn_layer(x, ada, layer, m, g2, w_gate, w_up, w_down):
    tm = FFN_TM
    return pl.pallas_call(
        _ffn_kernel,
        grid=(N_TOK // tm,),
        in_specs=[
            pl.BlockSpec((tm, D_MODEL), lambda i: (i, 0)),
            _ada_spec(layer, tm),
            _resident((1, D_MODEL)),
            _resident_at(m, (D_MODEL, D_FF)),
            _resident_at(m, (D_MODEL, D_FF)),
            _resident_at(m, (D_FF, D_MODEL)),
        ],
        out_specs=pl.BlockSpec((tm, D_MODEL), lambda i: (i, 0)),
        out_shape=jax.ShapeDtypeStruct((N_TOK, D_MODEL), F32),
        compiler_params=_params("arbitrary"),
        name=f"dense_ffn_{layer}",
    )(x, ada, g2, w_gate, w_up, w_down)


def _attention(qk_parts, v_ref, heads, kv_of_head, dv, scale, o_scr, finish):
    def q_block(rows):
        for h in range(heads):
            g = kv_of_head(h)
            s = None
            for q_ref, k_ref, width, k_shared in qk_parts:
                kb = 0 if k_shared else g
                part = _dot_t(q_ref[rows, h * width:(h + 1) * width],
                              k_ref[:, kb * width:(kb + 1) * width])
                s = part if s is None else s + part
            s = s * scale
            e = jnp.exp(s - jnp.max(s, axis=-1, keepdims=True))
            denom = jnp.sum(e, axis=-1, keepdims=True)
            o = _dot(e.astype(BF16), v_ref[:, g * dv:(g + 1) * dv]) / denom
            o_scr[rows, h * dv:(h + 1) * dv] = o.astype(BF16)
        finish(rows)

    _for_row_blocks(o_scr.shape[0], q_block)


def _for_row_blocks(n_rows, fn):
    if n_rows == Q_TILE:
        fn(pl.ds(0, Q_TILE))
    else:
        def body(b, carry):
            fn(pl.ds(pl.multiple_of(b * Q_TILE, Q_TILE), Q_TILE))
            return carry
        lax.fori_loop(0, n_rows // Q_TILE, body, 0)


MLA_QK = MLA_NOPE + MLA_ROPE
MLA_DOWN_COLS = MLA_Q_RANK + MLA_KV_RANK + 2 * MLA_ROPE
MLA_Q_NOPE_COLS = MLA_HEADS * MLA_NOPE
MLA_Q_ROPE_COLS = MLA_HEADS * MLA_ROPE


def _ctx_then_latent(body, n_in, refs):
    ins, lat_ref, rest = refs[:n_in], refs[n_in], refs[n_in + 1:]
    o_ref = rest[0]
    i = pl.program_id(0)

    @pl.when(i < BATCH)
    def _():
        body(*ins, *rest, latent=False)

    @pl.when(i >= BATCH)
    def _():
        o_ref[...] = lat_ref[...]


def _ctx_specs(n_out_cols):
    seq = lambda i: (jnp.minimum(i, BATCH - 1), 0)
    x_spec = pl.BlockSpec((SEQ, D_MODEL), seq)
    lat_spec = pl.BlockSpec((SEQ, D_MODEL), lambda i: (jnp.maximum(i - BATCH, 0), 0))
    out_specs = [pl.BlockSpec((SEQ, D_MODEL), lambda i: (i, 0))]
    out_specs += [pl.BlockSpec((SEQ, n), seq) for n in n_out_cols]
    return x_spec, lat_spec, out_specs


CTX_STEPS = BATCH + N_LAT // SEQ


def _mla_kernel(*refs, latent):
    if latent:
        _mla_body(*refs, latent=True)
    else:
        _ctx_then_latent(_mla_body, 9, refs)


def _mla_body(*refs, latent):
    if latent:
        (x_ref, ada_ref, g1_ref, wdown_ref, gq_ref, gkv_ref, wuq_ref, wukv_ref, wo_ref,
         cosq_ref, sinq_ref, cosk_ref, sink_ref, cckv_ref, ckpe_ref,
         o_ref, qn_scr, qr_scr, kn_scr, kr_scr, v_scr, o_scr) = refs
    else:
        (x_ref, ada_ref, g1_ref, wdown_ref, gq_ref, gkv_ref, wuq_ref, wukv_ref, wo_ref,
         o_ref, ckv_out_ref, kpe_out_ref, qn_scr, qr_scr, kn_scr, kr_scr, v_scr, o_scr) = refs
    tm = x_ref.shape[0]
    kpe_lo = MLA_Q_RANK + MLA_KV_RANK

    def project(rows):
        h = _modulate(x_ref[rows, :], g1_ref[...], ada_ref[0:1, :], ada_ref[1:2, :]).astype(BF16)
        down = _dot(h, wdown_ref[...])
        c_q = down[:, :MLA_Q_RANK]
        c_kv = down[:, MLA_Q_RANK:kpe_lo]
        kpe = down[:, kpe_lo:kpe_lo + MLA_ROPE]
        q = _dot((_rms(c_q) * gq_ref[...]).astype(BF16), wuq_ref[...])
        ckv_n = _rms(c_kv) * gkv_ref[...]
        kv = _dot(ckv_n.astype(BF16), wukv_ref[...])
        qn_scr[rows, :] = q[:, :MLA_Q_NOPE_COLS].astype(BF16)
        q_rope = q[:, MLA_Q_NOPE_COLS:MLA_Q_NOPE_COLS + MLA_Q_ROPE_COLS]
        kn_scr[rows, :] = kv[:, :MLA_Q_NOPE_COLS].astype(BF16)
        v_scr[rows, :] = kv[:, MLA_Q_NOPE_COLS:].astype(BF16)
        if latent:
            q_swap = q[:, MLA_Q_NOPE_COLS + MLA_Q_ROPE_COLS:]
            q_rope = q_rope * cosq_ref[rows, :] + q_swap * sinq_ref[rows, :]
            kpe_swap = down[:, kpe_lo + MLA_ROPE:kpe_lo + 2 * MLA_ROPE]
            kpe = kpe * cosk_ref[rows, :] + kpe_swap * sink_ref[rows, :]
        else:
            ckv_out_ref[rows, :] = ckv_n
            kpe_out_ref[rows, :] = kpe
        qr_scr[rows, :] = q_rope.astype(BF16)
        kr_scr[rows, :] = kpe.astype(BF16)

    _for_row_blocks(tm, project)

    if latent:
        def expand_cache(rows):
            kv_c = _dot(cckv_ref[rows, :].astype(BF16), wukv_ref[...])
            dst = pl.ds(pl.multiple_of(tm + rows.start, Q_TILE), Q_TILE)
            kn_scr[dst, :] = kv_c[:, :MLA_Q_NOPE_COLS].astype(BF16)
            v_scr[dst, :] = kv_c[:, MLA_Q_NOPE_COLS:].astype(BF16)
            kr_scr[dst, :] = ckpe_ref[rows, :].astype(BF16)

        _for_row_blocks(PAST_LEN, expand_cache)

    def finish(rows):
        o_ref[rows, :] = x_ref[rows, :] + ada_ref[2:3, :] * _dot(o_scr[rows, :], wo_ref[...])

    _attention([(qn_scr, kn_scr, MLA_NOPE, False), (qr_scr, kr_scr, MLA_ROPE, True)],
               v_scr, MLA_HEADS, lambda hd: hd, MLA_V, MLA_QK ** -0.5, o_scr, finish)


def _mla_scratch(tm, sk):
    return [
        pltpu.VMEM((tm, MLA_Q_NOPE_COLS), BF16),
        pltpu.VMEM((tm, MLA_Q_ROPE_COLS), BF16),
        pltpu.VMEM((sk, MLA_Q_NOPE_COLS), BF16),
        pltpu.VMEM((sk, MLA_ROPE), BF16),
        pltpu.VMEM((sk, MLA_HEADS * MLA_V), BF16),
        pltpu.VMEM((tm, MLA_HEADS * MLA_V), BF16),
    ]


def _mla_layer(x, ada, layer, g1, w_down, g_q, g_kv, w_uq, w_ukv, w_o,
               cache_ckv, cache_kpe, rope):
    n_uq = w_uq.shape[1]
    weights = [
        _resident((1, D_MODEL)),
        _resident((D_MODEL, MLA_DOWN_COLS)),
        _resident((1, MLA_Q_RANK)),
        _resident((1, MLA_KV_RANK)),
        _resident((MLA_Q_RANK, n_uq)),
        _resident((MLA_KV_RANK, 2 * MLA_Q_NOPE_COLS)),
        _resident((MLA_HEADS * MLA_V, D_MODEL)),
    ]
    weight_args = (g1, w_down, g_q, g_kv, w_uq, w_ukv, w_o)
    cos_q, sin_q, cos_k, sin_k = rope
    ctx_blocks = N_CTX // DEC_SEQ
    lat_out = pl.pallas_call(
        functools.partial(_mla_kernel, latent=True),
        grid=(DEC_BATCH,),
        in_specs=[pl.BlockSpec((DEC_SEQ, D_MODEL), lambda b: (ctx_blocks + b, 0)),
                  _ada_spec_at(layer, lambda b: 1 + b)] + weights + [
            _resident((DEC_SEQ, MLA_Q_ROPE_COLS)),
            _resident((DEC_SEQ, MLA_Q_ROPE_COLS)),
            _resident((DEC_SEQ, MLA_ROPE)),
            _resident((DEC_SEQ, MLA_ROPE)),
            pl.BlockSpec((None, PAST_LEN, MLA_KV_RANK), lambda b: (b, 0, 0)),
            pl.BlockSpec((None, PAST_LEN, MLA_ROPE), lambda b: (b, 0, 0)),
        ],
        out_specs=pl.BlockSpec((DEC_SEQ, D_MODEL), lambda b: (b, 0)),
        out_shape=jax.ShapeDtypeStruct((N_LAT, D_MODEL), F32),
        scratch_shapes=_mla_scratch(DEC_SEQ, DEC_SEQ + PAST_LEN),
        compiler_params=_params("arbitrary"),
        name=f"mla_latent_{layer}",
    )(x, ada, *weight_args, cos_q, sin_q, cos_k, sin_k, cache_ckv, cache_kpe)

    x_spec, lat_spec, out_specs = _ctx_specs((MLA_KV_RANK, MLA_ROPE))
    return pl.pallas_call(
        functools.partial(_mla_kernel, latent=False),
        grid=(CTX_STEPS,),
        in_specs=[x_spec, _ada_spec_at(layer, lambda i: 0)] + weights + [lat_spec],
        out_specs=out_specs,
        out_shape=[
            jax.ShapeDtypeStruct((N_TOK, D_MODEL), F32),
            jax.ShapeDtypeStruct((N_CTX, MLA_KV_RANK), F32),
            jax.ShapeDtypeStruct((N_CTX, MLA_ROPE), F32),
        ],
        scratch_shapes=_mla_scratch(SEQ, SEQ),
        compiler_params=_params("arbitrary"),
        name=f"mla_ctx_{layer}",
    )(x, ada, *weight_args, lat_out)


GQA_Q_COLS = GQA_HEADS * GQA_HEAD_DIM
GQA_KV_COLS = GQA_KV_HEADS * GQA_HEAD_DIM
GQA_REP = GQA_HEADS // GQA_KV_HEADS


def _head_rms(t, n_heads):
    cols = []
    for hd in range(n_heads):
        th = t[:, hd * GQA_HEAD_DIM:(hd + 1) * GQA_HEAD_DIM]
        r = lax.rsqrt(jnp.mean(th * th, axis=-1, keepdims=True) + EPS)
        cols.append(jnp.broadcast_to(r, th.shape))
    return jnp.concatenate(cols, axis=-1)


def _gqa_kernel(*refs, latent):
    if latent:
        _gqa_body(*refs, latent=True)
    else:
        _ctx_then_latent(_gqa_body, 7, refs)


def _gqa_body(*refs, latent):
    if latent:
        (x_ref, ada_ref, g1_ref, wqkv_ref, gq_ref, gk_ref, wo_ref,
         gqs_ref, gks_ref, cosq_ref, sinq_ref, cosk_ref, sink_ref, ck_ref, cv_ref,
         o_ref, q_scr, k_scr, v_scr, o_scr) = refs
    else:
        (x_ref, ada_ref, g1_ref, wqkv_ref, gq_ref, gk_ref, wo_ref,
         o_ref, k_out_ref, v_out_ref, q_scr, k_scr, v_scr, o_scr) = refs
    tm = x_ref.shape[0]

    def project(rows):
        h = _modulate(x_ref[rows, :], g1_ref[...], ada_ref[0:1, :], ada_ref[1:2, :]).astype(BF16)
        qkv = _dot(h, wqkv_ref[...])
        q_raw = qkv[:, :GQA_Q_COLS]
        k_raw = qkv[:, GQA_Q_COLS:GQA_Q_COLS + GQA_KV_COLS]
        v = qkv[:, GQA_Q_COLS + GQA_KV_COLS:GQA_Q_COLS + 2 * GQA_KV_COLS]
        q_r = _head_rms(q_raw, GQA_HEADS)
        k_r = _head_rms(k_raw, GQA_KV_HEADS)
        q = q_raw * q_r * gq_ref[...]
        k = k_raw * k_r * gk_ref[...]
        if latent:
            lo = GQA_Q_COLS + 2 * GQA_KV_COLS
            q_swap = qkv[:, lo:lo + GQA_Q_COLS] * q_r * gqs_ref[...]
            k_swap = qkv[:, lo + GQA_Q_COLS:lo + GQA_Q_COLS + GQA_KV_COLS] * k_r * gks_ref[...]
            q = q * cosq_ref[rows, :] + q_swap * sinq_ref[rows, :]
            k = k * cosk_ref[rows, :] + k_swap * sink_ref[rows, :]
        else:
            k_out_ref[rows, :] = k
            v_out_ref[rows, :] = v
        q_scr[rows, :] = q.astype(BF16)
        k_scr[rows, :] = k.astype(BF16)
        v_scr[rows, :] = v.astype(BF16)

    _for_row_blocks(tm, project)

    if latent:
        k_scr[tm:, :] = ck_ref[...].astype(BF16)
        v_scr[tm:, :] = cv_ref[...].astype(BF16)

    def finish(rows):
        o_ref[rows, :] = x_ref[rows, :] + ada_ref[2:3, :] * _dot(o_scr[rows, :], wo_ref[...])

    _attention([(q_scr, k_scr, GQA_HEAD_DIM, False)], v_scr, GQA_HEADS,
               lambda hd: hd // GQA_REP, GQA_HEAD_DIM, GQA_HEAD_DIM ** -0.5, o_scr, finish)


def _gqa_scratch(tm, sk):
    return [
        pltpu.VMEM((tm, GQA_Q_COLS), BF16),
        pltpu.VMEM((sk, GQA_KV_COLS), BF16),
        pltpu.VMEM((sk, GQA_KV_COLS), BF16),
        pltpu.VMEM((tm, GQA_Q_COLS), BF16),
    ]


def _gqa_layer(x, ada, layer, g1, w_qkv_ctx, w_qkv_lat, g_q, g_k, g_q_swap, g_k_swap, w_o,
               cache_k, cache_v, rope):
    cos_q, sin_q, cos_k, sin_k = rope
    ctx_blocks = N_CTX // DEC_SEQ
    lat_out = pl.pallas_call(
        functools.partial(_gqa_kernel, latent=True),
        grid=(DEC_BATCH,),
        in_specs=[pl.BlockSpec((DEC_SEQ, D_MODEL), lambda b: (ctx_blocks + b, 0)),
                  _ada_spec_at(layer, lambda b: 1 + b),
                  _resident((1, D_MODEL)),
                  _resident(w_qkv_lat.shape),
                  _resident((1, GQA_Q_COLS)),
                  _resident((1, GQA_KV_COLS)),
                  _resident((GQA_Q_COLS, D_MODEL)),
                  _resident((1, GQA_Q_COLS)),
                  _resident((1, GQA_KV_COLS)),
                  _resident((DEC_SEQ, GQA_Q_COLS)),
                  _resident((DEC_SEQ, GQA_Q_COLS)),
                  _resident((DEC_SEQ, GQA_KV_COLS)),
                  _resident((DEC_SEQ, GQA_KV_COLS)),
                  pl.BlockSpec((None, PAST_LEN, GQA_KV_COLS), lambda b: (b, 0, 0)),
                  pl.BlockSpec((None, PAST_LEN, GQA_KV_COLS), lambda b: (b, 0, 0))],
        out_specs=pl.BlockSpec((DEC_SEQ, D_MODEL), lambda b: (b, 0)),
        out_shape=jax.ShapeDtypeStruct((N_LAT, D_MODEL), F32),
        scratch_shapes=_gqa_scratch(DEC_SEQ, DEC_SEQ + PAST_LEN),
        compiler_params=_params("arbitrary"),
        name=f"gqa_latent_{layer}",
    )(x, ada, g1, w_qkv_lat, g_q, g_k, w_o, g_q_swap, g_k_swap,
      cos_q, sin_q, cos_k, sin_k, cache_k, cache_v)

    x_spec, lat_spec, out_specs = _ctx_specs((GQA_KV_COLS, GQA_KV_COLS))
    return pl.pallas_call(
        functools.partial(_gqa_kernel, latent=False),
        grid=(CTX_STEPS,),
        in_specs=[x_spec,
                  _ada_spec_at(layer, lambda i: 0),
                  _resident((1, D_MODEL)),
                  _resident(w_qkv_ctx.shape),
                  _resident((1, GQA_Q_COLS)),
                  _resident((1, GQA_KV_COLS)),
                  _resident((GQA_Q_COLS, D_MODEL)),
                  lat_spec],
        out_specs=out_specs,
        out_shape=[
            jax.ShapeDtypeStruct((N_TOK, D_MODEL), F32),
            jax.ShapeDtypeStruct((N_CTX, GQA_KV_COLS), F32),
            jax.ShapeDtypeStruct((N_CTX, GQA_KV_COLS), F32),
        ],
        scratch_shapes=_gqa_scratch(SEQ, SEQ),
        compiler_params=_params("arbitrary"),
        name=f"gqa_ctx_{layer}",
    )(x, ada, g1, w_qkv_ctx, g_q, g_k, w_o, lat_out)


ROUTE_TM = 512
MOE_TG = 256
MOE_MAX_TILES = 2 * N_TOK // MOE_TG + N_EXPERTS
MOE_ROWS = MOE_MAX_TILES * MOE_TG
MOE_FILLS = 2 * N_EXPERTS
MOE_FC = 1024
DISPATCH_TM = 1024
COMBINE_TM = 256


def _route_kernel(x_ref, ada_ref, g2_ref, wr_ref, h_ref, dest_ref, gatew_ref, count_ref, carry_scr):
    tm = x_ref.shape[0]

    @pl.when(pl.program_id(0) == 0)
    def _():
        carry_scr[...] = jnp.zeros_like(carry_scr)

    h = _modulate(x_ref[...], g2_ref[...], ada_ref[3:4, :], ada_ref[4:5, :])
    h_ref[...] = h.reshape(tm, 1, D_MODEL)
    logits = jnp.dot(h, wr_ref[...], precision=lax.Precision.HIGHEST, preferred_element_type=F32)
    lane = lax.broadcasted_iota(jnp.int32, logits.shape, 1)
    neg = jnp.float32(-jnp.inf)
    l1 = jnp.where(lane < N_EXPERTS, logits, neg)
    m1 = jnp.max(l1, axis=-1, keepdims=True)
    i1 = jnp.min(jnp.where(l1 == m1, lane, LANES), axis=-1, keepdims=True)
    l2 = jnp.where(lane == i1, neg, l1)
    m2 = jnp.max(l2, axis=-1, keepdims=True)
    i2 = jnp.min(jnp.where(l2 == m2, lane, LANES), axis=-1, keepdims=True)
    e2 = jnp.exp(m2 - m1)
    w1 = 1.0 / (1.0 + e2)
    w2 = e2 / (1.0 + e2)
    member = jnp.where((lane == i1) | (lane == i2), 1.0, 0.0)
    r = lax.broadcasted_iota(jnp.int32, (tm, tm), 0)
    c = lax.broadcasted_iota(jnp.int32, (tm, tm), 1)
    tri = jnp.where(c < r, 1.0, 0.0).astype(BF16)
    before = _dot(tri, member.astype(BF16)) + carry_scr[0:1, :]
    rank1 = jnp.sum(jnp.where(lane == i1, before, 0.0), axis=-1, keepdims=True).astype(jnp.int32)
    rank2 = jnp.sum(jnp.where(lane == i2, before, 0.0), axis=-1, keepdims=True).astype(jnp.int32)
    dest_ref[...] = jnp.where(lane == 0, i1, jnp.where(lane == 1, i2, jnp.where(
        lane == 2, rank1, jnp.where(lane == 3, rank2, 0))))
    gatew_ref[...] = jnp.where(lane == 0, w1, jnp.where(lane == 1, w2, 0.0))
    carry_scr[...] = carry_scr[...] + jnp.sum(member, axis=0, keepdims=True)
    count_ref[...] = carry_scr[...]


def _route(x, ada, layer, g2, w_router):
    tm = ROUTE_TM
    return pl.pallas_call(
        _route_kernel,
        grid=(N_TOK // tm,),
        in_specs=[
            pl.BlockSpec((tm, D_MODEL), lambda i: (i, 0)),
            _ada_spec(layer, tm),
            _resident((1, D_MODEL)),
            _resident((D_MODEL, LANES)),
        ],
        out_specs=[
            pl.BlockSpec((tm, 1, D_MODEL), lambda i: (i, 0, 0)),
            pl.BlockSpec((tm, LANES), lambda i: (i, 0)),
            pl.BlockSpec((tm, LANES), lambda i: (i, 0)),
            pl.BlockSpec((8, LANES), lambda i: (0, 0)),
        ],
        out_shape=[
            jax.ShapeDtypeStruct((N_TOK, 1, D_MODEL), F32),
            jax.ShapeDtypeStruct((N_TOK, LANES), jnp.int32),
            jax.ShapeDtypeStruct((N_TOK, LANES), F32),
            jax.ShapeDtypeStruct((8, LANES), F32),
        ],
        scratch_shapes=[pltpu.VMEM((8, LANES), F32)],
        compiler_params=_params("arbitrary"),
        name=f"moe_route_{layer}",
    )(x, ada, g2, w_router)


def _dispatch_kernel(d1_ref, d2_ref, fill_start_ref, fill_rows_ref, h_ref, hs_ref, zero_scr, sem):
    i = pl.program_id(0)
    tm = h_ref.shape[0]
    base = i * tm

    def fill_copy(k):
        n = fill_rows_ref[k]
        return pltpu.make_async_copy(zero_scr.at[pl.ds(0, n)],
                                     hs_ref.at[pl.ds(fill_start_ref[k], n)], sem.at[1])

    @pl.when(i == 0)
    def _():
        zero_scr[...] = jnp.zeros_like(zero_scr)
        for k in range(MOE_FILLS):
            @pl.when(fill_rows_ref[k] > 0)
            def _():
                fill_copy(k).start()
        for k in range(MOE_FILLS):
            @pl.when(fill_rows_ref[k] > 0)
            def _():
                fill_copy(k).wait()

    def row_copy(r, d_ref):
        return pltpu.make_async_copy(h_ref.at[r], hs_ref.at[d_ref[base + r]], sem.at[0])

    def issue(r, carry):
        row_copy(r, d1_ref).start()
        row_copy(r, d2_ref).start()
        return carry

    def drain(r, carry):
        row_copy(r, d1_ref).wait()
        row_copy(r, d2_ref).wait()
        return carry

    lax.fori_loop(0, tm, issue, 0)
    lax.fori_loop(0, tm, drain, 0)


def _dispatch(d1, d2, fill_start, fill_rows, h_rows):
    tm = DISPATCH_TM
    return pl.pallas_call(
        _dispatch_kernel,
        grid_spec=pltpu.PrefetchScalarGridSpec(
            num_scalar_prefetch=4,
            grid=(N_TOK // tm,),
            in_specs=[pl.BlockSpec((tm, 1, D_MODEL), lambda i, *_: (i, 0, 0))],
            out_specs=pl.BlockSpec(memory_space=pl.ANY),
            scratch_shapes=[pltpu.VMEM((MOE_TG, 1, D_MODEL), F32),
                            pltpu.SemaphoreType.DMA((2,))],
        ),
        out_shape=jax.ShapeDtypeStruct((MOE_ROWS, 1, D_MODEL), F32),
        compiler_params=_params("arbitrary"),
        name="moe_dispatch",
    )(d1, d2, fill_start, fill_rows, h_rows)


def _experts_kernel(te_ref, nt_ref, hs_ref, wg_ref, wu_ref, wd_ref, ys_ref, h2d_scr):
    used = pl.program_id(0) < nt_ref[0]

    @pl.when(jnp.logical_not(used))
    def _():
        ys_ref[...] = jnp.zeros_like(ys_ref)

    @pl.when(used)
    def _():
        h2d_scr[...] = hs_ref[...].reshape(MOE_TG, D_MODEL)
        h = h2d_scr[...].astype(BF16)
        y = None
        for lo in range(0, MOE_FF, MOE_FC):
            g = _dot(h, wg_ref[:, lo:lo + MOE_FC])
            u = _dot(h, wu_ref[:, lo:lo + MOE_FC])
            part = _dot((_silu(g) * u).astype(BF16), wd_ref[lo:lo + MOE_FC, :])
            y = part if y is None else y + part
        ys_ref[...] = y.reshape(MOE_TG, 1, D_MODEL)


def _experts(tile_expert, n_tiles, hs, m, w_gate, w_up, w_down):
    rows = pl.BlockSpec((MOE_TG, 1, D_MODEL), lambda t, te, nt: (t, 0, 0))
    return pl.pallas_call(
        _experts_kernel,
        grid_spec=pltpu.PrefetchScalarGridSpec(
            num_scalar_prefetch=2,
            grid=(MOE_MAX_TILES,),
            in_specs=[
                rows,
                pl.BlockSpec((None, None, D_MODEL, MOE_FF), lambda t, te, nt: (m, te[t], 0, 0)),
                pl.BlockSpec((None, None, D_MODEL, MOE_FF), lambda t, te, nt: (m, te[t], 0, 0)),
                pl.BlockSpec((None, None, MOE_FF, D_MODEL), lambda t, te, nt: (m, te[t], 0, 0)),
            ],
            out_specs=rows,
            scratch_shapes=[pltpu.VMEM((MOE_TG, D_MODEL), F32)],
        ),
        out_shape=jax.ShapeDtypeStruct((MOE_ROWS, 1, D_MODEL), F32),
        compiler_params=_params("arbitrary"),
        name="moe_experts",
    )(tile_expert, n_tiles, hs, w_gate, w_up, w_down)


def _combine_kernel(d1_ref, d2_ref, x_ref, ada_ref, gw_ref, ys_ref, o_ref,
                    ya0, yb0, ya1, yb1, ya2d, yb2d, sem):
    i = pl.program_id(0)
    tm = x_ref.shape[0]

    def row_copies(step, r, buf_a, buf_b, slot):
        t = step * tm + r
        return (pltpu.make_async_copy(ys_ref.at[d1_ref[t]], buf_a.at[r], sem.at[slot]),
                pltpu.make_async_copy(ys_ref.at[d2_ref[t]], buf_b.at[r], sem.at[slot]))

    def start_tile(step, buf_a, buf_b, slot):
        def body(r, carry):
            for cp in row_copies(step, r, buf_a, buf_b, slot):
                cp.start()
            return carry
        lax.fori_loop(0, tm, body, 0)

    def wait_tile(step, buf_a, buf_b, slot):
        def body(r, carry):
            for cp in row_copies(step, r, buf_a, buf_b, slot):
                cp.wait()
            return carry
        lax.fori_loop(0, tm, body, 0)

    @pl.when(i == 0)
    def _():
        start_tile(0, ya0, yb0, 0)

    def run(cur, nxt):
        @pl.when(i + 1 < pl.num_programs(0))
        def _():
            start_tile(i + 1, *nxt)

        wait_tile(i, *cur)
        ya2d[...] = cur[0][...].reshape(tm, D_MODEL)
        yb2d[...] = cur[1][...].reshape(tm, D_MODEL)
        mix = gw_ref[:, 0:1] * ya2d[...] + gw_ref[:, 1:2] * yb2d[...]
        o_ref[...] = x_ref[...] + ada_ref[5:6, :] * mix

    @pl.when(i % 2 == 0)
    def _():
        run((ya0, yb0, 0), (ya1, yb1, 1))

    @pl.when(i % 2 == 1)
    def _():
        run((ya1, yb1, 1), (ya0, yb0, 0))


def _combine(d1, d2, x, ada, layer, gate_w, ys):
    tm = COMBINE_TM
    row_buf = pltpu.VMEM((tm, 1, D_MODEL), F32)
    return pl.pallas_call(
        _combine_kernel,
        grid_spec=pltpu.PrefetchScalarGridSpec(
            num_scalar_prefetch=2,
            grid=(N_TOK // tm,),
            in_specs=[
                pl.BlockSpec((tm, D_MODEL), lambda i, *_: (i, 0)),
                pl.BlockSpec((None, None, 6, D_MODEL),
                             lambda i, *_: (layer, _cond_of_tile(i, tm), 0, 0)),
                pl.BlockSpec((tm, LANES), lambda i, *_: (i, 0)),
                pl.BlockSpec(memory_space=pl.ANY),
            ],
            out_specs=pl.BlockSpec((tm, D_MODEL), lambda i, *_: (i, 0)),
            scratch_shapes=[row_buf, row_buf, row_buf, row_buf,
                            pltpu.VMEM((tm, D_MODEL), F32), pltpu.VMEM((tm, D_MODEL), F32),
                            pltpu.SemaphoreType.DMA((2,))],
        ),
        out_shape=jax.ShapeDtypeStruct((N_TOK, D_MODEL), F32),
        compiler_params=_params("arbitrary"),
        name="moe_combine",
    )(d1, d2, x, ada, gate_w, ys)


def _moe_layer(x, ada, layer, m, g2, w_router, w_gate, w_up, w_down):
    h_rows, route, gate_w, counts = _route(x, ada, layer, g2, w_router)
    cnt = counts[0, :N_EXPERTS].astype(jnp.int32)
    tiles_per_expert = (cnt + MOE_TG - 1) // MOE_TG
    tile_end = jnp.cumsum(tiles_per_expert)
    row_start = (tile_end - tiles_per_expert) * MOE_TG
    n_tiles = tile_end[-1:]
    t = jnp.minimum(jnp.arange(MOE_MAX_TILES, dtype=jnp.int32), n_tiles - 1)
    tile_expert = jnp.sum((t[:, None] >= tile_end[None, :]).astype(jnp.int32), axis=1)
    experts = jnp.arange(N_EXPERTS, dtype=jnp.int32)
    d1 = jnp.sum(jnp.where(route[:, 0:1] == experts, row_start, 0), axis=1) + route[:, 2]
    d2 = jnp.sum(jnp.where(route[:, 1:2] == experts, row_start, 0), axis=1) + route[:, 3]
    spare_tile = jnp.minimum(n_tiles + experts, MOE_MAX_TILES - 1)
    fill_start = jnp.concatenate([row_start + cnt, spare_tile * MOE_TG])
    fill_rows = jnp.concatenate([tiles_per_expert * MOE_TG - cnt,
                                 jnp.where(n_tiles + experts < MOE_MAX_TILES, MOE_TG, 0)])
    hs = _dispatch(d1, d2, fill_start, fill_rows, h_rows)
    ys = _experts(tile_expert, n_tiles, hs, m, w_gate, w_up, w_down)
    return _combine(d1, d2, x, ada, layer, gate_w, ys)


def _final_kernel(x_ref, g_ref, o_ref):
    o_ref[...] = _rms(x_ref[...]) * g_ref[...]


def _final_norm(x, g, row0, rows):
    tm = 1024
    blk0 = row0 // tm
    return pl.pallas_call(
        _final_kernel,
        grid=(rows // tm,),
        in_specs=[pl.BlockSpec((tm, D_MODEL), lambda i: (blk0 + i, 0)),
                  _resident((1, D_MODEL))],
        out_specs=pl.BlockSpec((tm, D_MODEL), lambda i: (i, 0)),
        out_shape=jax.ShapeDtypeStruct((rows, D_MODEL), F32),
        compiler_params=_params("arbitrary"),
        name="final_norm",
    )(x, g)


def _rope_tables(dim):
    half = dim // 2
    quarter = half // 2
    t = np.arange(DEC_SEQ)
    pos = np.stack([t // GRID_W, t % GRID_W], axis=1).astype(np.float32)
    freqs = (ROPE_THETA ** (-np.arange(quarter, dtype=np.float32) / quarter)).astype(np.float32)
    ang = pos[:, :, None] * freqs[None, None, :]
    cos = np.cos(ang.astype(np.float64))
    sin = np.sin(ang.astype(np.float64))
    cos_t = np.concatenate([cos, cos], axis=-1).reshape(DEC_SEQ, dim)
    sin_t = np.concatenate([-sin, sin], axis=-1).reshape(DEC_SEQ, dim)
    lane = np.arange(dim)
    swap = np.where(lane % half < quarter, lane + quarter, lane - quarter)
    return cos_t.astype(np.float32), sin_t.astype(np.float32), swap


def kernel(x_prompt, x_sample, cache_mla_ckv, cache_mla_kpe, cache_gqa_k, cache_gqa_v, c, c_ctx, w_ada, b_ada, norm1_g, norm2_g, conv_w_in, conv_k, conv_w_out, mla_w_down, mla_q_norm_g, mla_kv_norm_g, mla_w_uq, mla_w_ukv, mla_w_o, gqa_w_qkv, gqa_q_norm_g, gqa_k_norm_g, gqa_w_o, ffn_w_gate, ffn_w_up, ffn_w_down, moe_w_router, moe_w_gate, moe_w_up, moe_w_down, final_norm_g):
    xs = (x_prompt.reshape(N_CTX, D_MODEL), x_sample.reshape(N_LAT, D_MODEL))
    conv_w_in, conv_w_out = conv_w_in.astype(BF16), conv_w_out.astype(BF16)
    ffn_w_gate, ffn_w_up, ffn_w_down = (w.astype(BF16) for w in (ffn_w_gate, ffn_w_up, ffn_w_down))
    moe_w_gate, moe_w_up, moe_w_down = (w.astype(BF16) for w in (moe_w_gate, moe_w_up, moe_w_down))
    cond = jnp.concatenate(
        [c_ctx[None, :], c, jnp.zeros((N_COND - 1 - DEC_BATCH, D_MODEL), F32)], axis=0)
    ada = _adaln_table(cond, w_ada, b_ada)

    new_ckv, new_kpe, new_k, new_v = [], [], [], []
    for i in range(DEPTH):
        j = i // N_MIXERS
        kind = i % N_MIXERS
        g1 = norm1_g[i][None, :]
        if kind == 0:
            x = _conv_layer(xs if i == 0 else (x,), ada, i, j, g1, conv_w_in, conv_k, conv_w_out)
        elif kind == 1:
            cos64, sin64, swap64 = _rope_tables(MLA_ROPE)
            wd = mla_w_down[j]
            kpe_cols = wd[:, MLA_Q_RANK + MLA_KV_RANK:]
            w_down = jnp.concatenate([wd, kpe_cols[:, swap64]], axis=1).astype(BF16)
            wq = mla_w_uq[j].reshape(MLA_Q_RANK, MLA_HEADS, MLA_QK)
            wq_nope = wq[:, :, :MLA_NOPE].reshape(MLA_Q_RANK, MLA_Q_NOPE_COLS)
            wq_rope = wq[:, :, MLA_NOPE:]
            w_uq = jnp.concatenate(
                [wq_nope, wq_rope.reshape(MLA_Q_RANK, MLA_Q_ROPE_COLS),
                 wq_rope[:, :, swap64].reshape(MLA_Q_RANK, MLA_Q_ROPE_COLS)], axis=1).astype(BF16)
            wkv = mla_w_ukv[j].reshape(MLA_KV_RANK, MLA_HEADS, MLA_NOPE + MLA_V)
            w_ukv = jnp.concatenate(
                [wkv[:, :, :MLA_NOPE].reshape(MLA_KV_RANK, MLA_Q_NOPE_COLS),
                 wkv[:, :, MLA_NOPE:].reshape(MLA_KV_RANK, MLA_HEADS * MLA_V)], axis=1).astype(BF16)
            rope = (jnp.asarray(np.tile(cos64, (1, MLA_HEADS))), jnp.asarray(np.tile(sin64, (1, MLA_HEADS))),
                    jnp.asarray(cos64), jnp.asarray(sin64))
            x, ckv_p, kpe_p = _mla_layer(
                x, ada, i, g1, w_down, mla_q_norm_g[j][None, :], mla_kv_norm_g[j][None, :],
                w_uq, w_ukv, mla_w_o[j].astype(BF16), cache_mla_ckv[:, j], cache_mla_kpe[:, j], rope)
            new_ckv.append(ckv_p.reshape(BATCH, SEQ, MLA_KV_RANK))
            new_kpe.append(kpe_p.reshape(BATCH, SEQ, MLA_ROPE))
        else:
            cos128, sin128, swap128 = _rope_tables(GQA_HEAD_DIM)
            wqkv = gqa_w_qkv[j]
            wq = wqkv[:, :GQA_Q_COLS].reshape(D_MODEL, GQA_HEADS, GQA_HEAD_DIM)
            wk = wqkv[:, GQA_Q_COLS:GQA_Q_COLS + GQA_KV_COLS].reshape(D_MODEL, GQA_KV_HEADS, GQA_HEAD_DIM)
            w_qkv_ctx = wqkv.astype(BF16)
            w_qkv_lat = jnp.concatenate(
                [wqkv, wq[:, :, swap128].reshape(D_MODEL, GQA_Q_COLS),
                 wk[:, :, swap128].reshape(D_MODEL, GQA_KV_COLS)], axis=1).astype(BF16)
            gq = gqa_q_norm_g[j]
            gk = gqa_k_norm_g[j]
            rope = (jnp.asarray(np.tile(cos128, (1, GQA_HEADS))), jnp.asarray(np.tile(sin128, (1, GQA_HEADS))),
                    jnp.asarray(np.tile(cos128, (1, GQA_KV_HEADS))), jnp.asarray(np.tile(sin128, (1, GQA_KV_HEADS))))
            x, k_p, v_p = _gqa_layer(
                x, ada, i, g1, w_qkv_ctx, w_qkv_lat,
                jnp.tile(gq, GQA_HEADS)[None, :], jnp.tile(gk, GQA_KV_HEADS)[None, :],
                jnp.tile(gq[swap128], GQA_HEADS)[None, :], jnp.tile(gk[swap128], GQA_KV_HEADS)[None, :],
                gqa_w_o[j].astype(BF16),
                cache_gqa_k[:, j].reshape(DEC_BATCH, PAST_LEN, GQA_KV_COLS),
                cache_gqa_v[:, j].reshape(DEC_BATCH, PAST_LEN, GQA_KV_COLS), rope)
            new_k.append(k_p.reshape(BATCH, SEQ, GQA_KV_HEADS, GQA_HEAD_DIM))
            new_v.append(v_p.reshape(BATCH, SEQ, GQA_KV_HEADS, GQA_HEAD_DIM))

        m = i // 2
        g2 = norm2_g[i][None, :]
        if i % 2 == 0:
            x = _ffn_layer(x, ada, i, m, g2, ffn_w_gate, ffn_w_up, ffn_w_down)
        else:
            w_router = jnp.pad(moe_w_router[m], ((0, 0), (0, LANES - N_EXPERTS)))
            x = _moe_layer(x, ada, i, m, g2, w_router, moe_w_gate, moe_w_up, moe_w_down)

    g_final = final_norm_g[None, :]
    y_prompt = _final_norm(x, g_final, 0, N_CTX).reshape(BATCH, SEQ, D_MODEL)
    y_sample = _final_norm(x, g_final, N_CTX, N_LAT).reshape(DEC_BATCH, DEC_SEQ, D_MODEL)
    return (y_prompt, y_sample,
            jnp.stack(new_ckv, axis=1), jnp.stack(new_kpe, axis=1),
            jnp.stack(new_k, axis=1), jnp.stack(new_v, axis=1))
```

```python
import functools

import numpy as np
import jax
import jax.numpy as jnp
from jax import lax
from jax.experimental import pallas as pl
from jax.experimental.pallas import tpu as pltpu

D_MODEL = 1024
BATCH = 32
SEQ = 256
DEPTH = 4
DEC_BATCH = 2
DEC_SEQ = 1024
PAST_LEN = 512
GRID_W = 64
N_MIXERS = 3
CONV_WIDTH = 3
MLA_HEADS = 8
MLA_NOPE = 128
MLA_ROPE = 64
MLA_V = 128
MLA_Q_RANK = 384
MLA_KV_RANK = 256
GQA_HEADS = 8
GQA_KV_HEADS = 2
GQA_HEAD_DIM = 128
D_FF = 2816
N_EXPERTS = 8
MOE_FF = 2048
ROPE_THETA = 10000.0
EPS = 1e-6

N_CTX = BATCH * SEQ
N_LAT = DEC_BATCH * DEC_SEQ
N_TOK = N_CTX + N_LAT
N_COND = 8
LANES = 128
Q_TILE = 256
VMEM_LIMIT = 56 * 1024 * 1024

F32 = jnp.float32
BF16 = jnp.bfloat16


def _dot(a, b):
    return jnp.dot(a, b, preferred_element_type=F32)


def _dot_t(a, b):
    return lax.dot_general(a, b, (((1,), (1,)), ((), ())), preferred_element_type=F32)


def _rms(x):
    return x * lax.rsqrt(jnp.mean(x * x, axis=-1, keepdims=True) + EPS)


def _modulate(x, g, shift, scale):
    return _rms(x) * g * (1.0 + scale) + shift


def _silu(x):
    return x * jax.nn.sigmoid(x)


def _cond_of_tile(i, tm):
    start = i * tm
    return jnp.where(start < N_CTX, 0, 1 + (start - N_CTX) // DEC_SEQ)


def _resident(shape):
    return pl.BlockSpec(shape, lambda *_: (0,) * len(shape), pipeline_mode=pl.Buffered(1))


def _resident_at(index, shape):
    return pl.BlockSpec((None,) + tuple(shape), lambda *_: (index,) + (0,) * len(shape),
                        pipeline_mode=pl.Buffered(1))


def _params(*sem):
    return pltpu.CompilerParams(dimension_semantics=sem, vmem_limit_bytes=VMEM_LIMIT)


def _adaln_kernel(cond_ref, w_ref, b_ref, o_ref):
    s = _silu(cond_ref[...]).astype(BF16)
    o_ref[...] = _dot(s, w_ref[...].astype(BF16)) + b_ref[...]


def _adaln_table(cond, w_ada, b_ada):
    tn = 1536
    out = pl.pallas_call(
        _adaln_kernel,
        grid=(DEPTH, 6 * D_MODEL // tn),
        in_specs=[
            pl.BlockSpec((N_COND, D_MODEL), lambda l, j: (0, 0)),
            pl.BlockSpec((None, D_MODEL, tn), lambda l, j: (l, 0, j)),
            pl.BlockSpec((None, 1, tn), lambda l, j: (l, 0, j)),
        ],
        out_specs=pl.BlockSpec((None, N_COND, tn), lambda l, j: (l, 0, j)),
        out_shape=jax.ShapeDtypeStruct((DEPTH, N_COND, 6 * D_MODEL), F32),
        compiler_params=_params("arbitrary", "arbitrary"),
        name="adaln_table",
    )(cond, w_ada, b_ada.reshape(DEPTH, 1, 6 * D_MODEL))
    return out.reshape(DEPTH, N_COND, 6, D_MODEL)


def _ada_spec(layer, tm):
    return pl.BlockSpec((None, None, 6, D_MODEL), lambda i: (layer, _cond_of_tile(i, tm), 0, 0))


def _ada_spec_at(layer, cond_fn):
    return pl.BlockSpec((None, None, 6, D_MODEL), lambda i: (layer, cond_fn(i), 0, 0))


CONV_TM = 1024
CONV_CC = 256


def _conv_kernel(*refs, split):
    if split:
        xp_ref, x_ref, ada_ref, g1_ref, win_ref, ck_ref, wout_ref, o_ref, v_scr = refs
    else:
        x_ref, ada_ref, g1_ref, win_ref, ck_ref, wout_ref, o_ref, v_scr = refs
    tm = x_ref.shape[0]
    i = pl.program_id(0)
    x = x_ref[...]
    if split:
        x = jnp.where(i * tm < N_CTX, xp_ref[...], x)
    h = _modulate(x, g1_ref[...], ada_ref[0:1, :], ada_ref[1:2, :]).astype(BF16)
    period = jnp.where(i * tm < N_CTX, SEQ, DEC_SEQ)
    pos = lax.broadcasted_iota(jnp.int32, (tm, 1), 0) & (period - 1)
    first = pos == 0
    last = pos == period - 1
    for j in range(D_MODEL // CONV_CC):
        lo = j * CONV_CC
        b_gate = _dot(h, win_ref[:, lo:lo + CONV_CC])
        c_gate = _dot(h, win_ref[:, D_MODEL + lo:D_MODEL + lo + CONV_CC])
        x_in = _dot(h, win_ref[:, 2 * D_MODEL + lo:2 * D_MODEL + lo + CONV_CC])
        u = c_gate * x_in
        u_prev = jnp.where(first, 0.0, pltpu.roll(u, 1, 0))
        u_next = jnp.where(last, 0.0, pltpu.roll(u, tm - 1, 0))
        conv = (ck_ref[0:1, lo:lo + CONV_CC] * u_prev + ck_ref[1:2, lo:lo + CONV_CC] * u
                + ck_ref[2:3, lo:lo + CONV_CC] * u_next)
        v_scr[:, lo:lo + CONV_CC] = (b_gate * conv).astype(BF16)
    o_ref[...] = x + ada_ref[2:3, :] * _dot(v_scr[...], wout_ref[...])


def _conv_layer(xs, ada, layer, j, g1, w_in, conv_k, w_out):
    tm = CONV_TM
    ctx_tiles = N_CTX // tm
    if len(xs) == 1:
        x_specs = [pl.BlockSpec((tm, D_MODEL), lambda i: (i, 0))]
    else:
        x_specs = [pl.BlockSpec((tm, D_MODEL), lambda i: (jnp.minimum(i, ctx_tiles - 1), 0)),
                   pl.BlockSpec((tm, D_MODEL), lambda i: (jnp.maximum(i - ctx_tiles, 0), 0))]
    return pl.pallas_call(
        functools.partial(_conv_kernel, split=len(xs) == 2),
        grid=(N_TOK // tm,),
        in_specs=x_specs + [
            _ada_spec(layer, tm),
            _resident((1, D_MODEL)),
            _resident_at(j, (D_MODEL, 3 * D_MODEL)),
            _resident_at(j, (CONV_WIDTH, D_MODEL)),
            _resident_at(j, (D_MODEL, D_MODEL)),
        ],
        out_specs=pl.BlockSpec((tm, D_MODEL), lambda i: (i, 0)),
        out_shape=jax.ShapeDtypeStruct((N_TOK, D_MODEL), F32),
        scratch_shapes=[pltpu.VMEM((tm, D_MODEL), BF16)],
        compiler_params=_params("arbitrary"),
        name=f"conv_mixer_{layer}",
    )(*xs, ada, g1, w_in, conv_k, w_out)


FFN_TM = 512
FFN_CHUNKS = ((0, 1536), (1536, D_FF))


def _ffn_kernel(x_ref, ada_ref, g2_ref, wg_ref, wu_ref, wd_ref, o_ref):
    x = x_ref[...]
    h = _modulate(x, g2_ref[...], ada_ref[3:4, :], ada_ref[4:5, :]).astype(BF16)
    f = None
    for lo, hi in FFN_CHUNKS:
        g = _dot(h, wg_ref[:, lo:hi])
        u = _dot(h, wu_ref[:, lo:hi])
        part = _dot((_silu(g) * u).astype(BF16), wd_ref[lo:hi, :])
        f = part if f is None else f + part
    o_ref[...] = x + ada_ref[5:6, :] * f


def _ffn_layer(x, ada, layer, m, g2, w_gate, w_up, w_down):
    tm = FFN_TM
    return pl.pallas_call(
        _ffn_kernel,
        grid=(N_TOK // tm,),
        in_specs=[
            pl.BlockSpec((tm, D_MODEL), lambda i: (i, 0)),
            _ada_spec(layer, tm),
            _resident((1, D_MODEL)),
            _resident_at(m, (D_MODEL, D_FF)),
            _resident_at(m, (D_MODEL, D_FF)),
            _resident_at(m, (D_FF, D_MODEL)),
        ],
        out_specs=pl.BlockSpec((tm, D_MODEL), lambda i: (i, 0)),
        out_shape=jax.ShapeDtypeStruct((N_TOK, D_MODEL), F32),
        compiler_params=_params("arbitrary"),
        name=f"dense_ffn_{layer}",
    )(x, ada, g2, w_gate, w_up, w_down)


def _attention(qk_parts, v_ref, heads, kv_of_head, dv, scale, o_scr, finish):
    def q_block(rows):
        for h in range(heads):
            g = kv_of_head(h)
            s = None
            for q_ref, k_ref, width, k_shared in qk_parts:
                kb = 0 if k_shared else g
                part = _dot_t(q_ref[rows, h * width:(h + 1) * width],
                              k_ref[:, kb * width:(kb + 1) * width])
                s = part if s is None else s + part
            s = s * scale
            e = jnp.exp(s - jnp.max(s, axis=-1, keepdims=True))
            denom = jnp.sum(e, axis=-1, keepdims=True)
            o = _dot(e.astype(BF16), v_ref[:, g * dv:(g + 1) * dv]) / denom
            o_scr[rows, h * dv:(h + 1) * dv] = o.astype(BF16)
        finish(rows)

    _for_row_blocks(o_scr.shape[0], q_block)


def _for_row_blocks(n_rows, fn):
    if n_rows == Q_TILE:
        fn(pl.ds(0, Q_TILE))
    else:
        def body(b, carry):
            fn(pl.ds(pl.multiple_of(b * Q_TILE, Q_TILE), Q_TILE))
            return carry
        lax.fori_loop(0, n_rows // Q_TILE, body, 0)


MLA_QK = MLA_NOPE + MLA_ROPE
MLA_DOWN_COLS = MLA_Q_RANK + MLA_KV_RANK + 2 * MLA_ROPE
MLA_Q_NOPE_COLS = MLA_HEADS * MLA_NOPE
MLA_Q_ROPE_COLS = MLA_HEADS * MLA_ROPE


def _ctx_then_latent(body, n_in, refs):
    ins, lat_ref, rest = refs[:n_in], refs[n_in], refs[n_in + 1:]
    o_ref = rest[0]
    i = pl.program_id(0)

    @pl.when(i < BATCH)
    def _():
        body(*ins, *rest, latent=False)

    @pl.when(i >= BATCH)
    def _():
        o_ref[...] = lat_ref[...]


def _ctx_specs(n_out_cols):
    seq = lambda i: (jnp.minimum(i, BATCH - 1), 0)
    x_spec = pl.BlockSpec((SEQ, D_MODEL), seq)
    lat_spec = pl.BlockSpec((SEQ, D_MODEL), lambda i: (jnp.maximum(i - BATCH, 0), 0))
    out_specs = [pl.BlockSpec((SEQ, D_MODEL), lambda i: (i, 0))]
    out_specs += [pl.BlockSpec((SEQ, n), seq) for n in n_out_cols]
    return x_spec, lat_spec, out_specs


CTX_STEPS = BATCH + N_LAT // SEQ


def _mla_kernel(*refs, latent):
    if latent:
        _mla_body(*refs, latent=True)
    else:
        _ctx_then_latent(_mla_body, 9, refs)


def _mla_body(*refs, latent):
    if latent:
        (x_ref, ada_ref, g1_ref, wdown_ref, gq_ref, gkv_ref, wuq_ref, wukv_ref, wo_ref,
         cosq_ref, sinq_ref, cosk_ref, sink_ref, cckv_ref, ckpe_ref,
         o_ref, qn_scr, qr_scr, kn_scr, kr_scr, v_scr, o_scr) = refs
    else:
        (x_ref, ada_ref, g1_ref, wdown_ref, gq_ref, gkv_ref, wuq_ref, wukv_ref, wo_ref,
         o_ref, ckv_out_ref, kpe_out_ref, qn_scr, qr_scr, kn_scr, kr_scr, v_scr, o_scr) = refs
    tm = x_ref.shape[0]
    kpe_lo = MLA_Q_RANK + MLA_KV_RANK

    def project(rows):
        h = _modulate(x_ref[rows, :], g1_ref[...], ada_ref[0:1, :], ada_ref[1:2, :]).astype(BF16)
        down = _dot(h, wdown_ref[...])
        c_q = down[:, :MLA_Q_RANK]
        c_kv = down[:, MLA_Q_RANK:kpe_lo]
        kpe = down[:, kpe_lo:kpe_lo + MLA_ROPE]
        q = _dot((_rms(c_q) * gq_ref[...]).astype(BF16), wuq_ref[...])
        ckv_n = _rms(c_kv) * gkv_ref[...]
        kv = _dot(ckv_n.astype(BF16), wukv_ref[...])
        qn_scr[rows, :] = q[:, :MLA_Q_NOPE_COLS].astype(BF16)
        q_rope = q[:, MLA_Q_NOPE_COLS:MLA_Q_NOPE_COLS + MLA_Q_ROPE_COLS]
        kn_scr[rows, :] = kv[:, :MLA_Q_NOPE_COLS].astype(BF16)
        v_scr[rows, :] = kv[:, MLA_Q_NOPE_COLS:].astype(BF16)
        if latent:
            q_swap = q[:, MLA_Q_NOPE_COLS + MLA_Q_ROPE_COLS:]
            q_rope = q_rope * cosq_ref[rows, :] + q_swap * sinq_ref[rows, :]
            kpe_swap = down[:, kpe_lo + MLA_ROPE:kpe_lo + 2 * MLA_ROPE]
            kpe = kpe * cosk_ref[rows, :] + kpe_swap * sink_ref[rows, :]
        else:
            ckv_out_ref[rows, :] = ckv_n
            kpe_out_ref[rows, :] = kpe
        qr_scr[rows, :] = q_rope.astype(BF16)
        kr_scr[rows, :] = kpe.astype(BF16)

    _for_row_blocks(tm, project)

    if latent:
        def expand_cache(rows):
            kv_c = _dot(cckv_ref[rows, :].astype(BF16), wukv_ref[...])
            dst = pl.ds(pl.multiple_of(tm + rows.start, Q_TILE), Q_TILE)
            kn_scr[dst, :] = kv_c[:, :MLA_Q_NOPE_COLS].astype(BF16)
            v_scr[dst, :] = kv_c[:, MLA_Q_NOPE_COLS:].astype(BF16)
            kr_scr[dst, :] = ckpe_ref[rows, :].astype(BF16)

        _for_row_blocks(PAST_LEN, expand_cache)

    def finish(rows):
        o_ref[rows, :] = x_ref[rows, :] + ada_ref[2:3, :] * _dot(o_scr[rows, :], wo_ref[...])

    _attention([(qn_scr, kn_scr, MLA_NOPE, False), (qr_scr, kr_scr, MLA_ROPE, True)],
               v_scr, MLA_HEADS, lambda hd: hd, MLA_V, MLA_QK ** -0.5, o_scr, finish)


def _mla_scratch(tm, sk):
    return [
        pltpu.VMEM((tm, MLA_Q_NOPE_COLS), BF16),
        pltpu.VMEM((tm, MLA_Q_ROPE_COLS), BF16),
        pltpu.VMEM((sk, MLA_Q_NOPE_COLS), BF16),
        pltpu.VMEM((sk, MLA_ROPE), BF16),
        pltpu.VMEM((sk, MLA_HEADS * MLA_V), BF16),
        pltpu.VMEM((tm, MLA_HEADS * MLA_V), BF16),
    ]


def _mla_layer(x, ada, layer, g1, w_down, g_q, g_kv, w_uq, w_ukv, w_o,
               cache_ckv, cache_kpe, rope):
    n_uq = w_uq.shape[1]
    weights = [
        _resident((1, D_MODEL)),
        _resident((D_MODEL, MLA_DOWN_COLS)),
        _resident((1, MLA_Q_RANK)),
        _resident((1, MLA_KV_RANK)),
        _resident((MLA_Q_RANK, n_uq)),
        _resident((MLA_KV_RANK, 2 * MLA_Q_NOPE_COLS)),
        _resident((MLA_HEADS * MLA_V, D_MODEL)),
    ]
    weight_args = (g1, w_down, g_q, g_kv, w_uq, w_ukv, w_o)
    cos_q, sin_q, cos_k, sin_k = rope
    ctx_blocks = N_CTX // DEC_SEQ
    lat_out = pl.pallas_call(
        functools.partial(_mla_kernel, latent=True),
        grid=(DEC_BATCH,),
        in_specs=[pl.BlockSpec((DEC_SEQ, D_MODEL), lambda b: (ctx_blocks + b, 0)),
                  _ada_spec_at(layer, lambda b: 1 + b)] + weights + [
            _resident((DEC_SEQ, MLA_Q_ROPE_COLS)),
            _resident((DEC_SEQ, MLA_Q_ROPE_COLS)),
            _resident((DEC_SEQ, MLA_ROPE)),
            _resident((DEC_SEQ, MLA_ROPE)),
            pl.BlockSpec((None, PAST_LEN, MLA_KV_RANK), lambda b: (b, 0, 0)),
            pl.BlockSpec((None, PAST_LEN, MLA_ROPE), lambda b: (b, 0, 0)),
        ],
        out_specs=pl.BlockSpec((DEC_SEQ, D_MODEL), lambda b: (b, 0)),
        out_shape=jax.ShapeDtypeStruct((N_LAT, D_MODEL), F32),
        scratch_shapes=_mla_scratch(DEC_SEQ, DEC_SEQ + PAST_LEN),
        compiler_params=_params("arbitrary"),
        name=f"mla_latent_{layer}",
    )(x, ada, *weight_args, cos_q, sin_q, cos_k, sin_k, cache_ckv, cache_kpe)

    x_spec, lat_spec, out_specs = _ctx_specs((MLA_KV_RANK, MLA_ROPE))
    return pl.pallas_call(
        functools.partial(_mla_kernel, latent=False),
        grid=(CTX_STEPS,),
        in_specs=[x_spec, _ada_spec_at(layer, lambda i: 0)] + weights + [lat_spec],
        out_specs=out_specs,
        out_shape=[
            jax.ShapeDtypeStruct((N_TOK, D_MODEL), F32),
            jax.ShapeDtypeStruct((N_CTX, MLA_KV_RANK), F32),
            jax.ShapeDtypeStruct((N_CTX, MLA_ROPE), F32),
        ],
        scratch_shapes=_mla_scratch(SEQ, SEQ),
        compiler_params=_params("arbitrary"),
        name=f"mla_ctx_{layer}",
    )(x, ada, *weight_args, lat_out)


GQA_Q_COLS = GQA_HEADS * GQA_HEAD_DIM
GQA_KV_COLS = GQA_KV_HEADS * GQA_HEAD_DIM
GQA_REP = GQA_HEADS // GQA_KV_HEADS


def _head_rms(t, n_heads):
    cols = []
    for hd in range(n_heads):
        th = t[:, hd * GQA_HEAD_DIM:(hd + 1) * GQA_HEAD_DIM]
        r = lax.rsqrt(jnp.mean(th * th, axis=-1, keepdims=True) + EPS)
        cols.append(jnp.broadcast_to(r, th.shape))
    return jnp.concatenate(cols, axis=-1)


def _gqa_kernel(*refs, latent):
    if latent:
        _gqa_body(*refs, latent=True)
    else:
        _ctx_then_latent(_gqa_body, 7, refs)


def _gqa_body(*refs, latent):
    if latent:
        (x_ref, ada_ref, g1_ref, wqkv_ref, gq_ref, gk_ref, wo_ref,
         gqs_ref, gks_ref, cosq_ref, sinq_ref, cosk_ref, sink_ref, ck_ref, cv_ref,
         o_ref, q_scr, k_scr, v_scr, o_scr) = refs
    else:
        (x_ref, ada_ref, g1_ref, wqkv_ref, gq_ref, gk_ref, wo_ref,
         o_ref, k_out_ref, v_out_ref, q_scr, k_scr, v_scr, o_scr) = refs
    tm = x_ref.shape[0]

    def project(rows):
        h = _modulate(x_ref[rows, :], g1_ref[...], ada_ref[0:1, :], ada_ref[1:2, :]).astype(BF16)
        qkv = _dot(h, wqkv_ref[...])
        q_raw = qkv[:, :GQA_Q_COLS]
        k_raw = qkv[:, GQA_Q_COLS:GQA_Q_COLS + GQA_KV_COLS]
        v = qkv[:, GQA_Q_COLS + GQA_KV_COLS:GQA_Q_COLS + 2 * GQA_KV_COLS]
        q_r = _head_rms(q_raw, GQA_HEADS)
        k_r = _head_rms(k_raw, GQA_KV_HEADS)
        q = q_raw * q_r * gq_ref[...]
        k = k_raw * k_r * gk_ref[...]
        if latent:
            lo = GQA_Q_COLS + 2 * GQA_KV_COLS
            q_swap = qkv[:, lo:lo + GQA_Q_COLS] * q_r * gqs_ref[...]
            k_swap = qkv[:, lo + GQA_Q_COLS:lo + GQA_Q_COLS + GQA_KV_COLS] * k_r * gks_ref[...]
            q = q * cosq_ref[rows, :] + q_swap * sinq_ref[rows, :]
            k = k * cosk_ref[rows, :] + k_swap * sink_ref[rows, :]
        else:
            k_out_ref[rows, :] = k
            v_out_ref[rows, :] = v
        q_scr[rows, :] = q.astype(BF16)
        k_scr[rows, :] = k.astype(BF16)
        v_scr[rows, :] = v.astype(BF16)

    _for_row_blocks(tm, project)

    if latent:
        k_scr[tm:, :] = ck_ref[...].astype(BF16)
        v_scr[tm:, :] = cv_ref[...].astype(BF16)

    def finish(rows):
        o_ref[rows, :] = x_ref[rows, :] + ada_ref[2:3, :] * _dot(o_scr[rows, :], wo_ref[...])

    _attention([(q_scr, k_scr, GQA_HEAD_DIM, False)], v_scr, GQA_HEADS,
               lambda hd: hd // GQA_REP, GQA_HEAD_DIM, GQA_HEAD_DIM ** -0.5, o_scr, finish)


def _gqa_scratch(tm, sk):
    return [
        pltpu.VMEM((tm, GQA_Q_COLS), BF16),
        pltpu.VMEM((sk, GQA_KV_COLS), BF16),
        pltpu.VMEM((sk, GQA_KV_COLS), BF16),
        pltpu.VMEM((tm, GQA_Q_COLS), BF16),
    ]


def _gqa_layer(x, ada, layer, g1, w_qkv_ctx, w_qkv_lat, g_q, g_k, g_q_swap, g_k_swap, w_o,
               cache_k, cache_v, rope):
    cos_q, sin_q, cos_k, sin_k = rope
    ctx_blocks = N_CTX // DEC_SEQ
    lat_out = pl.pallas_call(
        functools.partial(_gqa_kernel, latent=True),
        grid=(DEC_BATCH,),
        in_specs=[pl.BlockSpec((DEC_SEQ, D_MODEL), lambda b: (ctx_blocks + b, 0)),
                  _ada_spec_at(layer, lambda b: 1 + b),
                  _resident((1, D_MODEL)),
                  _resident(w_qkv_lat.shape),
                  _resident((1, GQA_Q_COLS)),
                  _resident((1, GQA_KV_COLS)),
                  _resident((GQA_Q_COLS, D_MODEL)),
                  _resident((1, GQA_Q_COLS)),
                  _resident((1, GQA_KV_COLS)),
                  _resident((DEC_SEQ, GQA_Q_COLS)),
                  _resident((DEC_SEQ, GQA_Q_COLS)),
                  _resident((DEC_SEQ, GQA_KV_COLS)),
                  _resident((DEC_SEQ, GQA_KV_COLS)),
                  pl.BlockSpec((None, PAST_LEN, GQA_KV_COLS), lambda b: (b, 0, 0)),
                  pl.BlockSpec((None, PAST_LEN, GQA_KV_COLS), lambda b: (b, 0, 0))],
        out_specs=pl.BlockSpec((DEC_SEQ, D_MODEL), lambda b: (b, 0)),
        out_shape=jax.ShapeDtypeStruct((N_LAT, D_MODEL), F32),
        scratch_shapes=_gqa_scratch(DEC_SEQ, DEC_SEQ + PAST_LEN),
        compiler_params=_params("arbitrary"),
        name=f"gqa_latent_{layer}",
    )(x, ada, g1, w_qkv_lat, g_q, g_k, w_o, g_q_swap, g_k_swap,
      cos_q, sin_q, cos_k, sin_k, cache_k, cache_v)

    x_spec, lat_spec, out_specs = _ctx_specs((GQA_KV_COLS, GQA_KV_COLS))
    return pl.pallas_call(
        functools.partial(_gqa_kernel, latent=False),
        grid=(CTX_STEPS,),
        in_specs=[x_spec,
                  _ada_spec_at(layer, lambda i: 0),
                  _resident((1, D_MODEL)),
                  _resident(w_qkv_ctx.shape),
                  _resident((1, GQA_Q_COLS)),
                  _resident((1, GQA_KV_COLS)),
                  _resident((GQA_Q_COLS, D_MODEL)),
                  lat_spec],
        out_specs=out_specs,
        out_shape=[
            jax.ShapeDtypeStruct((N_TOK, D_MODEL), F32),
            jax.ShapeDtypeStruct((N_CTX, GQA_KV_COLS), F32),
            jax.ShapeDtypeStruct((N_CTX, GQA_KV_COLS), F32),
        ],
        scratch_shapes=_gqa_scratch(SEQ, SEQ),
        compiler_params=_params("arbitrary"),
        name=f"gqa_ctx_{layer}",
    )(x, ada, g1, w_qkv_ctx, g_q, g_k, w_o, lat_out)


ROUTE_TM = 512
MOE_TG = 256
MOE_MAX_TILES = 2 * N_TOK // MOE_TG + N_EXPERTS
MOE_ROWS = MOE_MAX_TILES * MOE_TG
MOE_FILLS = 2 * N_EXPERTS
MOE_FC = 1024
DISPATCH_TM = 1024
COMBINE_TM = 256


def _route_kernel(x_ref, ada_ref, g2_ref, wr_hi_ref, wr_lo_ref, h_ref, dest_ref, gatew_ref,
                  count_ref, carry_scr):
    tm = x_ref.shape[0]

    @pl.when(pl.program_id(0) == 0)
    def _():
        carry_scr[...] = jnp.zeros_like(carry_scr)

    h = _modulate(x_ref[...], g2_ref[...], ada_ref[3:4, :], ada_ref[4:5, :])
    h_ref[...] = h.reshape(tm, 1, D_MODEL)
    h_hi = h.astype(BF16)
    h_lo = (h - h_hi.astype(F32)).astype(BF16)
    logits = _dot(h_hi, wr_hi_ref[...]) + (_dot(h_lo, wr_hi_ref[...]) + _dot(h_hi, wr_lo_ref[...]))
    lane = lax.broadcasted_iota(jnp.int32, logits.shape, 1)
    neg = jnp.float32(-jnp.inf)
    l1 = jnp.where(lane < N_EXPERTS, logits, neg)
    m1 = jnp.max(l1, axis=-1, keepdims=True)
    i1 = jnp.min(jnp.where(l1 == m1, lane, LANES), axis=-1, keepdims=True)
    l2 = jnp.where(lane == i1, neg, l1)
    m2 = jnp.max(l2, axis=-1, keepdims=True)
    i2 = jnp.min(jnp.where(l2 == m2, lane, LANES), axis=-1, keepdims=True)
    e2 = jnp.exp(m2 - m1)
    w1 = 1.0 / (1.0 + e2)
    w2 = e2 / (1.0 + e2)
    member = jnp.where((lane == i1) | (lane == i2), 1.0, 0.0)
    r = lax.broadcasted_iota(jnp.int32, (tm, tm), 0)
    c = lax.broadcasted_iota(jnp.int32, (tm, tm), 1)
    tri = jnp.where(c < r, 1.0, 0.0).astype(BF16)
    before = _dot(tri, member.astype(BF16)) + carry_scr[0:1, :]
    rank1 = jnp.sum(jnp.where(lane == i1, before, 0.0), axis=-1, keepdims=True).astype(jnp.int32)
    rank2 = jnp.sum(jnp.where(lane == i2, before, 0.0), axis=-1, keepdims=True).astype(jnp.int32)
    dest_ref[...] = jnp.where(lane == 0, i1, jnp.where(lane == 1, i2, jnp.where(
        lane == 2, rank1, jnp.where(lane == 3, rank2, 0))))
    gatew_ref[...] = jnp.where(lane == 0, w1, jnp.where(lane == 1, w2, 0.0))
    carry_scr[...] = carry_scr[...] + jnp.sum(member, axis=0, keepdims=True)
    count_ref[...] = carry_scr[...]


def _route(x, ada, layer, g2, w_router):
    tm = ROUTE_TM
    w_router_hi = w_router.astype(BF16)
    return pl.pallas_call(
        _route_kernel,
        grid=(N_TOK // tm,),
        in_specs=[
            pl.BlockSpec((tm, D_MODEL), lambda i: (i, 0)),
            _ada_spec(layer, tm),
            _resident((1, D_MODEL)),
            _resident((D_MODEL, LANES)),
            _resident((D_MODEL, LANES)),
        ],
        out_specs=[
            pl.BlockSpec((tm, 1, D_MODEL), lambda i: (i, 0, 0)),
            pl.BlockSpec((tm, LANES), lambda i: (i, 0)),
            pl.BlockSpec((tm, LANES), lambda i: (i, 0)),
            pl.BlockSpec((8, LANES), lambda i: (0, 0)),
        ],
        out_shape=[
            jax.ShapeDtypeStruct((N_TOK, 1, D_MODEL), F32),
            jax.ShapeDtypeStruct((N_TOK, LANES), jnp.int32),
            jax.ShapeDtypeStruct((N_TOK, LANES), F32),
            jax.ShapeDtypeStruct((8, LANES), F32),
        ],
        scratch_shapes=[pltpu.VMEM((8, LANES), F32)],
        compiler_params=_params("arbitrary"),
        name=f"moe_route_{layer}",
    )(x, ada, g2, w_router_hi, (w_router - w_router_hi.astype(F32)).astype(BF16))


def _dispatch_kernel(d1_ref, d2_ref, fill_start_ref, fill_rows_ref, h_ref, hs_ref, zero_scr, sem):
    i = pl.program_id(0)
    tm = h_ref.shape[0]
    base = i * tm

    def fill_copy(k):
        n = fill_rows_ref[k]
        return pltpu.make_async_copy(zero_scr.at[pl.ds(0, n)],
                                     hs_ref.at[pl.ds(fill_start_ref[k], n)], sem.at[1])

    @pl.when(i == 0)
    def _():
        zero_scr[...] = jnp.zeros_like(zero_scr)
        for k in range(MOE_FILLS):
            @pl.when(fill_rows_ref[k] > 0)
            def _():
                fill_copy(k).start()
        for k in range(MOE_FILLS):
            @pl.when(fill_rows_ref[k] > 0)
            def _():
                fill_copy(k).wait()

    def row_copy(r, d_ref):
        return pltpu.make_async_copy(h_ref.at[r], hs_ref.at[d_ref[base + r]], sem.at[0])

    def issue(r, carry):
        row_copy(r, d1_ref).start(priority=0)
        row_copy(r, d2_ref).start(priority=1)
        return carry

    def drain(r, carry):
        row_copy(r, d1_ref).wait()
        row_copy(r, d2_ref).wait()
        return carry

    lax.fori_loop(0, tm, issue, 0)
    lax.fori_loop(0, tm, drain, 0)


def _dispatch(d1, d2, fill_start, fill_rows, h_rows):
    tm = DISPATCH_TM
    return pl.pallas_call(
        _dispatch_kernel,
        grid_spec=pltpu.PrefetchScalarGridSpec(
            num_scalar_prefetch=4,
            grid=(N_TOK // tm,),
            in_specs=[pl.BlockSpec((tm, 1, D_MODEL), lambda i, *_: (i, 0, 0))],
            out_specs=pl.BlockSpec(memory_space=pl.ANY),
            scratch_shapes=[pltpu.VMEM((MOE_TG, 1, D_MODEL), F32),
                            pltpu.SemaphoreType.DMA((2,))],
        ),
        out_shape=jax.ShapeDtypeStruct((MOE_ROWS, 1, D_MODEL), F32),
        compiler_params=_params("arbitrary"),
        name="moe_dispatch",
    )(d1, d2, fill_start, fill_rows, h_rows)


def _experts_kernel(te_ref, nt_ref, hs_ref, wg_ref, wu_ref, wd_ref, ys_ref, h2d_scr):
    used = pl.program_id(0) < nt_ref[0]

    @pl.when(jnp.logical_not(used))
    def _():
        ys_ref[...] = jnp.zeros_like(ys_ref)

    @pl.when(used)
    def _():
        h2d_scr[...] = hs_ref[...].reshape(MOE_TG, D_MODEL)
        h = h2d_scr[...].astype(BF16)
        y = None
        for lo in range(0, MOE_FF, MOE_FC):
            g = _dot(h, wg_ref[:, lo:lo + MOE_FC])
            u = _dot(h, wu_ref[:, lo:lo + MOE_FC])
            part = _dot((_silu(g) * u).astype(BF16), wd_ref[lo:lo + MOE_FC, :])
            y = part if y is None else y + part
        ys_ref[...] = y.reshape(MOE_TG, 1, D_MODEL)


def _experts(tile_expert, n_tiles, hs, m, w_gate, w_up, w_down):
    rows = pl.BlockSpec((MOE_TG, 1, D_MODEL), lambda t, te, nt: (t, 0, 0))
    return pl.pallas_call(
        _experts_kernel,
        grid_spec=pltpu.PrefetchScalarGridSpec(
            num_scalar_prefetch=2,
            grid=(MOE_MAX_TILES,),
            in_specs=[
                rows,
                pl.BlockSpec((None, None, D_MODEL, MOE_FF), lambda t, te, nt: (m, te[t], 0, 0)),
                pl.BlockSpec((None, None, D_MODEL, MOE_FF), lambda t, te, nt: (m, te[t], 0, 0)),
                pl.BlockSpec((None, None, MOE_FF, D_MODEL), lambda t, te, nt: (m, te[t], 0, 0)),
            ],
            out_specs=rows,
            scratch_shapes=[pltpu.VMEM((MOE_TG, D_MODEL), F32)],
        ),
        out_shape=jax.ShapeDtypeStruct((MOE_ROWS, 1, D_MODEL), F32),
        compiler_params=_params("arbitrary"),
        name="moe_experts",
    )(tile_expert, n_tiles, hs, w_gate, w_up, w_down)


def _combine_kernel(d1_ref, d2_ref, x_ref, ada_ref, gw_ref, ys_ref, o_ref,
                    ya0, yb0, ya1, yb1, ya2d, yb2d, sem):
    i = pl.program_id(0)
    tm = x_ref.shape[0]

    def row_copies(step, r, buf_a, buf_b, slot):
        t = step * tm + r
        return (pltpu.make_async_copy(ys_ref.at[d1_ref[t]], buf_a.at[r], sem.at[slot]),
                pltpu.make_async_copy(ys_ref.at[d2_ref[t]], buf_b.at[r], sem.at[slot]))

    def start_tile(step, buf_a, buf_b, slot):
        def body(r, carry):
            for priority, cp in enumerate(row_copies(step, r, buf_a, buf_b, slot)):
                cp.start(priority=priority)
            return carry
        lax.fori_loop(0, tm, body, 0)

    def wait_tile(step, buf_a, buf_b, slot):
        def body(r, carry):
            for cp in row_copies(step, r, buf_a, buf_b, slot):
                cp.wait()
            return carry
        lax.fori_loop(0, tm, body, 0)

    @pl.when(i == 0)
    def _():
        start_tile(0, ya0, yb0, 0)

    def run(cur, nxt):
        @pl.when(i + 1 < pl.num_programs(0))
        def _():
            start_tile(i + 1, *nxt)

        wait_tile(i, *cur)
        ya2d[...] = cur[0][...].reshape(tm, D_MODEL)
        yb2d[...] = cur[1][...].reshape(tm, D_MODEL)
        mix = gw_ref[:, 0:1] * ya2d[...] + gw_ref[:, 1:2] * yb2d[...]
        o_ref[...] = x_ref[...] + ada_ref[5:6, :] * mix

    @pl.when(i % 2 == 0)
    def _():
        run((ya0, yb0, 0), (ya1, yb1, 1))

    @pl.when(i % 2 == 1)
    def _():
        run((ya1, yb1, 1), (ya0, yb0, 0))


def _combine(d1, d2, x, ada, layer, gate_w, ys):
    tm = COMBINE_TM
    row_buf = pltpu.VMEM((tm, 1, D_MODEL), F32)
    return pl.pallas_call(
        _combine_kernel,
        grid_spec=pltpu.PrefetchScalarGridSpec(
            num_scalar_prefetch=2,
            grid=(N_TOK // tm,),
            in_specs=[
                pl.BlockSpec((tm, D_MODEL), lambda i, *_: (i, 0)),
                pl.BlockSpec((None, None, 6, D_MODEL),
                             lambda i, *_: (layer, _cond_of_tile(i, tm), 0, 0)),
                pl.BlockSpec((tm, LANES), lambda i, *_: (i, 0)),
                pl.BlockSpec(memory_space=pl.ANY),
            ],
            out_specs=pl.BlockSpec((tm, D_MODEL), lambda i, *_: (i, 0)),
            scratch_shapes=[row_buf, row_buf, row_buf, row_buf,
                            pltpu.VMEM((tm, D_MODEL), F32), pltpu.VMEM((tm, D_MODEL), F32),
                            pltpu.SemaphoreType.DMA((2,))],
        ),
        out_shape=jax.ShapeDtypeStruct((N_TOK, D_MODEL), F32),
        compiler_params=_params("arbitrary"),
        name="moe_combine",
    )(d1, d2, x, ada, gate_w, ys)


def _moe_layer(x, ada, layer, m, g2, w_router, w_gate, w_up, w_down):
    h_rows, route, gate_w, counts = _route(x, ada, layer, g2, w_router)
    cnt = counts[0, :N_EXPERTS].astype(jnp.int32)
    tiles_per_expert = (cnt + MOE_TG - 1) // MOE_TG
    tile_end = jnp.cumsum(tiles_per_expert)
    row_start = (tile_end - tiles_per_expert) * MOE_TG
    n_tiles = tile_end[-1:]
    t = jnp.minimum(jnp.arange(MOE_MAX_TILES, dtype=jnp.int32), n_tiles - 1)
    tile_expert = jnp.sum((t[:, None] >= tile_end[None, :]).astype(jnp.int32), axis=1)
    experts = jnp.arange(N_EXPERTS, dtype=jnp.int32)
    d1 = jnp.sum(jnp.where(route[:, 0:1] == experts, row_start, 0), axis=1) + route[:, 2]
    d2 = jnp.sum(jnp.where(route[:, 1:2] == experts, row_start, 0), axis=1) + route[:, 3]
    spare_tile = jnp.minimum(n_tiles + experts, MOE_MAX_TILES - 1)
    fill_start = jnp.concatenate([row_start + cnt, spare_tile * MOE_TG])
    fill_rows = jnp.concatenate([tiles_per_expert * MOE_TG - cnt,
                                 jnp.where(n_tiles + experts < MOE_MAX_TILES, MOE_TG, 0)])
    hs = _dispatch(d1, d2, fill_start, fill_rows, h_rows)
    ys = _experts(tile_expert, n_tiles, hs, m, w_gate, w_up, w_down)
    return _combine(d1, d2, x, ada, layer, gate_w, ys)


def _final_kernel(x_ref, g_ref, o_ref):
    o_ref[...] = _rms(x_ref[...]) * g_ref[...]


def _final_norm(x, g, row0, rows):
    tm = 1024
    blk0 = row0 // tm
    return pl.pallas_call(
        _final_kernel,
        grid=(rows // tm,),
        in_specs=[pl.BlockSpec((tm, D_MODEL), lambda i: (blk0 + i, 0)),
                  _resident((1, D_MODEL))],
        out_specs=pl.BlockSpec((tm, D_MODEL), lambda i: (i, 0)),
        out_shape=jax.ShapeDtypeStruct((rows, D_MODEL), F32),
        compiler_params=_params("arbitrary"),
        name="final_norm",
    )(x, g)


def _rope_tables(dim):
    half = dim // 2
    quarter = half // 2
    t = np.arange(DEC_SEQ)
    pos = np.stack([t // GRID_W, t % GRID_W], axis=1).astype(np.float32)
    freqs = (ROPE_THETA ** (-np.arange(quarter, dtype=np.float32) / quarter)).astype(np.float32)
    ang = pos[:, :, None] * freqs[None, None, :]
    cos = np.cos(ang.astype(np.float64))
    sin = np.sin(ang.astype(np.float64))
    cos_t = np.concatenate([cos, cos], axis=-1).reshape(DEC_SEQ, dim)
    sin_t = np.concatenate([-sin, sin], axis=-1).reshape(DEC_SEQ, dim)
    lane = np.arange(dim)
    swap = np.where(lane % half < quarter, lane + quarter, lane - quarter)
    return cos_t.astype(np.float32), sin_t.astype(np.float32), swap


def kernel(x_prompt, x_sample, cache_mla_ckv, cache_mla_kpe, cache_gqa_k, cache_gqa_v, c, c_ctx, w_ada, b_ada, norm1_g, norm2_g, conv_w_in, conv_k, conv_w_out, mla_w_down, mla_q_norm_g, mla_kv_norm_g, mla_w_uq, mla_w_ukv, mla_w_o, gqa_w_qkv, gqa_q_norm_g, gqa_k_norm_g, gqa_w_o, ffn_w_gate, ffn_w_up, ffn_w_down, moe_w_router, moe_w_gate, moe_w_up, moe_w_down, final_norm_g):
    xs = (x_prompt.reshape(N_CTX, D_MODEL), x_sample.reshape(N_LAT, D_MODEL))
    conv_w_in, conv_w_out = conv_w_in.astype(BF16), conv_w_out.astype(BF16)
    ffn_w_gate, ffn_w_up, ffn_w_down = (w.astype(BF16) for w in (ffn_w_gate, ffn_w_up, ffn_w_down))
    moe_w_gate, moe_w_up, moe_w_down = (w.astype(BF16) for w in (moe_w_gate, moe_w_up, moe_w_down))
    cond = jnp.concatenate(
        [c_ctx[None, :], c, jnp.zeros((N_COND - 1 - DEC_BATCH, D_MODEL), F32)], axis=0)
    ada = _adaln_table(cond, w_ada, b_ada)

    new_ckv, new_kpe, new_k, new_v = [], [], [], []
    for i in range(DEPTH):
        j = i // N_MIXERS
        kind = i % N_MIXERS
        g1 = norm1_g[i][None, :]
        if kind == 0:
            x = _conv_layer(xs if i == 0 else (x,), ada, i, j, g1, conv_w_in, conv_k, conv_w_out)
        elif kind == 1:
            cos64, sin64, swap64 = _rope_tables(MLA_ROPE)
            wd = mla_w_down[j]
            kpe_cols = wd[:, MLA_Q_RANK + MLA_KV_RANK:]
            w_down = jnp.concatenate([wd, kpe_cols[:, swap64]], axis=1).astype(BF16)
            wq = mla_w_uq[j].reshape(MLA_Q_RANK, MLA_HEADS, MLA_QK)
            wq_nope = wq[:, :, :MLA_NOPE].reshape(MLA_Q_RANK, MLA_Q_NOPE_COLS)
            wq_rope = wq[:, :, MLA_NOPE:]
            w_uq = jnp.concatenate(
                [wq_nope, wq_rope.reshape(MLA_Q_RANK, MLA_Q_ROPE_COLS),
                 wq_rope[:, :, swap64].reshape(MLA_Q_RANK, MLA_Q_ROPE_COLS)], axis=1).astype(BF16)
            wkv = mla_w_ukv[j].reshape(MLA_KV_RANK, MLA_HEADS, MLA_NOPE + MLA_V)
            w_ukv = jnp.concatenate(
                [wkv[:, :, :MLA_NOPE].reshape(MLA_KV_RANK, MLA_Q_NOPE_COLS),
                 wkv[:, :, MLA_NOPE:].reshape(MLA_KV_RANK, MLA_HEADS * MLA_V)], axis=1).astype(BF16)
            rope = (jnp.asarray(np.tile(cos64, (1, MLA_HEADS))), jnp.asarray(np.tile(sin64, (1, MLA_HEADS))),
                    jnp.asarray(cos64), jnp.asarray(sin64))
            x, ckv_p, kpe_p = _mla_layer(
                x, ada, i, g1, w_down, mla_q_norm_g[j][None, :], mla_kv_norm_g[j][None, :],
                w_uq, w_ukv, mla_w_o[j].astype(BF16), cache_mla_ckv[:, j], cache_mla_kpe[:, j], rope)
            new_ckv.append(ckv_p.reshape(BATCH, SEQ, MLA_KV_RANK))
            new_kpe.append(kpe_p.reshape(BATCH, SEQ, MLA_ROPE))
        else:
            cos128, sin128, swap128 = _rope_tables(GQA_HEAD_DIM)
            wqkv = gqa_w_qkv[j]
            wq = wqkv[:, :GQA_Q_COLS].reshape(D_MODEL, GQA_HEADS, GQA_HEAD_DIM)
            wk = wqkv[:, GQA_Q_COLS:GQA_Q_COLS + GQA_KV_COLS].reshape(D_MODEL, GQA_KV_HEADS, GQA_HEAD_DIM)
            w_qkv_ctx = wqkv.astype(BF16)
            w_qkv_lat = jnp.concatenate(
                [wqkv, wq[:, :, swap128].reshape(D_MODEL, GQA_Q_COLS),
                 wk[:, :, swap128].reshape(D_MODEL, GQA_KV_COLS)], axis=1).astype(BF16)
            gq = gqa_q_norm_g[j]
            gk = gqa_k_norm_g[j]
            rope = (jnp.asarray(np.tile(cos128, (1, GQA_HEADS))), jnp.asarray(np.tile(sin128, (1, GQA_HEADS))),
                    jnp.asarray(np.tile(cos128, (1, GQA_KV_HEADS))), jnp.asarray(np.tile(sin128, (1, GQA_KV_HEADS))))
            x, k_p, v_p = _gqa_layer(
                x, ada, i, g1, w_qkv_ctx, w_qkv_lat,
                jnp.tile(gq, GQA_HEADS)[None, :], jnp.tile(gk, GQA_KV_HEADS)[None, :],
                jnp.tile(gq[swap128], GQA_HEADS)[None, :], jnp.tile(gk[swap128], GQA_KV_HEADS)[None, :],
                gqa_w_o[j].astype(BF16),
                cache_gqa_k[:, j].reshape(DEC_BATCH, PAST_LEN, GQA_KV_COLS),
                cache_gqa_v[:, j].reshape(DEC_BATCH, PAST_LEN, GQA_KV_COLS), rope)
            new_k.append(k_p.reshape(BATCH, SEQ, GQA_KV_HEADS, GQA_HEAD_DIM))
            new_v.append(v_p.reshape(BATCH, SEQ, GQA_KV_HEADS, GQA_HEAD_DIM))

        m = i // 2
        g2 = norm2_g[i][None, :]
        if i % 2 == 0:
            x = _ffn_layer(x, ada, i, m, g2, ffn_w_gate, ffn_w_up, ffn_w_down)
        else:
            w_router = jnp.pad(moe_w_router[m], ((0, 0), (0, LANES - N_EXPERTS)))
            x = _moe_layer(x, ada, i, m, g2, w_router, moe_w_gate, moe_w_up, moe_w_down)

    g_final = final_norm_g[None, :]
    y_prompt = _final_norm(x, g_final, 0, N_CTX).reshape(BATCH, SEQ, D_MODEL)
    y_sample = _final_norm(x, g_final, N_CTX, N_LAT).reshape(DEC_BATCH, DEC_SEQ, D_MODEL)
    return (y_prompt, y_sample,
            jnp.stack(new_ckv, axis=1), jnp.stack(new_kpe, axis=1),
            jnp.stack(new_k, axis=1), jnp.stack(new_v, axis=1))
```

```python
import functools

import numpy as np
import jax
import jax.numpy as jnp
from jax import lax
from jax.experimental import pallas as pl
from jax.experimental.pallas import tpu as pltpu

D_MODEL = 1024
BATCH = 32
SEQ = 256
DEPTH = 4
DEC_BATCH = 2
DEC_SEQ = 1024
PAST_LEN = 512
GRID_W = 64
N_MIXERS = 3
CONV_WIDTH = 3
MLA_HEADS = 8
MLA_NOPE = 128
MLA_ROPE = 64
MLA_V = 128
MLA_Q_RANK = 384
MLA_KV_RANK = 256
GQA_HEADS = 8
GQA_KV_HEADS = 2
GQA_HEAD_DIM = 128
D_FF = 2816
N_EXPERTS = 8
MOE_FF = 2048
ROPE_THETA = 10000.0
EPS = 1e-6

N_CTX = BATCH * SEQ
N_LAT = DEC_BATCH * DEC_SEQ
N_TOK = N_CTX + N_LAT
N_COND = 8
LANES = 128
Q_TILE = 256
VMEM_LIMIT = 56 * 1024 * 1024

F32 = jnp.float32
BF16 = jnp.bfloat16


def _dot(a, b):
    return jnp.dot(a, b, preferred_element_type=F32)


def _dot_t(a, b):
    return lax.dot_general(a, b, (((1,), (1,)), ((), ())), preferred_element_type=F32)


def _rms(x):
    return x * lax.rsqrt(jnp.mean(x * x, axis=-1, keepdims=True) + EPS)


def _modulate(x, g, shift, scale):
    return _rms(x) * g * (1.0 + scale) + shift


def _silu(x):
    return x * jax.nn.sigmoid(x)


def _cond_of_tile(i, tm):
    start = i * tm
    return jnp.where(start < N_CTX, 0, 1 + (start - N_CTX) // DEC_SEQ)


def _resident(shape):
    return pl.BlockSpec(shape, lambda *_: (0,) * len(shape), pipeline_mode=pl.Buffered(1))


def _resident_at(index, shape):
    return pl.BlockSpec((None,) + tuple(shape), lambda *_: (index,) + (0,) * len(shape),
                        pipeline_mode=pl.Buffered(1))


def _params(*sem):
    return pltpu.CompilerParams(dimension_semantics=sem, vmem_limit_bytes=VMEM_LIMIT)


def _adaln_kernel(cond_ref, w_ref, b_ref, o_ref):
    s = _silu(cond_ref[...]).astype(BF16)
    o_ref[...] = _dot(s, w_ref[...].astype(BF16)) + b_ref[...]


def _adaln_table(cond, w_ada, b_ada):
    tn = 1536
    out = pl.pallas_call(
        _adaln_kernel,
        grid=(DEPTH, 6 * D_MODEL // tn),
        in_specs=[
            pl.BlockSpec((N_COND, D_MODEL), lambda l, j: (0, 0)),
            pl.BlockSpec((None, D_MODEL, tn), lambda l, j: (l, 0, j)),
            pl.BlockSpec((None, 1, tn), lambda l, j: (l, 0, j)),
        ],
        out_specs=pl.BlockSpec((None, N_COND, tn), lambda l, j: (l, 0, j)),
        out_shape=jax.ShapeDtypeStruct((DEPTH, N_COND, 6 * D_MODEL), F32),
        compiler_params=_params("arbitrary", "arbitrary"),
        name="adaln_table",
    )(cond, w_ada, b_ada.reshape(DEPTH, 1, 6 * D_MODEL))
    return out.reshape(DEPTH, N_COND, 6, D_MODEL)


def _ada_spec(layer, tm):
    return pl.BlockSpec((None, None, 6, D_MODEL), lambda i: (layer, _cond_of_tile(i, tm), 0, 0))


def _ada_spec_at(layer, cond_fn):
    return pl.BlockSpec((None, None, 6, D_MODEL), lambda i: (layer, cond_fn(i), 0, 0))


CONV_TM = 1024
CONV_CC = 256


def _conv_kernel(*refs, split):
    if split:
        xp_ref, x_ref, ada_ref, g1_ref, win_ref, ck_ref, wout_ref, o_ref, v_scr = refs
    else:
        x_ref, ada_ref, g1_ref, win_ref, ck_ref, wout_ref, o_ref, v_scr = refs
    tm = x_ref.shape[0]
    i = pl.program_id(0)
    x = x_ref[...]
    if split:
        x = jnp.where(i * tm < N_CTX, xp_ref[...], x)
    h = _modulate(x, g1_ref[...], ada_ref[0:1, :], ada_ref[1:2, :]).astype(BF16)
    period = jnp.where(i * tm < N_CTX, SEQ, DEC_SEQ)
    pos = lax.broadcasted_iota(jnp.int32, (tm, 1), 0) & (period - 1)
    first = pos == 0
    last = pos == period - 1
    for j in range(D_MODEL // CONV_CC):
        lo = j * CONV_CC
        b_gate = _dot(h, win_ref[:, lo:lo + CONV_CC])
        c_gate = _dot(h, win_ref[:, D_MODEL + lo:D_MODEL + lo + CONV_CC])
        x_in = _dot(h, win_ref[:, 2 * D_MODEL + lo:2 * D_MODEL + lo + CONV_CC])
        u = c_gate * x_in
        u_prev = jnp.where(first, 0.0, pltpu.roll(u, 1, 0))
        u_next = jnp.where(last, 0.0, pltpu.roll(u, tm - 1, 0))
        conv = (ck_ref[0:1, lo:lo + CONV_CC] * u_prev + ck_ref[1:2, lo:lo + CONV_CC] * u
                + ck_ref[2:3, lo:lo + CONV_CC] * u_next)
        v_scr[:, lo:lo + CONV_CC] = (b_gate * conv).astype(BF16)
    o_ref[...] = x + ada_ref[2:3, :] * _dot(v_scr[...], wout_ref[...])


def _conv_layer(xs, ada, layer, j, g1, w_in, conv_k, w_out):
    tm = CONV_TM
    ctx_tiles = N_CTX // tm
    if len(xs) == 1:
        x_specs = [pl.BlockSpec((tm, D_MODEL), lambda i: (i, 0))]
    else:
        x_specs = [pl.BlockSpec((tm, D_MODEL), lambda i: (jnp.minimum(i, ctx_tiles - 1), 0)),
                   pl.BlockSpec((tm, D_MODEL), lambda i: (jnp.maximum(i - ctx_tiles, 0), 0))]
    return pl.pallas_call(
        functools.partial(_conv_kernel, split=len(xs) == 2),
        grid=(N_TOK // tm,),
        in_specs=x_specs + [
            _ada_spec(layer, tm),
            _resident((1, D_MODEL)),
            _resident_at(j, (D_MODEL, 3 * D_MODEL)),
            _resident_at(j, (CONV_WIDTH, D_MODEL)),
            _resident_at(j, (D_MODEL, D_MODEL)),
        ],
        out_specs=pl.BlockSpec((tm, D_MODEL), lambda i: (i, 0)),
        out_shape=jax.ShapeDtypeStruct((N_TOK, D_MODEL), F32),
        scratch_shapes=[pltpu.VMEM((tm, D_MODEL), BF16)],
        compiler_params=_params("arbitrary"),
        name=f"conv_mixer_{layer}",
    )(*xs, ada, g1, w_in, conv_k, w_out)


FFN_TM = 512
FFN_CHUNKS = ((0, 1536), (1536, D_FF))


def _ffn_kernel(x_ref, ada_ref, g2_ref, wg_ref, wu_ref, wd_ref, o_ref):
    x = x_ref[...]
    h = _modulate(x, g2_ref[...], ada_ref[3:4, :], ada_ref[4:5, :]).astype(BF16)
    f = None
    for lo, hi in FFN_CHUNKS:
        g = _dot(h, wg_ref[:, lo:hi])
        u = _dot(h, wu_ref[:, lo:hi])
        part = _dot((_silu(g) * u).astype(BF16), wd_ref[lo:hi, :])
        f = part if f is None else f + part
    o_ref[...] = x + ada_ref[5:6, :] * f


def _ffn_layer(x, ada, layer, m, g2, w_gate, w_up, w_down):
    tm = FFN_TM
    return pl.pallas_call(
        _ffn_kernel,
        grid=(N_TOK // tm,),
        in_specs=[
            pl.BlockSpec((tm, D_MODEL), lambda i: (i, 0)),
            _ada_spec(layer, tm),
            _resident((1, D_MODEL)),
            _resident_at(m, (D_MODEL, D_FF)),
            _resident_at(m, (D_MODEL, D_FF)),
            _resident_at(m, (D_FF, D_MODEL)),
        ],
        out_specs=pl.BlockSpec((tm, D_MODEL), lambda i: (i, 0)),
        out_shape=jax.ShapeDtypeStruct((N_TOK, D_MODEL), F32),
        compiler_params=_params("arbitrary"),
        name=f"dense_ffn_{layer}",
    )(x, ada, g2, w_gate, w_up, w_down)


def _attention(qk_parts, v_ref, heads, kv_of_head, dv, scale, o_scr, finish):
    def q_block(rows):
        for h in range(heads):
            g = kv_of_head(h)
            s = None
            for q_ref, k_ref, width, k_shared in qk_parts:
                kb = 0 if k_shared else g
                part = _dot_t(q_ref[rows, h * width:(h + 1) * width],
                              k_ref[:, kb * width:(kb + 1) * width])
                s = part if s is None else s + part
            s = s * scale
            e = jnp.exp(s - jnp.max(s, axis=-1, keepdims=True))
            denom = jnp.sum(e, axis=-1, keepdims=True)
            o = _dot(e.astype(BF16), v_ref[:, g * dv:(g + 1) * dv]) / denom
            o_scr[rows, h * dv:(h + 1) * dv] = o.astype(BF16)
        finish(rows)

    _for_row_blocks(o_scr.shape[0], q_block)


def _for_row_blocks(n_rows, fn):
    if n_rows == Q_TILE:
        fn(pl.ds(0, Q_TILE))
    else:
        def body(b, carry):
            fn(pl.ds(pl.multiple_of(b * Q_TILE, Q_TILE), Q_TILE))
            return carry
        lax.fori_loop(0, n_rows // Q_TILE, body, 0)


MLA_QK = MLA_NOPE + MLA_ROPE
MLA_DOWN_COLS = MLA_Q_RANK + MLA_KV_RANK + 2 * MLA_ROPE
MLA_Q_NOPE_COLS = MLA_HEADS * MLA_NOPE
MLA_Q_ROPE_COLS = MLA_HEADS * MLA_ROPE


def _ctx_then_latent(body, n_in, refs):
    ins, lat_ref, rest = refs[:n_in], refs[n_in], refs[n_in + 1:]
    o_ref = rest[0]
    i = pl.program_id(0)

    @pl.when(i < BATCH)
    def _():
        body(*ins, *rest, latent=False)

    @pl.when(i >= BATCH)
    def _():
        o_ref[...] = lat_ref[...]


def _ctx_specs(n_out_cols):
    seq = lambda i: (jnp.minimum(i, BATCH - 1), 0)
    x_spec = pl.BlockSpec((SEQ, D_MODEL), seq)
    lat_spec = pl.BlockSpec((SEQ, D_MODEL), lambda i: (jnp.maximum(i - BATCH, 0), 0))
    out_specs = [pl.BlockSpec((SEQ, D_MODEL), lambda i: (i, 0))]
    out_specs += [pl.BlockSpec((SEQ, n), seq) for n in n_out_cols]
    return x_spec, lat_spec, out_specs


CTX_STEPS = BATCH + N_LAT // SEQ


def _mla_kernel(*refs, latent):
    if latent:
        _mla_body(*refs, latent=True)
    else:
        _ctx_then_latent(_mla_body, 9, refs)


def _mla_body(*refs, latent):
    if latent:
        (x_ref, ada_ref, g1_ref, wdown_ref, gq_ref, gkv_ref, wuq_ref, wukv_ref, wo_ref,
         cosq_ref, sinq_ref, cosk_ref, sink_ref, cckv_ref, ckpe_ref,
         o_ref, qn_scr, qr_scr, kn_scr, kr_scr, v_scr, o_scr) = refs
    else:
        (x_ref, ada_ref, g1_ref, wdown_ref, gq_ref, gkv_ref, wuq_ref, wukv_ref, wo_ref,
         o_ref, ckv_out_ref, kpe_out_ref, qn_scr, qr_scr, kn_scr, kr_scr, v_scr, o_scr) = refs
    tm = x_ref.shape[0]
    kpe_lo = MLA_Q_RANK + MLA_KV_RANK

    def project(rows):
        h = _modulate(x_ref[rows, :], g1_ref[...], ada_ref[0:1, :], ada_ref[1:2, :]).astype(BF16)
        down = _dot(h, wdown_ref[...])
        c_q = down[:, :MLA_Q_RANK]
        c_kv = down[:, MLA_Q_RANK:kpe_lo]
        kpe = down[:, kpe_lo:kpe_lo + MLA_ROPE]
        q = _dot((_rms(c_q) * gq_ref[...]).astype(BF16), wuq_ref[...])
        ckv_n = _rms(c_kv) * gkv_ref[...]
        kv = _dot(ckv_n.astype(BF16), wukv_ref[...])
        qn_scr[rows, :] = q[:, :MLA_Q_NOPE_COLS].astype(BF16)
        q_rope = q[:, MLA_Q_NOPE_COLS:MLA_Q_NOPE_COLS + MLA_Q_ROPE_COLS]
        kn_scr[rows, :] = kv[:, :MLA_Q_NOPE_COLS].astype(BF16)
        v_scr[rows, :] = kv[:, MLA_Q_NOPE_COLS:].astype(BF16)
        if latent:
            q_swap = q[:, MLA_Q_NOPE_COLS + MLA_Q_ROPE_COLS:]
            q_rope = q_rope * cosq_ref[rows, :] + q_swap * sinq_ref[rows, :]
            kpe_swap = down[:, kpe_lo + MLA_ROPE:kpe_lo + 2 * MLA_ROPE]
            kpe = kpe * cosk_ref[rows, :] + kpe_swap * sink_ref[rows, :]
        else:
            ckv_out_ref[rows, :] = ckv_n
            kpe_out_ref[rows, :] = kpe
        qr_scr[rows, :] = q_rope.astype(BF16)
        kr_scr[rows, :] = kpe.astype(BF16)

    _for_row_blocks(tm, project)

    if latent:
        def expand_cache(rows):
            kv_c = _dot(cckv_ref[rows, :].astype(BF16), wukv_ref[...])
            dst = pl.ds(pl.multiple_of(tm + rows.start, Q_TILE), Q_TILE)
            kn_scr[dst, :] = kv_c[:, :MLA_Q_NOPE_COLS].astype(BF16)
            v_scr[dst, :] = kv_c[:, MLA_Q_NOPE_COLS:].astype(BF16)
            kr_scr[dst, :] = ckpe_ref[rows, :].astype(BF16)

        _for_row_blocks(PAST_LEN, expand_cache)

    def finish(rows):
        o_ref[rows, :] = x_ref[rows, :] + ada_ref[2:3, :] * _dot(o_scr[rows, :], wo_ref[...])

    _attention([(qn_scr, kn_scr, MLA_NOPE, False), (qr_scr, kr_scr, MLA_ROPE, True)],
               v_scr, MLA_HEADS, lambda hd: hd, MLA_V, MLA_QK ** -0.5, o_scr, finish)


def _mla_scratch(tm, sk):
    return [
        pltpu.VMEM((tm, MLA_Q_NOPE_COLS), BF16),
        pltpu.VMEM((tm, MLA_Q_ROPE_COLS), BF16),
        pltpu.VMEM((sk, MLA_Q_NOPE_COLS), BF16),
        pltpu.VMEM((sk, MLA_ROPE), BF16),
        pltpu.VMEM((sk, MLA_HEADS * MLA_V), BF16),
        pltpu.VMEM((tm, MLA_HEADS * MLA_V), BF16),
    ]


def _mla_layer(x, ada, layer, g1, w_down, g_q, g_kv, w_uq, w_ukv, w_o,
               cache_ckv, cache_kpe, rope):
    n_uq = w_uq.shape[1]
    weights = [
        _resident((1, D_MODEL)),
        _resident((D_MODEL, MLA_DOWN_COLS)),
        _resident((1, MLA_Q_RANK)),
        _resident((1, MLA_KV_RANK)),
        _resident((MLA_Q_RANK, n_uq)),
        _resident((MLA_KV_RANK, 2 * MLA_Q_NOPE_COLS)),
        _resident((MLA_HEADS * MLA_V, D_MODEL)),
    ]
    weight_args = (g1, w_down, g_q, g_kv, w_uq, w_ukv, w_o)
    cos_q, sin_q, cos_k, sin_k = rope
    ctx_blocks = N_CTX // DEC_SEQ
    lat_out = pl.pallas_call(
        functools.partial(_mla_kernel, latent=True),
        grid=(DEC_BATCH,),
        in_specs=[pl.BlockSpec((DEC_SEQ, D_MODEL), lambda b: (ctx_blocks + b, 0)),
                  _ada_spec_at(layer, lambda b: 1 + b)] + weights + [
            _resident((DEC_SEQ, MLA_Q_ROPE_COLS)),
            _resident((DEC_SEQ, MLA_Q_ROPE_COLS)),
            _resident((DEC_SEQ, MLA_ROPE)),
            _resident((DEC_SEQ, MLA_ROPE)),
            pl.BlockSpec((None, PAST_LEN, MLA_KV_RANK), lambda b: (b, 0, 0)),
            pl.BlockSpec((None, PAST_LEN, MLA_ROPE), lambda b: (b, 0, 0)),
        ],
        out_specs=pl.BlockSpec((DEC_SEQ, D_MODEL), lambda b: (b, 0)),
        out_shape=jax.ShapeDtypeStruct((N_LAT, D_MODEL), F32),
        scratch_shapes=_mla_scratch(DEC_SEQ, DEC_SEQ + PAST_LEN),
        compiler_params=_params("arbitrary"),
        name=f"mla_latent_{layer}",
    )(x, ada, *weight_args, cos_q, sin_q, cos_k, sin_k, cache_ckv, cache_kpe)

    x_spec, lat_spec, out_specs = _ctx_specs((MLA_KV_RANK, MLA_ROPE))
    return pl.pallas_call(
        functools.partial(_mla_kernel, latent=False),
        grid=(CTX_STEPS,),
        in_specs=[x_spec, _ada_spec_at(layer, lambda i: 0)] + weights + [lat_spec],
        out_specs=out_specs,
        out_shape=[
            jax.ShapeDtypeStruct((N_TOK, D_MODEL), F32),
            jax.ShapeDtypeStruct((N_CTX, MLA_KV_RANK), F32),
            jax.ShapeDtypeStruct((N_CTX, MLA_ROPE), F32),
        ],
        scratch_shapes=_mla_scratch(SEQ, SEQ),
        compiler_params=_params("arbitrary"),
        name=f"mla_ctx_{layer}",
    )(x, ada, *weight_args, lat_out)


GQA_Q_COLS = GQA_HEADS * GQA_HEAD_DIM
GQA_KV_COLS = GQA_KV_HEADS * GQA_HEAD_DIM
GQA_REP = GQA_HEADS // GQA_KV_HEADS


def _head_rms(t, n_heads):
    cols = []
    for hd in range(n_heads):
        th = t[:, hd * GQA_HEAD_DIM:(hd + 1) * GQA_HEAD_DIM]
        r = lax.rsqrt(jnp.mean(th * th, axis=-1, keepdims=True) + EPS)
        cols.append(jnp.broadcast_to(r, th.shape))
    return jnp.concatenate(cols, axis=-1)


def _gqa_kernel(*refs, latent):
    if latent:
        _gqa_body(*refs, latent=True)
    else:
        _ctx_then_latent(_gqa_body, 7, refs)


def _gqa_body(*refs, latent):
    if latent:
        (x_ref, ada_ref, g1_ref, wqkv_ref, gq_ref, gk_ref, wo_ref,
         gqs_ref, gks_ref, cosq_ref, sinq_ref, cosk_ref, sink_ref, ck_ref, cv_ref,
         o_ref, q_scr, k_scr, v_scr, o_scr) = refs
    else:
        (x_ref, ada_ref, g1_ref, wqkv_ref, gq_ref, gk_ref, wo_ref,
         o_ref, k_out_ref, v_out_ref, q_scr, k_scr, v_scr, o_scr) = refs
    tm = x_ref.shape[0]

    def project(rows):
        h = _modulate(x_ref[rows, :], g1_ref[...], ada_ref[0:1, :], ada_ref[1:2, :]).astype(BF16)
        qkv = _dot(h, wqkv_ref[...])
        q_raw = qkv[:, :GQA_Q_COLS]
        k_raw = qkv[:, GQA_Q_COLS:GQA_Q_COLS + GQA_KV_COLS]
        v = qkv[:, GQA_Q_COLS + GQA_KV_COLS:GQA_Q_COLS + 2 * GQA_KV_COLS]
        q_r = _head_rms(q_raw, GQA_HEADS)
        k_r = _head_rms(k_raw, GQA_KV_HEADS)
        q = q_raw * q_r * gq_ref[...]
        k = k_raw * k_r * gk_ref[...]
        if latent:
            lo = GQA_Q_COLS + 2 * GQA_KV_COLS
            q_swap = qkv[:, lo:lo + GQA_Q_COLS] * q_r * gqs_ref[...]
            k_swap = qkv[:, lo + GQA_Q_COLS:lo + GQA_Q_COLS + GQA_KV_COLS] * k_r * gks_ref[...]
            q = q * cosq_ref[rows, :] + q_swap * sinq_ref[rows, :]
            k = k * cosk_ref[rows, :] + k_swap * sink_ref[rows, :]
        else:
            k_out_ref[rows, :] = k
            v_out_ref[rows, :] = v
        q_scr[rows, :] = q.astype(BF16)
        k_scr[rows, :] = k.astype(BF16)
        v_scr[rows, :] = v.astype(BF16)

    _for_row_blocks(tm, project)

    if latent:
        k_scr[tm:, :] = ck_ref[...].astype(BF16)
        v_scr[tm:, :] = cv_ref[...].astype(BF16)

    def finish(rows):
        o_ref[rows, :] = x_ref[rows, :] + ada_ref[2:3, :] * _dot(o_scr[rows, :], wo_ref[...])

    _attention([(q_scr, k_scr, GQA_HEAD_DIM, False)], v_scr, GQA_HEADS,
               lambda hd: hd // GQA_REP, GQA_HEAD_DIM, GQA_HEAD_DIM ** -0.5, o_scr, finish)


def _gqa_scratch(tm, sk):
    return [
        pltpu.VMEM((tm, GQA_Q_COLS), BF16),
        pltpu.VMEM((sk, GQA_KV_COLS), BF16),
        pltpu.VMEM((sk, GQA_KV_COLS), BF16),
        pltpu.VMEM((tm, GQA_Q_COLS), BF16),
    ]


def _gqa_layer(x, ada, layer, g1, w_qkv_ctx, w_qkv_lat, g_q, g_k, g_q_swap, g_k_swap, w_o,
               cache_k, cache_v, rope):
    cos_q, sin_q, cos_k, sin_k = rope
    ctx_blocks = N_CTX // DEC_SEQ
    lat_out = pl.pallas_call(
        functools.partial(_gqa_kernel, latent=True),
        grid=(DEC_BATCH,),
        in_specs=[pl.BlockSpec((DEC_SEQ, D_MODEL), lambda b: (ctx_blocks + b, 0)),
                  _ada_spec_at(layer, lambda b: 1 + b),
                  _resident((1, D_MODEL)),
                  _resident(w_qkv_lat.shape),
                  _resident((1, GQA_Q_COLS)),
                  _resident((1, GQA_KV_COLS)),
                  _resident((GQA_Q_COLS, D_MODEL)),
                  _resident((1, GQA_Q_COLS)),
                  _resident((1, GQA_KV_COLS)),
                  _resident((DEC_SEQ, GQA_Q_COLS)),
                  _resident((DEC_SEQ, GQA_Q_COLS)),
                  _resident((DEC_SEQ, GQA_KV_COLS)),
                  _resident((DEC_SEQ, GQA_KV_COLS)),
                  pl.BlockSpec((None, PAST_LEN, GQA_KV_COLS), lambda b: (b, 0, 0)),
                  pl.BlockSpec((None, PAST_LEN, GQA_KV_COLS), lambda b: (b, 0, 0))],
        out_specs=pl.BlockSpec((DEC_SEQ, D_MODEL), lambda b: (b, 0)),
        out_shape=jax.ShapeDtypeStruct((N_LAT, D_MODEL), F32),
        scratch_shapes=_gqa_scratch(DEC_SEQ, DEC_SEQ + PAST_LEN),
        compiler_params=_params("arbitrary"),
        name=f"gqa_latent_{layer}",
    )(x, ada, g1, w_qkv_lat, g_q, g_k, w_o, g_q_swap, g_k_swap,
      cos_q, sin_q, cos_k, sin_k, cache_k, cache_v)

    x_spec, lat_spec, out_specs = _ctx_specs((GQA_KV_COLS, GQA_KV_COLS))
    return pl.pallas_call(
        functools.partial(_gqa_kernel, latent=False),
        grid=(CTX_STEPS,),
        in_specs=[x_spec,
                  _ada_spec_at(layer, lambda i: 0),
                  _resident((1, D_MODEL)),
                  _resident(w_qkv_ctx.shape),
                  _resident((1, GQA_Q_COLS)),
                  _resident((1, GQA_KV_COLS)),
                  _resident((GQA_Q_COLS, D_MODEL)),
                  lat_spec],
        out_specs=out_specs,
        out_shape=[
            jax.ShapeDtypeStruct((N_TOK, D_MODEL), F32),
            jax.ShapeDtypeStruct((N_CTX, GQA_KV_COLS), F32),
            jax.ShapeDtypeStruct((N_CTX, GQA_KV_COLS), F32),
        ],
        scratch_shapes=_gqa_scratch(SEQ, SEQ),
        compiler_params=_params("arbitrary"),
        name=f"gqa_ctx_{layer}",
    )(x, ada, g1, w_qkv_ctx, g_q, g_k, w_o, lat_out)


ROUTE_TM = 512
MOE_TG = 256
MOE_MAX_TILES = 2 * N_TOK // MOE_TG + N_EXPERTS
MOE_ROWS = MOE_MAX_TILES * MOE_TG
MOE_FILLS = 2 * N_EXPERTS
MOE_FC = 1024
COMBINE_TM = 256


def _route_kernel(x_ref, ada_ref, g2_ref, wr_hi_ref, wr_lo_ref, h_ref, dest_ref, gatew_ref,
                  count_ref, carry_scr):
    tm = x_ref.shape[0]

    @pl.when(pl.program_id(0) == 0)
    def _():
        carry_scr[...] = jnp.zeros_like(carry_scr)

    h = _modulate(x_ref[...], g2_ref[...], ada_ref[3:4, :], ada_ref[4:5, :])
    h_ref[...] = h.reshape(tm, 1, D_MODEL)
    h_hi = h.astype(BF16)
    h_lo = (h - h_hi.astype(F32)).astype(BF16)
    logits = _dot(h_hi, wr_hi_ref[...]) + (_dot(h_lo, wr_hi_ref[...]) + _dot(h_hi, wr_lo_ref[...]))
    lane = lax.broadcasted_iota(jnp.int32, logits.shape, 1)
    neg = jnp.float32(-jnp.inf)
    l1 = jnp.where(lane < N_EXPERTS, logits, neg)
    m1 = jnp.max(l1, axis=-1, keepdims=True)
    i1 = jnp.min(jnp.where(l1 == m1, lane, LANES), axis=-1, keepdims=True)
    l2 = jnp.where(lane == i1, neg, l1)
    m2 = jnp.max(l2, axis=-1, keepdims=True)
    i2 = jnp.min(jnp.where(l2 == m2, lane, LANES), axis=-1, keepdims=True)
    e2 = jnp.exp(m2 - m1)
    w1 = 1.0 / (1.0 + e2)
    w2 = e2 / (1.0 + e2)
    member = jnp.where((lane == i1) | (lane == i2), 1.0, 0.0)
    r = lax.broadcasted_iota(jnp.int32, (tm, tm), 0)
    c = lax.broadcasted_iota(jnp.int32, (tm, tm), 1)
    tri = jnp.where(c < r, 1.0, 0.0).astype(BF16)
    before = _dot(tri, member.astype(BF16)) + carry_scr[0:1, :]
    rank1 = jnp.sum(jnp.where(lane == i1, before, 0.0), axis=-1, keepdims=True).astype(jnp.int32)
    rank2 = jnp.sum(jnp.where(lane == i2, before, 0.0), axis=-1, keepdims=True).astype(jnp.int32)
    dest_ref[...] = jnp.where(lane == 0, i1, jnp.where(lane == 1, i2, jnp.where(
        lane == 2, rank1, jnp.where(lane == 3, rank2, 0))))
    gatew_ref[...] = jnp.where(lane == 0, w1, jnp.where(lane == 1, w2, 0.0))
    carry_scr[...] = carry_scr[...] + jnp.sum(member, axis=0, keepdims=True)
    count_ref[...] = carry_scr[...]


def _route(x, ada, layer, g2, w_router):
    tm = ROUTE_TM
    w_router_hi = w_router.astype(BF16)
    return pl.pallas_call(
        _route_kernel,
        grid=(N_TOK // tm,),
        in_specs=[
            pl.BlockSpec((tm, D_MODEL), lambda i: (i, 0)),
            _ada_spec(layer, tm),
            _resident((1, D_MODEL)),
            _resident((D_MODEL, LANES)),
            _resident((D_MODEL, LANES)),
        ],
        out_specs=[
            pl.BlockSpec((tm, 1, D_MODEL), lambda i: (i, 0, 0)),
            pl.BlockSpec((tm, LANES), lambda i: (i, 0)),
            pl.BlockSpec((tm, LANES), lambda i: (i, 0)),
            pl.BlockSpec((8, LANES), lambda i: (0, 0)),
        ],
        out_shape=[
            jax.ShapeDtypeStruct((N_TOK, 1, D_MODEL), F32),
            jax.ShapeDtypeStruct((N_TOK, LANES), jnp.int32),
            jax.ShapeDtypeStruct((N_TOK, LANES), F32),
            jax.ShapeDtypeStruct((8, LANES), F32),
        ],
        scratch_shapes=[pltpu.VMEM((8, LANES), F32)],
        compiler_params=_params("arbitrary"),
        name=f"moe_route_{layer}",
    )(x, ada, g2, w_router_hi, (w_router - w_router_hi.astype(F32)).astype(BF16))


def _dispatch_kernel(d1_ref, d2_ref, fill_start_ref, fill_rows_ref, h_ref, wg_ref, wu_ref, wd_ref,
                     hs_ref, wg_out_ref, wu_out_ref, wd_out_ref, zero_scr, sem):
    i = pl.program_id(0)
    tm = h_ref.shape[0]
    base = i * tm

    def fill_copy(k):
        n = fill_rows_ref[k]
        return pltpu.make_async_copy(zero_scr.at[pl.ds(0, n)],
                                     hs_ref.at[pl.ds(fill_start_ref[k], n)], sem.at[1])

    @pl.when(i == 0)
    def _():
        zero_scr[...] = jnp.zeros_like(zero_scr)
        for k in range(MOE_FILLS):
            @pl.when(fill_rows_ref[k] > 0)
            def _():
                fill_copy(k).start()
        for k in range(MOE_FILLS):
            @pl.when(fill_rows_ref[k] > 0)
            def _():
                fill_copy(k).wait()

    def row_copy(r, d_ref):
        return pltpu.make_async_copy(h_ref.at[r], hs_ref.at[d_ref[base + r]], sem.at[0])

    def issue(r, carry):
        row_copy(r, d1_ref).start(priority=0)
        row_copy(r, d2_ref).start(priority=1)
        return carry

    def drain(r, carry):
        row_copy(r, d1_ref).wait()
        row_copy(r, d2_ref).wait()
        return carry

    lax.fori_loop(0, tm, issue, 0)
    wg_out_ref[...] = wg_ref[...].astype(BF16)
    wu_out_ref[...] = wu_ref[...].astype(BF16)
    wd_out_ref[...] = wd_ref[...].astype(BF16)
    lax.fori_loop(0, tm, drain, 0)


def _dispatch(d1, d2, fill_start, fill_rows, h_rows, m, w_gate, w_up, w_down):
    steps = 2 * N_EXPERTS
    tm = N_TOK // steps
    fh = MOE_FF // 2
    return pl.pallas_call(
        _dispatch_kernel,
        grid_spec=pltpu.PrefetchScalarGridSpec(
            num_scalar_prefetch=4,
            grid=(steps,),
            in_specs=[
                pl.BlockSpec((tm, 1, D_MODEL), lambda i, *_: (i, 0, 0)),
                pl.BlockSpec((None, None, D_MODEL, fh), lambda i, *_: (m, i // 2, 0, i % 2)),
                pl.BlockSpec((None, None, D_MODEL, fh), lambda i, *_: (m, i // 2, 0, i % 2)),
                pl.BlockSpec((None, None, fh, D_MODEL), lambda i, *_: (m, i // 2, i % 2, 0)),
            ],
            out_specs=[
                pl.BlockSpec(memory_space=pl.ANY),
                pl.BlockSpec((None, D_MODEL, fh), lambda i, *_: (i // 2, 0, i % 2)),
                pl.BlockSpec((None, D_MODEL, fh), lambda i, *_: (i // 2, 0, i % 2)),
                pl.BlockSpec((None, fh, D_MODEL), lambda i, *_: (i // 2, i % 2, 0)),
            ],
            scratch_shapes=[pltpu.VMEM((MOE_TG, 1, D_MODEL), F32),
                            pltpu.SemaphoreType.DMA((2,))],
        ),
        out_shape=[
            jax.ShapeDtypeStruct((MOE_ROWS, 1, D_MODEL), F32),
            jax.ShapeDtypeStruct((N_EXPERTS, D_MODEL, MOE_FF), BF16),
            jax.ShapeDtypeStruct((N_EXPERTS, D_MODEL, MOE_FF), BF16),
            jax.ShapeDtypeStruct((N_EXPERTS, MOE_FF, D_MODEL), BF16),
        ],
        compiler_params=_params("arbitrary"),
        name="moe_dispatch",
    )(d1, d2, fill_start, fill_rows, h_rows, w_gate, w_up, w_down)


def _experts_kernel(te_ref, nt_ref, hs_ref, wg_ref, wu_ref, wd_ref, ys_ref, h2d_scr):
    used = pl.program_id(0) < nt_ref[0]

    @pl.when(jnp.logical_not(used))
    def _():
        ys_ref[...] = jnp.zeros_like(ys_ref)

    @pl.when(used)
    def _():
        h2d_scr[...] = hs_ref[...].reshape(MOE_TG, D_MODEL)
        h = h2d_scr[...].astype(BF16)
        y = None
        for lo in range(0, MOE_FF, MOE_FC):
            g = _dot(h, wg_ref[:, lo:lo + MOE_FC])
            u = _dot(h, wu_ref[:, lo:lo + MOE_FC])
            part = _dot((_silu(g) * u).astype(BF16), wd_ref[lo:lo + MOE_FC, :])
            y = part if y is None else y + part
        ys_ref[...] = y.reshape(MOE_TG, 1, D_MODEL)


def _experts(tile_expert, n_tiles, hs, w_gate, w_up, w_down):
    rows = pl.BlockSpec((MOE_TG, 1, D_MODEL), lambda t, te, nt: (t, 0, 0))
    return pl.pallas_call(
        _experts_kernel,
        grid_spec=pltpu.PrefetchScalarGridSpec(
            num_scalar_prefetch=2,
            grid=(MOE_MAX_TILES,),
            in_specs=[
                rows,
                pl.BlockSpec((None, D_MODEL, MOE_FF), lambda t, te, nt: (te[t], 0, 0)),
                pl.BlockSpec((None, D_MODEL, MOE_FF), lambda t, te, nt: (te[t], 0, 0)),
                pl.BlockSpec((None, MOE_FF, D_MODEL), lambda t, te, nt: (te[t], 0, 0)),
            ],
            out_specs=rows,
            scratch_shapes=[pltpu.VMEM((MOE_TG, D_MODEL), F32)],
        ),
        out_shape=jax.ShapeDtypeStruct((MOE_ROWS, 1, D_MODEL), F32),
        compiler_params=_params("arbitrary"),
        name="moe_experts",
    )(tile_expert, n_tiles, hs, w_gate, w_up, w_down)


def _combine_kernel(d1_ref, d2_ref, x_ref, ada_ref, gw_ref, ys_ref, o_ref,
                    ya0, yb0, ya1, yb1, ya2d, yb2d, sem):
    i = pl.program_id(0)
    tm = x_ref.shape[0]

    def row_copies(step, r, buf_a, buf_b, slot):
        t = step * tm + r
        return (pltpu.make_async_copy(ys_ref.at[d1_ref[t]], buf_a.at[r], sem.at[slot]),
                pltpu.make_async_copy(ys_ref.at[d2_ref[t]], buf_b.at[r], sem.at[slot]))

    def start_tile(step, buf_a, buf_b, slot):
        def body(r, carry):
            for priority, cp in enumerate(row_copies(step, r, buf_a, buf_b, slot)):
                cp.start(priority=priority)
            return carry
        lax.fori_loop(0, tm, body, 0)

    def wait_tile(step, buf_a, buf_b, slot):
        def body(r, carry):
            for cp in row_copies(step, r, buf_a, buf_b, slot):
                cp.wait()
            return carry
        lax.fori_loop(0, tm, body, 0)

    @pl.when(i == 0)
    def _():
        start_tile(0, ya0, yb0, 0)

    def run(cur, nxt):
        @pl.when(i + 1 < pl.num_programs(0))
        def _():
            start_tile(i + 1, *nxt)

        wait_tile(i, *cur)
        ya2d[...] = cur[0][...].reshape(tm, D_MODEL)
        yb2d[...] = cur[1][...].reshape(tm, D_MODEL)
        mix = gw_ref[:, 0:1] * ya2d[...] + gw_ref[:, 1:2] * yb2d[...]
        o_ref[...] = x_ref[...] + ada_ref[5:6, :] * mix

    @pl.when(i % 2 == 0)
    def _():
        run((ya0, yb0, 0), (ya1, yb1, 1))

    @pl.when(i % 2 == 1)
    def _():
        run((ya1, yb1, 1), (ya0, yb0, 0))


def _combine(d1, d2, x, ada, layer, gate_w, ys):
    tm = COMBINE_TM
    row_buf = pltpu.VMEM((tm, 1, D_MODEL), F32)
    return pl.pallas_call(
        _combine_kernel,
        grid_spec=pltpu.PrefetchScalarGridSpec(
            num_scalar_prefetch=2,
            grid=(N_TOK // tm,),
            in_specs=[
                pl.BlockSpec((tm, D_MODEL), lambda i, *_: (i, 0)),
                pl.BlockSpec((None, None, 6, D_MODEL),
                             lambda i, *_: (layer, _cond_of_tile(i, tm), 0, 0)),
                pl.BlockSpec((tm, LANES), lambda i, *_: (i, 0)),
                pl.BlockSpec(memory_space=pl.ANY),
            ],
            out_specs=pl.BlockSpec((tm, D_MODEL), lambda i, *_: (i, 0)),
            scratch_shapes=[row_buf, row_buf, row_buf, row_buf,
                            pltpu.VMEM((tm, D_MODEL), F32), pltpu.VMEM((tm, D_MODEL), F32),
                            pltpu.SemaphoreType.DMA((2,))],
        ),
        out_shape=jax.ShapeDtypeStruct((N_TOK, D_MODEL), F32),
        compiler_params=_params("arbitrary"),
        name="moe_combine",
    )(d1, d2, x, ada, gate_w, ys)


def _moe_layer(x, ada, layer, m, g2, w_router, w_gate, w_up, w_down):
    h_rows, route, gate_w, counts = _route(x, ada, layer, g2, w_router)
    cnt = counts[0, :N_EXPERTS].astype(jnp.int32)
    tiles_per_expert = (cnt + MOE_TG - 1) // MOE_TG
    tile_end = jnp.cumsum(tiles_per_expert)
    row_start = (tile_end - tiles_per_expert) * MOE_TG
    n_tiles = tile_end[-1:]
    t = jnp.minimum(jnp.arange(MOE_MAX_TILES, dtype=jnp.int32), n_tiles - 1)
    tile_expert = jnp.sum((t[:, None] >= tile_end[None, :]).astype(jnp.int32), axis=1)
    experts = jnp.arange(N_EXPERTS, dtype=jnp.int32)
    d1 = jnp.sum(jnp.where(route[:, 0:1] == experts, row_start, 0), axis=1) + route[:, 2]
    d2 = jnp.sum(jnp.where(route[:, 1:2] == experts, row_start, 0), axis=1) + route[:, 3]
    spare_tile = jnp.minimum(n_tiles + experts, MOE_MAX_TILES - 1)
    fill_start = jnp.concatenate([row_start + cnt, spare_tile * MOE_TG])
    fill_rows = jnp.concatenate([tiles_per_expert * MOE_TG - cnt,
                                 jnp.where(n_tiles + experts < MOE_MAX_TILES, MOE_TG, 0)])
    hs, wg_bf, wu_bf, wd_bf = _dispatch(d1, d2, fill_start, fill_rows, h_rows, m, w_gate, w_up, w_down)
    ys = _experts(tile_expert, n_tiles, hs, wg_bf, wu_bf, wd_bf)
    return _combine(d1, d2, x, ada, layer, gate_w, ys)


def _final_kernel(x_ref, g_ref, o_ref):
    o_ref[...] = _rms(x_ref[...]) * g_ref[...]


def _final_norm(x, g, row0, rows):
    tm = 1024
    blk0 = row0 // tm
    return pl.pallas_call(
        _final_kernel,
        grid=(rows // tm,),
        in_specs=[pl.BlockSpec((tm, D_MODEL), lambda i: (blk0 + i, 0)),
                  _resident((1, D_MODEL))],
        out_specs=pl.BlockSpec((tm, D_MODEL), lambda i: (i, 0)),
        out_shape=jax.ShapeDtypeStruct((rows, D_MODEL), F32),
        compiler_params=_params("arbitrary"),
        name="final_norm",
    )(x, g)


def _rope_tables(dim):
    half = dim // 2
    quarter = half // 2
    t = np.arange(DEC_SEQ)
    pos = np.stack([t // GRID_W, t % GRID_W], axis=1).astype(np.float32)
    freqs = (ROPE_THETA ** (-np.arange(quarter, dtype=np.float32) / quarter)).astype(np.float32)
    ang = pos[:, :, None] * freqs[None, None, :]
    cos = np.cos(ang.astype(np.float64))
    sin = np.sin(ang.astype(np.float64))
    cos_t = np.concatenate([cos, cos], axis=-1).reshape(DEC_SEQ, dim)
    sin_t = np.concatenate([-sin, sin], axis=-1).reshape(DEC_SEQ, dim)
    lane = np.arange(dim)
    swap = np.where(lane % half < quarter, lane + quarter, lane - quarter)
    return cos_t.astype(np.float32), sin_t.astype(np.float32), swap


def kernel(x_prompt, x_sample, cache_mla_ckv, cache_mla_kpe, cache_gqa_k, cache_gqa_v, c, c_ctx, w_ada, b_ada, norm1_g, norm2_g, conv_w_in, conv_k, conv_w_out, mla_w_down, mla_q_norm_g, mla_kv_norm_g, mla_w_uq, mla_w_ukv, mla_w_o, gqa_w_qkv, gqa_q_norm_g, gqa_k_norm_g, gqa_w_o, ffn_w_gate, ffn_w_up, ffn_w_down, moe_w_router, moe_w_gate, moe_w_up, moe_w_down, final_norm_g):
    xs = (x_prompt.reshape(N_CTX, D_MODEL), x_sample.reshape(N_LAT, D_MODEL))
    conv_w_in, conv_w_out = conv_w_in.astype(BF16), conv_w_out.astype(BF16)
    ffn_w_gate, ffn_w_up, ffn_w_down = (w.astype(BF16) for w in (ffn_w_gate, ffn_w_up, ffn_w_down))
    cond = jnp.concatenate(
        [c_ctx[None, :], c, jnp.zeros((N_COND - 1 - DEC_BATCH, D_MODEL), F32)], axis=0)
    ada = _adaln_table(cond, w_ada, b_ada)

    new_ckv, new_kpe, new_k, new_v = [], [], [], []
    for i in range(DEPTH):
        j = i // N_MIXERS
        kind = i % N_MIXERS
        g1 = norm1_g[i][None, :]
        if kind == 0:
            x = _conv_layer(xs if i == 0 else (x,), ada, i, j, g1, conv_w_in, conv_k, conv_w_out)
        elif kind == 1:
            cos64, sin64, swap64 = _rope_tables(MLA_ROPE)
            wd = mla_w_down[j]
            kpe_cols = wd[:, MLA_Q_RANK + MLA_KV_RANK:]
            w_down = jnp.concatenate([wd, kpe_cols[:, swap64]], axis=1).astype(BF16)
            wq = mla_w_uq[j].reshape(MLA_Q_RANK, MLA_HEADS, MLA_QK)
            wq_nope = wq[:, :, :MLA_NOPE].reshape(MLA_Q_RANK, MLA_Q_NOPE_COLS)
            wq_rope = wq[:, :, MLA_NOPE:]
            w_uq = jnp.concatenate(
                [wq_nope, wq_rope.reshape(MLA_Q_RANK, MLA_Q_ROPE_COLS),
                 wq_rope[:, :, swap64].reshape(MLA_Q_RANK, MLA_Q_ROPE_COLS)], axis=1).astype(BF16)
            wkv = mla_w_ukv[j].reshape(MLA_KV_RANK, MLA_HEADS, MLA_NOPE + MLA_V)
            w_ukv = jnp.concatenate(
                [wkv[:, :, :MLA_NOPE].reshape(MLA_KV_RANK, MLA_Q_NOPE_COLS),
                 wkv[:, :, MLA_NOPE:].reshape(MLA_KV_RANK, MLA_HEADS * MLA_V)], axis=1).astype(BF16)
            rope = (jnp.asarray(np.tile(cos64, (1, MLA_HEADS))), jnp.asarray(np.tile(sin64, (1, MLA_HEADS))),
                    jnp.asarray(cos64), jnp.asarray(sin64))
            x, ckv_p, kpe_p = _mla_layer(
                x, ada, i, g1, w_down, mla_q_norm_g[j][None, :], mla_kv_norm_g[j][None, :],
                w_uq, w_ukv, mla_w_o[j].astype(BF16), cache_mla_ckv[:, j], cache_mla_kpe[:, j], rope)
            new_ckv.append(ckv_p.reshape(BATCH, SEQ, MLA_KV_RANK))
            new_kpe.append(kpe_p.reshape(BATCH, SEQ, MLA_ROPE))
        else:
            cos128, sin128, swap128 = _rope_tables(GQA_HEAD_DIM)
            wqkv = gqa_w_qkv[j]
            wq = wqkv[:, :GQA_Q_COLS].reshape(D_MODEL, GQA_HEADS, GQA_HEAD_DIM)
            wk = wqkv[:, GQA_Q_COLS:GQA_Q_COLS + GQA_KV_COLS].reshape(D_MODEL, GQA_KV_HEADS, GQA_HEAD_DIM)
            w_qkv_ctx = wqkv.astype(BF16)
            w_qkv_lat = jnp.concatenate(
                [wqkv, wq[:, :, swap128].reshape(D_MODEL, GQA_Q_COLS),
                 wk[:, :, swap128].reshape(D_MODEL, GQA_KV_COLS)], axis=1).astype(BF16)
            gq = gqa_q_norm_g[j]
            gk = gqa_k_norm_g[j]
            rope = (jnp.asarray(np.tile(cos128, (1, GQA_HEADS))), jnp.asarray(np.tile(sin128, (1, GQA_HEADS))),
                    jnp.asarray(np.tile(cos128, (1, GQA_KV_HEADS))), jnp.asarray(np.tile(sin128, (1, GQA_KV_HEADS))))
            x, k_p, v_p = _gqa_layer(
                x, ada, i, g1, w_qkv_ctx, w_qkv_lat,
                jnp.tile(gq, GQA_HEADS)[None, :], jnp.tile(gk, GQA_KV_HEADS)[None, :],
                jnp.tile(gq[swap128], GQA_HEADS)[None, :], jnp.tile(gk[swap128], GQA_KV_HEADS)[None, :],
                gqa_w_o[j].astype(BF16),
                cache_gqa_k[:, j].reshape(DEC_BATCH, PAST_LEN, GQA_KV_COLS),
                cache_gqa_v[:, j].reshape(DEC_BATCH, PAST_LEN, GQA_KV_COLS), rope)
            new_k.append(k_p.reshape(BATCH, SEQ, GQA_KV_HEADS, GQA_HEAD_DIM))
            new_v.append(v_p.reshape(BATCH, SEQ, GQA_KV_HEADS, GQA_HEAD_DIM))

        m = i // 2
        g2 = norm2_g[i][None, :]
        if i % 2 == 0:
            x = _ffn_layer(x, ada, i, m, g2, ffn_w_gate, ffn_w_up, ffn_w_down)
        else:
            w_router = jnp.pad(moe_w_router[m], ((0, 0), (0, LANES - N_EXPERTS)))
            x = _moe_layer(x, ada, i, m, g2, w_router, moe_w_gate, moe_w_up, moe_w_down)

    g_final = final_norm_g[None, :]
    y_prompt = _final_norm(x, g_final, 0, N_CTX).reshape(BATCH, SEQ, D_MODEL)
    y_sample = _final_norm(x, g_final, N_CTX, N_LAT).reshape(DEC_BATCH, DEC_SEQ, D_MODEL)
    return (y_prompt, y_sample,
            jnp.stack(new_ckv, axis=1), jnp.stack(new_kpe, axis=1),
            jnp.stack(new_k, axis=1), jnp.stack(new_v, axis=1))
```

```python
import functools

import numpy as np
import jax
import jax.numpy as jnp
from jax import lax
from jax.experimental import pallas as pl
from jax.experimental.pallas import tpu as pltpu

D_MODEL = 1024
BATCH = 32
SEQ = 256
DEPTH = 4
DEC_BATCH = 2
DEC_SEQ = 1024
PAST_LEN = 512
GRID_W = 64
N_MIXERS = 3
CONV_WIDTH = 3
MLA_HEADS = 8
MLA_NOPE = 128
MLA_ROPE = 64
MLA_V = 128
MLA_Q_RANK = 384
MLA_KV_RANK = 256
GQA_HEADS = 8
GQA_KV_HEADS = 2
GQA_HEAD_DIM = 128
D_FF = 2816
N_EXPERTS = 8
MOE_FF = 2048
ROPE_THETA = 10000.0
EPS = 1e-6

N_CTX = BATCH * SEQ
N_LAT = DEC_BATCH * DEC_SEQ
N_TOK = N_CTX + N_LAT
N_COND = 8
LANES = 128
Q_TILE = 256
VMEM_LIMIT = 56 * 1024 * 1024

F32 = jnp.float32
BF16 = jnp.bfloat16


def _dot(a, b):
    return jnp.dot(a, b, preferred_element_type=F32)


def _dot_t(a, b):
    return lax.dot_general(a, b, (((1,), (1,)), ((), ())), preferred_element_type=F32)


def _rms(x):
    return x * lax.rsqrt(jnp.mean(x * x, axis=-1, keepdims=True) + EPS)


def _modulate(x, g, shift, scale):
    return _rms(x) * g * (1.0 + scale) + shift


def _silu(x):
    return x * jax.nn.sigmoid(x)


def _cond_of_tile(i, tm):
    start = i * tm
    return jnp.where(start < N_CTX, 0, 1 + (start - N_CTX) // DEC_SEQ)


def _resident(shape):
    return pl.BlockSpec(shape, lambda *_: (0,) * len(shape), pipeline_mode=pl.Buffered(1))


def _resident_at(index, shape):
    return pl.BlockSpec((None,) + tuple(shape), lambda *_: (index,) + (0,) * len(shape),
                        pipeline_mode=pl.Buffered(1))


def _params(*sem):
    return pltpu.CompilerParams(dimension_semantics=sem, vmem_limit_bytes=VMEM_LIMIT)


def _adaln_kernel(cond_ref, w_ref, b_ref, o_ref):
    s = _silu(cond_ref[...]).astype(BF16)
    o_ref[...] = _dot(s, w_ref[...].astype(BF16)) + b_ref[...]


def _adaln_table(cond, w_ada, b_ada):
    tn = 1536
    out = pl.pallas_call(
        _adaln_kernel,
        grid=(DEPTH, 6 * D_MODEL // tn),
        in_specs=[
            pl.BlockSpec((N_COND, D_MODEL), lambda l, j: (0, 0)),
            pl.BlockSpec((None, D_MODEL, tn), lambda l, j: (l, 0, j)),
            pl.BlockSpec((None, 1, tn), lambda l, j: (l, 0, j)),
        ],
        out_specs=pl.BlockSpec((None, N_COND, tn), lambda l, j: (l, 0, j)),
        out_shape=jax.ShapeDtypeStruct((DEPTH, N_COND, 6 * D_MODEL), F32),
        compiler_params=_params("arbitrary", "arbitrary"),
        name="adaln_table",
    )(cond, w_ada, b_ada.reshape(DEPTH, 1, 6 * D_MODEL))
    return out.reshape(DEPTH, N_COND, 6, D_MODEL)


def _ada_spec(layer, tm):
    return pl.BlockSpec((None, None, 6, D_MODEL), lambda i: (layer, _cond_of_tile(i, tm), 0, 0))


def _ada_spec_at(layer, cond_fn):
    return pl.BlockSpec((None, None, 6, D_MODEL), lambda i: (layer, cond_fn(i), 0, 0))


CONV_TM = 1024
CONV_CC = 256


def _conv_kernel(*refs, split):
    if split:
        xp_ref, x_ref, ada_ref, g1_ref, win_ref, ck_ref, wout_ref, o_ref, v_scr = refs
    else:
        x_ref, ada_ref, g1_ref, win_ref, ck_ref, wout_ref, o_ref, v_scr = refs
    tm = x_ref.shape[0]
    i = pl.program_id(0)
    x = x_ref[...]
    if split:
        x = jnp.where(i * tm < N_CTX, xp_ref[...], x)
    h = _modulate(x, g1_ref[...], ada_ref[0:1, :], ada_ref[1:2, :]).astype(BF16)
    period = jnp.where(i * tm < N_CTX, SEQ, DEC_SEQ)
    pos = lax.broadcasted_iota(jnp.int32, (tm, 1), 0) & (period - 1)
    first = pos == 0
    last = pos == period - 1
    for j in range(D_MODEL // CONV_CC):
        lo = j * CONV_CC
        b_gate = _dot(h, win_ref[:, lo:lo + CONV_CC])
        c_gate = _dot(h, win_ref[:, D_MODEL + lo:D_MODEL + lo + CONV_CC])
        x_in = _dot(h, win_ref[:, 2 * D_MODEL + lo:2 * D_MODEL + lo + CONV_CC])
        u = c_gate * x_in
        u_prev = jnp.where(first, 0.0, pltpu.roll(u, 1, 0))
        u_next = jnp.where(last, 0.0, pltpu.roll(u, tm - 1, 0))
        conv = (ck_ref[0:1, lo:lo + CONV_CC] * u_prev + ck_ref[1:2, lo:lo + CONV_CC] * u
                + ck_ref[2:3, lo:lo + CONV_CC] * u_next)
        v_scr[:, lo:lo + CONV_CC] = (b_gate * conv).astype(BF16)
    o_ref[...] = x + ada_ref[2:3, :] * _dot(v_scr[...], wout_ref[...])


def _conv_layer(xs, ada, layer, j, g1, w_in, conv_k, w_out):
    tm = CONV_TM
    ctx_tiles = N_CTX // tm
    if len(xs) == 1:
        x_specs = [pl.BlockSpec((tm, D_MODEL), lambda i: (i, 0))]
    else:
        x_specs = [pl.BlockSpec((tm, D_MODEL), lambda i: (jnp.minimum(i, ctx_tiles - 1), 0)),
                   pl.BlockSpec((tm, D_MODEL), lambda i: (jnp.maximum(i - ctx_tiles, 0), 0))]
    return pl.pallas_call(
        functools.partial(_conv_kernel, split=len(xs) == 2),
        grid=(N_TOK // tm,),
        in_specs=x_specs + [
            _ada_spec(layer, tm),
            _resident((1, D_MODEL)),
            _resident_at(j, (D_MODEL, 3 * D_MODEL)),
            _resident_at(j, (CONV_WIDTH, D_MODEL)),
            _resident_at(j, (D_MODEL, D_MODEL)),
        ],
        out_specs=pl.BlockSpec((tm, D_MODEL), lambda i: (i, 0)),
        out_shape=jax.ShapeDtypeStruct((N_TOK, D_MODEL), F32),
        scratch_shapes=[pltpu.VMEM((tm, D_MODEL), BF16)],
        compiler_params=_params("arbitrary"),
        name=f"conv_mixer_{layer}",
    )(*xs, ada, g1, w_in, conv_k, w_out)


FFN_TM = 512
FFN_CHUNKS = ((0, 1536), (1536, D_FF))


def _ffn_kernel(x_ref, ada_ref, g2_ref, wg_ref, wu_ref, wd_ref, o_ref):
    x = x_ref[...]
    h = _modulate(x, g2_ref[...], ada_ref[3:4, :], ada_ref[4:5, :]).astype(BF16)
    f = None
    for lo, hi in FFN_CHUNKS:
        g = _dot(h, wg_ref[:, lo:hi])
        u = _dot(h, wu_ref[:, lo:hi])
        part = _dot((_silu(g) * u).astype(BF16), wd_ref[lo:hi, :])
        f = part if f is None else f + part
    o_ref[...] = x + ada_ref[5:6, :] * f


def _ffn_layer(x, ada, layer, m, g2, w_gate, w_up, w_down):
    tm = FFN_TM
    return pl.pallas_call(
        _ffn_kernel,
        grid=(N_TOK // tm,),
        in_specs=[
            pl.BlockSpec((tm, D_MODEL), lambda i: (i, 0)),
            _ada_spec(layer, tm),
            _resident((1, D_MODEL)),
            _resident_at(m, (D_MODEL, D_FF)),
            _resident_at(m, (D_MODEL, D_FF)),
            _resident_at(m, (D_FF, D_MODEL)),
        ],
        out_specs=pl.BlockSpec((tm, D_MODEL), lambda i: (i, 0)),
        out_shape=jax.ShapeDtypeStruct((N_TOK, D_MODEL), F32),
        compiler_params=_params("arbitrary"),
        name=f"dense_ffn_{layer}",
    )(x, ada, g2, w_gate, w_up, w_down)


def _attention(qk_parts, v_ref, heads, kv_of_head, dv, scale, o_scr, finish, keys_of):
    def q_block(rows):
        keys = keys_of(rows)
        for h in range(heads):
            g = kv_of_head(h)
            s = None
            for q_ref, k_ref, width, k_shared in qk_parts:
                kb = 0 if k_shared else g
                part = _dot_t(q_ref[rows, h * width:(h + 1) * width],
                              k_ref[keys, kb * width:(kb + 1) * width])
                s = part if s is None else s + part
            s = s * scale
            e = jnp.exp(s - jnp.max(s, axis=-1, keepdims=True))
            denom = jnp.sum(e, axis=-1, keepdims=True)
            o = _dot(e.astype(BF16), v_ref[keys, g * dv:(g + 1) * dv]) / denom
            o_scr[rows, h * dv:(h + 1) * dv] = o.astype(BF16)
        finish(rows)

    _for_row_blocks(o_scr.shape[0], q_block)


def _keys_of(latent):
    assert SEQ == Q_TILE
    return (lambda rows: slice(None)) if latent else (lambda rows: rows)


def _for_row_blocks(n_rows, fn):
    if n_rows <= 2 * Q_TILE:
        for b in range(n_rows // Q_TILE):
            fn(pl.ds(b * Q_TILE, Q_TILE))
    else:
        def body(b, carry):
            fn(pl.ds(pl.multiple_of(b * Q_TILE, Q_TILE), Q_TILE))
            return carry
        lax.fori_loop(0, n_rows // Q_TILE, body, 0)


MLA_QK = MLA_NOPE + MLA_ROPE
MLA_DOWN_COLS = MLA_Q_RANK + MLA_KV_RANK + 2 * MLA_ROPE
MLA_Q_NOPE_COLS = MLA_HEADS * MLA_NOPE
MLA_Q_ROPE_COLS = MLA_HEADS * MLA_ROPE


def _ctx_then_latent(body, n_in, refs):
    ins, lat_ref, rest = refs[:n_in], refs[n_in], refs[n_in + 1:]
    o_ref = rest[0]
    i = pl.program_id(0)

    @pl.when(i < CTX_TILES)
    def _():
        body(*ins, *rest, latent=False)

    @pl.when(i >= CTX_TILES)
    def _():
        o_ref[...] = lat_ref[...]


def _ctx_specs(out_tails):
    def tile(n_trailing):
        return lambda i: (jnp.minimum(i, CTX_TILES - 1),) + (0,) * n_trailing
    x_spec = pl.BlockSpec((CTX_TM, D_MODEL), tile(1))
    lat_spec = pl.BlockSpec((CTX_TM, D_MODEL), lambda i: (jnp.maximum(i - CTX_TILES, 0), 0))
    out_specs = [pl.BlockSpec((CTX_TM, D_MODEL), lambda i: (i, 0))]
    out_specs += [pl.BlockSpec((CTX_TM,) + tail, tile(len(tail))) for tail in out_tails]
    return x_spec, lat_spec, out_specs


CTX_TM = 2 * SEQ
CTX_TILES = N_CTX // CTX_TM
CTX_STEPS = CTX_TILES + N_LAT // CTX_TM


def _mla_kernel(*refs, latent):
    if latent:
        _mla_body(*refs, latent=True)
    else:
        _ctx_then_latent(_mla_body, 9, refs)


def _mla_body(*refs, latent):
    if latent:
        (x_ref, ada_ref, g1_ref, wdown_ref, gq_ref, gkv_ref, wuq_ref, wukv_ref, wo_ref,
         cosq_ref, sinq_ref, cosk_ref, sink_ref, cckv_ref, ckpe_ref,
         o_ref, qn_scr, qr_scr, kn_scr, kr_scr, v_scr, o_scr) = refs
    else:
        (x_ref, ada_ref, g1_ref, wdown_ref, gq_ref, gkv_ref, wuq_ref, wukv_ref, wo_ref,
         o_ref, ckv_out_ref, kpe_out_ref, qn_scr, qr_scr, kn_scr, kr_scr, v_scr, o_scr) = refs
    tm = x_ref.shape[0]
    kpe_lo = MLA_Q_RANK + MLA_KV_RANK

    def project(rows):
        h = _modulate(x_ref[rows, :], g1_ref[...], ada_ref[0:1, :], ada_ref[1:2, :]).astype(BF16)
        down = _dot(h, wdown_ref[...])
        c_q = down[:, :MLA_Q_RANK]
        c_kv = down[:, MLA_Q_RANK:kpe_lo]
        kpe = down[:, kpe_lo:kpe_lo + MLA_ROPE]
        q = _dot((_rms(c_q) * gq_ref[...]).astype(BF16), wuq_ref[...])
        ckv_n = _rms(c_kv) * gkv_ref[...]
        kv = _dot(ckv_n.astype(BF16), wukv_ref[...])
        qn_scr[rows, :] = q[:, :MLA_Q_NOPE_COLS].astype(BF16)
        q_rope = q[:, MLA_Q_NOPE_COLS:MLA_Q_NOPE_COLS + MLA_Q_ROPE_COLS]
        kn_scr[rows, :] = kv[:, :MLA_Q_NOPE_COLS].astype(BF16)
        v_scr[rows, :] = kv[:, MLA_Q_NOPE_COLS:].astype(BF16)
        if latent:
            q_swap = q[:, MLA_Q_NOPE_COLS + MLA_Q_ROPE_COLS:]
            q_rope = q_rope * cosq_ref[rows, :] + q_swap * sinq_ref[rows, :]
            kpe_swap = down[:, kpe_lo + MLA_ROPE:kpe_lo + 2 * MLA_ROPE]
            kpe = kpe * cosk_ref[rows, :] + kpe_swap * sink_ref[rows, :]
        else:
            ckv_out_ref[rows, :] = ckv_n
            kpe_out_ref[rows, :] = kpe
        qr_scr[rows, :] = q_rope.astype(BF16)
        kr_scr[rows, :] = kpe.astype(BF16)

    _for_row_blocks(tm, project)

    if latent:
        def expand_cache(rows):
            kv_c = _dot(cckv_ref[rows, :].astype(BF16), wukv_ref[...])
            dst = pl.ds(tm + rows.start, Q_TILE)
            kn_scr[dst, :] = kv_c[:, :MLA_Q_NOPE_COLS].astype(BF16)
            v_scr[dst, :] = kv_c[:, MLA_Q_NOPE_COLS:].astype(BF16)
            kr_scr[dst, :] = ckpe_ref[rows, :].astype(BF16)

        _for_row_blocks(PAST_LEN, expand_cache)

    def finish(rows):
        o_ref[rows, :] = x_ref[rows, :] + ada_ref[2:3, :] * _dot(o_scr[rows, :], wo_ref[...])

    _attention([(qn_scr, kn_scr, MLA_NOPE, False), (qr_scr, kr_scr, MLA_ROPE, True)],
               v_scr, MLA_HEADS, lambda hd: hd, MLA_V, MLA_QK ** -0.5, o_scr, finish,
               _keys_of(latent))


def _mla_scratch(tm, sk):
    return [
        pltpu.VMEM((tm, MLA_Q_NOPE_COLS), BF16),
        pltpu.VMEM((tm, MLA_Q_ROPE_COLS), BF16),
        pltpu.VMEM((sk, MLA_Q_NOPE_COLS), BF16),
        pltpu.VMEM((sk, MLA_ROPE), BF16),
        pltpu.VMEM((sk, MLA_HEADS * MLA_V), BF16),
        pltpu.VMEM((tm, MLA_HEADS * MLA_V), BF16),
    ]


def _mla_layer(x, ada, layer, g1, w_down, g_q, g_kv, w_uq, w_ukv, w_o,
               cache_ckv, cache_kpe, rope):
    n_uq = w_uq.shape[1]
    weights = [
        _resident((1, D_MODEL)),
        _resident((D_MODEL, MLA_DOWN_COLS)),
        _resident((1, MLA_Q_RANK)),
        _resident((1, MLA_KV_RANK)),
        _resident((MLA_Q_RANK, n_uq)),
        _resident((MLA_KV_RANK, 2 * MLA_Q_NOPE_COLS)),
        _resident((MLA_HEADS * MLA_V, D_MODEL)),
    ]
    weight_args = (g1, w_down, g_q, g_kv, w_uq, w_ukv, w_o)
    cos_q, sin_q, cos_k, sin_k = rope
    ctx_blocks = N_CTX // DEC_SEQ
    lat_out = pl.pallas_call(
        functools.partial(_mla_kernel, latent=True),
        grid=(DEC_BATCH,),
        in_specs=[pl.BlockSpec((DEC_SEQ, D_MODEL), lambda b: (ctx_blocks + b, 0)),
                  _ada_spec_at(layer, lambda b: 1 + b)] + weights + [
            _resident((DEC_SEQ, MLA_Q_ROPE_COLS)),
            _resident((DEC_SEQ, MLA_Q_ROPE_COLS)),
            _resident((DEC_SEQ, MLA_ROPE)),
            _resident((DEC_SEQ, MLA_ROPE)),
            pl.BlockSpec((None, PAST_LEN, MLA_KV_RANK), lambda b: (b, 0, 0)),
            pl.BlockSpec((None, PAST_LEN, MLA_ROPE), lambda b: (b, 0, 0)),
        ],
        out_specs=pl.BlockSpec((DEC_SEQ, D_MODEL), lambda b: (b, 0)),
        out_shape=jax.ShapeDtypeStruct((N_LAT, D_MODEL), F32),
        scratch_shapes=_mla_scratch(DEC_SEQ, DEC_SEQ + PAST_LEN),
        compiler_params=_params("arbitrary"),
        name=f"mla_latent_{layer}",
    )(x, ada, *weight_args, cos_q, sin_q, cos_k, sin_k, cache_ckv, cache_kpe)

    x_spec, lat_spec, out_specs = _ctx_specs(((MLA_KV_RANK,), (MLA_ROPE,)))
    return pl.pallas_call(
        functools.partial(_mla_kernel, latent=False),
        grid=(CTX_STEPS,),
        in_specs=[x_spec, _ada_spec_at(layer, lambda i: 0)] + weights + [lat_spec],
        out_specs=out_specs,
        out_shape=[
            jax.ShapeDtypeStruct((N_TOK, D_MODEL), F32),
            jax.ShapeDtypeStruct((N_CTX, MLA_KV_RANK), F32),
            jax.ShapeDtypeStruct((N_CTX, MLA_ROPE), F32),
        ],
        scratch_shapes=_mla_scratch(CTX_TM, CTX_TM),
        compiler_params=_params("arbitrary"),
        name=f"mla_ctx_{layer}",
    )(x, ada, *weight_args, lat_out)


GQA_Q_COLS = GQA_HEADS * GQA_HEAD_DIM
GQA_KV_COLS = GQA_KV_HEADS * GQA_HEAD_DIM
GQA_REP = GQA_HEADS // GQA_KV_HEADS


def _head_rms(t, n_heads):
    cols = []
    for hd in range(n_heads):
        th = t[:, hd * GQA_HEAD_DIM:(hd + 1) * GQA_HEAD_DIM]
        r = lax.rsqrt(jnp.mean(th * th, axis=-1, keepdims=True) + EPS)
        cols.append(jnp.broadcast_to(r, th.shape))
    return jnp.concatenate(cols, axis=-1)


def _gqa_kernel(*refs, latent):
    if latent:
        _gqa_body(*refs, latent=True)
    else:
        _ctx_then_latent(_gqa_body, 7, refs)


def _gqa_body(*refs, latent):
    if latent:
        (x_ref, ada_ref, g1_ref, wqkv_ref, gq_ref, gk_ref, wo_ref,
         gqs_ref, gks_ref, cosq_ref, sinq_ref, cosk_ref, sink_ref, ck_ref, cv_ref,
         o_ref, q_scr, k_scr, v_scr, o_scr) = refs
    else:
        (x_ref, ada_ref, g1_ref, wqkv_ref, gq_ref, gk_ref, wo_ref,
         o_ref, k_out_ref, v_out_ref, q_scr, k_scr, v_scr, o_scr) = refs
    tm = x_ref.shape[0]

    def project(rows):
        h = _modulate(x_ref[rows, :], g1_ref[...], ada_ref[0:1, :], ada_ref[1:2, :]).astype(BF16)
        qkv = _dot(h, wqkv_ref[...])
        q_raw = qkv[:, :GQA_Q_COLS]
        k_raw = qkv[:, GQA_Q_COLS:GQA_Q_COLS + GQA_KV_COLS]
        v = qkv[:, GQA_Q_COLS + GQA_KV_COLS:GQA_Q_COLS + 2 * GQA_KV_COLS]
        q_r = _head_rms(q_raw, GQA_HEADS)
        k_r = _head_rms(k_raw, GQA_KV_HEADS)
        q = q_raw * q_r * gq_ref[...]
        k = k_raw * k_r * gk_ref[...]
        if latent:
            lo = GQA_Q_COLS + 2 * GQA_KV_COLS
            q_swap = qkv[:, lo:lo + GQA_Q_COLS] * q_r * gqs_ref[...]
            k_swap = qkv[:, lo + GQA_Q_COLS:lo + GQA_Q_COLS + GQA_KV_COLS] * k_r * gks_ref[...]
            q = q * cosq_ref[rows, :] + q_swap * sinq_ref[rows, :]
            k = k * cosk_ref[rows, :] + k_swap * sink_ref[rows, :]
        else:
            for g in range(GQA_KV_HEADS):
                cols = slice(g * GQA_HEAD_DIM, (g + 1) * GQA_HEAD_DIM)
                k_out_ref[rows, g, :] = k[:, cols]
                v_out_ref[rows, g, :] = v[:, cols]
        q_scr[rows, :] = q.astype(BF16)
        k_scr[rows, :] = k.astype(BF16)
        v_scr[rows, :] = v.astype(BF16)

    _for_row_blocks(tm, project)

    if latent:
        k_scr[tm:, :] = ck_ref[...].astype(BF16)
        v_scr[tm:, :] = cv_ref[...].astype(BF16)

    def finish(rows):
        o_ref[rows, :] = x_ref[rows, :] + ada_ref[2:3, :] * _dot(o_scr[rows, :], wo_ref[...])

    _attention([(q_scr, k_scr, GQA_HEAD_DIM, False)], v_scr, GQA_HEADS,
               lambda hd: hd // GQA_REP, GQA_HEAD_DIM, GQA_HEAD_DIM ** -0.5, o_scr, finish,
               _keys_of(latent))


def _gqa_scratch(tm, sk):
    return [
        pltpu.VMEM((tm, GQA_Q_COLS), BF16),
        pltpu.VMEM((sk, GQA_KV_COLS), BF16),
        pltpu.VMEM((sk, GQA_KV_COLS), BF16),
        pltpu.VMEM((tm, GQA_Q_COLS), BF16),
    ]


def _gqa_layer(x, ada, layer, g1, w_qkv_ctx, w_qkv_lat, g_q, g_k, g_q_swap, g_k_swap, w_o,
               cache_k, cache_v, rope):
    cos_q, sin_q, cos_k, sin_k = rope
    ctx_blocks = N_CTX // DEC_SEQ
    lat_out = pl.pallas_call(
        functools.partial(_gqa_kernel, latent=True),
        grid=(DEC_BATCH,),
        in_specs=[pl.BlockSpec((DEC_SEQ, D_MODEL), lambda b: (ctx_blocks + b, 0)),
                  _ada_spec_at(layer, lambda b: 1 + b),
                  _resident((1, D_MODEL)),
                  _resident(w_qkv_lat.shape),
                  _resident((1, GQA_Q_COLS)),
                  _resident((1, GQA_KV_COLS)),
                  _resident((GQA_Q_COLS, D_MODEL)),
                  _resident((1, GQA_Q_COLS)),
                  _resident((1, GQA_KV_COLS)),
                  _resident((DEC_SEQ, GQA_Q_COLS)),
                  _resident((DEC_SEQ, GQA_Q_COLS)),
                  _resident((DEC_SEQ, GQA_KV_COLS)),
                  _resident((DEC_SEQ, GQA_KV_COLS)),
                  pl.BlockSpec((None, PAST_LEN, GQA_KV_COLS), lambda b: (b, 0, 0)),
                  pl.BlockSpec((None, PAST_LEN, GQA_KV_COLS), lambda b: (b, 0, 0))],
        out_specs=pl.BlockSpec((DEC_SEQ, D_MODEL), lambda b: (b, 0)),
        out_shape=jax.ShapeDtypeStruct((N_LAT, D_MODEL), F32),
        scratch_shapes=_gqa_scratch(DEC_SEQ, DEC_SEQ + PAST_LEN),
        compiler_params=_params("arbitrary"),
        name=f"gqa_latent_{layer}",
    )(x, ada, g1, w_qkv_lat, g_q, g_k, w_o, g_q_swap, g_k_swap,
      cos_q, sin_q, cos_k, sin_k, cache_k, cache_v)

    kv_tail = (GQA_KV_HEADS, GQA_HEAD_DIM)
    x_spec, lat_spec, out_specs = _ctx_specs((kv_tail, kv_tail))
    return pl.pallas_call(
        functools.partial(_gqa_kernel, latent=False),
        grid=(CTX_STEPS,),
        in_specs=[x_spec,
                  _ada_spec_at(layer, lambda i: 0),
                  _resident((1, D_MODEL)),
                  _resident(w_qkv_ctx.shape),
                  _resident((1, GQA_Q_COLS)),
                  _resident((1, GQA_KV_COLS)),
                  _resident((GQA_Q_COLS, D_MODEL)),
                  lat_spec],
        out_specs=out_specs,
        out_shape=[
            jax.ShapeDtypeStruct((N_TOK, D_MODEL), F32),
            jax.ShapeDtypeStruct((N_CTX,) + kv_tail, F32),
            jax.ShapeDtypeStruct((N_CTX,) + kv_tail, F32),
        ],
        scratch_shapes=_gqa_scratch(CTX_TM, CTX_TM),
        compiler_params=_params("arbitrary"),
        name=f"gqa_ctx_{layer}",
    )(x, ada, g1, w_qkv_ctx, g_q, g_k, w_o, lat_out)


ROUTE_TM = 512
MOE_TG = 256
MOE_MAX_TILES = 2 * N_TOK // MOE_TG + N_EXPERTS
MOE_ROWS = MOE_MAX_TILES * MOE_TG
MOE_FILLS = 2 * N_EXPERTS
MOE_FC = 1024
COMBINE_TM = 256


def _route_kernel(x_ref, ada_ref, g2_ref, wr_hi_ref, wr_lo_ref, h_ref, dest_ref, gatew_ref,
                  count_ref, carry_scr):
    tm = x_ref.shape[0]

    @pl.when(pl.program_id(0) == 0)
    def _():
        carry_scr[...] = jnp.zeros_like(carry_scr)

    h = _modulate(x_ref[...], g2_ref[...], ada_ref[3:4, :], ada_ref[4:5, :])
    h_ref[...] = h.reshape(tm, 1, D_MODEL)
    h_hi = h.astype(BF16)
    h_lo = (h - h_hi.astype(F32)).astype(BF16)
    logits = _dot(h_hi, wr_hi_ref[...]) + (_dot(h_lo, wr_hi_ref[...]) + _dot(h_hi, wr_lo_ref[...]))
    lane = lax.broadcasted_iota(jnp.int32, logits.shape, 1)
    neg = jnp.float32(-jnp.inf)
    l1 = jnp.where(lane < N_EXPERTS, logits, neg)
    m1 = jnp.max(l1, axis=-1, keepdims=True)
    i1 = jnp.min(jnp.where(l1 == m1, lane, LANES), axis=-1, keepdims=True)
    l2 = jnp.where(lane == i1, neg, l1)
    m2 = jnp.max(l2, axis=-1, keepdims=True)
    i2 = jnp.min(jnp.where(l2 == m2, lane, LANES), axis=-1, keepdims=True)
    e2 = jnp.exp(m2 - m1)
    w1 = 1.0 / (1.0 + e2)
    w2 = e2 / (1.0 + e2)
    member = jnp.where((lane == i1) | (lane == i2), 1.0, 0.0)
    r = lax.broadcasted_iota(jnp.int32, (tm, tm), 0)
    c = lax.broadcasted_iota(jnp.int32, (tm, tm), 1)
    tri = jnp.where(c < r, 1.0, 0.0).astype(BF16)
    before = _dot(tri, member.astype(BF16)) + carry_scr[0:1, :]
    rank1 = jnp.sum(jnp.where(lane == i1, before, 0.0), axis=-1, keepdims=True).astype(jnp.int32)
    rank2 = jnp.sum(jnp.where(lane == i2, before, 0.0), axis=-1, keepdims=True).astype(jnp.int32)
    table = jnp.where(lane == 0, i1, jnp.where(lane == 1, i2, jnp.where(
        lane == 2, rank1, jnp.where(lane == 3, rank2, 0))))
    dest_ref[...] = jnp.transpose(table)[0:8, :]
    gatew_ref[...] = jnp.where(lane == 0, w1, jnp.where(lane == 1, w2, 0.0))
    carry_scr[...] = carry_scr[...] + jnp.sum(member, axis=0, keepdims=True)
    count_ref[...] = carry_scr[...]


def _route(x, ada, layer, g2, w_router):
    tm = ROUTE_TM
    w_router_hi = w_router.astype(BF16)
    return pl.pallas_call(
        _route_kernel,
        grid=(N_TOK // tm,),
        in_specs=[
            pl.BlockSpec((tm, D_MODEL), lambda i: (i, 0)),
            _ada_spec(layer, tm),
            _resident((1, D_MODEL)),
            _resident((D_MODEL, LANES)),
            _resident((D_MODEL, LANES)),
        ],
        out_specs=[
            pl.BlockSpec((tm, 1, D_MODEL), lambda i: (i, 0, 0)),
            pl.BlockSpec((8, tm), lambda i: (0, i)),
            pl.BlockSpec((tm, LANES), lambda i: (i, 0)),
            pl.BlockSpec((8, LANES), lambda i: (0, 0)),
        ],
        out_shape=[
            jax.ShapeDtypeStruct((N_TOK, 1, D_MODEL), F32),
            jax.ShapeDtypeStruct((8, N_TOK), jnp.int32),
            jax.ShapeDtypeStruct((N_TOK, LANES), F32),
            jax.ShapeDtypeStruct((8, LANES), F32),
        ],
        scratch_shapes=[pltpu.VMEM((8, LANES), F32)],
        compiler_params=_params("arbitrary"),
        name=f"moe_route_{layer}",
    )(x, ada, g2, w_router_hi, (w_router - w_router_hi.astype(F32)).astype(BF16))


def _dispatch_kernel(d1_ref, d2_ref, fill_start_ref, fill_rows_ref, h_ref, wg_ref, wu_ref, wd_ref,
                     hs_ref, wg_out_ref, wu_out_ref, wd_out_ref, zero_scr, sem):
    i = pl.program_id(0)
    tm = h_ref.shape[0]
    base = i * tm

    def fill_copy(k):
        n = fill_rows_ref[k]
        return pltpu.make_async_copy(zero_scr.at[pl.ds(0, n)],
                                     hs_ref.at[pl.ds(fill_start_ref[k], n)], sem.at[1])

    @pl.when(i == 0)
    def _():
        zero_scr[...] = jnp.zeros_like(zero_scr)
        for k in range(MOE_FILLS):
            @pl.when(fill_rows_ref[k] > 0)
            def _():
                fill_copy(k).start()
        for k in range(MOE_FILLS):
            @pl.when(fill_rows_ref[k] > 0)
            def _():
                fill_copy(k).wait()

    def row_copy(r, d_ref):
        return pltpu.make_async_copy(h_ref.at[r], hs_ref.at[d_ref[base + r]], sem.at[0])

    def issue(r, carry):
        row_copy(r, d1_ref).start(priority=0)
        row_copy(r, d2_ref).start(priority=1)
        return carry

    def drain(r, carry):
        row_copy(r, d1_ref).wait()
        row_copy(r, d2_ref).wait()
        return carry

    lax.fori_loop(0, tm, issue, 0)
    wg_out_ref[...] = wg_ref[...].astype(BF16)
    wu_out_ref[...] = wu_ref[...].astype(BF16)
    wd_out_ref[...] = wd_ref[...].astype(BF16)
    lax.fori_loop(0, tm, drain, 0)


def _dispatch(d1, d2, fill_start, fill_rows, h_rows, m, w_gate, w_up, w_down):
    steps = 2 * N_EXPERTS
    tm = N_TOK // steps
    fh = MOE_FF // 2
    return pl.pallas_call(
        _dispatch_kernel,
        grid_spec=pltpu.PrefetchScalarGridSpec(
            num_scalar_prefetch=4,
            grid=(steps,),
            in_specs=[
                pl.BlockSpec((tm, 1, D_MODEL), lambda i, *_: (i, 0, 0)),
                pl.BlockSpec((None, None, D_MODEL, fh), lambda i, *_: (m, i // 2, 0, i % 2)),
                pl.BlockSpec((None, None, D_MODEL, fh), lambda i, *_: (m, i // 2, 0, i % 2)),
                pl.BlockSpec((None, None, fh, D_MODEL), lambda i, *_: (m, i // 2, i % 2, 0)),
            ],
            out_specs=[
                pl.BlockSpec(memory_space=pl.ANY),
                pl.BlockSpec((None, D_MODEL, fh), lambda i, *_: (i // 2, 0, i % 2)),
                pl.BlockSpec((None, D_MODEL, fh), lambda i, *_: (i // 2, 0, i % 2)),
                pl.BlockSpec((None, fh, D_MODEL), lambda i, *_: (i // 2, i % 2, 0)),
            ],
            scratch_shapes=[pltpu.VMEM((MOE_TG, 1, D_MODEL), F32),
                            pltpu.SemaphoreType.DMA((2,))],
        ),
        out_shape=[
            jax.ShapeDtypeStruct((MOE_ROWS, 1, D_MODEL), F32),
            jax.ShapeDtypeStruct((N_EXPERTS, D_MODEL, MOE_FF), BF16),
            jax.ShapeDtypeStruct((N_EXPERTS, D_MODEL, MOE_FF), BF16),
            jax.ShapeDtypeStruct((N_EXPERTS, MOE_FF, D_MODEL), BF16),
        ],
        compiler_params=_params("arbitrary"),
        name="moe_dispatch",
    )(d1, d2, fill_start, fill_rows, h_rows, w_gate, w_up, w_down)


def _experts_kernel(te_ref, nt_ref, hs_ref, wg_ref, wu_ref, wd_ref, ys_ref, h2d_scr):
    used = pl.program_id(0) < nt_ref[0]

    @pl.when(jnp.logical_not(used))
    def _():
        ys_ref[...] = jnp.zeros_like(ys_ref)

    @pl.when(used)
    def _():
        h2d_scr[...] = hs_ref[...].reshape(MOE_TG, D_MODEL)
        h = h2d_scr[...].astype(BF16)
        y = None
        for lo in range(0, MOE_FF, MOE_FC):
            g = _dot(h, wg_ref[:, lo:lo + MOE_FC])
            u = _dot(h, wu_ref[:, lo:lo + MOE_FC])
            part = _dot((_silu(g) * u).astype(BF16), wd_ref[lo:lo + MOE_FC, :])
            y = part if y is None else y + part
        ys_ref[...] = y.reshape(MOE_TG, 1, D_MODEL)


def _experts(tile_expert, n_tiles, hs, w_gate, w_up, w_down):
    rows = pl.BlockSpec((MOE_TG, 1, D_MODEL), lambda t, te, nt: (t, 0, 0))
    return pl.pallas_call(
        _experts_kernel,
        grid_spec=pltpu.PrefetchScalarGridSpec(
            num_scalar_prefetch=2,
            grid=(MOE_MAX_TILES,),
            in_specs=[
                rows,
                pl.BlockSpec((None, D_MODEL, MOE_FF), lambda t, te, nt: (te[t], 0, 0)),
                pl.BlockSpec((None, D_MODEL, MOE_FF), lambda t, te, nt: (te[t], 0, 0)),
                pl.BlockSpec((None, MOE_FF, D_MODEL), lambda t, te, nt: (te[t], 0, 0)),
            ],
            out_specs=rows,
            scratch_shapes=[pltpu.VMEM((MOE_TG, D_MODEL), F32)],
        ),
        out_shape=jax.ShapeDtypeStruct((MOE_ROWS, 1, D_MODEL), F32),
        compiler_params=_params("arbitrary"),
        name="moe_experts",
    )(tile_expert, n_tiles, hs, w_gate, w_up, w_down)


def _combine_kernel(*refs, final):
    if final:
        (d1_ref, d2_ref, x_ref, ada_ref, gw_ref, gf_ref, ys_ref, yp_ref, yl_ref,
         ya0, yb0, ya1, yb1, ya2d, yb2d, sem) = refs
    else:
        (d1_ref, d2_ref, x_ref, ada_ref, gw_ref, ys_ref, o_ref,
         ya0, yb0, ya1, yb1, ya2d, yb2d, sem) = refs
    i = pl.program_id(0)
    tm = x_ref.shape[0]

    def row_copies(step, r, buf_a, buf_b, slot):
        t = step * tm + r
        return (pltpu.make_async_copy(ys_ref.at[d1_ref[t]], buf_a.at[r], sem.at[slot]),
                pltpu.make_async_copy(ys_ref.at[d2_ref[t]], buf_b.at[r], sem.at[slot]))

    def start_tile(step, buf_a, buf_b, slot):
        def body(r, carry):
            for priority, cp in enumerate(row_copies(step, r, buf_a, buf_b, slot)):
                cp.start(priority=priority)
            return carry
        lax.fori_loop(0, tm, body, 0)

    def wait_tile(step, buf_a, buf_b, slot):
        def body(r, carry):
            for cp in row_copies(step, r, buf_a, buf_b, slot):
                cp.wait()
            return carry
        lax.fori_loop(0, tm, body, 0)

    @pl.when(i == 0)
    def _():
        start_tile(0, ya0, yb0, 0)

    def run(cur, nxt):
        @pl.when(i + 1 < pl.num_programs(0))
        def _():
            start_tile(i + 1, *nxt)

        wait_tile(i, *cur)
        ya2d[...] = cur[0][...].reshape(tm, D_MODEL)
        yb2d[...] = cur[1][...].reshape(tm, D_MODEL)
        mix = gw_ref[:, 0:1] * ya2d[...] + gw_ref[:, 1:2] * yb2d[...]
        out = x_ref[...] + ada_ref[5:6, :] * mix
        if final:
            y = _rms(out) * gf_ref[...]

            @pl.when(i * tm < N_CTX)
            def _():
                yp_ref[...] = y

            @pl.when(i * tm >= N_CTX)
            def _():
                yl_ref[...] = y
        else:
            o_ref[...] = out

    @pl.when(i % 2 == 0)
    def _():
        run((ya0, yb0, 0), (ya1, yb1, 1))

    @pl.when(i % 2 == 1)
    def _():
        run((ya1, yb1, 1), (ya0, yb0, 0))


def _combine(d1, d2, x, ada, layer, gate_w, ys, final_g=None):
    tm = COMBINE_TM
    final = final_g is not None
    ctx_tiles = N_CTX // tm
    row_buf = pltpu.VMEM((tm, 1, D_MODEL), F32)
    in_specs = [
        pl.BlockSpec((tm, D_MODEL), lambda i, *_: (i, 0)),
        pl.BlockSpec((None, None, 6, D_MODEL),
                     lambda i, *_: (layer, _cond_of_tile(i, tm), 0, 0)),
        pl.BlockSpec((tm, LANES), lambda i, *_: (i, 0)),
    ]
    if final:
        in_specs.append(pl.BlockSpec((1, D_MODEL), lambda i, *_: (0, 0)))
        out_specs = [
            pl.BlockSpec((tm, D_MODEL), lambda i, *_: (jnp.minimum(i, ctx_tiles - 1), 0)),
            pl.BlockSpec((tm, D_MODEL), lambda i, *_: (jnp.maximum(i - ctx_tiles, 0), 0)),
        ]
        out_shape = [jax.ShapeDtypeStruct((N_CTX, D_MODEL), F32),
                     jax.ShapeDtypeStruct((N_LAT, D_MODEL), F32)]
        args = (d1, d2, x, ada, gate_w, final_g, ys)
    else:
        out_specs = pl.BlockSpec((tm, D_MODEL), lambda i, *_: (i, 0))
        out_shape = jax.ShapeDtypeStruct((N_TOK, D_MODEL), F32)
        args = (d1, d2, x, ada, gate_w, ys)
    in_specs.append(pl.BlockSpec(memory_space=pl.ANY))
    return pl.pallas_call(
        functools.partial(_combine_kernel, final=final),
        grid_spec=pltpu.PrefetchScalarGridSpec(
            num_scalar_prefetch=2,
            grid=(N_TOK // tm,),
            in_specs=in_specs,
            out_specs=out_specs,
            scratch_shapes=[row_buf, row_buf, row_buf, row_buf,
                            pltpu.VMEM((tm, D_MODEL), F32), pltpu.VMEM((tm, D_MODEL), F32),
                            pltpu.SemaphoreType.DMA((2,))],
        ),
        out_shape=out_shape,
        compiler_params=_params("arbitrary"),
        name="moe_combine",
    )(*args)


def _moe_layer(x, ada, layer, m, g2, w_router, w_gate, w_up, w_down, final_g=None):
    h_rows, route, gate_w, counts = _route(x, ada, layer, g2, w_router)
    cnt = counts[0, :N_EXPERTS].astype(jnp.int32)
    tiles_per_expert = (cnt + MOE_TG - 1) // MOE_TG
    tile_end = jnp.cumsum(tiles_per_expert)
    row_start = (tile_end - tiles_per_expert) * MOE_TG
    n_tiles = tile_end[-1:]
    t = jnp.minimum(jnp.arange(MOE_MAX_TILES, dtype=jnp.int32), n_tiles - 1)
    tile_expert = jnp.sum((t[:, None] >= tile_end[None, :]).astype(jnp.int32), axis=1)
    experts = jnp.arange(N_EXPERTS, dtype=jnp.int32)
    d1 = jnp.sum(jnp.where(route[0][:, None] == experts, row_start, 0), axis=1) + route[2]
    d2 = jnp.sum(jnp.where(route[1][:, None] == experts, row_start, 0), axis=1) + route[3]
    spare_tile = jnp.minimum(n_tiles + experts, MOE_MAX_TILES - 1)
    fill_start = jnp.concatenate([row_start + cnt, spare_tile * MOE_TG])
    fill_rows = jnp.concatenate([tiles_per_expert * MOE_TG - cnt,
                                 jnp.where(n_tiles + experts < MOE_MAX_TILES, MOE_TG, 0)])
    hs, wg_bf, wu_bf, wd_bf = _dispatch(d1, d2, fill_start, fill_rows, h_rows, m, w_gate, w_up, w_down)
    ys = _experts(tile_expert, n_tiles, hs, wg_bf, wu_bf, wd_bf)
    return _combine(d1, d2, x, ada, layer, gate_w, ys, final_g)


assert DEPTH % 2 == 0


def _rope_tables(dim):
    half = dim // 2
    quarter = half // 2
    t = np.arange(DEC_SEQ)
    pos = np.stack([t // GRID_W, t % GRID_W], axis=1).astype(np.float32)
    freqs = (ROPE_THETA ** (-np.arange(quarter, dtype=np.float32) / quarter)).astype(np.float32)
    ang = pos[:, :, None] * freqs[None, None, :]
    cos = np.cos(ang.astype(np.float64))
    sin = np.sin(ang.astype(np.float64))
    cos_t = np.concatenate([cos, cos], axis=-1).reshape(DEC_SEQ, dim)
    sin_t = np.concatenate([-sin, sin], axis=-1).reshape(DEC_SEQ, dim)
    lane = np.arange(dim)
    swap = np.where(lane % half < quarter, lane + quarter, lane - quarter)
    return cos_t.astype(np.float32), sin_t.astype(np.float32), swap


def kernel(x_prompt, x_sample, cache_mla_ckv, cache_mla_kpe, cache_gqa_k, cache_gqa_v, c, c_ctx, w_ada, b_ada, norm1_g, norm2_g, conv_w_in, conv_k, conv_w_out, mla_w_down, mla_q_norm_g, mla_kv_norm_g, mla_w_uq, mla_w_ukv, mla_w_o, gqa_w_qkv, gqa_q_norm_g, gqa_k_norm_g, gqa_w_o, ffn_w_gate, ffn_w_up, ffn_w_down, moe_w_router, moe_w_gate, moe_w_up, moe_w_down, final_norm_g):
    xs = (x_prompt.reshape(N_CTX, D_MODEL), x_sample.reshape(N_LAT, D_MODEL))
    conv_w_in, conv_w_out = conv_w_in.astype(BF16), conv_w_out.astype(BF16)
    ffn_w_gate, ffn_w_up, ffn_w_down = (w.astype(BF16) for w in (ffn_w_gate, ffn_w_up, ffn_w_down))
    cond = jnp.concatenate(
        [c_ctx[None, :], c, jnp.zeros((N_COND - 1 - DEC_BATCH, D_MODEL), F32)], axis=0)
    ada = _adaln_table(cond, w_ada, b_ada)

    new_ckv, new_kpe, new_k, new_v = [], [], [], []
    for i in range(DEPTH):
        j = i // N_MIXERS
        kind = i % N_MIXERS
        g1 = norm1_g[i][None, :]
        if kind == 0:
            x = _conv_layer(xs if i == 0 else (x,), ada, i, j, g1, conv_w_in, conv_k, conv_w_out)
        elif kind == 1:
            cos64, sin64, swap64 = _rope_tables(MLA_ROPE)
            wd = mla_w_down[j]
            kpe_cols = wd[:, MLA_Q_RANK + MLA_KV_RANK:]
            w_down = jnp.concatenate([wd, kpe_cols[:, swap64]], axis=1).astype(BF16)
            wq = mla_w_uq[j].reshape(MLA_Q_RANK, MLA_HEADS, MLA_QK)
            wq_nope = wq[:, :, :MLA_NOPE].reshape(MLA_Q_RANK, MLA_Q_NOPE_COLS)
            wq_rope = wq[:, :, MLA_NOPE:]
            w_uq = jnp.concatenate(
                [wq_nope, wq_rope.reshape(MLA_Q_RANK, MLA_Q_ROPE_COLS),
                 wq_rope[:, :, swap64].reshape(MLA_Q_RANK, MLA_Q_ROPE_COLS)], axis=1).astype(BF16)
            wkv = mla_w_ukv[j].reshape(MLA_KV_RANK, MLA_HEADS, MLA_NOPE + MLA_V)
            w_ukv = jnp.concatenate(
                [wkv[:, :, :MLA_NOPE].reshape(MLA_KV_RANK, MLA_Q_NOPE_COLS),
                 wkv[:, :, MLA_NOPE:].reshape(MLA_KV_RANK, MLA_HEADS * MLA_V)], axis=1).astype(BF16)
            rope = (jnp.asarray(np.tile(cos64, (1, MLA_HEADS))), jnp.asarray(np.tile(sin64, (1, MLA_HEADS))),
                    jnp.asarray(cos64), jnp.asarray(sin64))
            x, ckv_p, kpe_p = _mla_layer(
                x, ada, i, g1, w_down, mla_q_norm_g[j][None, :], mla_kv_norm_g[j][None, :],
                w_uq, w_ukv, mla_w_o[j].astype(BF16), cache_mla_ckv[:, j], cache_mla_kpe[:, j], rope)
            new_ckv.append(ckv_p.reshape(BATCH, SEQ, MLA_KV_RANK))
            new_kpe.append(kpe_p.reshape(BATCH, SEQ, MLA_ROPE))
        else:
            cos128, sin128, swap128 = _rope_tables(GQA_HEAD_DIM)
            wqkv = gqa_w_qkv[j]
            wq = wqkv[:, :GQA_Q_COLS].reshape(D_MODEL, GQA_HEADS, GQA_HEAD_DIM)
            wk = wqkv[:, GQA_Q_COLS:GQA_Q_COLS + GQA_KV_COLS].reshape(D_MODEL, GQA_KV_HEADS, GQA_HEAD_DIM)
            w_qkv_ctx = wqkv.astype(BF16)
            w_qkv_lat = jnp.concatenate(
                [wqkv, wq[:, :, swap128].reshape(D_MODEL, GQA_Q_COLS),
                 wk[:, :, swap128].reshape(D_MODEL, GQA_KV_COLS)], axis=1).astype(BF16)
            gq = gqa_q_norm_g[j]
            gk = gqa_k_norm_g[j]
            rope = (jnp.asarray(np.tile(cos128, (1, GQA_HEADS))), jnp.asarray(np.tile(sin128, (1, GQA_HEADS))),
                    jnp.asarray(np.tile(cos128, (1, GQA_KV_HEADS))), jnp.asarray(np.tile(sin128, (1, GQA_KV_HEADS))))
            x, k_p, v_p = _gqa_layer(
                x, ada, i, g1, w_qkv_ctx, w_qkv_lat,
                jnp.tile(gq, GQA_HEADS)[None, :], jnp.tile(gk, GQA_KV_HEADS)[None, :],
                jnp.tile(gq[swap128], GQA_HEADS)[None, :], jnp.tile(gk[swap128], GQA_KV_HEADS)[None, :],
                gqa_w_o[j].astype(BF16),
                cache_gqa_k[:, j].reshape(DEC_BATCH, PAST_LEN, GQA_KV_COLS),
                cache_gqa_v[:, j].reshape(DEC_BATCH, PAST_LEN, GQA_KV_COLS), rope)
            new_k.append(k_p.reshape(BATCH, SEQ, GQA_KV_HEADS, GQA_HEAD_DIM))
            new_v.append(v_p.reshape(BATCH, SEQ, GQA_KV_HEADS, GQA_HEAD_DIM))

        m = i // 2
        g2 = norm2_g[i][None, :]
        if i % 2 == 0:
            x = _ffn_layer(x, ada, i, m, g2, ffn_w_gate, ffn_w_up, ffn_w_down)
        else:
            w_router = jnp.pad(moe_w_router[m], ((0, 0), (0, LANES - N_EXPERTS)))
            final_g = final_norm_g[None, :] if i == DEPTH - 1 else None
            x = _moe_layer(x, ada, i, m, g2, w_router, moe_w_gate, moe_w_up, moe_w_down, final_g)

    y_prompt, y_sample = x
    y_prompt = y_prompt.reshape(BATCH, SEQ, D_MODEL)
    y_sample = y_sample.reshape(DEC_BATCH, DEC_SEQ, D_MODEL)
    return (y_prompt, y_sample,
            jnp.stack(new_ckv, axis=1), jnp.stack(new_kpe, axis=1),
            jnp.stack(new_k, axis=1), jnp.stack(new_v, axis=1))
```

```python
import functools

import numpy as np
import jax
import jax.numpy as jnp
from jax import lax
from jax.experimental import pallas as pl
from jax.experimental.pallas import tpu as pltpu

D_MODEL = 1024
BATCH = 32
SEQ = 256
DEPTH = 4
DEC_BATCH = 2
DEC_SEQ = 1024
PAST_LEN = 512
GRID_W = 64
N_MIXERS = 3
CONV_WIDTH = 3
MLA_HEADS = 8
MLA_NOPE = 128
MLA_ROPE = 64
MLA_V = 128
MLA_Q_RANK = 384
MLA_KV_RANK = 256
GQA_HEADS = 8
GQA_KV_HEADS = 2
GQA_HEAD_DIM = 128
D_FF = 2816
N_EXPERTS = 8
MOE_FF = 2048
ROPE_THETA = 10000.0
EPS = 1e-6

N_CTX = BATCH * SEQ
N_LAT = DEC_BATCH * DEC_SEQ
N_TOK = N_CTX + N_LAT
N_COND = 8
LANES = 128
Q_TILE = 256
VMEM_LIMIT = 56 * 1024 * 1024

F32 = jnp.float32
BF16 = jnp.bfloat16


def _dot(a, b):
    return jnp.dot(a, b, preferred_element_type=F32)


def _dot_t(a, b):
    return lax.dot_general(a, b, (((1,), (1,)), ((), ())), preferred_element_type=F32)


def _rms(x):
    return x * lax.rsqrt(jnp.mean(x * x, axis=-1, keepdims=True) + EPS)


def _modulate(x, g, shift, scale):
    return _rms(x) * g * (1.0 + scale) + shift


def _silu(x):
    return x * jax.nn.sigmoid(x)


def _cond_of_tile(i, tm):
    start = i * tm
    return jnp.where(start < N_CTX, 0, 1 + (start - N_CTX) // DEC_SEQ)


def _resident(shape):
    return pl.BlockSpec(shape, lambda *_: (0,) * len(shape), pipeline_mode=pl.Buffered(1))


def _resident_at(index, shape):
    return pl.BlockSpec((None,) + tuple(shape), lambda *_: (index,) + (0,) * len(shape),
                        pipeline_mode=pl.Buffered(1))


def _params(*sem):
    return pltpu.CompilerParams(dimension_semantics=sem, vmem_limit_bytes=VMEM_LIMIT)


def _adaln_kernel(cond_ref, w_ref, b_ref, o_ref):
    s = _silu(cond_ref[...]).astype(BF16)
    o_ref[...] = _dot(s, w_ref[...].astype(BF16)) + b_ref[...]


def _adaln_table(cond, w_ada, b_ada):
    tn = 1536
    out = pl.pallas_call(
        _adaln_kernel,
        grid=(DEPTH, 6 * D_MODEL // tn),
        in_specs=[
            pl.BlockSpec((N_COND, D_MODEL), lambda l, j: (0, 0)),
            pl.BlockSpec((None, D_MODEL, tn), lambda l, j: (l, 0, j)),
            pl.BlockSpec((None, 1, tn), lambda l, j: (l, 0, j)),
        ],
        out_specs=pl.BlockSpec((None, N_COND, tn), lambda l, j: (l, 0, j)),
        out_shape=jax.ShapeDtypeStruct((DEPTH, N_COND, 6 * D_MODEL), F32),
        compiler_params=_params("arbitrary", "arbitrary"),
        name="adaln_table",
    )(cond, w_ada, b_ada.reshape(DEPTH, 1, 6 * D_MODEL))
    return out.reshape(DEPTH, N_COND, 6, D_MODEL)


def _ada_spec(layer, tm):
    return pl.BlockSpec((None, None, 6, D_MODEL), lambda i: (layer, _cond_of_tile(i, tm), 0, 0))


def _ada_spec_at(layer, cond_fn):
    return pl.BlockSpec((None, None, 6, D_MODEL), lambda i: (layer, cond_fn(i), 0, 0))


CONV_TM = 1024
CONV_CC = 256


def _conv_kernel(*refs, split):
    if split:
        xp_ref, x_ref, ada_ref, g1_ref, win_ref, ck_ref, wout_ref, o_ref, v_scr = refs
    else:
        x_ref, ada_ref, g1_ref, win_ref, ck_ref, wout_ref, o_ref, v_scr = refs
    tm = x_ref.shape[0]
    i = pl.program_id(0)
    x = x_ref[...]
    if split:
        x = jnp.where(i * tm < N_CTX, xp_ref[...], x)
    h = _modulate(x, g1_ref[...], ada_ref[0:1, :], ada_ref[1:2, :]).astype(BF16)
    period = jnp.where(i * tm < N_CTX, SEQ, DEC_SEQ)
    pos = lax.broadcasted_iota(jnp.int32, (tm, 1), 0) & (period - 1)
    first = pos == 0
    last = pos == period - 1
    for j in range(D_MODEL // CONV_CC):
        lo = j * CONV_CC
        b_gate = _dot(h, win_ref[:, lo:lo + CONV_CC])
        c_gate = _dot(h, win_ref[:, D_MODEL + lo:D_MODEL + lo + CONV_CC])
        x_in = _dot(h, win_ref[:, 2 * D_MODEL + lo:2 * D_MODEL + lo + CONV_CC])
        u = c_gate * x_in
        u_prev = jnp.where(first, 0.0, pltpu.roll(u, 1, 0))
        u_next = jnp.where(last, 0.0, pltpu.roll(u, tm - 1, 0))
        conv = (ck_ref[0:1, lo:lo + CONV_CC] * u_prev + ck_ref[1:2, lo:lo + CONV_CC] * u
                + ck_ref[2:3, lo:lo + CONV_CC] * u_next)
        v_scr[:, lo:lo + CONV_CC] = (b_gate * conv).astype(BF16)
    o_ref[...] = x + ada_ref[2:3, :] * _dot(v_scr[...], wout_ref[...])


def _conv_layer(xs, ada, layer, j, g1, w_in, conv_k, w_out):
    tm = CONV_TM
    ctx_tiles = N_CTX // tm
    if len(xs) == 1:
        x_specs = [pl.BlockSpec((tm, D_MODEL), lambda i: (i, 0))]
    else:
        x_specs = [pl.BlockSpec((tm, D_MODEL), lambda i: (jnp.minimum(i, ctx_tiles - 1), 0)),
                   pl.BlockSpec((tm, D_MODEL), lambda i: (jnp.maximum(i - ctx_tiles, 0), 0))]
    return pl.pallas_call(
        functools.partial(_conv_kernel, split=len(xs) == 2),
        grid=(N_TOK // tm,),
        in_specs=x_specs + [
            _ada_spec(layer, tm),
            _resident((1, D_MODEL)),
            _resident_at(j, (D_MODEL, 3 * D_MODEL)),
            _resident_at(j, (CONV_WIDTH, D_MODEL)),
            _resident_at(j, (D_MODEL, D_MODEL)),
        ],
        out_specs=pl.BlockSpec((tm, D_MODEL), lambda i: (i, 0)),
        out_shape=jax.ShapeDtypeStruct((N_TOK, D_MODEL), F32),
        scratch_shapes=[pltpu.VMEM((tm, D_MODEL), BF16)],
        compiler_params=_params("arbitrary"),
        name=f"conv_mixer_{layer}",
    )(*xs, ada, g1, w_in, conv_k, w_out)


FFN_TM = 512
FFN_CHUNKS = ((0, 1536), (1536, D_FF))


def _ffn_kernel(x_ref, ada_ref, g2_ref, wg_ref, wu_ref, wd_ref, o_ref):
    x = x_ref[...]
    h = _modulate(x, g2_ref[...], ada_ref[3:4, :], ada_ref[4:5, :]).astype(BF16)
    f = None
    for lo, hi in FFN_CHUNKS:
        g = _dot(h, wg_ref[:, lo:hi])
        u = _dot(h, wu_ref[:, lo:hi])
        part = _dot((_silu(g) * u).astype(BF16), wd_ref[lo:hi, :])
        f = part if f is None else f + part
    o_ref[...] = x + ada_ref[5:6, :] * f


def _ffn_layer(x, ada, layer, m, g2, w_gate, w_up, w_down):
    tm = FFN_TM
    return pl.pallas_call(
        _ffn_kernel,
        grid=(N_TOK // tm,),
        in_specs=[
            pl.BlockSpec((tm, D_MODEL), lambda i: (i, 0)),
            _ada_spec(layer, tm),
            _resident((1, D_MODEL)),
            _resident_at(m, (D_MODEL, D_FF)),
            _resident_at(m, (D_MODEL, D_FF)),
            _resident_at(m, (D_FF, D_MODEL)),
        ],
        out_specs=pl.BlockSpec((tm, D_MODEL), lambda i: (i, 0)),
        out_shape=jax.ShapeDtypeStruct((N_TOK, D_MODEL), F32),
        compiler_params=_params("arbitrary"),
        name=f"dense_ffn_{layer}",
    )(x, ada, g2, w_gate, w_up, w_down)


def _attention(qk_parts, v_ref, heads, kv_of_head, dv, scale, o_scr, finish, keys_of):
    def q_block(rows):
        keys = keys_of(rows)
        for h in range(heads):
            g = kv_of_head(h)
            s = None
            for q_ref, k_ref, width, k_shared in qk_parts:
                kb = 0 if k_shared else g
                part = _dot_t(q_ref[rows, h * width:(h + 1) * width],
                              k_ref[keys, kb * width:(kb + 1) * width])
                s = part if s is None else s + part
            s = s * scale
            e = jnp.exp(s - jnp.max(s, axis=-1, keepdims=True))
            denom = jnp.sum(e, axis=-1, keepdims=True)
            o = _dot(e.astype(BF16), v_ref[keys, g * dv:(g + 1) * dv]) / denom
            o_scr[rows, h * dv:(h + 1) * dv] = o.astype(BF16)
        finish(rows)

    _for_row_blocks(o_scr.shape[0], q_block)


def _keys_of(latent):
    assert SEQ == Q_TILE
    return (lambda rows: slice(None)) if latent else (lambda rows: rows)


def _for_row_blocks(n_rows, fn):
    if n_rows <= 2 * Q_TILE:
        for b in range(n_rows // Q_TILE):
            fn(pl.ds(b * Q_TILE, Q_TILE))
    else:
        def body(b, carry):
            fn(pl.ds(pl.multiple_of(b * Q_TILE, Q_TILE), Q_TILE))
            return carry
        lax.fori_loop(0, n_rows // Q_TILE, body, 0)


MLA_QK = MLA_NOPE + MLA_ROPE
MLA_DOWN_COLS = MLA_Q_RANK + MLA_KV_RANK + 2 * MLA_ROPE
MLA_Q_NOPE_COLS = MLA_HEADS * MLA_NOPE
MLA_Q_ROPE_COLS = MLA_HEADS * MLA_ROPE


def _ctx_then_latent(body, n_in, refs):
    ins, lat_ref, rest = refs[:n_in], refs[n_in], refs[n_in + 1:]
    o_ref = rest[0]
    i = pl.program_id(0)

    @pl.when(i < CTX_TILES)
    def _():
        body(*ins, *rest, latent=False)

    @pl.when(i >= CTX_TILES)
    def _():
        o_ref[...] = lat_ref[...]


def _ctx_specs(out_tails):
    def tile(n_trailing):
        return lambda i: (jnp.minimum(i, CTX_TILES - 1),) + (0,) * n_trailing
    x_spec = pl.BlockSpec((CTX_TM, D_MODEL), tile(1))
    lat_spec = pl.BlockSpec((CTX_TM, D_MODEL), lambda i: (jnp.maximum(i - CTX_TILES, 0), 0))
    out_specs = [pl.BlockSpec((CTX_TM, D_MODEL), lambda i: (i, 0))]
    out_specs += [pl.BlockSpec((CTX_TM,) + tail, tile(len(tail))) for tail in out_tails]
    return x_spec, lat_spec, out_specs


CTX_TM = 2 * SEQ
CTX_TILES = N_CTX // CTX_TM
CTX_STEPS = CTX_TILES + N_LAT // CTX_TM


def _mla_kernel(*refs, latent):
    if latent:
        _mla_body(*refs, latent=True)
    else:
        _ctx_then_latent(_mla_body, 9, refs)


def _mla_body(*refs, latent):
    if latent:
        (x_ref, ada_ref, g1_ref, wdown_ref, gq_ref, gkv_ref, wuq_ref, wukv_ref, wo_ref,
         cosq_ref, sinq_ref, cosk_ref, sink_ref, cckv_ref, ckpe_ref,
         o_ref, qn_scr, qr_scr, kn_scr, kr_scr, v_scr, o_scr) = refs
    else:
        (x_ref, ada_ref, g1_ref, wdown_ref, gq_ref, gkv_ref, wuq_ref, wukv_ref, wo_ref,
         o_ref, ckv_out_ref, kpe_out_ref, qn_scr, qr_scr, kn_scr, kr_scr, v_scr, o_scr) = refs
    tm = x_ref.shape[0]
    kpe_lo = MLA_Q_RANK + MLA_KV_RANK

    def project(rows):
        h = _modulate(x_ref[rows, :], g1_ref[...], ada_ref[0:1, :], ada_ref[1:2, :]).astype(BF16)
        down = _dot(h, wdown_ref[...])
        c_q = down[:, :MLA_Q_RANK]
        c_kv = down[:, MLA_Q_RANK:kpe_lo]
        kpe = down[:, kpe_lo:kpe_lo + MLA_ROPE]
        q = _dot((_rms(c_q) * gq_ref[...]).astype(BF16), wuq_ref[...])
        ckv_n = _rms(c_kv) * gkv_ref[...]
        kv = _dot(ckv_n.astype(BF16), wukv_ref[...])
        qn_scr[rows, :] = q[:, :MLA_Q_NOPE_COLS].astype(BF16)
        q_rope = q[:, MLA_Q_NOPE_COLS:MLA_Q_NOPE_COLS + MLA_Q_ROPE_COLS]
        kn_scr[rows, :] = kv[:, :MLA_Q_NOPE_COLS].astype(BF16)
        v_scr[rows, :] = kv[:, MLA_Q_NOPE_COLS:].astype(BF16)
        if latent:
            q_swap = q[:, MLA_Q_NOPE_COLS + MLA_Q_ROPE_COLS:]
            q_rope = q_rope * cosq_ref[rows, :] + q_swap * sinq_ref[rows, :]
            kpe_swap = down[:, kpe_lo + MLA_ROPE:kpe_lo + 2 * MLA_ROPE]
            kpe = kpe * cosk_ref[rows, :] + kpe_swap * sink_ref[rows, :]
        else:
            ckv_out_ref[rows, :] = ckv_n
            kpe_out_ref[rows, :] = kpe
        qr_scr[rows, :] = q_rope.astype(BF16)
        kr_scr[rows, :] = kpe.astype(BF16)

    _for_row_blocks(tm, project)

    if latent:
        def expand_cache(rows):
            kv_c = _dot(cckv_ref[rows, :].astype(BF16), wukv_ref[...])
            dst = pl.ds(tm + rows.start, Q_TILE)
            kn_scr[dst, :] = kv_c[:, :MLA_Q_NOPE_COLS].astype(BF16)
            v_scr[dst, :] = kv_c[:, MLA_Q_NOPE_COLS:].astype(BF16)
            kr_scr[dst, :] = ckpe_ref[rows, :].astype(BF16)

        _for_row_blocks(PAST_LEN, expand_cache)

    def finish(rows):
        o_ref[rows, :] = x_ref[rows, :] + ada_ref[2:3, :] * _dot(o_scr[rows, :], wo_ref[...])

    _attention([(qn_scr, kn_scr, MLA_NOPE, False), (qr_scr, kr_scr, MLA_ROPE, True)],
               v_scr, MLA_HEADS, lambda hd: hd, MLA_V, MLA_QK ** -0.5, o_scr, finish,
               _keys_of(latent))


def _mla_scratch(tm, sk):
    return [
        pltpu.VMEM((tm, MLA_Q_NOPE_COLS), BF16),
        pltpu.VMEM((tm, MLA_Q_ROPE_COLS), BF16),
        pltpu.VMEM((sk, MLA_Q_NOPE_COLS), BF16),
        pltpu.VMEM((sk, MLA_ROPE), BF16),
        pltpu.VMEM((sk, MLA_HEADS * MLA_V), BF16),
        pltpu.VMEM((tm, MLA_HEADS * MLA_V), BF16),
    ]


def _mla_layer(x, ada, layer, g1, w_down, g_q, g_kv, w_uq, w_ukv, w_o,
               cache_ckv, cache_kpe, rope):
    n_uq = w_uq.shape[1]
    weights = [
        _resident((1, D_MODEL)),
        _resident((D_MODEL, MLA_DOWN_COLS)),
        _resident((1, MLA_Q_RANK)),
        _resident((1, MLA_KV_RANK)),
        _resident((MLA_Q_RANK, n_uq)),
        _resident((MLA_KV_RANK, 2 * MLA_Q_NOPE_COLS)),
        _resident((MLA_HEADS * MLA_V, D_MODEL)),
    ]
    ctx_weights = list(weights)
    ctx_weights[4] = pl.BlockSpec((MLA_Q_RANK, MLA_Q_NOPE_COLS + MLA_Q_ROPE_COLS), lambda *_: (0, 0),
                                  pipeline_mode=pl.Buffered(1))
    weight_args = (g1, w_down, g_q, g_kv, w_uq, w_ukv, w_o)
    cos_q, sin_q, cos_k, sin_k = rope
    ctx_blocks = N_CTX // DEC_SEQ
    lat_out = pl.pallas_call(
        functools.partial(_mla_kernel, latent=True),
        grid=(DEC_BATCH,),
        in_specs=[pl.BlockSpec((DEC_SEQ, D_MODEL), lambda b: (ctx_blocks + b, 0)),
                  _ada_spec_at(layer, lambda b: 1 + b)] + weights + [
            _resident((DEC_SEQ, MLA_Q_ROPE_COLS)),
            _resident((DEC_SEQ, MLA_Q_ROPE_COLS)),
            _resident((DEC_SEQ, MLA_ROPE)),
            _resident((DEC_SEQ, MLA_ROPE)),
            pl.BlockSpec((None, PAST_LEN, MLA_KV_RANK), lambda b: (b, 0, 0)),
            pl.BlockSpec((None, PAST_LEN, MLA_ROPE), lambda b: (b, 0, 0)),
        ],
        out_specs=pl.BlockSpec((DEC_SEQ, D_MODEL), lambda b: (b, 0)),
        out_shape=jax.ShapeDtypeStruct((N_LAT, D_MODEL), F32),
        scratch_shapes=_mla_scratch(DEC_SEQ, DEC_SEQ + PAST_LEN),
        compiler_params=_params("arbitrary"),
        name=f"mla_latent_{layer}",
    )(x, ada, *weight_args, cos_q, sin_q, cos_k, sin_k, cache_ckv, cache_kpe)

    x_spec, lat_spec, out_specs = _ctx_specs(((MLA_KV_RANK,), (MLA_ROPE,)))
    return pl.pallas_call(
        functools.partial(_mla_kernel, latent=False),
        grid=(CTX_STEPS,),
        in_specs=[x_spec, _ada_spec_at(layer, lambda i: 0)] + ctx_weights + [lat_spec],
        out_specs=out_specs,
        out_shape=[
            jax.ShapeDtypeStruct((N_TOK, D_MODEL), F32),
            jax.ShapeDtypeStruct((N_CTX, MLA_KV_RANK), F32),
            jax.ShapeDtypeStruct((N_CTX, MLA_ROPE), F32),
        ],
        scratch_shapes=_mla_scratch(CTX_TM, CTX_TM),
        compiler_params=_params("arbitrary"),
        name=f"mla_ctx_{layer}",
    )(x, ada, *weight_args, lat_out)


GQA_Q_COLS = GQA_HEADS * GQA_HEAD_DIM
GQA_KV_COLS = GQA_KV_HEADS * GQA_HEAD_DIM
GQA_REP = GQA_HEADS // GQA_KV_HEADS


def _head_rms(t, n_heads):
    cols = []
    for hd in range(n_heads):
        th = t[:, hd * GQA_HEAD_DIM:(hd + 1) * GQA_HEAD_DIM]
        r = lax.rsqrt(jnp.mean(th * th, axis=-1, keepdims=True) + EPS)
        cols.append(jnp.broadcast_to(r, th.shape))
    return jnp.concatenate(cols, axis=-1)


def _gqa_kernel(*refs, latent):
    if latent:
        _gqa_body(*refs, latent=True)
    else:
        _ctx_then_latent(_gqa_body, 7, refs)


def _gqa_body(*refs, latent):
    if latent:
        (x_ref, ada_ref, g1_ref, wqkv_ref, gq_ref, gk_ref, wo_ref,
         gqs_ref, gks_ref, cosq_ref, sinq_ref, cosk_ref, sink_ref, ck_ref, cv_ref,
         o_ref, q_scr, k_scr, v_scr, o_scr) = refs
    else:
        (x_ref, ada_ref, g1_ref, wqkv_ref, gq_ref, gk_ref, wo_ref,
         o_ref, k_out_ref, v_out_ref, q_scr, k_scr, v_scr, o_scr) = refs
    tm = x_ref.shape[0]

    def project(rows):
        h = _modulate(x_ref[rows, :], g1_ref[...], ada_ref[0:1, :], ada_ref[1:2, :]).astype(BF16)
        qkv = _dot(h, wqkv_ref[...])
        q_raw = qkv[:, :GQA_Q_COLS]
        k_raw = qkv[:, GQA_Q_COLS:GQA_Q_COLS + GQA_KV_COLS]
        v = qkv[:, GQA_Q_COLS + GQA_KV_COLS:GQA_Q_COLS + 2 * GQA_KV_COLS]
        q_r = _head_rms(q_raw, GQA_HEADS)
        k_r = _head_rms(k_raw, GQA_KV_HEADS)
        q = q_raw * q_r * gq_ref[...]
        k = k_raw * k_r * gk_ref[...]
        if latent:
            lo = GQA_Q_COLS + 2 * GQA_KV_COLS
            q_swap = qkv[:, lo:lo + GQA_Q_COLS] * q_r * gqs_ref[...]
            k_swap = qkv[:, lo + GQA_Q_COLS:lo + GQA_Q_COLS + GQA_KV_COLS] * k_r * gks_ref[...]
            q = q * cosq_ref[rows, :] + q_swap * sinq_ref[rows, :]
            k = k * cosk_ref[rows, :] + k_swap * sink_ref[rows, :]
        else:
            for g in range(GQA_KV_HEADS):
                cols = slice(g * GQA_HEAD_DIM, (g + 1) * GQA_HEAD_DIM)
                k_out_ref[rows, g, :] = k[:, cols]
                v_out_ref[rows, g, :] = v[:, cols]
        q_scr[rows, :] = q.astype(BF16)
        k_scr[rows, :] = k.astype(BF16)
        v_scr[rows, :] = v.astype(BF16)

    _for_row_blocks(tm, project)

    if latent:
        k_scr[tm:, :] = ck_ref[...].astype(BF16)
        v_scr[tm:, :] = cv_ref[...].astype(BF16)

    def finish(rows):
        o_ref[rows, :] = x_ref[rows, :] + ada_ref[2:3, :] * _dot(o_scr[rows, :], wo_ref[...])

    _attention([(q_scr, k_scr, GQA_HEAD_DIM, False)], v_scr, GQA_HEADS,
               lambda hd: hd // GQA_REP, GQA_HEAD_DIM, GQA_HEAD_DIM ** -0.5, o_scr, finish,
               _keys_of(latent))


def _gqa_scratch(tm, sk):
    return [
        pltpu.VMEM((tm, GQA_Q_COLS), BF16),
        pltpu.VMEM((sk, GQA_KV_COLS), BF16),
        pltpu.VMEM((sk, GQA_KV_COLS), BF16),
        pltpu.VMEM((tm, GQA_Q_COLS), BF16),
    ]


def _gqa_layer(x, ada, layer, g1, w_qkv_ctx, w_qkv_lat, g_q, g_k, g_q_swap, g_k_swap, w_o,
               cache_k, cache_v, rope):
    cos_q, sin_q, cos_k, sin_k = rope
    ctx_blocks = N_CTX // DEC_SEQ
    lat_out = pl.pallas_call(
        functools.partial(_gqa_kernel, latent=True),
        grid=(DEC_BATCH,),
        in_specs=[pl.BlockSpec((DEC_SEQ, D_MODEL), lambda b: (ctx_blocks + b, 0)),
                  _ada_spec_at(layer, lambda b: 1 + b),
                  _resident((1, D_MODEL)),
                  _resident(w_qkv_lat.shape),
                  _resident((1, GQA_Q_COLS)),
                  _resident((1, GQA_KV_COLS)),
                  _resident((GQA_Q_COLS, D_MODEL)),
                  _resident((1, GQA_Q_COLS)),
                  _resident((1, GQA_KV_COLS)),
                  _resident((DEC_SEQ, GQA_Q_COLS)),
                  _resident((DEC_SEQ, GQA_Q_COLS)),
                  _resident((DEC_SEQ, GQA_KV_COLS)),
                  _resident((DEC_SEQ, GQA_KV_COLS)),
                  pl.BlockSpec((None, PAST_LEN, GQA_KV_COLS), lambda b: (b, 0, 0)),
                  pl.BlockSpec((None, PAST_LEN, GQA_KV_COLS), lambda b: (b, 0, 0))],
        out_specs=pl.BlockSpec((DEC_SEQ, D_MODEL), lambda b: (b, 0)),
        out_shape=jax.ShapeDtypeStruct((N_LAT, D_MODEL), F32),
        scratch_shapes=_gqa_scratch(DEC_SEQ, DEC_SEQ + PAST_LEN),
        compiler_params=_params("arbitrary"),
        name=f"gqa_latent_{layer}",
    )(x, ada, g1, w_qkv_lat, g_q, g_k, w_o, g_q_swap, g_k_swap,
      cos_q, sin_q, cos_k, sin_k, cache_k, cache_v)

    kv_tail = (GQA_KV_HEADS, GQA_HEAD_DIM)
    x_spec, lat_spec, out_specs = _ctx_specs((kv_tail, kv_tail))
    return pl.pallas_call(
        functools.partial(_gqa_kernel, latent=False),
        grid=(CTX_STEPS,),
        in_specs=[x_spec,
                  _ada_spec_at(layer, lambda i: 0),
                  _resident((1, D_MODEL)),
                  _resident(w_qkv_ctx.shape),
                  _resident((1, GQA_Q_COLS)),
                  _resident((1, GQA_KV_COLS)),
                  _resident((GQA_Q_COLS, D_MODEL)),
                  lat_spec],
        out_specs=out_specs,
        out_shape=[
            jax.ShapeDtypeStruct((N_TOK, D_MODEL), F32),
            jax.ShapeDtypeStruct((N_CTX,) + kv_tail, F32),
            jax.ShapeDtypeStruct((N_CTX,) + kv_tail, F32),
        ],
        scratch_shapes=_gqa_scratch(CTX_TM, CTX_TM),
        compiler_params=_params("arbitrary"),
        name=f"gqa_ctx_{layer}",
    )(x, ada, g1, w_qkv_ctx, g_q, g_k, w_o, lat_out)


ROUTE_TM = 512
MOE_TG = 256
MOE_MAX_TILES = 2 * N_TOK // MOE_TG + N_EXPERTS
MOE_ROWS = MOE_MAX_TILES * MOE_TG
MOE_FILLS = 2 * N_EXPERTS
MOE_FC = 1024
COMBINE_TM = 256


def _route_kernel(x_ref, ada_ref, g2_ref, wr_hi_ref, wr_lo_ref, h_ref, dest_ref, gatew_ref,
                  count_ref, carry_scr):
    tm = x_ref.shape[0]

    @pl.when(pl.program_id(0) == 0)
    def _():
        carry_scr[...] = jnp.zeros_like(carry_scr)

    h = _modulate(x_ref[...], g2_ref[...], ada_ref[3:4, :], ada_ref[4:5, :])
    h_ref[...] = h.reshape(tm, 1, D_MODEL)
    h_hi = h.astype(BF16)
    h_lo = (h - h_hi.astype(F32)).astype(BF16)
    logits = _dot(h_hi, wr_hi_ref[...]) + (_dot(h_lo, wr_hi_ref[...]) + _dot(h_hi, wr_lo_ref[...]))
    lane = lax.broadcasted_iota(jnp.int32, logits.shape, 1)
    neg = jnp.float32(-jnp.inf)
    l1 = jnp.where(lane < N_EXPERTS, logits, neg)
    m1 = jnp.max(l1, axis=-1, keepdims=True)
    i1 = jnp.min(jnp.where(l1 == m1, lane, LANES), axis=-1, keepdims=True)
    l2 = jnp.where(lane == i1, neg, l1)
    m2 = jnp.max(l2, axis=-1, keepdims=True)
    i2 = jnp.min(jnp.where(l2 == m2, lane, LANES), axis=-1, keepdims=True)
    e2 = jnp.exp(m2 - m1)
    w1 = 1.0 / (1.0 + e2)
    w2 = e2 / (1.0 + e2)
    member = jnp.where((lane == i1) | (lane == i2), 1.0, 0.0)
    r = lax.broadcasted_iota(jnp.int32, (tm, tm), 0)
    c = lax.broadcasted_iota(jnp.int32, (tm, tm), 1)
    tri = jnp.where(c < r, 1.0, 0.0).astype(BF16)
    before = _dot(tri, member.astype(BF16)) + carry_scr[0:1, :]
    rank1 = jnp.sum(jnp.where(lane == i1, before, 0.0), axis=-1, keepdims=True).astype(jnp.int32)
    rank2 = jnp.sum(jnp.where(lane == i2, before, 0.0), axis=-1, keepdims=True).astype(jnp.int32)
    table = jnp.where(lane == 0, i1, jnp.where(lane == 1, i2, jnp.where(
        lane == 2, rank1, jnp.where(lane == 3, rank2, 0))))
    dest_ref[...] = jnp.transpose(table)[0:8, :]
    gatew_ref[...] = jnp.where(lane == 0, w1, jnp.where(lane == 1, w2, 0.0))
    carry_scr[...] = carry_scr[...] + jnp.sum(member, axis=0, keepdims=True)
    count_ref[...] = carry_scr[...]


def _route(x, ada, layer, g2, w_router):
    tm = ROUTE_TM
    w_router_hi = w_router.astype(BF16)
    return pl.pallas_call(
        _route_kernel,
        grid=(N_TOK // tm,),
        in_specs=[
            pl.BlockSpec((tm, D_MODEL), lambda i: (i, 0)),
            _ada_spec(layer, tm),
            _resident((1, D_MODEL)),
            _resident((D_MODEL, LANES)),
            _resident((D_MODEL, LANES)),
        ],
        out_specs=[
            pl.BlockSpec((tm, 1, D_MODEL), lambda i: (i, 0, 0)),
            pl.BlockSpec((8, tm), lambda i: (0, i)),
            pl.BlockSpec((tm, LANES), lambda i: (i, 0)),
            pl.BlockSpec((8, LANES), lambda i: (0, 0)),
        ],
        out_shape=[
            jax.ShapeDtypeStruct((N_TOK, 1, D_MODEL), F32),
            jax.ShapeDtypeStruct((8, N_TOK), jnp.int32),
            jax.ShapeDtypeStruct((N_TOK, LANES), F32),
            jax.ShapeDtypeStruct((8, LANES), F32),
        ],
        scratch_shapes=[pltpu.VMEM((8, LANES), F32)],
        compiler_params=_params("arbitrary"),
        name=f"moe_route_{layer}",
    )(x, ada, g2, w_router_hi, (w_router - w_router_hi.astype(F32)).astype(BF16))


def _dispatch_kernel(d1_ref, d2_ref, fill_start_ref, fill_rows_ref, h_ref, wg_ref, wu_ref, wd_ref,
                     hs_ref, wg_out_ref, wu_out_ref, wd_out_ref, zero_scr, sem):
    i = pl.program_id(0)
    tm = h_ref.shape[0]
    base = i * tm

    def fill_copy(k):
        n = fill_rows_ref[k]
        return pltpu.make_async_copy(zero_scr.at[pl.ds(0, n)],
                                     hs_ref.at[pl.ds(fill_start_ref[k], n)], sem.at[1])

    @pl.when(i == 0)
    def _():
        zero_scr[...] = jnp.zeros_like(zero_scr)
        for k in range(MOE_FILLS):
            @pl.when(fill_rows_ref[k] > 0)
            def _():
                fill_copy(k).start()
        for k in range(MOE_FILLS):
            @pl.when(fill_rows_ref[k] > 0)
            def _():
                fill_copy(k).wait()

    def row_copy(r, d_ref):
        return pltpu.make_async_copy(h_ref.at[r], hs_ref.at[d_ref[base + r]], sem.at[0])

    def issue(r, carry):
        row_copy(r, d1_ref).start(priority=0)
        row_copy(r, d2_ref).start(priority=1)
        return carry

    def drain():
        for _ in range(2):
            pltpu.make_async_copy(h_ref, hs_ref.at[pl.ds(0, tm)], sem.at[0]).wait()

    lax.fori_loop(0, tm, issue, 0)
    wg_out_ref[...] = wg_ref[...].astype(BF16)
    wu_out_ref[...] = wu_ref[...].astype(BF16)
    wd_out_ref[...] = wd_ref[...].astype(BF16)
    drain()


def _dispatch(d1, d2, fill_start, fill_rows, h_rows, m, w_gate, w_up, w_down):
    steps = 2 * N_EXPERTS
    tm = N_TOK // steps
    fh = MOE_FF // 2
    return pl.pallas_call(
        _dispatch_kernel,
        grid_spec=pltpu.PrefetchScalarGridSpec(
            num_scalar_prefetch=4,
            grid=(steps,),
            in_specs=[
                pl.BlockSpec((tm, 1, D_MODEL), lambda i, *_: (i, 0, 0)),
                pl.BlockSpec((None, None, D_MODEL, fh), lambda i, *_: (m, i // 2, 0, i % 2)),
                pl.BlockSpec((None, None, D_MODEL, fh), lambda i, *_: (m, i // 2, 0, i % 2)),
                pl.BlockSpec((None, None, fh, D_MODEL), lambda i, *_: (m, i // 2, i % 2, 0)),
            ],
            out_specs=[
                pl.BlockSpec(memory_space=pl.ANY),
                pl.BlockSpec((None, D_MODEL, fh), lambda i, *_: (i // 2, 0, i % 2)),
                pl.BlockSpec((None, D_MODEL, fh), lambda i, *_: (i // 2, 0, i % 2)),
                pl.BlockSpec((None, fh, D_MODEL), lambda i, *_: (i // 2, i % 2, 0)),
            ],
            scratch_shapes=[pltpu.VMEM((MOE_TG, 1, D_MODEL), F32),
                            pltpu.SemaphoreType.DMA((2,))],
        ),
        out_shape=[
            jax.ShapeDtypeStruct((MOE_ROWS, 1, D_MODEL), F32),
            jax.ShapeDtypeStruct((N_EXPERTS, D_MODEL, MOE_FF), BF16),
            jax.ShapeDtypeStruct((N_EXPERTS, D_MODEL, MOE_FF), BF16),
            jax.ShapeDtypeStruct((N_EXPERTS, MOE_FF, D_MODEL), BF16),
        ],
        compiler_params=_params("arbitrary"),
        name="moe_dispatch",
    )(d1, d2, fill_start, fill_rows, h_rows, w_gate, w_up, w_down)


def _experts_kernel(te_ref, nt_ref, hs_ref, wg_ref, wu_ref, wd_ref, ys_ref, h2d_scr):
    used = pl.program_id(0) < nt_ref[0]

    @pl.when(jnp.logical_not(used))
    def _():
        ys_ref[...] = jnp.zeros_like(ys_ref)

    @pl.when(used)
    def _():
        h2d_scr[...] = hs_ref[...].reshape(MOE_TG, D_MODEL)
        h = h2d_scr[...].astype(BF16)
        y = None
        for lo in range(0, MOE_FF, MOE_FC):
            g = _dot(h, wg_ref[:, lo:lo + MOE_FC])
            u = _dot(h, wu_ref[:, lo:lo + MOE_FC])
            part = _dot((_silu(g) * u).astype(BF16), wd_ref[lo:lo + MOE_FC, :])
            y = part if y is None else y + part
        ys_ref[...] = y.reshape(MOE_TG, 1, D_MODEL)


def _experts(tile_expert, n_tiles, hs, w_gate, w_up, w_down):
    rows = pl.BlockSpec((MOE_TG, 1, D_MODEL), lambda t, te, nt: (t, 0, 0))
    return pl.pallas_call(
        _experts_kernel,
        grid_spec=pltpu.PrefetchScalarGridSpec(
            num_scalar_prefetch=2,
            grid=(MOE_MAX_TILES,),
            in_specs=[
                rows,
                pl.BlockSpec((None, D_MODEL, MOE_FF), lambda t, te, nt: (te[t], 0, 0)),
                pl.BlockSpec((None, D_MODEL, MOE_FF), lambda t, te, nt: (te[t], 0, 0)),
                pl.BlockSpec((None, MOE_FF, D_MODEL), lambda t, te, nt: (te[t], 0, 0)),
            ],
            out_specs=rows,
            scratch_shapes=[pltpu.VMEM((MOE_TG, D_MODEL), F32)],
        ),
        out_shape=jax.ShapeDtypeStruct((MOE_ROWS, 1, D_MODEL), F32),
        compiler_params=_params("arbitrary"),
        name="moe_experts",
    )(tile_expert, n_tiles, hs, w_gate, w_up, w_down)


def _combine_kernel(*refs, final):
    if final:
        (d1_ref, d2_ref, x_ref, ada_ref, gw_ref, gf_ref, ys_ref, yp_ref, yl_ref,
         ya0, yb0, ya1, yb1, ya2d, yb2d, sem) = refs
    else:
        (d1_ref, d2_ref, x_ref, ada_ref, gw_ref, ys_ref, o_ref,
         ya0, yb0, ya1, yb1, ya2d, yb2d, sem) = refs
    i = pl.program_id(0)
    tm = x_ref.shape[0]

    def row_copies(step, r, buf_a, buf_b, slot):
        t = step * tm + r
        return (pltpu.make_async_copy(ys_ref.at[d1_ref[t]], buf_a.at[r], sem.at[slot]),
                pltpu.make_async_copy(ys_ref.at[d2_ref[t]], buf_b.at[r], sem.at[slot]))

    def start_tile(step, buf_a, buf_b, slot):
        def body(r, carry):
            for priority, cp in enumerate(row_copies(step, r, buf_a, buf_b, slot)):
                cp.start(priority=priority)
            return carry
        lax.fori_loop(0, tm, body, 0)

    def wait_tile(buf_a, buf_b, slot):
        for buf in (buf_a, buf_b):
            pltpu.make_async_copy(ys_ref.at[pl.ds(0, tm)], buf, sem.at[slot]).wait()

    @pl.when(i == 0)
    def _():
        start_tile(0, ya0, yb0, 0)

    def run(cur, nxt):
        @pl.when(i + 1 < pl.num_programs(0))
        def _():
            start_tile(i + 1, *nxt)

        wait_tile(*cur)
        ya2d[...] = cur[0][...].reshape(tm, D_MODEL)
        yb2d[...] = cur[1][...].reshape(tm, D_MODEL)
        mix = gw_ref[:, 0:1] * ya2d[...] + gw_ref[:, 1:2] * yb2d[...]
        out = x_ref[...] + ada_ref[5:6, :] * mix
        if final:
            y = _rms(out) * gf_ref[...]

            @pl.when(i * tm < N_CTX)
            def _():
                yp_ref[...] = y

            @pl.when(i * tm >= N_CTX)
            def _():
                yl_ref[...] = y
        else:
            o_ref[...] = out

    @pl.when(i % 2 == 0)
    def _():
        run((ya0, yb0, 0), (ya1, yb1, 1))

    @pl.when(i % 2 == 1)
    def _():
        run((ya1, yb1, 1), (ya0, yb0, 0))


def _combine(d1, d2, x, ada, layer, gate_w, ys, final_g=None):
    tm = COMBINE_TM
    final = final_g is not None
    ctx_tiles = N_CTX // tm
    row_buf = pltpu.VMEM((tm, 1, D_MODEL), F32)
    in_specs = [
        pl.BlockSpec((tm, D_MODEL), lambda i, *_: (i, 0)),
        pl.BlockSpec((None, None, 6, D_MODEL),
                     lambda i, *_: (layer, _cond_of_tile(i, tm), 0, 0)),
        pl.BlockSpec((tm, LANES), lambda i, *_: (i, 0)),
    ]
    if final:
        in_specs.append(pl.BlockSpec((1, D_MODEL), lambda i, *_: (0, 0)))
        out_specs = [
            pl.BlockSpec((tm, D_MODEL), lambda i, *_: (jnp.minimum(i, ctx_tiles - 1), 0)),
            pl.BlockSpec((tm, D_MODEL), lambda i, *_: (jnp.maximum(i - ctx_tiles, 0), 0)),
        ]
        out_shape = [jax.ShapeDtypeStruct((N_CTX, D_MODEL), F32),
                     jax.ShapeDtypeStruct((N_LAT, D_MODEL), F32)]
        args = (d1, d2, x, ada, gate_w, final_g, ys)
    else:
        out_specs = pl.BlockSpec((tm, D_MODEL), lambda i, *_: (i, 0))
        out_shape = jax.ShapeDtypeStruct((N_TOK, D_MODEL), F32)
        args = (d1, d2, x, ada, gate_w, ys)
    in_specs.append(pl.BlockSpec(memory_space=pl.ANY))
    return pl.pallas_call(
        functools.partial(_combine_kernel, final=final),
        grid_spec=pltpu.PrefetchScalarGridSpec(
            num_scalar_prefetch=2,
            grid=(N_TOK // tm,),
            in_specs=in_specs,
            out_specs=out_specs,
            scratch_shapes=[row_buf, row_buf, row_buf, row_buf,
                            pltpu.VMEM((tm, D_MODEL), F32), pltpu.VMEM((tm, D_MODEL), F32),
                            pltpu.SemaphoreType.DMA((2,))],
        ),
        out_shape=out_shape,
        compiler_params=_params("arbitrary"),
        name="moe_combine",
    )(*args)


def _moe_layer(x, ada, layer, m, g2, w_router, w_gate, w_up, w_down, final_g=None):
    h_rows, route, gate_w, counts = _route(x, ada, layer, g2, w_router)
    cnt = counts[0, :N_EXPERTS].astype(jnp.int32)
    tiles_per_expert = (cnt + MOE_TG - 1) // MOE_TG
    tile_end = jnp.cumsum(tiles_per_expert)
    row_start = (tile_end - tiles_per_expert) * MOE_TG
    n_tiles = tile_end[-1:]
    t = jnp.minimum(jnp.arange(MOE_MAX_TILES, dtype=jnp.int32), n_tiles - 1)
    tile_expert = jnp.sum((t[:, None] >= tile_end[None, :]).astype(jnp.int32), axis=1)
    experts = jnp.arange(N_EXPERTS, dtype=jnp.int32)
    d1 = jnp.sum(jnp.where(route[0][:, None] == experts, row_start, 0), axis=1) + route[2]
    d2 = jnp.sum(jnp.where(route[1][:, None] == experts, row_start, 0), axis=1) + route[3]
    spare_tile = jnp.minimum(n_tiles + experts, MOE_MAX_TILES - 1)
    fill_start = jnp.concatenate([row_start + cnt, spare_tile * MOE_TG])
    fill_rows = jnp.concatenate([tiles_per_expert * MOE_TG - cnt,
                                 jnp.where(n_tiles + experts < MOE_MAX_TILES, MOE_TG, 0)])
    hs, wg_bf, wu_bf, wd_bf = _dispatch(d1, d2, fill_start, fill_rows, h_rows, m, w_gate, w_up, w_down)
    ys = _experts(tile_expert, n_tiles, hs, wg_bf, wu_bf, wd_bf)
    return _combine(d1, d2, x, ada, layer, gate_w, ys, final_g)


assert DEPTH % 2 == 0


def _rope_tables(dim):
    half = dim // 2
    quarter = half // 2
    t = np.arange(DEC_SEQ)
    pos = np.stack([t // GRID_W, t % GRID_W], axis=1).astype(np.float32)
    freqs = (ROPE_THETA ** (-np.arange(quarter, dtype=np.float32) / quarter)).astype(np.float32)
    ang = pos[:, :, None] * freqs[None, None, :]
    cos = np.cos(ang.astype(np.float64))
    sin = np.sin(ang.astype(np.float64))
    cos_t = np.concatenate([cos, cos], axis=-1).reshape(DEC_SEQ, dim)
    sin_t = np.concatenate([-sin, sin], axis=-1).reshape(DEC_SEQ, dim)
    lane = np.arange(dim)
    swap = np.where(lane % half < quarter, lane + quarter, lane - quarter)
    return cos_t.astype(np.float32), sin_t.astype(np.float32), swap


def kernel(x_prompt, x_sample, cache_mla_ckv, cache_mla_kpe, cache_gqa_k, cache_gqa_v, c, c_ctx, w_ada, b_ada, norm1_g, norm2_g, conv_w_in, conv_k, conv_w_out, mla_w_down, mla_q_norm_g, mla_kv_norm_g, mla_w_uq, mla_w_ukv, mla_w_o, gqa_w_qkv, gqa_q_norm_g, gqa_k_norm_g, gqa_w_o, ffn_w_gate, ffn_w_up, ffn_w_down, moe_w_router, moe_w_gate, moe_w_up, moe_w_down, final_norm_g):
    xs = (x_prompt.reshape(N_CTX, D_MODEL), x_sample.reshape(N_LAT, D_MODEL))
    conv_w_in, conv_w_out = conv_w_in.astype(BF16), conv_w_out.astype(BF16)
    ffn_w_gate, ffn_w_up, ffn_w_down = (w.astype(BF16) for w in (ffn_w_gate, ffn_w_up, ffn_w_down))
    cond = jnp.concatenate(
        [c_ctx[None, :], c, jnp.zeros((N_COND - 1 - DEC_BATCH, D_MODEL), F32)], axis=0)
    ada = _adaln_table(cond, w_ada, b_ada)

    new_ckv, new_kpe, new_k, new_v = [], [], [], []
    for i in range(DEPTH):
        j = i // N_MIXERS
        kind = i % N_MIXERS
        g1 = norm1_g[i][None, :]
        if kind == 0:
            x = _conv_layer(xs if i == 0 else (x,), ada, i, j, g1, conv_w_in, conv_k, conv_w_out)
        elif kind == 1:
            cos64, sin64, swap64 = _rope_tables(MLA_ROPE)
            wd = mla_w_down[j]
            kpe_cols = wd[:, MLA_Q_RANK + MLA_KV_RANK:]
            w_down = jnp.concatenate([wd, kpe_cols[:, swap64]], axis=1).astype(BF16)
            wq = mla_w_uq[j].reshape(MLA_Q_RANK, MLA_HEADS, MLA_QK)
            wq_nope = wq[:, :, :MLA_NOPE].reshape(MLA_Q_RANK, MLA_Q_NOPE_COLS)
            wq_rope = wq[:, :, MLA_NOPE:]
            w_uq = jnp.concatenate(
                [wq_nope, wq_rope.reshape(MLA_Q_RANK, MLA_Q_ROPE_COLS),
                 wq_rope[:, :, swap64].reshape(MLA_Q_RANK, MLA_Q_ROPE_COLS)], axis=1).astype(BF16)
            wkv = mla_w_ukv[j].reshape(MLA_KV_RANK, MLA_HEADS, MLA_NOPE + MLA_V)
            w_ukv = jnp.concatenate(
                [wkv[:, :, :MLA_NOPE].reshape(MLA_KV_RANK, MLA_Q_NOPE_COLS),
                 wkv[:, :, MLA_NOPE:].reshape(MLA_KV_RANK, MLA_HEADS * MLA_V)], axis=1).astype(BF16)
            rope = (jnp.asarray(np.tile(cos64, (1, MLA_HEADS))), jnp.asarray(np.tile(sin64, (1, MLA_HEADS))),
                    jnp.asarray(cos64), jnp.asarray(sin64))
            x, ckv_p, kpe_p = _mla_layer(
                x, ada, i, g1, w_down, mla_q_norm_g[j][None, :], mla_kv_norm_g[j][None, :],
                w_uq, w_ukv, mla_w_o[j].astype(BF16), cache_mla_ckv[:, j], cache_mla_kpe[:, j], rope)
            new_ckv.append(ckv_p.reshape(BATCH, SEQ, MLA_KV_RANK))
            new_kpe.append(kpe_p.reshape(BATCH, SEQ, MLA_ROPE))
        else:
            cos128, sin128, swap128 = _rope_tables(GQA_HEAD_DIM)
            wqkv = gqa_w_qkv[j]
            wq = wqkv[:, :GQA_Q_COLS].reshape(D_MODEL, GQA_HEADS, GQA_HEAD_DIM)
            wk = wqkv[:, GQA_Q_COLS:GQA_Q_COLS + GQA_KV_COLS].reshape(D_MODEL, GQA_KV_HEADS, GQA_HEAD_DIM)
            w_qkv_ctx = wqkv.astype(BF16)
            w_qkv_lat = jnp.concatenate(
                [wqkv, wq[:, :, swap128].reshape(D_MODEL, GQA_Q_COLS),
                 wk[:, :, swap128].reshape(D_MODEL, GQA_KV_COLS)], axis=1).astype(BF16)
            gq = gqa_q_norm_g[j]
            gk = gqa_k_norm_g[j]
            rope = (jnp.asarray(np.tile(cos128, (1, GQA_HEADS))), jnp.asarray(np.tile(sin128, (1, GQA_HEADS))),
                    jnp.asarray(np.tile(cos128, (1, GQA_KV_HEADS))), jnp.asarray(np.tile(sin128, (1, GQA_KV_HEADS))))
            x, k_p, v_p = _gqa_layer(
                x, ada, i, g1, w_qkv_ctx, w_qkv_lat,
                jnp.tile(gq, GQA_HEADS)[None, :], jnp.tile(gk, GQA_KV_HEADS)[None, :],
                jnp.tile(gq[swap128], GQA_HEADS)[None, :], jnp.tile(gk[swap128], GQA_KV_HEADS)[None, :],
                gqa_w_o[j].astype(BF16),
                cache_gqa_k[:, j].reshape(DEC_BATCH, PAST_LEN, GQA_KV_COLS),
                cache_gqa_v[:, j].reshape(DEC_BATCH, PAST_LEN, GQA_KV_COLS), rope)
            new_k.append(k_p.reshape(BATCH, SEQ, GQA_KV_HEADS, GQA_HEAD_DIM))
            new_v.append(v_p.reshape(BATCH, SEQ, GQA_KV_HEADS, GQA_HEAD_DIM))

        m = i // 2
        g2 = norm2_g[i][None, :]
        if i % 2 == 0:
            x = _ffn_layer(x, ada, i, m, g2, ffn_w_gate, ffn_w_up, ffn_w_down)
        else:
            w_router = jnp.pad(moe_w_router[m], ((0, 0), (0, LANES - N_EXPERTS)))
            final_g = final_norm_g[None, :] if i == DEPTH - 1 else None
            x = _moe_layer(x, ada, i, m, g2, w_router, moe_w_gate, moe_w_up, moe_w_down, final_g)

    y_prompt, y_sample = x
    y_prompt = y_prompt.reshape(BATCH, SEQ, D_MODEL)
    y_sample = y_sample.reshape(DEC_BATCH, DEC_SEQ, D_MODEL)
    return (y_prompt, y_sample,
            jnp.stack(new_ckv, axis=1), jnp.stack(new_kpe, axis=1),
            jnp.stack(new_k, axis=1), jnp.stack(new_v, axis=1))
```

```python
import functools

import numpy as np
import jax
import jax.numpy as jnp
from jax import lax
from jax.experimental import pallas as pl
from jax.experimental.pallas import tpu as pltpu

D_MODEL = 1024
BATCH = 32
SEQ = 256
DEPTH = 4
DEC_BATCH = 2
DEC_SEQ = 1024
PAST_LEN = 512
GRID_W = 64
N_MIXERS = 3
CONV_WIDTH = 3
MLA_HEADS = 8
MLA_NOPE = 128
MLA_ROPE = 64
MLA_V = 128
MLA_Q_RANK = 384
MLA_KV_RANK = 256
GQA_HEADS = 8
GQA_KV_HEADS = 2
GQA_HEAD_DIM = 128
D_FF = 2816
N_EXPERTS = 8
MOE_FF = 2048
ROPE_THETA = 10000.0
EPS = 1e-6

N_CTX = BATCH * SEQ
N_LAT = DEC_BATCH * DEC_SEQ
N_TOK = N_CTX + N_LAT
N_COND = 8
LANES = 128
Q_TILE = 256
VMEM_LIMIT = 56 * 1024 * 1024

F32 = jnp.float32
BF16 = jnp.bfloat16


def _dot(a, b):
    return jnp.dot(a, b, preferred_element_type=F32)


def _dot_t(a, b):
    return lax.dot_general(a, b, (((1,), (1,)), ((), ())), preferred_element_type=F32)


def _rms(x):
    return x * lax.rsqrt(jnp.mean(x * x, axis=-1, keepdims=True) + EPS)


def _modulate(x, g, shift, scale):
    return _rms(x) * g * (1.0 + scale) + shift


def _silu(x):
    return x * jax.nn.sigmoid(x)


def _cond_of_tile(i, tm):
    start = i * tm
    return jnp.where(start < N_CTX, 0, 1 + (start - N_CTX) // DEC_SEQ)


def _resident(shape):
    return pl.BlockSpec(shape, lambda *_: (0,) * len(shape), pipeline_mode=pl.Buffered(1))


def _resident_at(index, shape):
    return pl.BlockSpec((None,) + tuple(shape), lambda *_: (index,) + (0,) * len(shape),
                        pipeline_mode=pl.Buffered(1))


def _params(*sem):
    return pltpu.CompilerParams(dimension_semantics=sem, vmem_limit_bytes=VMEM_LIMIT)


def _adaln_kernel(cond_ref, w_ref, b_ref, o_ref):
    s = _silu(cond_ref[...]).astype(BF16)
    o_ref[...] = _dot(s, w_ref[...].astype(BF16)) + b_ref[...]


def _adaln_table(cond, w_ada, b_ada):
    tn = 1536
    out = pl.pallas_call(
        _adaln_kernel,
        grid=(DEPTH, 6 * D_MODEL // tn),
        in_specs=[
            pl.BlockSpec((N_COND, D_MODEL), lambda l, j: (0, 0)),
            pl.BlockSpec((None, D_MODEL, tn), lambda l, j: (l, 0, j)),
            pl.BlockSpec((None, 1, tn), lambda l, j: (l, 0, j)),
        ],
        out_specs=pl.BlockSpec((None, N_COND, tn), lambda l, j: (l, 0, j)),
        out_shape=jax.ShapeDtypeStruct((DEPTH, N_COND, 6 * D_MODEL), F32),
        compiler_params=_params("arbitrary", "arbitrary"),
        name="adaln_table",
    )(cond, w_ada, b_ada.reshape(DEPTH, 1, 6 * D_MODEL))
    return out.reshape(DEPTH, N_COND, 6, D_MODEL)


def _ada_spec(layer, tm):
    return pl.BlockSpec((None, None, 6, D_MODEL), lambda i: (layer, _cond_of_tile(i, tm), 0, 0))


def _ada_spec_at(layer, cond_fn):
    return pl.BlockSpec((None, None, 6, D_MODEL), lambda i: (layer, cond_fn(i), 0, 0))


CONV_TM = 1024
CONV_CC = 256


def _conv_kernel(*refs, split):
    if split:
        xp_ref, x_ref, ada_ref, g1_ref, win_ref, ck_ref, wout_ref, o_ref, v_scr = refs
    else:
        x_ref, ada_ref, g1_ref, win_ref, ck_ref, wout_ref, o_ref, v_scr = refs
    tm = x_ref.shape[0]
    i = pl.program_id(0)
    x = x_ref[...]
    if split:
        x = jnp.where(i * tm < N_CTX, xp_ref[...], x)
    h = _modulate(x, g1_ref[...], ada_ref[0:1, :], ada_ref[1:2, :]).astype(BF16)
    period = jnp.where(i * tm < N_CTX, SEQ, DEC_SEQ)
    pos = lax.broadcasted_iota(jnp.int32, (tm, 1), 0) & (period - 1)
    first = pos == 0
    last = pos == period - 1
    for j in range(D_MODEL // CONV_CC):
        lo = j * CONV_CC
        b_gate = _dot(h, win_ref[:, lo:lo + CONV_CC])
        c_gate = _dot(h, win_ref[:, D_MODEL + lo:D_MODEL + lo + CONV_CC])
        x_in = _dot(h, win_ref[:, 2 * D_MODEL + lo:2 * D_MODEL + lo + CONV_CC])
        u = c_gate * x_in
        u_prev = jnp.where(first, 0.0, pltpu.roll(u, 1, 0))
        u_next = jnp.where(last, 0.0, pltpu.roll(u, tm - 1, 0))
        conv = (ck_ref[0:1, lo:lo + CONV_CC] * u_prev + ck_ref[1:2, lo:lo + CONV_CC] * u
                + ck_ref[2:3, lo:lo + CONV_CC] * u_next)
        v_scr[:, lo:lo + CONV_CC] = (b_gate * conv).astype(BF16)
    o_ref[...] = x + ada_ref[2:3, :] * _dot(v_scr[...], wout_ref[...])


def _conv_layer(xs, ada, layer, j, g1, w_in, conv_k, w_out):
    tm = CONV_TM
    ctx_tiles = N_CTX // tm
    if len(xs) == 1:
        x_specs = [pl.BlockSpec((tm, D_MODEL), lambda i: (i, 0))]
    else:
        x_specs = [pl.BlockSpec((tm, D_MODEL), lambda i: (jnp.minimum(i, ctx_tiles - 1), 0)),
                   pl.BlockSpec((tm, D_MODEL), lambda i: (jnp.maximum(i - ctx_tiles, 0), 0))]
    return pl.pallas_call(
        functools.partial(_conv_kernel, split=len(xs) == 2),
        grid=(N_TOK // tm,),
        in_specs=x_specs + [
            _ada_spec(layer, tm),
            _resident((1, D_MODEL)),
            _resident_at(j, (D_MODEL, 3 * D_MODEL)),
            _resident_at(j, (CONV_WIDTH, D_MODEL)),
            _resident_at(j, (D_MODEL, D_MODEL)),
        ],
        out_specs=pl.BlockSpec((tm, D_MODEL), lambda i: (i, 0)),
        out_shape=jax.ShapeDtypeStruct((N_TOK, D_MODEL), F32),
        scratch_shapes=[pltpu.VMEM((tm, D_MODEL), BF16)],
        compiler_params=_params("arbitrary"),
        name=f"conv_mixer_{layer}",
    )(*xs, ada, g1, w_in, conv_k, w_out)


FFN_TM = 512
FFN_CHUNKS = ((0, 1536), (1536, D_FF))


def _ffn_kernel(x_ref, ada_ref, g2_ref, wg_ref, wu_ref, wd_ref, o_ref):
    x = x_ref[...]
    h = _modulate(x, g2_ref[...], ada_ref[3:4, :], ada_ref[4:5, :]).astype(BF16)
    f = None
    for lo, hi in FFN_CHUNKS:
        g = _dot(h, wg_ref[:, lo:hi])
        u = _dot(h, wu_ref[:, lo:hi])
        part = _dot((_silu(g) * u).astype(BF16), wd_ref[lo:hi, :])
        f = part if f is None else f + part
    o_ref[...] = x + ada_ref[5:6, :] * f


def _ffn_layer(x, ada, layer, m, g2, w_gate, w_up, w_down):
    tm = FFN_TM
    return pl.pallas_call(
        _ffn_kernel,
        grid=(N_TOK // tm,),
        in_specs=[
            pl.BlockSpec((tm, D_MODEL), lambda i: (i, 0)),
            _ada_spec(layer, tm),
            _resident((1, D_MODEL)),
            _resident_at(m, (D_MODEL, D_FF)),
            _resident_at(m, (D_MODEL, D_FF)),
            _resident_at(m, (D_FF, D_MODEL)),
        ],
        out_specs=pl.BlockSpec((tm, D_MODEL), lambda i: (i, 0)),
        out_shape=jax.ShapeDtypeStruct((N_TOK, D_MODEL), F32),
        compiler_params=_params("arbitrary"),
        name=f"dense_ffn_{layer}",
    )(x, ada, g2, w_gate, w_up, w_down)


def _attention(qk_parts, v_ref, heads, kv_of_head, dv, scale, o_scr, finish, keys_of):
    def q_block(rows):
        keys = keys_of(rows)
        for h in range(heads):
            g = kv_of_head(h)
            s = None
            for q_ref, k_ref, width, k_shared in qk_parts:
                kb = 0 if k_shared else g
                part = _dot_t(q_ref[rows, h * width:(h + 1) * width],
                              k_ref[keys, kb * width:(kb + 1) * width])
                s = part if s is None else s + part
            s = s * scale
            e = jnp.exp(s - jnp.max(s, axis=-1, keepdims=True))
            denom = jnp.sum(e, axis=-1, keepdims=True)
            o = _dot(e.astype(BF16), v_ref[keys, g * dv:(g + 1) * dv]) / denom
            o_scr[rows, h * dv:(h + 1) * dv] = o.astype(BF16)
        finish(rows)

    _for_row_blocks(o_scr.shape[0], q_block)


def _keys_of(latent):
    assert SEQ == Q_TILE
    return (lambda rows: slice(None)) if latent else (lambda rows: rows)


def _for_row_blocks(n_rows, fn):
    if n_rows <= 2 * Q_TILE:
        for b in range(n_rows // Q_TILE):
            fn(pl.ds(b * Q_TILE, Q_TILE))
    else:
        def body(b, carry):
            fn(pl.ds(pl.multiple_of(b * Q_TILE, Q_TILE), Q_TILE))
            return carry
        lax.fori_loop(0, n_rows // Q_TILE, body, 0)


MLA_QK = MLA_NOPE + MLA_ROPE
MLA_DOWN_COLS = MLA_Q_RANK + MLA_KV_RANK + 2 * MLA_ROPE
MLA_Q_NOPE_COLS = MLA_HEADS * MLA_NOPE
MLA_Q_ROPE_COLS = MLA_HEADS * MLA_ROPE


def _ctx_then_latent(body, n_in, refs):
    ins, lat_ref, rest = refs[:n_in], refs[n_in], refs[n_in + 1:]
    o_ref = rest[0]
    i = pl.program_id(0)

    @pl.when(i < CTX_TILES)
    def _():
        body(*ins, *rest, latent=False)

    @pl.when(i >= CTX_TILES)
    def _():
        o_ref[...] = lat_ref[...]


def _ctx_specs(out_tails):
    def tile(n_trailing):
        return lambda i: (jnp.minimum(i, CTX_TILES - 1),) + (0,) * n_trailing
    x_spec = pl.BlockSpec((CTX_TM, D_MODEL), tile(1))
    lat_spec = pl.BlockSpec((CTX_TM, D_MODEL), lambda i: (jnp.maximum(i - CTX_TILES, 0), 0))
    out_specs = [pl.BlockSpec((CTX_TM, D_MODEL), lambda i: (i, 0))]
    out_specs += [pl.BlockSpec((CTX_TM,) + tail, tile(len(tail))) for tail in out_tails]
    return x_spec, lat_spec, out_specs


CTX_TM = 2 * SEQ
CTX_TILES = N_CTX // CTX_TM
CTX_STEPS = CTX_TILES + N_LAT // CTX_TM


def _mla_kernel(*refs, latent):
    if latent:
        _mla_body(*refs, latent=True)
    else:
        _ctx_then_latent(_mla_body, 9, refs)


def _mla_body(*refs, latent):
    if latent:
        (x_ref, ada_ref, g1_ref, wdown_ref, gq_ref, gkv_ref, wuq_ref, wukv_ref, wo_ref,
         cosq_ref, sinq_ref, cosk_ref, sink_ref, cckv_ref, ckpe_ref,
         o_ref, qn_scr, qr_scr, kn_scr, kr_scr, v_scr, o_scr) = refs
    else:
        (x_ref, ada_ref, g1_ref, wdown_ref, gq_ref, gkv_ref, wuq_ref, wukv_ref, wo_ref,
         o_ref, ckv_out_ref, kpe_out_ref, qn_scr, qr_scr, kn_scr, kr_scr, v_scr, o_scr) = refs
    tm = x_ref.shape[0]
    kpe_lo = MLA_Q_RANK + MLA_KV_RANK

    def project(rows):
        h = _modulate(x_ref[rows, :], g1_ref[...], ada_ref[0:1, :], ada_ref[1:2, :]).astype(BF16)
        down = _dot(h, wdown_ref[...])
        c_q = down[:, :MLA_Q_RANK]
        c_kv = down[:, MLA_Q_RANK:kpe_lo]
        kpe = down[:, kpe_lo:kpe_lo + MLA_ROPE]
        q = _dot((_rms(c_q) * gq_ref[...]).astype(BF16), wuq_ref[...])
        ckv_n = _rms(c_kv) * gkv_ref[...]
        kv = _dot(ckv_n.astype(BF16), wukv_ref[...])
        qn_scr[rows, :] = q[:, :MLA_Q_NOPE_COLS].astype(BF16)
        q_rope = q[:, MLA_Q_NOPE_COLS:MLA_Q_NOPE_COLS + MLA_Q_ROPE_COLS]
        kn_scr[rows, :] = kv[:, :MLA_Q_NOPE_COLS].astype(BF16)
        v_scr[rows, :] = kv[:, MLA_Q_NOPE_COLS:].astype(BF16)
        if latent:
            q_swap = q[:, MLA_Q_NOPE_COLS + MLA_Q_ROPE_COLS:]
            q_rope = q_rope * cosq_ref[rows, :] + q_swap * sinq_ref[rows, :]
            kpe_swap = down[:, kpe_lo + MLA_ROPE:kpe_lo + 2 * MLA_ROPE]
            kpe = kpe * cosk_ref[rows, :] + kpe_swap * sink_ref[rows, :]
        else:
            ckv_out_ref[rows, :] = ckv_n
            kpe_out_ref[rows, :] = kpe
        qr_scr[rows, :] = q_rope.astype(BF16)
        kr_scr[rows, :] = kpe.astype(BF16)

    _for_row_blocks(tm, project)

    if latent:
        def expand_cache(rows):
            kv_c = _dot(cckv_ref[rows, :].astype(BF16), wukv_ref[...])
            dst = pl.ds(tm + rows.start, Q_TILE)
            kn_scr[dst, :] = kv_c[:, :MLA_Q_NOPE_COLS].astype(BF16)
            v_scr[dst, :] = kv_c[:, MLA_Q_NOPE_COLS:].astype(BF16)
            kr_scr[dst, :] = ckpe_ref[rows, :].astype(BF16)

        _for_row_blocks(PAST_LEN, expand_cache)

    def finish(rows):
        o_ref[rows, :] = x_ref[rows, :] + ada_ref[2:3, :] * _dot(o_scr[rows, :], wo_ref[...])

    _attention([(qn_scr, kn_scr, MLA_NOPE, False), (qr_scr, kr_scr, MLA_ROPE, True)],
               v_scr, MLA_HEADS, lambda hd: hd, MLA_V, MLA_QK ** -0.5, o_scr, finish,
               _keys_of(latent))


def _mla_scratch(tm, sk):
    return [
        pltpu.VMEM((tm, MLA_Q_NOPE_COLS), BF16),
        pltpu.VMEM((tm, MLA_Q_ROPE_COLS), BF16),
        pltpu.VMEM((sk, MLA_Q_NOPE_COLS), BF16),
        pltpu.VMEM((sk, MLA_ROPE), BF16),
        pltpu.VMEM((sk, MLA_HEADS * MLA_V), BF16),
        pltpu.VMEM((tm, MLA_HEADS * MLA_V), BF16),
    ]


def _mla_layer(x, ada, layer, g1, w_down, g_q, g_kv, w_uq, w_ukv, w_o,
               cache_ckv, cache_kpe, rope):
    n_uq = w_uq.shape[1]
    weights = [
        _resident((1, D_MODEL)),
        _resident((D_MODEL, MLA_DOWN_COLS)),
        _resident((1, MLA_Q_RANK)),
        _resident((1, MLA_KV_RANK)),
        _resident((MLA_Q_RANK, n_uq)),
        _resident((MLA_KV_RANK, 2 * MLA_Q_NOPE_COLS)),
        _resident((MLA_HEADS * MLA_V, D_MODEL)),
    ]
    ctx_weights = list(weights)
    ctx_weights[4] = pl.BlockSpec((MLA_Q_RANK, MLA_Q_NOPE_COLS + MLA_Q_ROPE_COLS), lambda *_: (0, 0),
                                  pipeline_mode=pl.Buffered(1))
    weight_args = (g1, w_down, g_q, g_kv, w_uq, w_ukv, w_o)
    cos_q, sin_q, cos_k, sin_k = rope
    ctx_blocks = N_CTX // DEC_SEQ
    lat_out = pl.pallas_call(
        functools.partial(_mla_kernel, latent=True),
        grid=(DEC_BATCH,),
        in_specs=[pl.BlockSpec((DEC_SEQ, D_MODEL), lambda b: (ctx_blocks + b, 0)),
                  _ada_spec_at(layer, lambda b: 1 + b)] + weights + [
            _resident((DEC_SEQ, MLA_Q_ROPE_COLS)),
            _resident((DEC_SEQ, MLA_Q_ROPE_COLS)),
            _resident((DEC_SEQ, MLA_ROPE)),
            _resident((DEC_SEQ, MLA_ROPE)),
            pl.BlockSpec((None, PAST_LEN, MLA_KV_RANK), lambda b: (b, 0, 0)),
            pl.BlockSpec((None, PAST_LEN, MLA_ROPE), lambda b: (b, 0, 0)),
        ],
        out_specs=pl.BlockSpec((DEC_SEQ, D_MODEL), lambda b: (b, 0)),
        out_shape=jax.ShapeDtypeStruct((N_LAT, D_MODEL), F32),
        scratch_shapes=_mla_scratch(DEC_SEQ, DEC_SEQ + PAST_LEN),
        compiler_params=_params("arbitrary"),
        name=f"mla_latent_{layer}",
    )(x, ada, *weight_args, cos_q, sin_q, cos_k, sin_k, cache_ckv, cache_kpe)

    x_spec, lat_spec, out_specs = _ctx_specs(((MLA_KV_RANK,), (MLA_ROPE,)))
    return pl.pallas_call(
        functools.partial(_mla_kernel, latent=False),
        grid=(CTX_STEPS,),
        in_specs=[x_spec, _ada_spec_at(layer, lambda i: 0)] + ctx_weights + [lat_spec],
        out_specs=out_specs,
        out_shape=[
            jax.ShapeDtypeStruct((N_TOK, D_MODEL), F32),
            jax.ShapeDtypeStruct((N_CTX, MLA_KV_RANK), F32),
            jax.ShapeDtypeStruct((N_CTX, MLA_ROPE), F32),
        ],
        scratch_shapes=_mla_scratch(CTX_TM, CTX_TM),
        compiler_params=_params("arbitrary"),
        name=f"mla_ctx_{layer}",
    )(x, ada, *weight_args, lat_out)


GQA_Q_COLS = GQA_HEADS * GQA_HEAD_DIM
GQA_KV_COLS = GQA_KV_HEADS * GQA_HEAD_DIM
GQA_REP = GQA_HEADS // GQA_KV_HEADS


def _head_rms(t, n_heads):
    cols = []
    for hd in range(n_heads):
        th = t[:, hd * GQA_HEAD_DIM:(hd + 1) * GQA_HEAD_DIM]
        r = lax.rsqrt(jnp.mean(th * th, axis=-1, keepdims=True) + EPS)
        cols.append(jnp.broadcast_to(r, th.shape))
    return jnp.concatenate(cols, axis=-1)


def _gqa_kernel(*refs, latent):
    if latent:
        _gqa_body(*refs, latent=True)
    else:
        _ctx_then_latent(_gqa_body, 7, refs)


def _gqa_body(*refs, latent):
    if latent:
        (x_ref, ada_ref, g1_ref, wqkv_ref, gq_ref, gk_ref, wo_ref,
         gqs_ref, gks_ref, cosq_ref, sinq_ref, cosk_ref, sink_ref, ck_ref, cv_ref,
         o_ref, q_scr, k_scr, v_scr, o_scr) = refs
    else:
        (x_ref, ada_ref, g1_ref, wqkv_ref, gq_ref, gk_ref, wo_ref,
         o_ref, k_out_ref, v_out_ref, q_scr, k_scr, v_scr, o_scr) = refs
    tm = x_ref.shape[0]

    def project(rows):
        h = _modulate(x_ref[rows, :], g1_ref[...], ada_ref[0:1, :], ada_ref[1:2, :]).astype(BF16)
        qkv = _dot(h, wqkv_ref[...])
        q_raw = qkv[:, :GQA_Q_COLS]
        k_raw = qkv[:, GQA_Q_COLS:GQA_Q_COLS + GQA_KV_COLS]
        v = qkv[:, GQA_Q_COLS + GQA_KV_COLS:GQA_Q_COLS + 2 * GQA_KV_COLS]
        q_r = _head_rms(q_raw, GQA_HEADS)
        k_r = _head_rms(k_raw, GQA_KV_HEADS)
        q = q_raw * q_r * gq_ref[...]
        k = k_raw * k_r * gk_ref[...]
        if latent:
            lo = GQA_Q_COLS + 2 * GQA_KV_COLS
            q_swap = qkv[:, lo:lo + GQA_Q_COLS] * q_r * gqs_ref[...]
            k_swap = qkv[:, lo + GQA_Q_COLS:lo + GQA_Q_COLS + GQA_KV_COLS] * k_r * gks_ref[...]
            q = q * cosq_ref[rows, :] + q_swap * sinq_ref[rows, :]
            k = k * cosk_ref[rows, :] + k_swap * sink_ref[rows, :]
        else:
            for g in range(GQA_KV_HEADS):
                cols = slice(g * GQA_HEAD_DIM, (g + 1) * GQA_HEAD_DIM)
                k_out_ref[rows, g, :] = k[:, cols]
                v_out_ref[rows, g, :] = v[:, cols]
        q_scr[rows, :] = q.astype(BF16)
        k_scr[rows, :] = k.astype(BF16)
        v_scr[rows, :] = v.astype(BF16)

    _for_row_blocks(tm, project)

    if latent:
        k_scr[tm:, :] = ck_ref[...].astype(BF16)
        v_scr[tm:, :] = cv_ref[...].astype(BF16)

    def finish(rows):
        o_ref[rows, :] = x_ref[rows, :] + ada_ref[2:3, :] * _dot(o_scr[rows, :], wo_ref[...])

    _attention([(q_scr, k_scr, GQA_HEAD_DIM, False)], v_scr, GQA_HEADS,
               lambda hd: hd // GQA_REP, GQA_HEAD_DIM, GQA_HEAD_DIM ** -0.5, o_scr, finish,
               _keys_of(latent))


def _gqa_scratch(tm, sk):
    return [
        pltpu.VMEM((tm, GQA_Q_COLS), BF16),
        pltpu.VMEM((sk, GQA_KV_COLS), BF16),
        pltpu.VMEM((sk, GQA_KV_COLS), BF16),
        pltpu.VMEM((tm, GQA_Q_COLS), BF16),
    ]


def _gqa_layer(x, ada, layer, g1, w_qkv_ctx, w_qkv_lat, g_q, g_k, g_q_swap, g_k_swap, w_o,
               cache_k, cache_v, rope):
    cos_q, sin_q, cos_k, sin_k = rope
    ctx_blocks = N_CTX // DEC_SEQ
    lat_out = pl.pallas_call(
        functools.partial(_gqa_kernel, latent=True),
        grid=(DEC_BATCH,),
        in_specs=[pl.BlockSpec((DEC_SEQ, D_MODEL), lambda b: (ctx_blocks + b, 0)),
                  _ada_spec_at(layer, lambda b: 1 + b),
                  _resident((1, D_MODEL)),
                  _resident(w_qkv_lat.shape),
                  _resident((1, GQA_Q_COLS)),
                  _resident((1, GQA_KV_COLS)),
                  _resident((GQA_Q_COLS, D_MODEL)),
                  _resident((1, GQA_Q_COLS)),
                  _resident((1, GQA_KV_COLS)),
                  _resident((DEC_SEQ, GQA_Q_COLS)),
                  _resident((DEC_SEQ, GQA_Q_COLS)),
                  _resident((DEC_SEQ, GQA_KV_COLS)),
                  _resident((DEC_SEQ, GQA_KV_COLS)),
                  pl.BlockSpec((None, PAST_LEN, GQA_KV_COLS), lambda b: (b, 0, 0)),
                  pl.BlockSpec((None, PAST_LEN, GQA_KV_COLS), lambda b: (b, 0, 0))],
        out_specs=pl.BlockSpec((DEC_SEQ, D_MODEL), lambda b: (b, 0)),
        out_shape=jax.ShapeDtypeStruct((N_LAT, D_MODEL), F32),
        scratch_shapes=_gqa_scratch(DEC_SEQ, DEC_SEQ + PAST_LEN),
        compiler_params=_params("arbitrary"),
        name=f"gqa_latent_{layer}",
    )(x, ada, g1, w_qkv_lat, g_q, g_k, w_o, g_q_swap, g_k_swap,
      cos_q, sin_q, cos_k, sin_k, cache_k, cache_v)

    kv_tail = (GQA_KV_HEADS, GQA_HEAD_DIM)
    x_spec, lat_spec, out_specs = _ctx_specs((kv_tail, kv_tail))
    return pl.pallas_call(
        functools.partial(_gqa_kernel, latent=False),
        grid=(CTX_STEPS,),
        in_specs=[x_spec,
                  _ada_spec_at(layer, lambda i: 0),
                  _resident((1, D_MODEL)),
                  _resident(w_qkv_ctx.shape),
                  _resident((1, GQA_Q_COLS)),
                  _resident((1, GQA_KV_COLS)),
                  _resident((GQA_Q_COLS, D_MODEL)),
                  lat_spec],
        out_specs=out_specs,
        out_shape=[
            jax.ShapeDtypeStruct((N_TOK, D_MODEL), F32),
            jax.ShapeDtypeStruct((N_CTX,) + kv_tail, F32),
            jax.ShapeDtypeStruct((N_CTX,) + kv_tail, F32),
        ],
        scratch_shapes=_gqa_scratch(CTX_TM, CTX_TM),
        compiler_params=_params("arbitrary"),
        name=f"gqa_ctx_{layer}",
    )(x, ada, g1, w_qkv_ctx, g_q, g_k, w_o, lat_out)


ROUTE_TM = 512
MOE_TG = 256
MOE_MAX_TILES = 2 * N_TOK // MOE_TG + N_EXPERTS
MOE_ROWS = MOE_MAX_TILES * MOE_TG
MOE_FILLS = 2 * N_EXPERTS
MOE_FC = 1024
COMBINE_TM = 256
ISSUE_UNROLL = 8


def _route_kernel(x_ref, ada_ref, g2_ref, wr_hi_ref, wr_lo_ref, h_ref, dest_ref, gatew_ref,
                  count_ref, carry_scr):
    tm = x_ref.shape[0]

    @pl.when(pl.program_id(0) == 0)
    def _():
        carry_scr[...] = jnp.zeros_like(carry_scr)

    h = _modulate(x_ref[...], g2_ref[...], ada_ref[3:4, :], ada_ref[4:5, :])
    h_ref[...] = h.reshape(tm, 1, D_MODEL)
    h_hi = h.astype(BF16)
    h_lo = (h - h_hi.astype(F32)).astype(BF16)
    logits = _dot(h_hi, wr_hi_ref[...]) + (_dot(h_lo, wr_hi_ref[...]) + _dot(h_hi, wr_lo_ref[...]))
    lane = lax.broadcasted_iota(jnp.int32, logits.shape, 1)
    neg = jnp.float32(-jnp.inf)
    l1 = jnp.where(lane < N_EXPERTS, logits, neg)
    m1 = jnp.max(l1, axis=-1, keepdims=True)
    i1 = jnp.min(jnp.where(l1 == m1, lane, LANES), axis=-1, keepdims=True)
    l2 = jnp.where(lane == i1, neg, l1)
    m2 = jnp.max(l2, axis=-1, keepdims=True)
    i2 = jnp.min(jnp.where(l2 == m2, lane, LANES), axis=-1, keepdims=True)
    e2 = jnp.exp(m2 - m1)
    w1 = 1.0 / (1.0 + e2)
    w2 = e2 / (1.0 + e2)
    member = jnp.where((lane == i1) | (lane == i2), 1.0, 0.0)
    r = lax.broadcasted_iota(jnp.int32, (tm, tm), 0)
    c = lax.broadcasted_iota(jnp.int32, (tm, tm), 1)
    tri = jnp.where(c < r, 1.0, 0.0).astype(BF16)
    before = _dot(tri, member.astype(BF16)) + carry_scr[0:1, :]
    rank1 = jnp.sum(jnp.where(lane == i1, before, 0.0), axis=-1, keepdims=True).astype(jnp.int32)
    rank2 = jnp.sum(jnp.where(lane == i2, before, 0.0), axis=-1, keepdims=True).astype(jnp.int32)
    table = jnp.where(lane == 0, i1, jnp.where(lane == 1, i2, jnp.where(
        lane == 2, rank1, jnp.where(lane == 3, rank2, 0))))
    dest_ref[...] = jnp.transpose(table)[0:8, :]
    gatew_ref[...] = jnp.where(lane == 0, w1, jnp.where(lane == 1, w2, 0.0))
    carry_scr[...] = carry_scr[...] + jnp.sum(member, axis=0, keepdims=True)
    count_ref[...] = carry_scr[...]


def _route(x, ada, layer, g2, w_router):
    tm = ROUTE_TM
    w_router_hi = w_router.astype(BF16)
    return pl.pallas_call(
        _route_kernel,
        grid=(N_TOK // tm,),
        in_specs=[
            pl.BlockSpec((tm, D_MODEL), lambda i: (i, 0)),
            _ada_spec(layer, tm),
            _resident((1, D_MODEL)),
            _resident((D_MODEL, LANES)),
            _resident((D_MODEL, LANES)),
        ],
        out_specs=[
            pl.BlockSpec((tm, 1, D_MODEL), lambda i: (i, 0, 0)),
            pl.BlockSpec((8, tm), lambda i: (0, i)),
            pl.BlockSpec((tm, LANES), lambda i: (i, 0)),
            pl.BlockSpec((8, LANES), lambda i: (0, 0)),
        ],
        out_shape=[
            jax.ShapeDtypeStruct((N_TOK, 1, D_MODEL), F32),
            jax.ShapeDtypeStruct((8, N_TOK), jnp.int32),
            jax.ShapeDtypeStruct((N_TOK, LANES), F32),
            jax.ShapeDtypeStruct((8, LANES), F32),
        ],
        scratch_shapes=[pltpu.VMEM((8, LANES), F32)],
        compiler_params=_params("arbitrary"),
        name=f"moe_route_{layer}",
    )(x, ada, g2, w_router_hi, (w_router - w_router_hi.astype(F32)).astype(BF16))


def _dispatch_kernel(d1_ref, d2_ref, fill_start_ref, fill_rows_ref, h_ref, wg_ref, wu_ref, wd_ref,
                     hs_ref, wg_out_ref, wu_out_ref, wd_out_ref, zero_scr, sem):
    i = pl.program_id(0)
    tm = h_ref.shape[0]
    base = i * tm

    def fill_copy(k):
        n = fill_rows_ref[k]
        return pltpu.make_async_copy(zero_scr.at[pl.ds(0, n)],
                                     hs_ref.at[pl.ds(fill_start_ref[k], n)], sem.at[1])

    @pl.when(i == 0)
    def _():
        zero_scr[...] = jnp.zeros_like(zero_scr)
        for k in range(MOE_FILLS):
            @pl.when(fill_rows_ref[k] > 0)
            def _():
                fill_copy(k).start()
        for k in range(MOE_FILLS):
            @pl.when(fill_rows_ref[k] > 0)
            def _():
                fill_copy(k).wait()

    def row_copy(r, d_ref):
        return pltpu.make_async_copy(h_ref.at[r], hs_ref.at[d_ref[base + r]], sem.at[0])

    def issue(g, carry):
        for k in range(ISSUE_UNROLL):
            r = g * ISSUE_UNROLL + k
            row_copy(r, d1_ref).start(priority=0)
            row_copy(r, d2_ref).start(priority=1)
        return carry

    def drain():
        for _ in range(2):
            pltpu.make_async_copy(h_ref, hs_ref.at[pl.ds(0, tm)], sem.at[0]).wait()

    lax.fori_loop(0, tm // ISSUE_UNROLL, issue, 0)
    wg_out_ref[...] = wg_ref[...].astype(BF16)
    wu_out_ref[...] = wu_ref[...].astype(BF16)
    wd_out_ref[...] = wd_ref[...].astype(BF16)
    drain()


def _dispatch(d1, d2, fill_start, fill_rows, h_rows, m, w_gate, w_up, w_down):
    steps = 2 * N_EXPERTS
    tm = N_TOK // steps
    fh = MOE_FF // 2
    return pl.pallas_call(
        _dispatch_kernel,
        grid_spec=pltpu.PrefetchScalarGridSpec(
            num_scalar_prefetch=4,
            grid=(steps,),
            in_specs=[
                pl.BlockSpec((tm, 1, D_MODEL), lambda i, *_: (i, 0, 0)),
                pl.BlockSpec((None, None, D_MODEL, fh), lambda i, *_: (m, i // 2, 0, i % 2)),
                pl.BlockSpec((None, None, D_MODEL, fh), lambda i, *_: (m, i // 2, 0, i % 2)),
                pl.BlockSpec((None, None, fh, D_MODEL), lambda i, *_: (m, i // 2, i % 2, 0)),
            ],
            out_specs=[
                pl.BlockSpec(memory_space=pl.ANY),
                pl.BlockSpec((None, D_MODEL, fh), lambda i, *_: (i // 2, 0, i % 2)),
                pl.BlockSpec((None, D_MODEL, fh), lambda i, *_: (i // 2, 0, i % 2)),
                pl.BlockSpec((None, fh, D_MODEL), lambda i, *_: (i // 2, i % 2, 0)),
            ],
            scratch_shapes=[pltpu.VMEM((MOE_TG, 1, D_MODEL), F32),
                            pltpu.SemaphoreType.DMA((2,))],
        ),
        out_shape=[
            jax.ShapeDtypeStruct((MOE_ROWS, 1, D_MODEL), F32),
            jax.ShapeDtypeStruct((N_EXPERTS, D_MODEL, MOE_FF), BF16),
            jax.ShapeDtypeStruct((N_EXPERTS, D_MODEL, MOE_FF), BF16),
            jax.ShapeDtypeStruct((N_EXPERTS, MOE_FF, D_MODEL), BF16),
        ],
        compiler_params=_params("arbitrary"),
        name="moe_dispatch",
    )(d1, d2, fill_start, fill_rows, h_rows, w_gate, w_up, w_down)


def _experts_kernel(te_ref, nt_ref, hs_ref, wg_ref, wu_ref, wd_ref, ys_ref, h2d_scr):
    used = pl.program_id(0) < nt_ref[0]

    @pl.when(jnp.logical_not(used))
    def _():
        ys_ref[...] = jnp.zeros_like(ys_ref)

    @pl.when(used)
    def _():
        h2d_scr[...] = hs_ref[...].reshape(MOE_TG, D_MODEL)
        h = h2d_scr[...].astype(BF16)
        y = None
        for lo in range(0, MOE_FF, MOE_FC):
            g = _dot(h, wg_ref[:, lo:lo + MOE_FC])
            u = _dot(h, wu_ref[:, lo:lo + MOE_FC])
            part = _dot((_silu(g) * u).astype(BF16), wd_ref[lo:lo + MOE_FC, :])
            y = part if y is None else y + part
        ys_ref[...] = y.reshape(MOE_TG, 1, D_MODEL)


def _experts(tile_expert, n_tiles, hs, w_gate, w_up, w_down):
    rows = pl.BlockSpec((MOE_TG, 1, D_MODEL), lambda t, te, nt: (t, 0, 0))
    return pl.pallas_call(
        _experts_kernel,
        grid_spec=pltpu.PrefetchScalarGridSpec(
            num_scalar_prefetch=2,
            grid=(MOE_MAX_TILES,),
            in_specs=[
                rows,
                pl.BlockSpec((None, D_MODEL, MOE_FF), lambda t, te, nt: (te[t], 0, 0)),
                pl.BlockSpec((None, D_MODEL, MOE_FF), lambda t, te, nt: (te[t], 0, 0)),
                pl.BlockSpec((None, MOE_FF, D_MODEL), lambda t, te, nt: (te[t], 0, 0)),
            ],
            out_specs=rows,
            scratch_shapes=[pltpu.VMEM((MOE_TG, D_MODEL), F32)],
        ),
        out_shape=jax.ShapeDtypeStruct((MOE_ROWS, 1, D_MODEL), F32),
        compiler_params=_params("arbitrary"),
        name="moe_experts",
    )(tile_expert, n_tiles, hs, w_gate, w_up, w_down)


def _combine_kernel(*refs, final):
    if final:
        (d1_ref, d2_ref, x_ref, ada_ref, gw_ref, gf_ref, ys_ref, yp_ref, yl_ref,
         ya0, yb0, ya1, yb1, ya2d, yb2d, sem) = refs
    else:
        (d1_ref, d2_ref, x_ref, ada_ref, gw_ref, ys_ref, o_ref,
         ya0, yb0, ya1, yb1, ya2d, yb2d, sem) = refs
    i = pl.program_id(0)
    tm = x_ref.shape[0]

    def row_copies(step, r, buf_a, buf_b, slot):
        t = step * tm + r
        return (pltpu.make_async_copy(ys_ref.at[d1_ref[t]], buf_a.at[r], sem.at[slot]),
                pltpu.make_async_copy(ys_ref.at[d2_ref[t]], buf_b.at[r], sem.at[slot]))

    def start_tile(step, buf_a, buf_b, slot):
        def body(g, carry):
            for k in range(ISSUE_UNROLL):
                copies = row_copies(step, g * ISSUE_UNROLL + k, buf_a, buf_b, slot)
                for priority, cp in enumerate(copies):
                    cp.start(priority=priority)
            return carry
        lax.fori_loop(0, tm // ISSUE_UNROLL, body, 0)

    def wait_tile(buf_a, buf_b, slot):
        for buf in (buf_a, buf_b):
            pltpu.make_async_copy(ys_ref.at[pl.ds(0, tm)], buf, sem.at[slot]).wait()

    @pl.when(i == 0)
    def _():
        start_tile(0, ya0, yb0, 0)

    def run(cur, nxt):
        @pl.when(i + 1 < pl.num_programs(0))
        def _():
            start_tile(i + 1, *nxt)

        wait_tile(*cur)
        ya2d[...] = cur[0][...].reshape(tm, D_MODEL)
        yb2d[...] = cur[1][...].reshape(tm, D_MODEL)
        mix = gw_ref[:, 0:1] * ya2d[...] + gw_ref[:, 1:2] * yb2d[...]
        out = x_ref[...] + ada_ref[5:6, :] * mix
        if final:
            y = _rms(out) * gf_ref[...]

            @pl.when(i * tm < N_CTX)
            def _():
                yp_ref[...] = y

            @pl.when(i * tm >= N_CTX)
            def _():
                yl_ref[...] = y
        else:
            o_ref[...] = out

    @pl.when(i % 2 == 0)
    def _():
        run((ya0, yb0, 0), (ya1, yb1, 1))

    @pl.when(i % 2 == 1)
    def _():
        run((ya1, yb1, 1), (ya0, yb0, 0))


def _combine(d1, d2, x, ada, layer, gate_w, ys, final_g=None):
    tm = COMBINE_TM
    final = final_g is not None
    ctx_tiles = N_CTX // tm
    row_buf = pltpu.VMEM((tm, 1, D_MODEL), F32)
    in_specs = [
        pl.BlockSpec((tm, D_MODEL), lambda i, *_: (i, 0)),
        pl.BlockSpec((None, None, 6, D_MODEL),
                     lambda i, *_: (layer, _cond_of_tile(i, tm), 0, 0)),
        pl.BlockSpec((tm, LANES), lambda i, *_: (i, 0)),
    ]
    if final:
        in_specs.append(pl.BlockSpec((1, D_MODEL), lambda i, *_: (0, 0)))
        out_specs = [
            pl.BlockSpec((tm, D_MODEL), lambda i, *_: (jnp.minimum(i, ctx_tiles - 1), 0)),
            pl.BlockSpec((tm, D_MODEL), lambda i, *_: (jnp.maximum(i - ctx_tiles, 0), 0)),
        ]
        out_shape = [jax.ShapeDtypeStruct((N_CTX, D_MODEL), F32),
                     jax.ShapeDtypeStruct((N_LAT, D_MODEL), F32)]
        args = (d1, d2, x, ada, gate_w, final_g, ys)
    else:
        out_specs = pl.BlockSpec((tm, D_MODEL), lambda i, *_: (i, 0))
        out_shape = jax.ShapeDtypeStruct((N_TOK, D_MODEL), F32)
        args = (d1, d2, x, ada, gate_w, ys)
    in_specs.append(pl.BlockSpec(memory_space=pl.ANY))
    return pl.pallas_call(
        functools.partial(_combine_kernel, final=final),
        grid_spec=pltpu.PrefetchScalarGridSpec(
            num_scalar_prefetch=2,
            grid=(N_TOK // tm,),
            in_specs=in_specs,
            out_specs=out_specs,
            scratch_shapes=[row_buf, row_buf, row_buf, row_buf,
                            pltpu.VMEM((tm, D_MODEL), F32), pltpu.VMEM((tm, D_MODEL), F32),
                            pltpu.SemaphoreType.DMA((2,))],
        ),
        out_shape=out_shape,
        compiler_params=_params("arbitrary"),
        name="moe_combine",
    )(*args)


def _moe_layer(x, ada, layer, m, g2, w_router, w_gate, w_up, w_down, final_g=None):
    h_rows, route, gate_w, counts = _route(x, ada, layer, g2, w_router)
    cnt = counts[0, :N_EXPERTS].astype(jnp.int32)
    tiles_per_expert = (cnt + MOE_TG - 1) // MOE_TG
    tile_end = jnp.cumsum(tiles_per_expert)
    row_start = (tile_end - tiles_per_expert) * MOE_TG
    n_tiles = tile_end[-1:]
    t = jnp.minimum(jnp.arange(MOE_MAX_TILES, dtype=jnp.int32), n_tiles - 1)
    tile_expert = jnp.sum((t[:, None] >= tile_end[None, :]).astype(jnp.int32), axis=1)
    experts = jnp.arange(N_EXPERTS, dtype=jnp.int32)
    d1 = jnp.sum(jnp.where(route[0][:, None] == experts, row_start, 0), axis=1) + route[2]
    d2 = jnp.sum(jnp.where(route[1][:, None] == experts, row_start, 0), axis=1) + route[3]
    spare_tile = jnp.minimum(n_tiles + experts, MOE_MAX_TILES - 1)
    fill_start = jnp.concatenate([row_start + cnt, spare_tile * MOE_TG])
    fill_rows = jnp.concatenate([tiles_per_expert * MOE_TG - cnt,
                                 jnp.where(n_tiles + experts < MOE_MAX_TILES, MOE_TG, 0)])
    hs, wg_bf, wu_bf, wd_bf = _dispatch(d1, d2, fill_start, fill_rows, h_rows, m, w_gate, w_up, w_down)
    ys = _experts(tile_expert, n_tiles, hs, wg_bf, wu_bf, wd_bf)
    return _combine(d1, d2, x, ada, layer, gate_w, ys, final_g)


assert DEPTH % 2 == 0


def _rope_tables(dim):
    half = dim // 2
    quarter = half // 2
    t = np.arange(DEC_SEQ)
    pos = np.stack([t // GRID_W, t % GRID_W], axis=1).astype(np.float32)
    freqs = (ROPE_THETA ** (-np.arange(quarter, dtype=np.float32) / quarter)).astype(np.float32)
    ang = pos[:, :, None] * freqs[None, None, :]
    cos = np.cos(ang.astype(np.float64))
    sin = np.sin(ang.astype(np.float64))
    cos_t = np.concatenate([cos, cos], axis=-1).reshape(DEC_SEQ, dim)
    sin_t = np.concatenate([-sin, sin], axis=-1).reshape(DEC_SEQ, dim)
    lane = np.arange(dim)
    swap = np.where(lane % half < quarter, lane + quarter, lane - quarter)
    return cos_t.astype(np.float32), sin_t.astype(np.float32), swap


def kernel(x_prompt, x_sample, cache_mla_ckv, cache_mla_kpe, cache_gqa_k, cache_gqa_v, c, c_ctx, w_ada, b_ada, norm1_g, norm2_g, conv_w_in, conv_k, conv_w_out, mla_w_down, mla_q_norm_g, mla_kv_norm_g, mla_w_uq, mla_w_ukv, mla_w_o, gqa_w_qkv, gqa_q_norm_g, gqa_k_norm_g, gqa_w_o, ffn_w_gate, ffn_w_up, ffn_w_down, moe_w_router, moe_w_gate, moe_w_up, moe_w_down, final_norm_g):
    xs = (x_prompt.reshape(N_CTX, D_MODEL), x_sample.reshape(N_LAT, D_MODEL))
    conv_w_in, conv_w_out = conv_w_in.astype(BF16), conv_w_out.astype(BF16)
    ffn_w_gate, ffn_w_up, ffn_w_down = (w.astype(BF16) for w in (ffn_w_gate, ffn_w_up, ffn_w_down))
    cond = jnp.concatenate(
        [c_ctx[None, :], c, jnp.zeros((N_COND - 1 - DEC_BATCH, D_MODEL), F32)], axis=0)
    ada = _adaln_table(cond, w_ada, b_ada)

    new_ckv, new_kpe, new_k, new_v = [], [], [], []
    for i in range(DEPTH):
        j = i // N_MIXERS
        kind = i % N_MIXERS
        g1 = norm1_g[i][None, :]
        if kind == 0:
            x = _conv_layer(xs if i == 0 else (x,), ada, i, j, g1, conv_w_in, conv_k, conv_w_out)
        elif kind == 1:
            cos64, sin64, swap64 = _rope_tables(MLA_ROPE)
            wd = mla_w_down[j]
            kpe_cols = wd[:, MLA_Q_RANK + MLA_KV_RANK:]
            w_down = jnp.concatenate([wd, kpe_cols[:, swap64]], axis=1).astype(BF16)
            wq = mla_w_uq[j].reshape(MLA_Q_RANK, MLA_HEADS, MLA_QK)
            wq_nope = wq[:, :, :MLA_NOPE].reshape(MLA_Q_RANK, MLA_Q_NOPE_COLS)
            wq_rope = wq[:, :, MLA_NOPE:]
            w_uq = jnp.concatenate(
                [wq_nope, wq_rope.reshape(MLA_Q_RANK, MLA_Q_ROPE_COLS),
                 wq_rope[:, :, swap64].reshape(MLA_Q_RANK, MLA_Q_ROPE_COLS)], axis=1).astype(BF16)
            wkv = mla_w_ukv[j].reshape(MLA_KV_RANK, MLA_HEADS, MLA_NOPE + MLA_V)
            w_ukv = jnp.concatenate(
                [wkv[:, :, :MLA_NOPE].reshape(MLA_KV_RANK, MLA_Q_NOPE_COLS),
                 wkv[:, :, MLA_NOPE:].reshape(MLA_KV_RANK, MLA_HEADS * MLA_V)], axis=1).astype(BF16)
            rope = (jnp.asarray(np.tile(cos64, (1, MLA_HEADS))), jnp.asarray(np.tile(sin64, (1, MLA_HEADS))),
                    jnp.asarray(cos64), jnp.asarray(sin64))
            x, ckv_p, kpe_p = _mla_layer(
                x, ada, i, g1, w_down, mla_q_norm_g[j][None, :], mla_kv_norm_g[j][None, :],
                w_uq, w_ukv, mla_w_o[j].astype(BF16), cache_mla_ckv[:, j], cache_mla_kpe[:, j], rope)
            new_ckv.append(ckv_p.reshape(BATCH, SEQ, MLA_KV_RANK))
            new_kpe.append(kpe_p.reshape(BATCH, SEQ, MLA_ROPE))
        else:
            cos128, sin128, swap128 = _rope_tables(GQA_HEAD_DIM)
            wqkv = gqa_w_qkv[j]
            wq = wqkv[:, :GQA_Q_COLS].reshape(D_MODEL, GQA_HEADS, GQA_HEAD_DIM)
            wk = wqkv[:, GQA_Q_COLS:GQA_Q_COLS + GQA_KV_COLS].reshape(D_MODEL, GQA_KV_HEADS, GQA_HEAD_DIM)
            w_qkv_ctx = wqkv.astype(BF16)
            w_qkv_lat = jnp.concatenate(
                [wqkv, wq[:, :, swap128].reshape(D_MODEL, GQA_Q_COLS),
                 wk[:, :, swap128].reshape(D_MODEL, GQA_KV_COLS)], axis=1).astype(BF16)
            gq = gqa_q_norm_g[j]
            gk = gqa_k_norm_g[j]
            rope = (jnp.asarray(np.tile(cos128, (1, GQA_HEADS))), jnp.asarray(np.tile(sin128, (1, GQA_HEADS))),
                    jnp.asarray(np.tile(cos128, (1, GQA_KV_HEADS))), jnp.asarray(np.tile(sin128, (1, GQA_KV_HEADS))))
            x, k_p, v_p = _gqa_layer(
                x, ada, i, g1, w_qkv_ctx, w_qkv_lat,
                jnp.tile(gq, GQA_HEADS)[None, :], jnp.tile(gk, GQA_KV_HEADS)[None, :],
                jnp.tile(gq[swap128], GQA_HEADS)[None, :], jnp.tile(gk[swap128], GQA_KV_HEADS)[None, :],
                gqa_w_o[j].astype(BF16),
                cache_gqa_k[:, j].reshape(DEC_BATCH, PAST_LEN, GQA_KV_COLS),
                cache_gqa_v[:, j].reshape(DEC_BATCH, PAST_LEN, GQA_KV_COLS), rope)
            new_k.append(k_p.reshape(BATCH, SEQ, GQA_KV_HEADS, GQA_HEAD_DIM))
            new_v.append(v_p.reshape(BATCH, SEQ, GQA_KV_HEADS, GQA_HEAD_DIM))

        m = i // 2
        g2 = norm2_g[i][None, :]
        if i % 2 == 0:
            x = _ffn_layer(x, ada, i, m, g2, ffn_w_gate, ffn_w_up, ffn_w_down)
        else:
            w_router = jnp.pad(moe_w_router[m], ((0, 0), (0, LANES - N_EXPERTS)))
            final_g = final_norm_g[None, :] if i == DEPTH - 1 else None
            x = _moe_layer(x, ada, i, m, g2, w_router, moe_w_gate, moe_w_up, moe_w_down, final_g)

    y_prompt, y_sample = x
    y_prompt = y_prompt.reshape(BATCH, SEQ, D_MODEL)
    y_sample = y_sample.reshape(DEC_BATCH, DEC_SEQ, D_MODEL)
    return (y_prompt, y_sample,
            jnp.stack(new_ckv, axis=1), jnp.stack(new_kpe, axis=1),
            jnp.stack(new_k, axis=1), jnp.stack(new_v, axis=1))
```

```python
import functools

import numpy as np
import jax
import jax.numpy as jnp
from jax import lax
from jax.experimental import pallas as pl
from jax.experimental.pallas import tpu as pltpu

D_MODEL = 1024
BATCH = 32
SEQ = 256
DEPTH = 4
DEC_BATCH = 2
DEC_SEQ = 1024
PAST_LEN = 512
GRID_W = 64
N_MIXERS = 3
CONV_WIDTH = 3
MLA_HEADS = 8
MLA_NOPE = 128
MLA_ROPE = 64
MLA_V = 128
MLA_Q_RANK = 384
MLA_KV_RANK = 256
GQA_HEADS = 8
GQA_KV_HEADS = 2
GQA_HEAD_DIM = 128
D_FF = 2816
N_EXPERTS = 8
MOE_FF = 2048
ROPE_THETA = 10000.0
EPS = 1e-6

N_CTX = BATCH * SEQ
N_LAT = DEC_BATCH * DEC_SEQ
N_TOK = N_CTX + N_LAT
N_COND = 8
LANES = 128
Q_TILE = 256
VMEM_LIMIT = 56 * 1024 * 1024

F32 = jnp.float32
BF16 = jnp.bfloat16


def _dot(a, b):
    return jnp.dot(a, b, preferred_element_type=F32)


def _dot_t(a, b):
    return lax.dot_general(a, b, (((1,), (1,)), ((), ())), preferred_element_type=F32)


def _rms(x):
    return x * lax.rsqrt(jnp.mean(x * x, axis=-1, keepdims=True) + EPS)


def _modulate(x, g, shift, scale):
    return _rms(x) * g * (1.0 + scale) + shift


def _silu(x):
    return x * jax.nn.sigmoid(x)


def _cond_of_tile(i, tm):
    start = i * tm
    return jnp.where(start < N_CTX, 0, 1 + (start - N_CTX) // DEC_SEQ)


def _resident(shape):
    return pl.BlockSpec(shape, lambda *_: (0,) * len(shape), pipeline_mode=pl.Buffered(1))


def _resident_at(index, shape):
    return pl.BlockSpec((None,) + tuple(shape), lambda *_: (index,) + (0,) * len(shape),
                        pipeline_mode=pl.Buffered(1))


def _params(*sem):
    return pltpu.CompilerParams(dimension_semantics=sem, vmem_limit_bytes=VMEM_LIMIT)


def _adaln_kernel(cond_ref, w_ref, b_ref, o_ref):
    s = _silu(cond_ref[...]).astype(BF16)
    o_ref[...] = _dot(s, w_ref[...].astype(BF16)) + b_ref[...]


def _adaln_table(cond, w_ada, b_ada):
    tn = 1536
    out = pl.pallas_call(
        _adaln_kernel,
        grid=(DEPTH, 6 * D_MODEL // tn),
        in_specs=[
            pl.BlockSpec((N_COND, D_MODEL), lambda l, j: (0, 0)),
            pl.BlockSpec((None, D_MODEL, tn), lambda l, j: (l, 0, j)),
            pl.BlockSpec((None, 1, tn), lambda l, j: (l, 0, j)),
        ],
        out_specs=pl.BlockSpec((None, N_COND, tn), lambda l, j: (l, 0, j)),
        out_shape=jax.ShapeDtypeStruct((DEPTH, N_COND, 6 * D_MODEL), F32),
        compiler_params=_params("arbitrary", "arbitrary"),
        name="adaln_table",
    )(cond, w_ada, b_ada.reshape(DEPTH, 1, 6 * D_MODEL))
    return out.reshape(DEPTH, N_COND, 6, D_MODEL)


def _ada_spec(layer, tm):
    return pl.BlockSpec((None, None, 6, D_MODEL), lambda i: (layer, _cond_of_tile(i, tm), 0, 0))


def _ada_spec_at(layer, cond_fn):
    return pl.BlockSpec((None, None, 6, D_MODEL), lambda i: (layer, cond_fn(i), 0, 0))


CONV_TM = 1024
CONV_CC = 256


def _conv_kernel(*refs, split):
    if split:
        xp_ref, x_ref, ada_ref, g1_ref, win_ref, ck_ref, wout_ref, o_ref, v_scr = refs
    else:
        x_ref, ada_ref, g1_ref, win_ref, ck_ref, wout_ref, o_ref, v_scr = refs
    tm = x_ref.shape[0]
    i = pl.program_id(0)
    x = x_ref[...]
    if split:
        x = jnp.where(i * tm < N_CTX, xp_ref[...], x)
    h = _modulate(x, g1_ref[...], ada_ref[0:1, :], ada_ref[1:2, :]).astype(BF16)
    period = jnp.where(i * tm < N_CTX, SEQ, DEC_SEQ)
    pos = lax.broadcasted_iota(jnp.int32, (tm, 1), 0) & (period - 1)
    first = pos == 0
    last = pos == period - 1
    for j in range(D_MODEL // CONV_CC):
        lo = j * CONV_CC
        b_gate = _dot(h, win_ref[:, lo:lo + CONV_CC])
        c_gate = _dot(h, win_ref[:, D_MODEL + lo:D_MODEL + lo + CONV_CC])
        x_in = _dot(h, win_ref[:, 2 * D_MODEL + lo:2 * D_MODEL + lo + CONV_CC])
        u = c_gate * x_in
        u_prev = jnp.where(first, 0.0, pltpu.roll(u, 1, 0))
        u_next = jnp.where(last, 0.0, pltpu.roll(u, tm - 1, 0))
        conv = (ck_ref[0:1, lo:lo + CONV_CC] * u_prev + ck_ref[1:2, lo:lo + CONV_CC] * u
                + ck_ref[2:3, lo:lo + CONV_CC] * u_next)
        v_scr[:, lo:lo + CONV_CC] = (b_gate * conv).astype(BF16)
    o_ref[...] = x + ada_ref[2:3, :] * _dot(v_scr[...], wout_ref[...])


def _conv_layer(xs, ada, layer, j, g1, w_in, conv_k, w_out):
    tm = CONV_TM
    ctx_tiles = N_CTX // tm
    if len(xs) == 1:
        x_specs = [pl.BlockSpec((tm, D_MODEL), lambda i: (i, 0))]
    else:
        x_specs = [pl.BlockSpec((tm, D_MODEL), lambda i: (jnp.minimum(i, ctx_tiles - 1), 0)),
                   pl.BlockSpec((tm, D_MODEL), lambda i: (jnp.maximum(i - ctx_tiles, 0), 0))]
    return pl.pallas_call(
        functools.partial(_conv_kernel, split=len(xs) == 2),
        grid=(N_TOK // tm,),
        in_specs=x_specs + [
            _ada_spec(layer, tm),
            _resident((1, D_MODEL)),
            _resident_at(j, (D_MODEL, 3 * D_MODEL)),
            _resident_at(j, (CONV_WIDTH, D_MODEL)),
            _resident_at(j, (D_MODEL, D_MODEL)),
        ],
        out_specs=pl.BlockSpec((tm, D_MODEL), lambda i: (i, 0)),
        out_shape=jax.ShapeDtypeStruct((N_TOK, D_MODEL), F32),
        scratch_shapes=[pltpu.VMEM((tm, D_MODEL), BF16)],
        compiler_params=_params("arbitrary"),
        name=f"conv_mixer_{layer}",
    )(*xs, ada, g1, w_in, conv_k, w_out)


FFN_TM = 512
FFN_CHUNKS = ((0, 1536), (1536, D_FF))


def _ffn_kernel(x_ref, ada_ref, g2_ref, wg_ref, wu_ref, wd_ref, o_ref):
    x = x_ref[...]
    h = _modulate(x, g2_ref[...], ada_ref[3:4, :], ada_ref[4:5, :]).astype(BF16)
    f = None
    for lo, hi in FFN_CHUNKS:
        g = _dot(h, wg_ref[:, lo:hi])
        u = _dot(h, wu_ref[:, lo:hi])
        part = _dot((_silu(g) * u).astype(BF16), wd_ref[lo:hi, :])
        f = part if f is None else f + part
    o_ref[...] = x + ada_ref[5:6, :] * f


def _ffn_layer(x, ada, layer, m, g2, w_gate, w_up, w_down):
    tm = FFN_TM
    return pl.pallas_call(
        _ffn_kernel,
        grid=(N_TOK // tm,),
        in_specs=[
            pl.BlockSpec((tm, D_MODEL), lambda i: (i, 0)),
            _ada_spec(layer, tm),
            _resident((1, D_MODEL)),
            _resident_at(m, (D_MODEL, D_FF)),
            _resident_at(m, (D_MODEL, D_FF)),
            _resident_at(m, (D_FF, D_MODEL)),
        ],
        out_specs=pl.BlockSpec((tm, D_MODEL), lambda i: (i, 0)),
        out_shape=jax.ShapeDtypeStruct((N_TOK, D_MODEL), F32),
        compiler_params=_params("arbitrary"),
        name=f"dense_ffn_{layer}",
    )(x, ada, g2, w_gate, w_up, w_down)


def _attention(qk_parts, v_ref, heads, kv_of_head, dv, scale, o_scr, finish, keys_of, stage):
    def scores(rows, keys, h):
        s = None
        for q_ref, k_ref, width, k_shared in qk_parts:
            kb = 0 if k_shared else kv_of_head(h)
            part = _dot_t(q_ref[rows, h * width:(h + 1) * width],
                          k_ref[keys, kb * width:(kb + 1) * width])
            s = part if s is None else s + part
        return s * scale

    def q_block(rows):
        s_scr, p_scr = stage
        group = s_scr.shape[0]
        keys = keys_of(rows)
        for h0 in range(0, heads, group):
            for j in range(group):
                s_scr[j] = scores(rows, keys, h0 + j)
            for j in range(group):
                s = s_scr[j]
                e = jnp.exp(s - jnp.max(s, axis=-1, keepdims=True))
                p_scr[j] = (e * (1.0 / jnp.sum(e, axis=-1, keepdims=True))).astype(BF16)
            for j in range(group):
                h = h0 + j
                g = kv_of_head(h)
                o = _dot(p_scr[j], v_ref[keys, g * dv:(g + 1) * dv])
                o_scr[rows, h * dv:(h + 1) * dv] = o.astype(BF16)
        finish(rows)

    _for_row_blocks(o_scr.shape[0], q_block)


def _stage_scratch(group, keys):
    return [pltpu.VMEM((group, Q_TILE, keys), F32), pltpu.VMEM((group, Q_TILE, keys), BF16)]


LATENT_KEYS = DEC_SEQ + PAST_LEN
LATENT_STAGE_HEADS = 4


def _keys_of(latent):
    assert SEQ == Q_TILE
    return (lambda rows: slice(None)) if latent else (lambda rows: rows)


def _for_row_blocks(n_rows, fn):
    if n_rows <= 2 * Q_TILE:
        for b in range(n_rows // Q_TILE):
            fn(pl.ds(b * Q_TILE, Q_TILE))
    else:
        def body(b, carry):
            fn(pl.ds(pl.multiple_of(b * Q_TILE, Q_TILE), Q_TILE))
            return carry
        lax.fori_loop(0, n_rows // Q_TILE, body, 0)


MLA_QK = MLA_NOPE + MLA_ROPE
MLA_DOWN_COLS = MLA_Q_RANK + MLA_KV_RANK + 2 * MLA_ROPE
MLA_Q_NOPE_COLS = MLA_HEADS * MLA_NOPE
MLA_Q_ROPE_COLS = MLA_HEADS * MLA_ROPE


def _ctx_then_latent(body, n_in, refs):
    ins, lat_ref, rest = refs[:n_in], refs[n_in], refs[n_in + 1:]
    o_ref = rest[0]
    i = pl.program_id(0)

    @pl.when(i < CTX_TILES)
    def _():
        body(*ins, *rest, latent=False)

    @pl.when(i >= CTX_TILES)
    def _():
        o_ref[...] = lat_ref[...]


def _ctx_specs(out_tails):
    def tile(n_trailing):
        return lambda i: (jnp.minimum(i, CTX_TILES - 1),) + (0,) * n_trailing
    x_spec = pl.BlockSpec((CTX_TM, D_MODEL), tile(1))
    lat_spec = pl.BlockSpec((CTX_TM, D_MODEL), lambda i: (jnp.maximum(i - CTX_TILES, 0), 0))
    out_specs = [pl.BlockSpec((CTX_TM, D_MODEL), lambda i: (i, 0))]
    out_specs += [pl.BlockSpec((CTX_TM,) + tail, tile(len(tail))) for tail in out_tails]
    return x_spec, lat_spec, out_specs


CTX_TM = 2 * SEQ
CTX_TILES = N_CTX // CTX_TM
CTX_STEPS = CTX_TILES + N_LAT // CTX_TM


def _mla_kernel(*refs, latent):
    if latent:
        _mla_body(*refs, latent=True)
    else:
        _ctx_then_latent(_mla_body, 9, refs)


def _mla_body(*refs, latent):
    if latent:
        (x_ref, ada_ref, g1_ref, wdown_ref, gq_ref, gkv_ref, wuq_ref, wukv_ref, wo_ref,
         cosq_ref, sinq_ref, cosk_ref, sink_ref, cckv_ref, ckpe_ref,
         o_ref, qn_scr, qr_scr, kn_scr, kr_scr, v_scr, o_scr, *stage) = refs
    else:
        (x_ref, ada_ref, g1_ref, wdown_ref, gq_ref, gkv_ref, wuq_ref, wukv_ref, wo_ref,
         o_ref, ckv_out_ref, kpe_out_ref, qn_scr, qr_scr, kn_scr, kr_scr, v_scr, o_scr,
         *stage) = refs
    tm = x_ref.shape[0]
    kpe_lo = MLA_Q_RANK + MLA_KV_RANK

    def project(rows):
        h = _modulate(x_ref[rows, :], g1_ref[...], ada_ref[0:1, :], ada_ref[1:2, :]).astype(BF16)
        down = _dot(h, wdown_ref[...])
        c_q = down[:, :MLA_Q_RANK]
        c_kv = down[:, MLA_Q_RANK:kpe_lo]
        kpe = down[:, kpe_lo:kpe_lo + MLA_ROPE]
        q = _dot((_rms(c_q) * gq_ref[...]).astype(BF16), wuq_ref[...])
        ckv_n = _rms(c_kv) * gkv_ref[...]
        kv = _dot(ckv_n.astype(BF16), wukv_ref[...])
        qn_scr[rows, :] = q[:, :MLA_Q_NOPE_COLS].astype(BF16)
        q_rope = q[:, MLA_Q_NOPE_COLS:MLA_Q_NOPE_COLS + MLA_Q_ROPE_COLS]
        kn_scr[rows, :] = kv[:, :MLA_Q_NOPE_COLS].astype(BF16)
        v_scr[rows, :] = kv[:, MLA_Q_NOPE_COLS:].astype(BF16)
        if latent:
            q_swap = q[:, MLA_Q_NOPE_COLS + MLA_Q_ROPE_COLS:]
            q_rope = q_rope * cosq_ref[rows, :] + q_swap * sinq_ref[rows, :]
            kpe_swap = down[:, kpe_lo + MLA_ROPE:kpe_lo + 2 * MLA_ROPE]
            kpe = kpe * cosk_ref[rows, :] + kpe_swap * sink_ref[rows, :]
        else:
            ckv_out_ref[rows, :] = ckv_n
            kpe_out_ref[rows, :] = kpe
        qr_scr[rows, :] = q_rope.astype(BF16)
        kr_scr[rows, :] = kpe.astype(BF16)

    _for_row_blocks(tm, project)

    if latent:
        def expand_cache(rows):
            kv_c = _dot(cckv_ref[rows, :].astype(BF16), wukv_ref[...])
            dst = pl.ds(tm + rows.start, Q_TILE)
            kn_scr[dst, :] = kv_c[:, :MLA_Q_NOPE_COLS].astype(BF16)
            v_scr[dst, :] = kv_c[:, MLA_Q_NOPE_COLS:].astype(BF16)
            kr_scr[dst, :] = ckpe_ref[rows, :].astype(BF16)

        _for_row_blocks(PAST_LEN, expand_cache)

    def finish(rows):
        o_ref[rows, :] = x_ref[rows, :] + ada_ref[2:3, :] * _dot(o_scr[rows, :], wo_ref[...])

    _attention([(qn_scr, kn_scr, MLA_NOPE, False), (qr_scr, kr_scr, MLA_ROPE, True)],
               v_scr, MLA_HEADS, lambda hd: hd, MLA_V, MLA_QK ** -0.5, o_scr, finish,
               _keys_of(latent), stage)


def _mla_scratch(tm, sk):
    return [
        pltpu.VMEM((tm, MLA_Q_NOPE_COLS), BF16),
        pltpu.VMEM((tm, MLA_Q_ROPE_COLS), BF16),
        pltpu.VMEM((sk, MLA_Q_NOPE_COLS), BF16),
        pltpu.VMEM((sk, MLA_ROPE), BF16),
        pltpu.VMEM((sk, MLA_HEADS * MLA_V), BF16),
        pltpu.VMEM((tm, MLA_HEADS * MLA_V), BF16),
    ]


def _mla_layer(x, ada, layer, g1, w_down, g_q, g_kv, w_uq, w_ukv, w_o,
               cache_ckv, cache_kpe, rope):
    n_uq = w_uq.shape[1]
    weights = [
        _resident((1, D_MODEL)),
        _resident((D_MODEL, MLA_DOWN_COLS)),
        _resident((1, MLA_Q_RANK)),
        _resident((1, MLA_KV_RANK)),
        _resident((MLA_Q_RANK, n_uq)),
        _resident((MLA_KV_RANK, 2 * MLA_Q_NOPE_COLS)),
        _resident((MLA_HEADS * MLA_V, D_MODEL)),
    ]
    ctx_weights = list(weights)
    ctx_weights[4] = pl.BlockSpec((MLA_Q_RANK, MLA_Q_NOPE_COLS + MLA_Q_ROPE_COLS), lambda *_: (0, 0),
                                  pipeline_mode=pl.Buffered(1))
    weight_args = (g1, w_down, g_q, g_kv, w_uq, w_ukv, w_o)
    cos_q, sin_q, cos_k, sin_k = rope
    ctx_blocks = N_CTX // DEC_SEQ
    lat_out = pl.pallas_call(
        functools.partial(_mla_kernel, latent=True),
        grid=(DEC_BATCH,),
        in_specs=[pl.BlockSpec((DEC_SEQ, D_MODEL), lambda b: (ctx_blocks + b, 0)),
                  _ada_spec_at(layer, lambda b: 1 + b)] + weights + [
            _resident((DEC_SEQ, MLA_Q_ROPE_COLS)),
            _resident((DEC_SEQ, MLA_Q_ROPE_COLS)),
            _resident((DEC_SEQ, MLA_ROPE)),
            _resident((DEC_SEQ, MLA_ROPE)),
            pl.BlockSpec((None, PAST_LEN, MLA_KV_RANK), lambda b: (b, 0, 0)),
            pl.BlockSpec((None, PAST_LEN, MLA_ROPE), lambda b: (b, 0, 0)),
        ],
        out_specs=pl.BlockSpec((DEC_SEQ, D_MODEL), lambda b: (b, 0)),
        out_shape=jax.ShapeDtypeStruct((N_LAT, D_MODEL), F32),
        scratch_shapes=(_mla_scratch(DEC_SEQ, LATENT_KEYS)
                        + _stage_scratch(LATENT_STAGE_HEADS, LATENT_KEYS)),
        compiler_params=_params("arbitrary"),
        name=f"mla_latent_{layer}",
    )(x, ada, *weight_args, cos_q, sin_q, cos_k, sin_k, cache_ckv, cache_kpe)

    x_spec, lat_spec, out_specs = _ctx_specs(((MLA_KV_RANK,), (MLA_ROPE,)))
    return pl.pallas_call(
        functools.partial(_mla_kernel, latent=False),
        grid=(CTX_STEPS,),
        in_specs=[x_spec, _ada_spec_at(layer, lambda i: 0)] + ctx_weights + [lat_spec],
        out_specs=out_specs,
        out_shape=[
            jax.ShapeDtypeStruct((N_TOK, D_MODEL), F32),
            jax.ShapeDtypeStruct((N_CTX, MLA_KV_RANK), F32),
            jax.ShapeDtypeStruct((N_CTX, MLA_ROPE), F32),
        ],
        scratch_shapes=_mla_scratch(CTX_TM, CTX_TM) + _stage_scratch(MLA_HEADS, SEQ),
        compiler_params=_params("arbitrary"),
        name=f"mla_ctx_{layer}",
    )(x, ada, *weight_args, lat_out)


GQA_Q_COLS = GQA_HEADS * GQA_HEAD_DIM
GQA_KV_COLS = GQA_KV_HEADS * GQA_HEAD_DIM
GQA_REP = GQA_HEADS // GQA_KV_HEADS


def _head_rms(t, n_heads):
    cols = []
    for hd in range(n_heads):
        th = t[:, hd * GQA_HEAD_DIM:(hd + 1) * GQA_HEAD_DIM]
        r = lax.rsqrt(jnp.mean(th * th, axis=-1, keepdims=True) + EPS)
        cols.append(jnp.broadcast_to(r, th.shape))
    return jnp.concatenate(cols, axis=-1)


def _gqa_kernel(*refs, latent):
    if latent:
        _gqa_body(*refs, latent=True)
    else:
        _ctx_then_latent(_gqa_body, 7, refs)


def _gqa_body(*refs, latent):
    if latent:
        (x_ref, ada_ref, g1_ref, wqkv_ref, gq_ref, gk_ref, wo_ref,
         gqs_ref, gks_ref, cosq_ref, sinq_ref, cosk_ref, sink_ref, ck_ref, cv_ref,
         o_ref, q_scr, k_scr, v_scr, o_scr, *stage) = refs
    else:
        (x_ref, ada_ref, g1_ref, wqkv_ref, gq_ref, gk_ref, wo_ref,
         o_ref, k_out_ref, v_out_ref, q_scr, k_scr, v_scr, o_scr, *stage) = refs
    tm = x_ref.shape[0]

    def project(rows):
        h = _modulate(x_ref[rows, :], g1_ref[...], ada_ref[0:1, :], ada_ref[1:2, :]).astype(BF16)
        qkv = _dot(h, wqkv_ref[...])
        q_raw = qkv[:, :GQA_Q_COLS]
        k_raw = qkv[:, GQA_Q_COLS:GQA_Q_COLS + GQA_KV_COLS]
        v = qkv[:, GQA_Q_COLS + GQA_KV_COLS:GQA_Q_COLS + 2 * GQA_KV_COLS]
        q_r = _head_rms(q_raw, GQA_HEADS)
        k_r = _head_rms(k_raw, GQA_KV_HEADS)
        q = q_raw * q_r * gq_ref[...]
        k = k_raw * k_r * gk_ref[...]
        if latent:
            lo = GQA_Q_COLS + 2 * GQA_KV_COLS
            q_swap = qkv[:, lo:lo + GQA_Q_COLS] * q_r * gqs_ref[...]
            k_swap = qkv[:, lo + GQA_Q_COLS:lo + GQA_Q_COLS + GQA_KV_COLS] * k_r * gks_ref[...]
            q = q * cosq_ref[rows, :] + q_swap * sinq_ref[rows, :]
            k = k * cosk_ref[rows, :] + k_swap * sink_ref[rows, :]
        else:
            for g in range(GQA_KV_HEADS):
                cols = slice(g * GQA_HEAD_DIM, (g + 1) * GQA_HEAD_DIM)
                k_out_ref[rows, g, :] = k[:, cols]
                v_out_ref[rows, g, :] = v[:, cols]
        q_scr[rows, :] = q.astype(BF16)
        k_scr[rows, :] = k.astype(BF16)
        v_scr[rows, :] = v.astype(BF16)

    _for_row_blocks(tm, project)

    if latent:
        k_scr[tm:, :] = ck_ref[...].astype(BF16)
        v_scr[tm:, :] = cv_ref[...].astype(BF16)

    def finish(rows):
        o_ref[rows, :] = x_ref[rows, :] + ada_ref[2:3, :] * _dot(o_scr[rows, :], wo_ref[...])

    _attention([(q_scr, k_scr, GQA_HEAD_DIM, False)], v_scr, GQA_HEADS,
               lambda hd: hd // GQA_REP, GQA_HEAD_DIM, GQA_HEAD_DIM ** -0.5, o_scr, finish,
               _keys_of(latent), stage)


def _gqa_scratch(tm, sk):
    return [
        pltpu.VMEM((tm, GQA_Q_COLS), BF16),
        pltpu.VMEM((sk, GQA_KV_COLS), BF16),
        pltpu.VMEM((sk, GQA_KV_COLS), BF16),
        pltpu.VMEM((tm, GQA_Q_COLS), BF16),
    ]


def _gqa_layer(x, ada, layer, g1, w_qkv_ctx, w_qkv_lat, g_q, g_k, g_q_swap, g_k_swap, w_o,
               cache_k, cache_v, rope):
    cos_q, sin_q, cos_k, sin_k = rope
    ctx_blocks = N_CTX // DEC_SEQ
    lat_out = pl.pallas_call(
        functools.partial(_gqa_kernel, latent=True),
        grid=(DEC_BATCH,),
        in_specs=[pl.BlockSpec((DEC_SEQ, D_MODEL), lambda b: (ctx_blocks + b, 0)),
                  _ada_spec_at(layer, lambda b: 1 + b),
                  _resident((1, D_MODEL)),
                  _resident(w_qkv_lat.shape),
                  _resident((1, GQA_Q_COLS)),
                  _resident((1, GQA_KV_COLS)),
                  _resident((GQA_Q_COLS, D_MODEL)),
                  _resident((1, GQA_Q_COLS)),
                  _resident((1, GQA_KV_COLS)),
                  _resident((DEC_SEQ, GQA_Q_COLS)),
                  _resident((DEC_SEQ, GQA_Q_COLS)),
                  _resident((DEC_SEQ, GQA_KV_COLS)),
                  _resident((DEC_SEQ, GQA_KV_COLS)),
                  pl.BlockSpec((None, PAST_LEN, GQA_KV_COLS), lambda b: (b, 0, 0)),
                  pl.BlockSpec((None, PAST_LEN, GQA_KV_COLS), lambda b: (b, 0, 0))],
        out_specs=pl.BlockSpec((DEC_SEQ, D_MODEL), lambda b: (b, 0)),
        out_shape=jax.ShapeDtypeStruct((N_LAT, D_MODEL), F32),
        scratch_shapes=(_gqa_scratch(DEC_SEQ, LATENT_KEYS)
                        + _stage_scratch(LATENT_STAGE_HEADS, LATENT_KEYS)),
        compiler_params=_params("arbitrary"),
        name=f"gqa_latent_{layer}",
    )(x, ada, g1, w_qkv_lat, g_q, g_k, w_o, g_q_swap, g_k_swap,
      cos_q, sin_q, cos_k, sin_k, cache_k, cache_v)

    kv_tail = (GQA_KV_HEADS, GQA_HEAD_DIM)
    x_spec, lat_spec, out_specs = _ctx_specs((kv_tail, kv_tail))
    return pl.pallas_call(
        functools.partial(_gqa_kernel, latent=False),
        grid=(CTX_STEPS,),
        in_specs=[x_spec,
                  _ada_spec_at(layer, lambda i: 0),
                  _resident((1, D_MODEL)),
                  _resident(w_qkv_ctx.shape),
                  _resident((1, GQA_Q_COLS)),
                  _resident((1, GQA_KV_COLS)),
                  _resident((GQA_Q_COLS, D_MODEL)),
                  lat_spec],
        out_specs=out_specs,
        out_shape=[
            jax.ShapeDtypeStruct((N_TOK, D_MODEL), F32),
            jax.ShapeDtypeStruct((N_CTX,) + kv_tail, F32),
            jax.ShapeDtypeStruct((N_CTX,) + kv_tail, F32),
        ],
        scratch_shapes=_gqa_scratch(CTX_TM, CTX_TM) + _stage_scratch(GQA_HEADS, SEQ),
        compiler_params=_params("arbitrary"),
        name=f"gqa_ctx_{layer}",
    )(x, ada, g1, w_qkv_ctx, g_q, g_k, w_o, lat_out)


ROUTE_TM = 512
MOE_TG = 256
MOE_MAX_TILES = 2 * N_TOK // MOE_TG + N_EXPERTS
MOE_ROWS = MOE_MAX_TILES * MOE_TG
MOE_FILLS = 2 * N_EXPERTS
MOE_FC = 1024
COMBINE_TM = 256
ISSUE_UNROLL = 8


def _route_kernel(x_ref, ada_ref, g2_ref, wr_hi_ref, wr_lo_ref, h_ref, dest_ref, gatew_ref,
                  count_ref, carry_scr):
    tm = x_ref.shape[0]

    @pl.when(pl.program_id(0) == 0)
    def _():
        carry_scr[...] = jnp.zeros_like(carry_scr)

    h = _modulate(x_ref[...], g2_ref[...], ada_ref[3:4, :], ada_ref[4:5, :])
    h_ref[...] = h.reshape(tm, 1, D_MODEL)
    h_hi = h.astype(BF16)
    h_lo = (h - h_hi.astype(F32)).astype(BF16)
    logits = _dot(h_hi, wr_hi_ref[...]) + (_dot(h_lo, wr_hi_ref[...]) + _dot(h_hi, wr_lo_ref[...]))
    lane = lax.broadcasted_iota(jnp.int32, logits.shape, 1)
    neg = jnp.float32(-jnp.inf)
    l1 = jnp.where(lane < N_EXPERTS, logits, neg)
    m1 = jnp.max(l1, axis=-1, keepdims=True)
    i1 = jnp.min(jnp.where(l1 == m1, lane, LANES), axis=-1, keepdims=True)
    l2 = jnp.where(lane == i1, neg, l1)
    m2 = jnp.max(l2, axis=-1, keepdims=True)
    i2 = jnp.min(jnp.where(l2 == m2, lane, LANES), axis=-1, keepdims=True)
    e2 = jnp.exp(m2 - m1)
    w1 = 1.0 / (1.0 + e2)
    w2 = e2 / (1.0 + e2)
    member = jnp.where((lane == i1) | (lane == i2), 1.0, 0.0)
    r = lax.broadcasted_iota(jnp.int32, (tm, tm), 0)
    c = lax.broadcasted_iota(jnp.int32, (tm, tm), 1)
    tri = jnp.where(c < r, 1.0, 0.0).astype(BF16)
    before = _dot(tri, member.astype(BF16)) + carry_scr[0:1, :]
    rank1 = jnp.sum(jnp.where(lane == i1, before, 0.0), axis=-1, keepdims=True).astype(jnp.int32)
    rank2 = jnp.sum(jnp.where(lane == i2, before, 0.0), axis=-1, keepdims=True).astype(jnp.int32)
    table = jnp.where(lane == 0, i1, jnp.where(lane == 1, i2, jnp.where(
        lane == 2, rank1, jnp.where(lane == 3, rank2, 0))))
    dest_ref[...] = jnp.transpose(table)[0:8, :]
    gatew_ref[...] = jnp.where(lane == 0, w1, jnp.where(lane == 1, w2, 0.0))
    carry_scr[...] = carry_scr[...] + jnp.sum(member, axis=0, keepdims=True)
    count_ref[...] = carry_scr[...]


def _route(x, ada, layer, g2, w_router):
    tm = ROUTE_TM
    w_router_hi = w_router.astype(BF16)
    return pl.pallas_call(
        _route_kernel,
        grid=(N_TOK // tm,),
        in_specs=[
            pl.BlockSpec((tm, D_MODEL), lambda i: (i, 0)),
            _ada_spec(layer, tm),
            _resident((1, D_MODEL)),
            _resident((D_MODEL, LANES)),
            _resident((D_MODEL, LANES)),
        ],
        out_specs=[
            pl.BlockSpec((tm, 1, D_MODEL), lambda i: (i, 0, 0)),
            pl.BlockSpec((8, tm), lambda i: (0, i)),
            pl.BlockSpec((tm, LANES), lambda i: (i, 0)),
            pl.BlockSpec((8, LANES), lambda i: (0, 0)),
        ],
        out_shape=[
            jax.ShapeDtypeStruct((N_TOK, 1, D_MODEL), F32),
            jax.ShapeDtypeStruct((8, N_TOK), jnp.int32),
            jax.ShapeDtypeStruct((N_TOK, LANES), F32),
            jax.ShapeDtypeStruct((8, LANES), F32),
        ],
        scratch_shapes=[pltpu.VMEM((8, LANES), F32)],
        compiler_params=_params("arbitrary"),
        name=f"moe_route_{layer}",
    )(x, ada, g2, w_router_hi, (w_router - w_router_hi.astype(F32)).astype(BF16))


def _dispatch_kernel(d1_ref, d2_ref, fill_start_ref, fill_rows_ref, h_ref, wg_ref, wu_ref, wd_ref,
                     hs_ref, wg_out_ref, wu_out_ref, wd_out_ref, zero_scr, sem):
    i = pl.program_id(0)
    tm = h_ref.shape[0]
    base = i * tm

    def fill_copy(k):
        n = fill_rows_ref[k]
        return pltpu.make_async_copy(zero_scr.at[pl.ds(0, n)],
                                     hs_ref.at[pl.ds(fill_start_ref[k], n)], sem.at[1])

    @pl.when(i == 0)
    def _():
        zero_scr[...] = jnp.zeros_like(zero_scr)
        for k in range(MOE_FILLS):
            @pl.when(fill_rows_ref[k] > 0)
            def _():
                fill_copy(k).start()
        for k in range(MOE_FILLS):
            @pl.when(fill_rows_ref[k] > 0)
            def _():
                fill_copy(k).wait()

    def row_copy(r, d_ref):
        return pltpu.make_async_copy(h_ref.at[r], hs_ref.at[d_ref[base + r]], sem.at[0])

    def issue(g, carry):
        for k in range(ISSUE_UNROLL):
            r = g * ISSUE_UNROLL + k
            row_copy(r, d1_ref).start(priority=0)
            row_copy(r, d2_ref).start(priority=1)
        return carry

    def drain():
        for _ in range(2):
            pltpu.make_async_copy(h_ref, hs_ref.at[pl.ds(0, tm)], sem.at[0]).wait()

    lax.fori_loop(0, tm // ISSUE_UNROLL, issue, 0)
    wg_out_ref[...] = wg_ref[...].astype(BF16)
    wu_out_ref[...] = wu_ref[...].astype(BF16)
    wd_out_ref[...] = wd_ref[...].astype(BF16)
    drain()


def _dispatch(d1, d2, fill_start, fill_rows, h_rows, m, w_gate, w_up, w_down):
    steps = 2 * N_EXPERTS
    tm = N_TOK // steps
    fh = MOE_FF // 2
    return pl.pallas_call(
        _dispatch_kernel,
        grid_spec=pltpu.PrefetchScalarGridSpec(
            num_scalar_prefetch=4,
            grid=(steps,),
            in_specs=[
                pl.BlockSpec((tm, 1, D_MODEL), lambda i, *_: (i, 0, 0)),
                pl.BlockSpec((None, None, D_MODEL, fh), lambda i, *_: (m, i // 2, 0, i % 2)),
                pl.BlockSpec((None, None, D_MODEL, fh), lambda i, *_: (m, i // 2, 0, i % 2)),
                pl.BlockSpec((None, None, fh, D_MODEL), lambda i, *_: (m, i // 2, i % 2, 0)),
            ],
            out_specs=[
                pl.BlockSpec(memory_space=pl.ANY),
                pl.BlockSpec((None, D_MODEL, fh), lambda i, *_: (i // 2, 0, i % 2)),
                pl.BlockSpec((None, D_MODEL, fh), lambda i, *_: (i // 2, 0, i % 2)),
                pl.BlockSpec((None, fh, D_MODEL), lambda i, *_: (i // 2, i % 2, 0)),
            ],
            scratch_shapes=[pltpu.VMEM((MOE_TG, 1, D_MODEL), F32),
                            pltpu.SemaphoreType.DMA((2,))],
        ),
        out_shape=[
            jax.ShapeDtypeStruct((MOE_ROWS, 1, D_MODEL), F32),
            jax.ShapeDtypeStruct((N_EXPERTS, D_MODEL, MOE_FF), BF16),
            jax.ShapeDtypeStruct((N_EXPERTS, D_MODEL, MOE_FF), BF16),
            jax.ShapeDtypeStruct((N_EXPERTS, MOE_FF, D_MODEL), BF16),
        ],
        compiler_params=_params("arbitrary"),
        name="moe_dispatch",
    )(d1, d2, fill_start, fill_rows, h_rows, w_gate, w_up, w_down)


def _experts_kernel(te_ref, nt_ref, hs_ref, wg_ref, wu_ref, wd_ref, ys_ref, h2d_scr):
    used = pl.program_id(0) < nt_ref[0]

    @pl.when(jnp.logical_not(used))
    def _():
        ys_ref[...] = jnp.zeros_like(ys_ref)

    @pl.when(used)
    def _():
        h2d_scr[...] = hs_ref[...].reshape(MOE_TG, D_MODEL)
        h = h2d_scr[...].astype(BF16)
        y = None
        for lo in range(0, MOE_FF, MOE_FC):
            g = _dot(h, wg_ref[:, lo:lo + MOE_FC])
            u = _dot(h, wu_ref[:, lo:lo + MOE_FC])
            part = _dot((_silu(g) * u).astype(BF16), wd_ref[lo:lo + MOE_FC, :])
            y = part if y is None else y + part
        ys_ref[...] = y.reshape(MOE_TG, 1, D_MODEL)


def _experts(tile_expert, n_tiles, hs, w_gate, w_up, w_down):
    rows = pl.BlockSpec((MOE_TG, 1, D_MODEL), lambda t, te, nt: (t, 0, 0))
    return pl.pallas_call(
        _experts_kernel,
        grid_spec=pltpu.PrefetchScalarGridSpec(
            num_scalar_prefetch=2,
            grid=(MOE_MAX_TILES,),
            in_specs=[
                rows,
                pl.BlockSpec((None, D_MODEL, MOE_FF), lambda t, te, nt: (te[t], 0, 0)),
                pl.BlockSpec((None, D_MODEL, MOE_FF), lambda t, te, nt: (te[t], 0, 0)),
                pl.BlockSpec((None, MOE_FF, D_MODEL), lambda t, te, nt: (te[t], 0, 0)),
            ],
            out_specs=rows,
            scratch_shapes=[pltpu.VMEM((MOE_TG, D_MODEL), F32)],
        ),
        out_shape=jax.ShapeDtypeStruct((MOE_ROWS, 1, D_MODEL), F32),
        compiler_params=_params("arbitrary"),
        name="moe_experts",
    )(tile_expert, n_tiles, hs, w_gate, w_up, w_down)


def _combine_kernel(*refs, final):
    if final:
        (d1_ref, d2_ref, x_ref, ada_ref, gw_ref, gf_ref, ys_ref, yp_ref, yl_ref,
         ya0, yb0, ya1, yb1, ya2d, yb2d, sem) = refs
    else:
        (d1_ref, d2_ref, x_ref, ada_ref, gw_ref, ys_ref, o_ref,
         ya0, yb0, ya1, yb1, ya2d, yb2d, sem) = refs
    i = pl.program_id(0)
    tm = x_ref.shape[0]

    def row_copies(step, r, buf_a, buf_b, slot):
        t = step * tm + r
        return (pltpu.make_async_copy(ys_ref.at[d1_ref[t]], buf_a.at[r], sem.at[slot]),
                pltpu.make_async_copy(ys_ref.at[d2_ref[t]], buf_b.at[r], sem.at[slot]))

    def start_tile(step, buf_a, buf_b, slot):
        def body(g, carry):
            for k in range(ISSUE_UNROLL):
                copies = row_copies(step, g * ISSUE_UNROLL + k, buf_a, buf_b, slot)
                for priority, cp in enumerate(copies):
                    cp.start(priority=priority)
            return carry
        lax.fori_loop(0, tm // ISSUE_UNROLL, body, 0)

    def wait_tile(buf_a, buf_b, slot):
        for buf in (buf_a, buf_b):
            pltpu.make_async_copy(ys_ref.at[pl.ds(0, tm)], buf, sem.at[slot]).wait()

    @pl.when(i == 0)
    def _():
        start_tile(0, ya0, yb0, 0)

    def run(cur, nxt):
        @pl.when(i + 1 < pl.num_programs(0))
        def _():
            start_tile(i + 1, *nxt)

        wait_tile(*cur)
        ya2d[...] = cur[0][...].reshape(tm, D_MODEL)
        yb2d[...] = cur[1][...].reshape(tm, D_MODEL)
        mix = gw_ref[:, 0:1] * ya2d[...] + gw_ref[:, 1:2] * yb2d[...]
        out = x_ref[...] + ada_ref[5:6, :] * mix
        if final:
            y = _rms(out) * gf_ref[...]

            @pl.when(i * tm < N_CTX)
            def _():
                yp_ref[...] = y

            @pl.when(i * tm >= N_CTX)
            def _():
                yl_ref[...] = y
        else:
            o_ref[...] = out

    @pl.when(i % 2 == 0)
    def _():
        run((ya0, yb0, 0), (ya1, yb1, 1))

    @pl.when(i % 2 == 1)
    def _():
        run((ya1, yb1, 1), (ya0, yb0, 0))


def _combine(d1, d2, x, ada, layer, gate_w, ys, final_g=None):
    tm = COMBINE_TM
    final = final_g is not None
    ctx_tiles = N_CTX // tm
    row_buf = pltpu.VMEM((tm, 1, D_MODEL), F32)
    in_specs = [
        pl.BlockSpec((tm, D_MODEL), lambda i, *_: (i, 0)),
        pl.BlockSpec((None, None, 6, D_MODEL),
                     lambda i, *_: (layer, _cond_of_tile(i, tm), 0, 0)),
        pl.BlockSpec((tm, LANES), lambda i, *_: (i, 0)),
    ]
    if final:
        in_specs.append(pl.BlockSpec((1, D_MODEL), lambda i, *_: (0, 0)))
        out_specs = [
            pl.BlockSpec((tm, D_MODEL), lambda i, *_: (jnp.minimum(i, ctx_tiles - 1), 0)),
            pl.BlockSpec((tm, D_MODEL), lambda i, *_: (jnp.maximum(i - ctx_tiles, 0), 0)),
        ]
        out_shape = [jax.ShapeDtypeStruct((N_CTX, D_MODEL), F32),
                     jax.ShapeDtypeStruct((N_LAT, D_MODEL), F32)]
        args = (d1, d2, x, ada, gate_w, final_g, ys)
    else:
        out_specs = pl.BlockSpec((tm, D_MODEL), lambda i, *_: (i, 0))
        out_shape = jax.ShapeDtypeStruct((N_TOK, D_MODEL), F32)
        args = (d1, d2, x, ada, gate_w, ys)
    in_specs.append(pl.BlockSpec(memory_space=pl.ANY))
    return pl.pallas_call(
        functools.partial(_combine_kernel, final=final),
        grid_spec=pltpu.PrefetchScalarGridSpec(
            num_scalar_prefetch=2,
            grid=(N_TOK // tm,),
            in_specs=in_specs,
            out_specs=out_specs,
            scratch_shapes=[row_buf, row_buf, row_buf, row_buf,
                            pltpu.VMEM((tm, D_MODEL), F32), pltpu.VMEM((tm, D_MODEL), F32),
                            pltpu.SemaphoreType.DMA((2,))],
        ),
        out_shape=out_shape,
        compiler_params=_params("arbitrary"),
        name="moe_combine",
    )(*args)


def _moe_layer(x, ada, layer, m, g2, w_router, w_gate, w_up, w_down, final_g=None):
    h_rows, route, gate_w, counts = _route(x, ada, layer, g2, w_router)
    cnt = counts[0, :N_EXPERTS].astype(jnp.int32)
    tiles_per_expert = (cnt + MOE_TG - 1) // MOE_TG
    tile_end = jnp.cumsum(tiles_per_expert)
    row_start = (tile_end - tiles_per_expert) * MOE_TG
    n_tiles = tile_end[-1:]
    t = jnp.minimum(jnp.arange(MOE_MAX_TILES, dtype=jnp.int32), n_tiles - 1)
    tile_expert = jnp.sum((t[:, None] >= tile_end[None, :]).astype(jnp.int32), axis=1)
    experts = jnp.arange(N_EXPERTS, dtype=jnp.int32)
    d1 = jnp.sum(jnp.where(route[0][:, None] == experts, row_start, 0), axis=1) + route[2]
    d2 = jnp.sum(jnp.where(route[1][:, None] == experts, row_start, 0), axis=1) + route[3]
    spare_tile = jnp.minimum(n_tiles + experts, MOE_MAX_TILES - 1)
    fill_start = jnp.concatenate([row_start + cnt, spare_tile * MOE_TG])
    fill_rows = jnp.concatenate([tiles_per_expert * MOE_TG - cnt,
                                 jnp.where(n_tiles + experts < MOE_MAX_TILES, MOE_TG, 0)])
    hs, wg_bf, wu_bf, wd_bf = _dispatch(d1, d2, fill_start, fill_rows, h_rows, m, w_gate, w_up, w_down)
    ys = _experts(tile_expert, n_tiles, hs, wg_bf, wu_bf, wd_bf)
    return _combine(d1, d2, x, ada, layer, gate_w, ys, final_g)


assert DEPTH % 2 == 0


def _rope_tables(dim):
    half = dim // 2
    quarter = half // 2
    t = np.arange(DEC_SEQ)
    pos = np.stack([t // GRID_W, t % GRID_W], axis=1).astype(np.float32)
    freqs = (ROPE_THETA ** (-np.arange(quarter, dtype=np.float32) / quarter)).astype(np.float32)
    ang = pos[:, :, None] * freqs[None, None, :]
    cos = np.cos(ang.astype(np.float64))
    sin = np.sin(ang.astype(np.float64))
    cos_t = np.concatenate([cos, cos], axis=-1).reshape(DEC_SEQ, dim)
    sin_t = np.concatenate([-sin, sin], axis=-1).reshape(DEC_SEQ, dim)
    lane = np.arange(dim)
    swap = np.where(lane % half < quarter, lane + quarter, lane - quarter)
    return cos_t.astype(np.float32), sin_t.astype(np.float32), swap


def kernel(x_prompt, x_sample, cache_mla_ckv, cache_mla_kpe, cache_gqa_k, cache_gqa_v, c, c_ctx, w_ada, b_ada, norm1_g, norm2_g, conv_w_in, conv_k, conv_w_out, mla_w_down, mla_q_norm_g, mla_kv_norm_g, mla_w_uq, mla_w_ukv, mla_w_o, gqa_w_qkv, gqa_q_norm_g, gqa_k_norm_g, gqa_w_o, ffn_w_gate, ffn_w_up, ffn_w_down, moe_w_router, moe_w_gate, moe_w_up, moe_w_down, final_norm_g):
    xs = (x_prompt.reshape(N_CTX, D_MODEL), x_sample.reshape(N_LAT, D_MODEL))
    conv_w_in, conv_w_out = conv_w_in.astype(BF16), conv_w_out.astype(BF16)
    ffn_w_gate, ffn_w_up, ffn_w_down = (w.astype(BF16) for w in (ffn_w_gate, ffn_w_up, ffn_w_down))
    cond = jnp.concatenate(
        [c_ctx[None, :], c, jnp.zeros((N_COND - 1 - DEC_BATCH, D_MODEL), F32)], axis=0)
    ada = _adaln_table(cond, w_ada, b_ada)

    new_ckv, new_kpe, new_k, new_v = [], [], [], []
    for i in range(DEPTH):
        j = i // N_MIXERS
        kind = i % N_MIXERS
        g1 = norm1_g[i][None, :]
        if kind == 0:
            x = _conv_layer(xs if i == 0 else (x,), ada, i, j, g1, conv_w_in, conv_k, conv_w_out)
        elif kind == 1:
            cos64, sin64, swap64 = _rope_tables(MLA_ROPE)
            wd = mla_w_down[j]
            kpe_cols = wd[:, MLA_Q_RANK + MLA_KV_RANK:]
            w_down = jnp.concatenate([wd, kpe_cols[:, swap64]], axis=1).astype(BF16)
            wq = mla_w_uq[j].reshape(MLA_Q_RANK, MLA_HEADS, MLA_QK)
            wq_nope = wq[:, :, :MLA_NOPE].reshape(MLA_Q_RANK, MLA_Q_NOPE_COLS)
            wq_rope = wq[:, :, MLA_NOPE:]
            w_uq = jnp.concatenate(
                [wq_nope, wq_rope.reshape(MLA_Q_RANK, MLA_Q_ROPE_COLS),
                 wq_rope[:, :, swap64].reshape(MLA_Q_RANK, MLA_Q_ROPE_COLS)], axis=1).astype(BF16)
            wkv = mla_w_ukv[j].reshape(MLA_KV_RANK, MLA_HEADS, MLA_NOPE + MLA_V)
            w_ukv = jnp.concatenate(
                [wkv[:, :, :MLA_NOPE].reshape(MLA_KV_RANK, MLA_Q_NOPE_COLS),
                 wkv[:, :, MLA_NOPE:].reshape(MLA_KV_RANK, MLA_HEADS * MLA_V)], axis=1).astype(BF16)
            rope = (jnp.asarray(np.tile(cos64, (1, MLA_HEADS))), jnp.asarray(np.tile(sin64, (1, MLA_HEADS))),
                    jnp.asarray(cos64), jnp.asarray(sin64))
            x, ckv_p, kpe_p = _mla_layer(
                x, ada, i, g1, w_down, mla_q_norm_g[j][None, :], mla_kv_norm_g[j][None, :],
                w_uq, w_ukv, mla_w_o[j].astype(BF16), cache_mla_ckv[:, j], cache_mla_kpe[:, j], rope)
            new_ckv.append(ckv_p.reshape(BATCH, SEQ, MLA_KV_RANK))
            new_kpe.append(kpe_p.reshape(BATCH, SEQ, MLA_ROPE))
        else:
            cos128, sin128, swap128 = _rope_tables(GQA_HEAD_DIM)
            wqkv = gqa_w_qkv[j]
            wq = wqkv[:, :GQA_Q_COLS].reshape(D_MODEL, GQA_HEADS, GQA_HEAD_DIM)
            wk = wqkv[:, GQA_Q_COLS:GQA_Q_COLS + GQA_KV_COLS].reshape(D_MODEL, GQA_KV_HEADS, GQA_HEAD_DIM)
            w_qkv_ctx = wqkv.astype(BF16)
            w_qkv_lat = jnp.concatenate(
                [wqkv, wq[:, :, swap128].reshape(D_MODEL, GQA_Q_COLS),
                 wk[:, :, swap128].reshape(D_MODEL, GQA_KV_COLS)], axis=1).astype(BF16)
            gq = gqa_q_norm_g[j]
            gk = gqa_k_norm_g[j]
            rope = (jnp.asarray(np.tile(cos128, (1, GQA_HEADS))), jnp.asarray(np.tile(sin128, (1, GQA_HEADS))),
                    jnp.asarray(np.tile(cos128, (1, GQA_KV_HEADS))), jnp.asarray(np.tile(sin128, (1, GQA_KV_HEADS))))
            x, k_p, v_p = _gqa_layer(
                x, ada, i, g1, w_qkv_ctx, w_qkv_lat,
                jnp.tile(gq, GQA_HEADS)[None, :], jnp.tile(gk, GQA_KV_HEADS)[None, :],
                jnp.tile(gq[swap128], GQA_HEADS)[None, :], jnp.tile(gk[swap128], GQA_KV_HEADS)[None, :],
                gqa_w_o[j].astype(BF16),
                cache_gqa_k[:, j].reshape(DEC_BATCH, PAST_LEN, GQA_KV_COLS),
                cache_gqa_v[:, j].reshape(DEC_BATCH, PAST_LEN, GQA_KV_COLS), rope)
            new_k.append(k_p.reshape(BATCH, SEQ, GQA_KV_HEADS, GQA_HEAD_DIM))
            new_v.append(v_p.reshape(BATCH, SEQ, GQA_KV_HEADS, GQA_HEAD_DIM))

        m = i // 2
        g2 = norm2_g[i][None, :]
        if i % 2 == 0:
            x = _ffn_layer(x, ada, i, m, g2, ffn_w_gate, ffn_w_up, ffn_w_down)
        else:
            w_router = jnp.pad(moe_w_router[m], ((0, 0), (0, LANES - N_EXPERTS)))
            final_g = final_norm_g[None, :] if i == DEPTH - 1 else None
            x = _moe_layer(x, ada, i, m, g2, w_router, moe_w_gate, moe_w_up, moe_w_down, final_g)

    y_prompt, y_sample = x
    y_prompt = y_prompt.reshape(BATCH, SEQ, D_MODEL)
    y_sample = y_sample.reshape(DEC_BATCH, DEC_SEQ, D_MODEL)
    return (y_prompt, y_sample,
            jnp.stack(new_ckv, axis=1), jnp.stack(new_kpe, axis=1),
            jnp.stack(new_k, axis=1), jnp.stack(new_v, axis=1))
```

```python
import functools

import numpy as np
import jax
import jax.numpy as jnp
from jax import lax
from jax.experimental import pallas as pl
from jax.experimental.pallas import tpu as pltpu

D_MODEL = 1024
BATCH = 32
SEQ = 256
DEPTH = 4
DEC_BATCH = 2
DEC_SEQ = 1024
PAST_LEN = 512
GRID_W = 64
N_MIXERS = 3
CONV_WIDTH = 3
MLA_HEADS = 8
MLA_NOPE = 128
MLA_ROPE = 64
MLA_V = 128
MLA_Q_RANK = 384
MLA_KV_RANK = 256
GQA_HEADS = 8
GQA_KV_HEADS = 2
GQA_HEAD_DIM = 128
D_FF = 2816
N_EXPERTS = 8
MOE_FF = 2048
ROPE_THETA = 10000.0
EPS = 1e-6

N_CTX = BATCH * SEQ
N_LAT = DEC_BATCH * DEC_SEQ
N_TOK = N_CTX + N_LAT
N_COND = 8
LANES = 128
Q_TILE = 256
VMEM_LIMIT = 56 * 1024 * 1024

F32 = jnp.float32
BF16 = jnp.bfloat16


def _dot(a, b):
    return jnp.dot(a, b, preferred_element_type=F32)


def _dot_t(a, b):
    return lax.dot_general(a, b, (((1,), (1,)), ((), ())), preferred_element_type=F32)


def _rms(x):
    return x * lax.rsqrt(jnp.mean(x * x, axis=-1, keepdims=True) + EPS)


def _modulate(x, g, shift, scale):
    return _rms(x) * g * (1.0 + scale) + shift


def _silu(x):
    return x * jax.nn.sigmoid(x)


def _cond_of_tile(i, tm):
    start = i * tm
    return jnp.where(start < N_CTX, 0, 1 + (start - N_CTX) // DEC_SEQ)


def _resident(shape):
    return pl.BlockSpec(shape, lambda *_: (0,) * len(shape), pipeline_mode=pl.Buffered(1))


def _resident_at(index, shape):
    return pl.BlockSpec((None,) + tuple(shape), lambda *_: (index,) + (0,) * len(shape),
                        pipeline_mode=pl.Buffered(1))


def _params(*sem):
    return pltpu.CompilerParams(dimension_semantics=sem, vmem_limit_bytes=VMEM_LIMIT)


def _adaln_kernel(cond_ref, w_ref, b_ref, o_ref):
    s = _silu(cond_ref[...]).astype(BF16)
    o_ref[...] = _dot(s, w_ref[...].astype(BF16)) + b_ref[...]


def _adaln_table(cond, w_ada, b_ada):
    tn = 1536
    out = pl.pallas_call(
        _adaln_kernel,
        grid=(DEPTH, 6 * D_MODEL // tn),
        in_specs=[
            pl.BlockSpec((N_COND, D_MODEL), lambda l, j: (0, 0)),
            pl.BlockSpec((None, D_MODEL, tn), lambda l, j: (l, 0, j)),
            pl.BlockSpec((None, 1, tn), lambda l, j: (l, 0, j)),
        ],
        out_specs=pl.BlockSpec((None, N_COND, tn), lambda l, j: (l, 0, j)),
        out_shape=jax.ShapeDtypeStruct((DEPTH, N_COND, 6 * D_MODEL), F32),
        compiler_params=_params("arbitrary", "arbitrary"),
        name="adaln_table",
    )(cond, w_ada, b_ada.reshape(DEPTH, 1, 6 * D_MODEL))
    return out.reshape(DEPTH, N_COND, 6, D_MODEL)


def _ada_spec(layer, tm):
    return pl.BlockSpec((None, None, 6, D_MODEL), lambda i: (layer, _cond_of_tile(i, tm), 0, 0))


def _ada_spec_at(layer, cond_fn):
    return pl.BlockSpec((None, None, 6, D_MODEL), lambda i: (layer, cond_fn(i), 0, 0))


CONV_TM = 1024
CONV_CC = 256


def _conv_kernel(*refs, split):
    if split:
        xp_ref, x_ref, ada_ref, g1_ref, win_ref, ck_ref, wout_ref, o_ref, v_scr = refs
    else:
        x_ref, ada_ref, g1_ref, win_ref, ck_ref, wout_ref, o_ref, v_scr = refs
    tm = x_ref.shape[0]
    i = pl.program_id(0)
    x = x_ref[...]
    if split:
        x = jnp.where(i * tm < N_CTX, xp_ref[...], x)
    h = _modulate(x, g1_ref[...], ada_ref[0:1, :], ada_ref[1:2, :]).astype(BF16)
    period = jnp.where(i * tm < N_CTX, SEQ, DEC_SEQ)
    pos = lax.broadcasted_iota(jnp.int32, (tm, 1), 0) & (period - 1)
    first = pos == 0
    last = pos == period - 1
    for j in range(D_MODEL // CONV_CC):
        lo = j * CONV_CC
        b_gate = _dot(h, win_ref[:, lo:lo + CONV_CC])
        c_gate = _dot(h, win_ref[:, D_MODEL + lo:D_MODEL + lo + CONV_CC])
        x_in = _dot(h, win_ref[:, 2 * D_MODEL + lo:2 * D_MODEL + lo + CONV_CC])
        u = c_gate * x_in
        u_prev = jnp.where(first, 0.0, pltpu.roll(u, 1, 0))
        u_next = jnp.where(last, 0.0, pltpu.roll(u, tm - 1, 0))
        conv = (ck_ref[0:1, lo:lo + CONV_CC] * u_prev + ck_ref[1:2, lo:lo + CONV_CC] * u
                + ck_ref[2:3, lo:lo + CONV_CC] * u_next)
        v_scr[:, lo:lo + CONV_CC] = (b_gate * conv).astype(BF16)
    o_ref[...] = x + ada_ref[2:3, :] * _dot(v_scr[...], wout_ref[...])


def _conv_layer(xs, ada, layer, j, g1, w_in, conv_k, w_out):
    tm = CONV_TM
    ctx_tiles = N_CTX // tm
    if len(xs) == 1:
        x_specs = [pl.BlockSpec((tm, D_MODEL), lambda i: (i, 0))]
    else:
        x_specs = [pl.BlockSpec((tm, D_MODEL), lambda i: (jnp.minimum(i, ctx_tiles - 1), 0)),
                   pl.BlockSpec((tm, D_MODEL), lambda i: (jnp.maximum(i - ctx_tiles, 0), 0))]
    return pl.pallas_call(
        functools.partial(_conv_kernel, split=len(xs) == 2),
        grid=(N_TOK // tm,),
        in_specs=x_specs + [
            _ada_spec(layer, tm),
            _resident((1, D_MODEL)),
            _resident_at(j, (D_MODEL, 3 * D_MODEL)),
            _resident_at(j, (CONV_WIDTH, D_MODEL)),
            _resident_at(j, (D_MODEL, D_MODEL)),
        ],
        out_specs=pl.BlockSpec((tm, D_MODEL), lambda i: (i, 0)),
        out_shape=jax.ShapeDtypeStruct((N_TOK, D_MODEL), F32),
        scratch_shapes=[pltpu.VMEM((tm, D_MODEL), BF16)],
        compiler_params=_params("arbitrary"),
        name=f"conv_mixer_{layer}",
    )(*xs, ada, g1, w_in, conv_k, w_out)


FFN_TM = 512
FFN_CHUNKS = ((0, 1536), (1536, D_FF))


def _ffn_kernel(x_ref, ada_ref, g2_ref, wg_ref, wu_ref, wd_ref, o_ref):
    x = x_ref[...]
    h = _modulate(x, g2_ref[...], ada_ref[3:4, :], ada_ref[4:5, :]).astype(BF16)
    f = None
    for lo, hi in FFN_CHUNKS:
        g = _dot(h, wg_ref[:, lo:hi])
        u = _dot(h, wu_ref[:, lo:hi])
        part = _dot((_silu(g) * u).astype(BF16), wd_ref[lo:hi, :])
        f = part if f is None else f + part
    o_ref[...] = x + ada_ref[5:6, :] * f


def _ffn_layer(x, ada, layer, m, g2, w_gate, w_up, w_down):
    tm = FFN_TM
    return pl.pallas_call(
        _ffn_kernel,
        grid=(N_TOK // tm,),
        in_specs=[
            pl.BlockSpec((tm, D_MODEL), lambda i: (i, 0)),
            _ada_spec(layer, tm),
            _resident((1, D_MODEL)),
            _resident_at(m, (D_MODEL, D_FF)),
            _resident_at(m, (D_MODEL, D_FF)),
            _resident_at(m, (D_FF, D_MODEL)),
        ],
        out_specs=pl.BlockSpec((tm, D_MODEL), lambda i: (i, 0)),
        out_shape=jax.ShapeDtypeStruct((N_TOK, D_MODEL), F32),
        compiler_params=_params("arbitrary"),
        name=f"dense_ffn_{layer}",
    )(x, ada, g2, w_gate, w_up, w_down)


def _attention(qk_parts, v_ref, heads, kv_of_head, dv, scale, o_scr, finish, keys_of, stage):
    def scores(rows, keys, h):
        s = None
        for q_ref, k_ref, width, k_shared in qk_parts:
            kb = 0 if k_shared else kv_of_head(h)
            part = _dot_t(q_ref[rows, h * width:(h + 1) * width],
                          k_ref[keys, kb * width:(kb + 1) * width])
            s = part if s is None else s + part
        return s * scale

    def q_block(rows):
        s_scr, p_scr = stage
        group = s_scr.shape[0]
        keys = keys_of(rows)
        for h0 in range(0, heads, group):
            for j in range(group):
                s_scr[j] = scores(rows, keys, h0 + j)
            for j in range(group):
                s = s_scr[j]
                e = jnp.exp(s - jnp.max(s, axis=-1, keepdims=True))
                p_scr[j] = (e * (1.0 / jnp.sum(e, axis=-1, keepdims=True))).astype(BF16)
            for j in range(group):
                h = h0 + j
                g = kv_of_head(h)
                o = _dot(p_scr[j], v_ref[keys, g * dv:(g + 1) * dv])
                o_scr[rows, h * dv:(h + 1) * dv] = o.astype(BF16)
        finish(rows)

    _for_row_blocks(o_scr.shape[0], q_block)


def _stage_scratch(group, keys):
    return [pltpu.VMEM((group, Q_TILE, keys), F32), pltpu.VMEM((group, Q_TILE, keys), BF16)]


LATENT_KEYS = DEC_SEQ + PAST_LEN
LATENT_STAGE_HEADS = 4


def _keys_of(latent):
    assert SEQ == Q_TILE
    return (lambda rows: slice(None)) if latent else (lambda rows: rows)


def _for_row_blocks(n_rows, fn):
    if n_rows <= 2 * Q_TILE:
        for b in range(n_rows // Q_TILE):
            fn(pl.ds(b * Q_TILE, Q_TILE))
    else:
        def body(b, carry):
            fn(pl.ds(pl.multiple_of(b * Q_TILE, Q_TILE), Q_TILE))
            return carry
        lax.fori_loop(0, n_rows // Q_TILE, body, 0)


MLA_QK = MLA_NOPE + MLA_ROPE
MLA_DOWN_COLS = MLA_Q_RANK + MLA_KV_RANK + 2 * MLA_ROPE
MLA_Q_NOPE_COLS = MLA_HEADS * MLA_NOPE
MLA_Q_ROPE_COLS = MLA_HEADS * MLA_ROPE


def _ctx_then_latent(body, n_in, refs):
    ins, lat_ref, rest = refs[:n_in], refs[n_in], refs[n_in + 1:]
    o_ref = rest[0]
    i = pl.program_id(0)

    @pl.when(i < CTX_TILES)
    def _():
        body(*ins, *rest, latent=False)

    @pl.when(i >= CTX_TILES)
    def _():
        o_ref[...] = lat_ref[...]


def _ctx_specs(out_tails):
    def tile(n_trailing):
        return lambda i: (jnp.minimum(i, CTX_TILES - 1),) + (0,) * n_trailing
    x_spec = pl.BlockSpec((CTX_TM, D_MODEL), tile(1))
    lat_spec = pl.BlockSpec((CTX_TM, D_MODEL), lambda i: (jnp.maximum(i - CTX_TILES, 0), 0))
    out_specs = [pl.BlockSpec((CTX_TM, D_MODEL), lambda i: (i, 0))]
    out_specs += [pl.BlockSpec((CTX_TM,) + tail, tile(len(tail))) for tail in out_tails]
    return x_spec, lat_spec, out_specs


CTX_TM = 2 * SEQ
CTX_TILES = N_CTX // CTX_TM
CTX_STEPS = CTX_TILES + N_LAT // CTX_TM


def _mla_kernel(*refs, latent):
    if latent:
        _mla_body(*refs, latent=True)
    else:
        _ctx_then_latent(_mla_body, 9, refs)


def _mla_body(*refs, latent):
    if latent:
        (x_ref, ada_ref, g1_ref, wdown_ref, gq_ref, gkv_ref, wuq_ref, wukv_ref, wo_ref,
         cosq_ref, sinq_ref, cosk_ref, sink_ref, cckv_ref, ckpe_ref,
         o_ref, qn_scr, qr_scr, kn_scr, kr_scr, v_scr, o_scr, *stage) = refs
    else:
        (x_ref, ada_ref, g1_ref, wdown_ref, gq_ref, gkv_ref, wuq_ref, wukv_ref, wo_ref,
         o_ref, ckv_out_ref, kpe_out_ref, qn_scr, qr_scr, kn_scr, kr_scr, v_scr, o_scr,
         *stage) = refs
    tm = x_ref.shape[0]
    kpe_lo = MLA_Q_RANK + MLA_KV_RANK

    def project(rows):
        h = _modulate(x_ref[rows, :], g1_ref[...], ada_ref[0:1, :], ada_ref[1:2, :]).astype(BF16)
        down = _dot(h, wdown_ref[...])
        c_q = down[:, :MLA_Q_RANK]
        c_kv = down[:, MLA_Q_RANK:kpe_lo]
        kpe = down[:, kpe_lo:kpe_lo + MLA_ROPE]
        q = _dot((_rms(c_q) * gq_ref[...]).astype(BF16), wuq_ref[...])
        ckv_n = _rms(c_kv) * gkv_ref[...]
        kv = _dot(ckv_n.astype(BF16), wukv_ref[...])
        qn_scr[rows, :] = q[:, :MLA_Q_NOPE_COLS].astype(BF16)
        q_rope = q[:, MLA_Q_NOPE_COLS:MLA_Q_NOPE_COLS + MLA_Q_ROPE_COLS]
        kn_scr[rows, :] = kv[:, :MLA_Q_NOPE_COLS].astype(BF16)
        v_scr[rows, :] = kv[:, MLA_Q_NOPE_COLS:].astype(BF16)
        if latent:
            q_swap = q[:, MLA_Q_NOPE_COLS + MLA_Q_ROPE_COLS:]
            q_rope = q_rope * cosq_ref[rows, :] + q_swap * sinq_ref[rows, :]
            kpe_swap = down[:, kpe_lo + MLA_ROPE:kpe_lo + 2 * MLA_ROPE]
            kpe = kpe * cosk_ref[rows, :] + kpe_swap * sink_ref[rows, :]
        else:
            ckv_out_ref[rows, :] = ckv_n
            kpe_out_ref[rows, :] = kpe
        qr_scr[rows, :] = q_rope.astype(BF16)
        kr_scr[rows, :] = kpe.astype(BF16)

    _for_row_blocks(tm, project)

    if latent:
        def expand_cache(rows):
            kv_c = _dot(cckv_ref[rows, :].astype(BF16), wukv_ref[...])
            dst = pl.ds(tm + rows.start, Q_TILE)
            kn_scr[dst, :] = kv_c[:, :MLA_Q_NOPE_COLS].astype(BF16)
            v_scr[dst, :] = kv_c[:, MLA_Q_NOPE_COLS:].astype(BF16)
            kr_scr[dst, :] = ckpe_ref[rows, :].astype(BF16)

        _for_row_blocks(PAST_LEN, expand_cache)

    def finish(rows):
        o_ref[rows, :] = x_ref[rows, :] + ada_ref[2:3, :] * _dot(o_scr[rows, :], wo_ref[...])

    _attention([(qn_scr, kn_scr, MLA_NOPE, False), (qr_scr, kr_scr, MLA_ROPE, True)],
               v_scr, MLA_HEADS, lambda hd: hd, MLA_V, MLA_QK ** -0.5, o_scr, finish,
               _keys_of(latent), stage)


def _mla_scratch(tm, sk):
    return [
        pltpu.VMEM((tm, MLA_Q_NOPE_COLS), BF16),
        pltpu.VMEM((tm, MLA_Q_ROPE_COLS), BF16),
        pltpu.VMEM((sk, MLA_Q_NOPE_COLS), BF16),
        pltpu.VMEM((sk, MLA_ROPE), BF16),
        pltpu.VMEM((sk, MLA_HEADS * MLA_V), BF16),
        pltpu.VMEM((tm, MLA_HEADS * MLA_V), BF16),
    ]


def _mla_layer(x, ada, layer, g1, w_down, g_q, g_kv, w_uq, w_ukv, w_o,
               cache_ckv, cache_kpe, rope):
    n_uq = w_uq.shape[1]
    weights = [
        _resident((1, D_MODEL)),
        _resident((D_MODEL, MLA_DOWN_COLS)),
        _resident((1, MLA_Q_RANK)),
        _resident((1, MLA_KV_RANK)),
        _resident((MLA_Q_RANK, n_uq)),
        _resident((MLA_KV_RANK, 2 * MLA_Q_NOPE_COLS)),
        _resident((MLA_HEADS * MLA_V, D_MODEL)),
    ]
    ctx_weights = list(weights)
    ctx_weights[4] = pl.BlockSpec((MLA_Q_RANK, MLA_Q_NOPE_COLS + MLA_Q_ROPE_COLS), lambda *_: (0, 0),
                                  pipeline_mode=pl.Buffered(1))
    weight_args = (g1, w_down, g_q, g_kv, w_uq, w_ukv, w_o)
    cos_q, sin_q, cos_k, sin_k = rope
    ctx_blocks = N_CTX // DEC_SEQ
    lat_out = pl.pallas_call(
        functools.partial(_mla_kernel, latent=True),
        grid=(DEC_BATCH,),
        in_specs=[pl.BlockSpec((DEC_SEQ, D_MODEL), lambda b: (ctx_blocks + b, 0)),
                  _ada_spec_at(layer, lambda b: 1 + b)] + weights + [
            _resident((DEC_SEQ, MLA_Q_ROPE_COLS)),
            _resident((DEC_SEQ, MLA_Q_ROPE_COLS)),
            _resident((DEC_SEQ, MLA_ROPE)),
            _resident((DEC_SEQ, MLA_ROPE)),
            pl.BlockSpec((None, PAST_LEN, MLA_KV_RANK), lambda b: (b, 0, 0)),
            pl.BlockSpec((None, PAST_LEN, MLA_ROPE), lambda b: (b, 0, 0)),
        ],
        out_specs=pl.BlockSpec((DEC_SEQ, D_MODEL), lambda b: (b, 0)),
        out_shape=jax.ShapeDtypeStruct((N_LAT, D_MODEL), F32),
        scratch_shapes=(_mla_scratch(DEC_SEQ, LATENT_KEYS)
                        + _stage_scratch(LATENT_STAGE_HEADS, LATENT_KEYS)),
        compiler_params=_params("arbitrary"),
        name=f"mla_latent_{layer}",
    )(x, ada, *weight_args, cos_q, sin_q, cos_k, sin_k, cache_ckv, cache_kpe)

    x_spec, lat_spec, out_specs = _ctx_specs(((MLA_KV_RANK,), (MLA_ROPE,)))
    return pl.pallas_call(
        functools.partial(_mla_kernel, latent=False),
        grid=(CTX_STEPS,),
        in_specs=[x_spec, _ada_spec_at(layer, lambda i: 0)] + ctx_weights + [lat_spec],
        out_specs=out_specs,
        out_shape=[
            jax.ShapeDtypeStruct((N_TOK, D_MODEL), F32),
            jax.ShapeDtypeStruct((N_CTX, MLA_KV_RANK), F32),
            jax.ShapeDtypeStruct((N_CTX, MLA_ROPE), F32),
        ],
        scratch_shapes=_mla_scratch(CTX_TM, CTX_TM) + _stage_scratch(MLA_HEADS, SEQ),
        compiler_params=_params("arbitrary"),
        name=f"mla_ctx_{layer}",
    )(x, ada, *weight_args, lat_out)


GQA_Q_COLS = GQA_HEADS * GQA_HEAD_DIM
GQA_KV_COLS = GQA_KV_HEADS * GQA_HEAD_DIM
GQA_REP = GQA_HEADS // GQA_KV_HEADS


def _head_rms(t, n_heads):
    cols = []
    for hd in range(n_heads):
        th = t[:, hd * GQA_HEAD_DIM:(hd + 1) * GQA_HEAD_DIM]
        r = lax.rsqrt(jnp.mean(th * th, axis=-1, keepdims=True) + EPS)
        cols.append(jnp.broadcast_to(r, th.shape))
    return jnp.concatenate(cols, axis=-1)


def _gqa_kernel(*refs, latent):
    if latent:
        _gqa_body(*refs, latent=True)
    else:
        _ctx_then_latent(_gqa_body, 7, refs)


def _gqa_body(*refs, latent):
    if latent:
        (x_ref, ada_ref, g1_ref, wqkv_ref, gq_ref, gk_ref, wo_ref,
         gqs_ref, gks_ref, cosq_ref, sinq_ref, cosk_ref, sink_ref, ck_ref, cv_ref,
         o_ref, q_scr, k_scr, v_scr, o_scr, *stage) = refs
    else:
        (x_ref, ada_ref, g1_ref, wqkv_ref, gq_ref, gk_ref, wo_ref,
         o_ref, k_out_ref, v_out_ref, q_scr, k_scr, v_scr, o_scr, *stage) = refs
    tm = x_ref.shape[0]

    def project(rows):
        h = _modulate(x_ref[rows, :], g1_ref[...], ada_ref[0:1, :], ada_ref[1:2, :]).astype(BF16)
        qkv = _dot(h, wqkv_ref[...])
        q_raw = qkv[:, :GQA_Q_COLS]
        k_raw = qkv[:, GQA_Q_COLS:GQA_Q_COLS + GQA_KV_COLS]
        v = qkv[:, GQA_Q_COLS + GQA_KV_COLS:GQA_Q_COLS + 2 * GQA_KV_COLS]
        q_r = _head_rms(q_raw, GQA_HEADS)
        k_r = _head_rms(k_raw, GQA_KV_HEADS)
        q = q_raw * q_r * gq_ref[...]
        k = k_raw * k_r * gk_ref[...]
        if latent:
            lo = GQA_Q_COLS + 2 * GQA_KV_COLS
            q_swap = qkv[:, lo:lo + GQA_Q_COLS] * q_r * gqs_ref[...]
            k_swap = qkv[:, lo + GQA_Q_COLS:lo + GQA_Q_COLS + GQA_KV_COLS] * k_r * gks_ref[...]
            q = q * cosq_ref[rows, :] + q_swap * sinq_ref[rows, :]
            k = k * cosk_ref[rows, :] + k_swap * sink_ref[rows, :]
        else:
            for g in range(GQA_KV_HEADS):
                cols = slice(g * GQA_HEAD_DIM, (g + 1) * GQA_HEAD_DIM)
                k_out_ref[rows, g, :] = k[:, cols]
                v_out_ref[rows, g, :] = v[:, cols]
        q_scr[rows, :] = q.astype(BF16)
        k_scr[rows, :] = k.astype(BF16)
        v_scr[rows, :] = v.astype(BF16)

    _for_row_blocks(tm, project)

    if latent:
        k_scr[tm:, :] = ck_ref[...].astype(BF16)
        v_scr[tm:, :] = cv_ref[...].astype(BF16)

    def finish(rows):
        o_ref[rows, :] = x_ref[rows, :] + ada_ref[2:3, :] * _dot(o_scr[rows, :], wo_ref[...])

    _attention([(q_scr, k_scr, GQA_HEAD_DIM, False)], v_scr, GQA_HEADS,
               lambda hd: hd // GQA_REP, GQA_HEAD_DIM, GQA_HEAD_DIM ** -0.5, o_scr, finish,
               _keys_of(latent), stage)


def _gqa_scratch(tm, sk):
    return [
        pltpu.VMEM((tm, GQA_Q_COLS), BF16),
        pltpu.VMEM((sk, GQA_KV_COLS), BF16),
        pltpu.VMEM((sk, GQA_KV_COLS), BF16),
        pltpu.VMEM((tm, GQA_Q_COLS), BF16),
    ]


def _gqa_layer(x, ada, layer, g1, w_qkv_ctx, w_qkv_lat, g_q, g_k, g_q_swap, g_k_swap, w_o,
               cache_k, cache_v, rope):
    cos_q, sin_q, cos_k, sin_k = rope
    ctx_blocks = N_CTX // DEC_SEQ
    lat_out = pl.pallas_call(
        functools.partial(_gqa_kernel, latent=True),
        grid=(DEC_BATCH,),
        in_specs=[pl.BlockSpec((DEC_SEQ, D_MODEL), lambda b: (ctx_blocks + b, 0)),
                  _ada_spec_at(layer, lambda b: 1 + b),
                  _resident((1, D_MODEL)),
                  _resident(w_qkv_lat.shape),
                  _resident((1, GQA_Q_COLS)),
                  _resident((1, GQA_KV_COLS)),
                  _resident((GQA_Q_COLS, D_MODEL)),
                  _resident((1, GQA_Q_COLS)),
                  _resident((1, GQA_KV_COLS)),
                  _resident((DEC_SEQ, GQA_Q_COLS)),
                  _resident((DEC_SEQ, GQA_Q_COLS)),
                  _resident((DEC_SEQ, GQA_KV_COLS)),
                  _resident((DEC_SEQ, GQA_KV_COLS)),
                  pl.BlockSpec((None, PAST_LEN, GQA_KV_COLS), lambda b: (b, 0, 0)),
                  pl.BlockSpec((None, PAST_LEN, GQA_KV_COLS), lambda b: (b, 0, 0))],
        out_specs=pl.BlockSpec((DEC_SEQ, D_MODEL), lambda b: (b, 0)),
        out_shape=jax.ShapeDtypeStruct((N_LAT, D_MODEL), F32),
        scratch_shapes=(_gqa_scratch(DEC_SEQ, LATENT_KEYS)
                        + _stage_scratch(LATENT_STAGE_HEADS, LATENT_KEYS)),
        compiler_params=_params("arbitrary"),
        name=f"gqa_latent_{layer}",
    )(x, ada, g1, w_qkv_lat, g_q, g_k, w_o, g_q_swap, g_k_swap,
      cos_q, sin_q, cos_k, sin_k, cache_k, cache_v)

    kv_tail = (GQA_KV_HEADS, GQA_HEAD_DIM)
    x_spec, lat_spec, out_specs = _ctx_specs((kv_tail, kv_tail))
    return pl.pallas_call(
        functools.partial(_gqa_kernel, latent=False),
        grid=(CTX_STEPS,),
        in_specs=[x_spec,
                  _ada_spec_at(layer, lambda i: 0),
                  _resident((1, D_MODEL)),
                  _resident(w_qkv_ctx.shape),
                  _resident((1, GQA_Q_COLS)),
                  _resident((1, GQA_KV_COLS)),
                  _resident((GQA_Q_COLS, D_MODEL)),
                  lat_spec],
        out_specs=out_specs,
        out_shape=[
            jax.ShapeDtypeStruct((N_TOK, D_MODEL), F32),
            jax.ShapeDtypeStruct((N_CTX,) + kv_tail, F32),
            jax.ShapeDtypeStruct((N_CTX,) + kv_tail, F32),
        ],
        scratch_shapes=_gqa_scratch(CTX_TM, CTX_TM) + _stage_scratch(GQA_HEADS, SEQ),
        compiler_params=_params("arbitrary"),
        name=f"gqa_ctx_{layer}",
    )(x, ada, g1, w_qkv_ctx, g_q, g_k, w_o, lat_out)


ROUTE_TM = 512
MOE_TG = 256
MOE_MAX_TILES = 2 * N_TOK // MOE_TG + N_EXPERTS
MOE_ROWS = MOE_MAX_TILES * MOE_TG
MOE_FILLS = 2 * N_EXPERTS
MOE_FC = 1024
COMBINE_TM = 256
ISSUE_UNROLL = 8


def _route_kernel(x_ref, ada_ref, g2_ref, wr_hi_ref, wr_lo_ref, h_ref, dest_ref, gatew_ref,
                  count_ref, carry_scr):
    tm = x_ref.shape[0]

    @pl.when(pl.program_id(0) == 0)
    def _():
        carry_scr[...] = jnp.zeros_like(carry_scr)

    h = _modulate(x_ref[...], g2_ref[...], ada_ref[3:4, :], ada_ref[4:5, :])
    h_ref[...] = h.reshape(tm, 1, D_MODEL)
    h_hi = h.astype(BF16)
    h_lo = (h - h_hi.astype(F32)).astype(BF16)
    logits = _dot(h_hi, wr_hi_ref[...]) + (_dot(h_lo, wr_hi_ref[...]) + _dot(h_hi, wr_lo_ref[...]))
    lane = lax.broadcasted_iota(jnp.int32, logits.shape, 1)
    neg = jnp.float32(-jnp.inf)
    l1 = jnp.where(lane < N_EXPERTS, logits, neg)
    m1 = jnp.max(l1, axis=-1, keepdims=True)
    i1 = jnp.min(jnp.where(l1 == m1, lane, LANES), axis=-1, keepdims=True)
    l2 = jnp.where(lane == i1, neg, l1)
    m2 = jnp.max(l2, axis=-1, keepdims=True)
    i2 = jnp.min(jnp.where(l2 == m2, lane, LANES), axis=-1, keepdims=True)
    e2 = jnp.exp(m2 - m1)
    w1 = 1.0 / (1.0 + e2)
    w2 = e2 / (1.0 + e2)
    member = jnp.where((lane == i1) | (lane == i2), 1.0, 0.0)
    r = lax.broadcasted_iota(jnp.int32, (tm, tm), 0)
    c = lax.broadcasted_iota(jnp.int32, (tm, tm), 1)
    tri = jnp.where(c < r, 1.0, 0.0).astype(BF16)
    before = _dot(tri, member.astype(BF16)) + carry_scr[0:1, :]
    rank1 = jnp.sum(jnp.where(lane == i1, before, 0.0), axis=-1, keepdims=True).astype(jnp.int32)
    rank2 = jnp.sum(jnp.where(lane == i2, before, 0.0), axis=-1, keepdims=True).astype(jnp.int32)
    table = jnp.where(lane == 0, i1, jnp.where(lane == 1, i2, jnp.where(
        lane == 2, rank1, jnp.where(lane == 3, rank2, 0))))
    dest_ref[...] = jnp.transpose(table)[0:8, :]
    gatew_ref[...] = jnp.where(lane == 0, w1, jnp.where(lane == 1, w2, 0.0))
    carry_scr[...] = carry_scr[...] + jnp.sum(member, axis=0, keepdims=True)
    count_ref[...] = carry_scr[...]


def _route(x, ada, layer, g2, w_router):
    tm = ROUTE_TM
    w_router_hi = w_router.astype(BF16)
    return pl.pallas_call(
        _route_kernel,
        grid=(N_TOK // tm,),
        in_specs=[
            pl.BlockSpec((tm, D_MODEL), lambda i: (i, 0)),
            _ada_spec(layer, tm),
            _resident((1, D_MODEL)),
            _resident((D_MODEL, LANES)),
            _resident((D_MODEL, LANES)),
        ],
        out_specs=[
            pl.BlockSpec((tm, 1, D_MODEL), lambda i: (i, 0, 0)),
            pl.BlockSpec((8, tm), lambda i: (0, i)),
            pl.BlockSpec((tm, LANES), lambda i: (i, 0)),
            pl.BlockSpec((8, LANES), lambda i: (0, 0)),
        ],
        out_shape=[
            jax.ShapeDtypeStruct((N_TOK, 1, D_MODEL), F32),
            jax.ShapeDtypeStruct((8, N_TOK), jnp.int32),
            jax.ShapeDtypeStruct((N_TOK, LANES), F32),
            jax.ShapeDtypeStruct((8, LANES), F32),
        ],
        scratch_shapes=[pltpu.VMEM((8, LANES), F32)],
        compiler_params=_params("arbitrary"),
        name=f"moe_route_{layer}",
    )(x, ada, g2, w_router_hi, (w_router - w_router_hi.astype(F32)).astype(BF16))


def _cast_specs(m, parts):
    last = N_EXPERTS * parts - 1

    def src(i, *_):
        c = jnp.minimum(i, last)
        return (m, c // parts, c % parts, 0)

    def dst(i, *_):
        c = jnp.minimum(i, last)
        return (c // parts, c % parts, 0)

    shapes = [(D_MODEL // parts, MOE_FF), (D_MODEL // parts, MOE_FF), (MOE_FF // parts, D_MODEL)]
    in_specs = [pl.BlockSpec((None, None) + s, src) for s in shapes]
    out_specs = [pl.BlockSpec((None,) + s, dst) for s in shapes]
    out_shape = [jax.ShapeDtypeStruct((N_EXPERTS, D_MODEL, MOE_FF), BF16),
                 jax.ShapeDtypeStruct((N_EXPERTS, D_MODEL, MOE_FF), BF16),
                 jax.ShapeDtypeStruct((N_EXPERTS, MOE_FF, D_MODEL), BF16)]
    return in_specs, out_specs, out_shape


def _cast_chunk(parts, srcs, dsts):
    @pl.when(pl.program_id(0) < N_EXPERTS * parts)
    def _():
        for s_ref, d_ref in zip(srcs, dsts):
            d_ref[...] = s_ref[...].astype(BF16)


DISPATCH_STEPS = 2 * N_EXPERTS
DISPATCH_CAST_PARTS = 2
EXPERTS_CAST_PARTS = 4


def _dispatch_kernel(*refs, cast):
    if cast:
        (d1_ref, d2_ref, fill_start_ref, fill_rows_ref, h_ref, wg_ref, wu_ref, wd_ref,
         hs_ref, wg_out_ref, wu_out_ref, wd_out_ref, zero_scr, sem) = refs
    else:
        d1_ref, d2_ref, fill_start_ref, fill_rows_ref, h_ref, hs_ref, zero_scr, sem = refs
    i = pl.program_id(0)
    tm = h_ref.shape[0]
    base = i * tm

    def fill_copy(k):
        n = fill_rows_ref[k]
        return pltpu.make_async_copy(zero_scr.at[pl.ds(0, n)],
                                     hs_ref.at[pl.ds(fill_start_ref[k], n)], sem.at[1])

    @pl.when(i == 0)
    def _():
        zero_scr[...] = jnp.zeros_like(zero_scr)
        for k in range(MOE_FILLS):
            @pl.when(fill_rows_ref[k] > 0)
            def _():
                fill_copy(k).start()
        for k in range(MOE_FILLS):
            @pl.when(fill_rows_ref[k] > 0)
            def _():
                fill_copy(k).wait()

    def row_copy(r, d_ref):
        return pltpu.make_async_copy(h_ref.at[r], hs_ref.at[d_ref[base + r]], sem.at[0])

    def issue(g, carry):
        for k in range(ISSUE_UNROLL):
            r = g * ISSUE_UNROLL + k
            row_copy(r, d1_ref).start(priority=0)
            row_copy(r, d2_ref).start(priority=1)
        return carry

    def drain():
        for _ in range(2):
            pltpu.make_async_copy(h_ref, hs_ref.at[pl.ds(0, tm)], sem.at[0]).wait()

    lax.fori_loop(0, tm // ISSUE_UNROLL, issue, 0)
    if cast:
        _cast_chunk(DISPATCH_CAST_PARTS, (wg_ref, wu_ref, wd_ref),
                    (wg_out_ref, wu_out_ref, wd_out_ref))
    drain()


def _dispatch(d1, d2, fill_start, fill_rows, h_rows, cast=None):
    tm = N_TOK // DISPATCH_STEPS
    in_specs = [pl.BlockSpec((tm, 1, D_MODEL), lambda i, *_: (i, 0, 0))]
    out_specs = [pl.BlockSpec(memory_space=pl.ANY)]
    out_shape = [jax.ShapeDtypeStruct((MOE_ROWS, 1, D_MODEL), F32)]
    weights = ()
    if cast is not None:
        assert DISPATCH_STEPS == N_EXPERTS * DISPATCH_CAST_PARTS
        cast_in, cast_out, cast_shape = _cast_specs(cast[0], DISPATCH_CAST_PARTS)
        in_specs += cast_in
        out_specs += cast_out
        out_shape += cast_shape
        weights = cast[1:]
    out = pl.pallas_call(
        functools.partial(_dispatch_kernel, cast=cast is not None),
        grid_spec=pltpu.PrefetchScalarGridSpec(
            num_scalar_prefetch=4,
            grid=(DISPATCH_STEPS,),
            in_specs=in_specs,
            out_specs=out_specs,
            scratch_shapes=[pltpu.VMEM((MOE_TG, 1, D_MODEL), F32),
                            pltpu.SemaphoreType.DMA((2,))],
        ),
        out_shape=out_shape,
        compiler_params=_params("arbitrary"),
        name="moe_dispatch",
    )(d1, d2, fill_start, fill_rows, h_rows, *weights)
    return out if cast is not None else out[0]


def _experts_kernel(*refs, cast):
    if cast:
        (te_ref, nt_ref, hs_ref, wg_ref, wu_ref, wd_ref, ng_ref, nu_ref, nd_ref,
         ys_ref, ng_out_ref, nu_out_ref, nd_out_ref, h2d_scr) = refs
        _cast_chunk(EXPERTS_CAST_PARTS, (ng_ref, nu_ref, nd_ref),
                    (ng_out_ref, nu_out_ref, nd_out_ref))
    else:
        te_ref, nt_ref, hs_ref, wg_ref, wu_ref, wd_ref, ys_ref, h2d_scr = refs
    used = pl.program_id(0) < nt_ref[0]

    @pl.when(jnp.logical_not(used))
    def _():
        ys_ref[...] = jnp.zeros_like(ys_ref)

    @pl.when(used)
    def _():
        h2d_scr[...] = hs_ref[...].reshape(MOE_TG, D_MODEL)
        h = h2d_scr[...].astype(BF16)
        y = None
        for lo in range(0, MOE_FF, MOE_FC):
            g = _dot(h, wg_ref[:, lo:lo + MOE_FC])
            u = _dot(h, wu_ref[:, lo:lo + MOE_FC])
            part = _dot((_silu(g) * u).astype(BF16), wd_ref[lo:lo + MOE_FC, :])
            y = part if y is None else y + part
        ys_ref[...] = y.reshape(MOE_TG, 1, D_MODEL)


def _experts(tile_expert, n_tiles, hs, w_gate, w_up, w_down, cast=None):
    rows = pl.BlockSpec((MOE_TG, 1, D_MODEL), lambda t, te, nt: (t, 0, 0))
    in_specs = [
        rows,
        pl.BlockSpec((None, D_MODEL, MOE_FF), lambda t, te, nt: (te[t], 0, 0)),
        pl.BlockSpec((None, D_MODEL, MOE_FF), lambda t, te, nt: (te[t], 0, 0)),
        pl.BlockSpec((None, MOE_FF, D_MODEL), lambda t, te, nt: (te[t], 0, 0)),
    ]
    out_specs = [rows]
    out_shape = [jax.ShapeDtypeStruct((MOE_ROWS, 1, D_MODEL), F32)]
    next_weights = ()
    if cast is not None:
        assert MOE_MAX_TILES >= N_EXPERTS * EXPERTS_CAST_PARTS
        cast_in, cast_out, cast_shape = _cast_specs(cast[0], EXPERTS_CAST_PARTS)
        in_specs += cast_in
        out_specs += cast_out
        out_shape += cast_shape
        next_weights = cast[1:]
    out = pl.pallas_call(
        functools.partial(_experts_kernel, cast=cast is not None),
        grid_spec=pltpu.PrefetchScalarGridSpec(
            num_scalar_prefetch=2,
            grid=(MOE_MAX_TILES,),
            in_specs=in_specs,
            out_specs=out_specs,
            scratch_shapes=[pltpu.VMEM((MOE_TG, D_MODEL), F32)],
        ),
        out_shape=out_shape,
        compiler_params=_params("arbitrary"),
        name="moe_experts",
    )(tile_expert, n_tiles, hs, w_gate, w_up, w_down, *next_weights)
    return out if cast is not None else out[0]


def _combine_kernel(*refs, final):
    if final:
        (d1_ref, d2_ref, x_ref, ada_ref, gw_ref, gf_ref, ys_ref, yp_ref, yl_ref,
         ya0, yb0, ya1, yb1, ya2d, yb2d, sem) = refs
    else:
        (d1_ref, d2_ref, x_ref, ada_ref, gw_ref, ys_ref, o_ref,
         ya0, yb0, ya1, yb1, ya2d, yb2d, sem) = refs
    i = pl.program_id(0)
    tm = x_ref.shape[0]

    def row_copies(step, r, buf_a, buf_b, slot):
        t = step * tm + r
        return (pltpu.make_async_copy(ys_ref.at[d1_ref[t]], buf_a.at[r], sem.at[slot]),
                pltpu.make_async_copy(ys_ref.at[d2_ref[t]], buf_b.at[r], sem.at[slot]))

    def start_tile(step, buf_a, buf_b, slot):
        def body(g, carry):
            for k in range(ISSUE_UNROLL):
                copies = row_copies(step, g * ISSUE_UNROLL + k, buf_a, buf_b, slot)
                for priority, cp in enumerate(copies):
                    cp.start(priority=priority)
            return carry
        lax.fori_loop(0, tm // ISSUE_UNROLL, body, 0)

    def wait_tile(buf_a, buf_b, slot):
        for buf in (buf_a, buf_b):
            pltpu.make_async_copy(ys_ref.at[pl.ds(0, tm)], buf, sem.at[slot]).wait()

    @pl.when(i == 0)
    def _():
        start_tile(0, ya0, yb0, 0)

    def run(cur, nxt):
        @pl.when(i + 1 < pl.num_programs(0))
        def _():
            start_tile(i + 1, *nxt)

        wait_tile(*cur)
        ya2d[...] = cur[0][...].reshape(tm, D_MODEL)
        yb2d[...] = cur[1][...].reshape(tm, D_MODEL)
        mix = gw_ref[:, 0:1] * ya2d[...] + gw_ref[:, 1:2] * yb2d[...]
        out = x_ref[...] + ada_ref[5:6, :] * mix
        if final:
            y = _rms(out) * gf_ref[...]

            @pl.when(i * tm < N_CTX)
            def _():
                yp_ref[...] = y

            @pl.when(i * tm >= N_CTX)
            def _():
                yl_ref[...] = y
        else:
            o_ref[...] = out

    @pl.when(i % 2 == 0)
    def _():
        run((ya0, yb0, 0), (ya1, yb1, 1))

    @pl.when(i % 2 == 1)
    def _():
        run((ya1, yb1, 1), (ya0, yb0, 0))


def _combine(d1, d2, x, ada, layer, gate_w, ys, final_g=None):
    tm = COMBINE_TM
    final = final_g is not None
    ctx_tiles = N_CTX // tm
    row_buf = pltpu.VMEM((tm, 1, D_MODEL), F32)
    in_specs = [
        pl.BlockSpec((tm, D_MODEL), lambda i, *_: (i, 0)),
        pl.BlockSpec((None, None, 6, D_MODEL),
                     lambda i, *_: (layer, _cond_of_tile(i, tm), 0, 0)),
        pl.BlockSpec((tm, LANES), lambda i, *_: (i, 0)),
    ]
    if final:
        in_specs.append(pl.BlockSpec((1, D_MODEL), lambda i, *_: (0, 0)))
        out_specs = [
            pl.BlockSpec((tm, D_MODEL), lambda i, *_: (jnp.minimum(i, ctx_tiles - 1), 0)),
            pl.BlockSpec((tm, D_MODEL), lambda i, *_: (jnp.maximum(i - ctx_tiles, 0), 0)),
        ]
        out_shape = [jax.ShapeDtypeStruct((N_CTX, D_MODEL), F32),
                     jax.ShapeDtypeStruct((N_LAT, D_MODEL), F32)]
        args = (d1, d2, x, ada, gate_w, final_g, ys)
    else:
        out_specs = pl.BlockSpec((tm, D_MODEL), lambda i, *_: (i, 0))
        out_shape = jax.ShapeDtypeStruct((N_TOK, D_MODEL), F32)
        args = (d1, d2, x, ada, gate_w, ys)
    in_specs.append(pl.BlockSpec(memory_space=pl.ANY))
    return pl.pallas_call(
        functools.partial(_combine_kernel, final=final),
        grid_spec=pltpu.PrefetchScalarGridSpec(
            num_scalar_prefetch=2,
            grid=(N_TOK // tm,),
            in_specs=in_specs,
            out_specs=out_specs,
            scratch_shapes=[row_buf, row_buf, row_buf, row_buf,
                            pltpu.VMEM((tm, D_MODEL), F32), pltpu.VMEM((tm, D_MODEL), F32),
                            pltpu.SemaphoreType.DMA((2,))],
        ),
        out_shape=out_shape,
        compiler_params=_params("arbitrary"),
        name="moe_combine",
    )(*args)


def _moe_layer(x, ada, layer, m, g2, w_router, w_gate, w_up, w_down, weights_bf, final_g):
    h_rows, route, gate_w, counts = _route(x, ada, layer, g2, w_router)
    cnt = counts[0, :N_EXPERTS].astype(jnp.int32)
    tiles_per_expert = (cnt + MOE_TG - 1) // MOE_TG
    tile_end = jnp.cumsum(tiles_per_expert)
    row_start = (tile_end - tiles_per_expert) * MOE_TG
    n_tiles = tile_end[-1:]
    t = jnp.minimum(jnp.arange(MOE_MAX_TILES, dtype=jnp.int32), n_tiles - 1)
    tile_expert = jnp.sum((t[:, None] >= tile_end[None, :]).astype(jnp.int32), axis=1)
    experts = jnp.arange(N_EXPERTS, dtype=jnp.int32)
    d1 = jnp.sum(jnp.where(route[0][:, None] == experts, row_start, 0), axis=1) + route[2]
    d2 = jnp.sum(jnp.where(route[1][:, None] == experts, row_start, 0), axis=1) + route[3]
    spare_tile = jnp.minimum(n_tiles + experts, MOE_MAX_TILES - 1)
    fill_start = jnp.concatenate([row_start + cnt, spare_tile * MOE_TG])
    fill_rows = jnp.concatenate([tiles_per_expert * MOE_TG - cnt,
                                 jnp.where(n_tiles + experts < MOE_MAX_TILES, MOE_TG, 0)])
    f32_weights = (w_gate, w_up, w_down)
    if weights_bf is None:
        hs, *weights_bf = _dispatch(d1, d2, fill_start, fill_rows, h_rows, (m,) + f32_weights)
    else:
        hs = _dispatch(d1, d2, fill_start, fill_rows, h_rows)
    next_bf = None
    if m + 1 < w_gate.shape[0]:
        ys, *next_bf = _experts(tile_expert, n_tiles, hs, *weights_bf, (m + 1,) + f32_weights)
    else:
        ys = _experts(tile_expert, n_tiles, hs, *weights_bf)
    return _combine(d1, d2, x, ada, layer, gate_w, ys, final_g), next_bf


assert DEPTH % 2 == 0


def _rope_tables(dim):
    half = dim // 2
    quarter = half // 2
    t = np.arange(DEC_SEQ)
    pos = np.stack([t // GRID_W, t % GRID_W], axis=1).astype(np.float32)
    freqs = (ROPE_THETA ** (-np.arange(quarter, dtype=np.float32) / quarter)).astype(np.float32)
    ang = pos[:, :, None] * freqs[None, None, :]
    cos = np.cos(ang.astype(np.float64))
    sin = np.sin(ang.astype(np.float64))
    cos_t = np.concatenate([cos, cos], axis=-1).reshape(DEC_SEQ, dim)
    sin_t = np.concatenate([-sin, sin], axis=-1).reshape(DEC_SEQ, dim)
    lane = np.arange(dim)
    swap = np.where(lane % half < quarter, lane + quarter, lane - quarter)
    return cos_t.astype(np.float32), sin_t.astype(np.float32), swap


def kernel(x_prompt, x_sample, cache_mla_ckv, cache_mla_kpe, cache_gqa_k, cache_gqa_v, c, c_ctx, w_ada, b_ada, norm1_g, norm2_g, conv_w_in, conv_k, conv_w_out, mla_w_down, mla_q_norm_g, mla_kv_norm_g, mla_w_uq, mla_w_ukv, mla_w_o, gqa_w_qkv, gqa_q_norm_g, gqa_k_norm_g, gqa_w_o, ffn_w_gate, ffn_w_up, ffn_w_down, moe_w_router, moe_w_gate, moe_w_up, moe_w_down, final_norm_g):
    xs = (x_prompt.reshape(N_CTX, D_MODEL), x_sample.reshape(N_LAT, D_MODEL))
    conv_w_in, conv_w_out = conv_w_in.astype(BF16), conv_w_out.astype(BF16)
    ffn_w_gate, ffn_w_up, ffn_w_down = (w.astype(BF16) for w in (ffn_w_gate, ffn_w_up, ffn_w_down))
    cond = jnp.concatenate(
        [c_ctx[None, :], c, jnp.zeros((N_COND - 1 - DEC_BATCH, D_MODEL), F32)], axis=0)
    ada = _adaln_table(cond, w_ada, b_ada)

    new_ckv, new_kpe, new_k, new_v = [], [], [], []
    moe_bf = None
    for i in range(DEPTH):
        j = i // N_MIXERS
        kind = i % N_MIXERS
        g1 = norm1_g[i][None, :]
        if kind == 0:
            x = _conv_layer(xs if i == 0 else (x,), ada, i, j, g1, conv_w_in, conv_k, conv_w_out)
        elif kind == 1:
            cos64, sin64, swap64 = _rope_tables(MLA_ROPE)
            wd = mla_w_down[j]
            kpe_cols = wd[:, MLA_Q_RANK + MLA_KV_RANK:]
            w_down = jnp.concatenate([wd, kpe_cols[:, swap64]], axis=1).astype(BF16)
            wq = mla_w_uq[j].reshape(MLA_Q_RANK, MLA_HEADS, MLA_QK)
            wq_nope = wq[:, :, :MLA_NOPE].reshape(MLA_Q_RANK, MLA_Q_NOPE_COLS)
            wq_rope = wq[:, :, MLA_NOPE:]
            w_uq = jnp.concatenate(
                [wq_nope, wq_rope.reshape(MLA_Q_RANK, MLA_Q_ROPE_COLS),
                 wq_rope[:, :, swap64].reshape(MLA_Q_RANK, MLA_Q_ROPE_COLS)], axis=1).astype(BF16)
            wkv = mla_w_ukv[j].reshape(MLA_KV_RANK, MLA_HEADS, MLA_NOPE + MLA_V)
            w_ukv = jnp.concatenate(
                [wkv[:, :, :MLA_NOPE].reshape(MLA_KV_RANK, MLA_Q_NOPE_COLS),
                 wkv[:, :, MLA_NOPE:].reshape(MLA_KV_RANK, MLA_HEADS * MLA_V)], axis=1).astype(BF16)
            rope = (jnp.asarray(np.tile(cos64, (1, MLA_HEADS))), jnp.asarray(np.tile(sin64, (1, MLA_HEADS))),
                    jnp.asarray(cos64), jnp.asarray(sin64))
            x, ckv_p, kpe_p = _mla_layer(
                x, ada, i, g1, w_down, mla_q_norm_g[j][None, :], mla_kv_norm_g[j][None, :],
                w_uq, w_ukv, mla_w_o[j].astype(BF16), cache_mla_ckv[:, j], cache_mla_kpe[:, j], rope)
            new_ckv.append(ckv_p.reshape(BATCH, SEQ, MLA_KV_RANK))
            new_kpe.append(kpe_p.reshape(BATCH, SEQ, MLA_ROPE))
        else:
            cos128, sin128, swap128 = _rope_tables(GQA_HEAD_DIM)
            wqkv = gqa_w_qkv[j]
            wq = wqkv[:, :GQA_Q_COLS].reshape(D_MODEL, GQA_HEADS, GQA_HEAD_DIM)
            wk = wqkv[:, GQA_Q_COLS:GQA_Q_COLS + GQA_KV_COLS].reshape(D_MODEL, GQA_KV_HEADS, GQA_HEAD_DIM)
            w_qkv_ctx = wqkv.astype(BF16)
            w_qkv_lat = jnp.concatenate(
                [wqkv, wq[:, :, swap128].reshape(D_MODEL, GQA_Q_COLS),
                 wk[:, :, swap128].reshape(D_MODEL, GQA_KV_COLS)], axis=1).astype(BF16)
            gq = gqa_q_norm_g[j]
            gk = gqa_k_norm_g[j]
            rope = (jnp.asarray(np.tile(cos128, (1, GQA_HEADS))), jnp.asarray(np.tile(sin128, (1, GQA_HEADS))),
                    jnp.asarray(np.tile(cos128, (1, GQA_KV_HEADS))), jnp.asarray(np.tile(sin128, (1, GQA_KV_HEADS))))
            x, k_p, v_p = _gqa_layer(
                x, ada, i, g1, w_qkv_ctx, w_qkv_lat,
                jnp.tile(gq, GQA_HEADS)[None, :], jnp.tile(gk, GQA_KV_HEADS)[None, :],
                jnp.tile(gq[swap128], GQA_HEADS)[None, :], jnp.tile(gk[swap128], GQA_KV_HEADS)[None, :],
                gqa_w_o[j].astype(BF16),
                cache_gqa_k[:, j].reshape(DEC_BATCH, PAST_LEN, GQA_KV_COLS),
                cache_gqa_v[:, j].reshape(DEC_BATCH, PAST_LEN, GQA_KV_COLS), rope)
            new_k.append(k_p.reshape(BATCH, SEQ, GQA_KV_HEADS, GQA_HEAD_DIM))
            new_v.append(v_p.reshape(BATCH, SEQ, GQA_KV_HEADS, GQA_HEAD_DIM))

        m = i // 2
        g2 = norm2_g[i][None, :]
        if i % 2 == 0:
            x = _ffn_layer(x, ada, i, m, g2, ffn_w_gate, ffn_w_up, ffn_w_down)
        else:
            w_router = jnp.pad(moe_w_router[m], ((0, 0), (0, LANES - N_EXPERTS)))
            final_g = final_norm_g[None, :] if i == DEPTH - 1 else None
            x, moe_bf = _moe_layer(x, ada, i, m, g2, w_router, moe_w_gate, moe_w_up, moe_w_down,
                                   moe_bf, final_g)

    y_prompt, y_sample = x
    y_prompt = y_prompt.reshape(BATCH, SEQ, D_MODEL)
    y_sample = y_sample.reshape(DEC_BATCH, DEC_SEQ, D_MODEL)
    return (y_prompt, y_sample,
            jnp.stack(new_ckv, axis=1), jnp.stack(new_kpe, axis=1),
            jnp.stack(new_k, axis=1), jnp.stack(new_v, axis=1))
```

```python
import functools

import numpy as np
import jax
import jax.numpy as jnp
from jax import lax
from jax.experimental import pallas as pl
from jax.experimental.pallas import tpu as pltpu

D_MODEL = 1024
BATCH = 32
SEQ = 256
DEPTH = 4
DEC_BATCH = 2
DEC_SEQ = 1024
PAST_LEN = 512
GRID_W = 64
N_MIXERS = 3
CONV_WIDTH = 3
MLA_HEADS = 8
MLA_NOPE = 128
MLA_ROPE = 64
MLA_V = 128
MLA_Q_RANK = 384
MLA_KV_RANK = 256
GQA_HEADS = 8
GQA_KV_HEADS = 2
GQA_HEAD_DIM = 128
D_FF = 2816
N_EXPERTS = 8
MOE_FF = 2048
ROPE_THETA = 10000.0
EPS = 1e-6

N_CTX = BATCH * SEQ
N_LAT = DEC_BATCH * DEC_SEQ
N_TOK = N_CTX + N_LAT
N_COND = 8
LANES = 128
Q_TILE = 256
VMEM_LIMIT = 56 * 1024 * 1024

F32 = jnp.float32
BF16 = jnp.bfloat16


def _dot(a, b):
    return jnp.dot(a, b, preferred_element_type=F32)


def _dot_t(a, b):
    return lax.dot_general(a, b, (((1,), (1,)), ((), ())), preferred_element_type=F32)


def _rms(x):
    return x * lax.rsqrt(jnp.mean(x * x, axis=-1, keepdims=True) + EPS)


def _modulate(x, g, shift, scale):
    return _rms(x) * g * (1.0 + scale) + shift


def _silu(x):
    return x * jax.nn.sigmoid(x)


def _cond_of_tile(i, tm):
    start = i * tm
    return jnp.where(start < N_CTX, 0, 1 + (start - N_CTX) // DEC_SEQ)


def _resident(shape):
    return pl.BlockSpec(shape, lambda *_: (0,) * len(shape), pipeline_mode=pl.Buffered(1))


def _resident_at(index, shape):
    return pl.BlockSpec((None,) + tuple(shape), lambda *_: (index,) + (0,) * len(shape),
                        pipeline_mode=pl.Buffered(1))


def _params(*sem):
    return pltpu.CompilerParams(dimension_semantics=sem, vmem_limit_bytes=VMEM_LIMIT)


def _adaln_kernel(cond_ref, w_ref, b_ref, o_ref):
    s = _silu(cond_ref[...]).astype(BF16)
    o_ref[...] = _dot(s, w_ref[...].astype(BF16)) + b_ref[...]


def _adaln_table(cond, w_ada, b_ada):
    tn = 1536
    out = pl.pallas_call(
        _adaln_kernel,
        grid=(DEPTH, 6 * D_MODEL // tn),
        in_specs=[
            pl.BlockSpec((N_COND, D_MODEL), lambda l, j: (0, 0)),
            pl.BlockSpec((None, D_MODEL, tn), lambda l, j: (l, 0, j)),
            pl.BlockSpec((None, 1, tn), lambda l, j: (l, 0, j)),
        ],
        out_specs=pl.BlockSpec((None, N_COND, tn), lambda l, j: (l, 0, j)),
        out_shape=jax.ShapeDtypeStruct((DEPTH, N_COND, 6 * D_MODEL), F32),
        compiler_params=_params("arbitrary", "arbitrary"),
        name="adaln_table",
    )(cond, w_ada, b_ada.reshape(DEPTH, 1, 6 * D_MODEL))
    return out.reshape(DEPTH, N_COND, 6, D_MODEL)


def _ada_spec(layer, tm):
    return pl.BlockSpec((None, None, 6, D_MODEL), lambda i: (layer, _cond_of_tile(i, tm), 0, 0))


def _ada_spec_at(layer, cond_fn):
    return pl.BlockSpec((None, None, 6, D_MODEL), lambda i: (layer, cond_fn(i), 0, 0))


CONV_TM = 1024
CONV_CC = 256


def _conv_kernel(*refs, split):
    if split:
        xp_ref, x_ref, ada_ref, g1_ref, win_ref, ck_ref, wout_ref, o_ref, v_scr = refs
    else:
        x_ref, ada_ref, g1_ref, win_ref, ck_ref, wout_ref, o_ref, v_scr = refs
    tm = x_ref.shape[0]
    i = pl.program_id(0)
    x = x_ref[...]
    if split:
        x = jnp.where(i * tm < N_CTX, xp_ref[...], x)
    h = _modulate(x, g1_ref[...], ada_ref[0:1, :], ada_ref[1:2, :]).astype(BF16)
    period = jnp.where(i * tm < N_CTX, SEQ, DEC_SEQ)
    pos = lax.broadcasted_iota(jnp.int32, (tm, 1), 0) & (period - 1)
    first = pos == 0
    last = pos == period - 1
    for j in range(D_MODEL // CONV_CC):
        lo = j * CONV_CC
        b_gate = _dot(h, win_ref[:, lo:lo + CONV_CC])
        c_gate = _dot(h, win_ref[:, D_MODEL + lo:D_MODEL + lo + CONV_CC])
        x_in = _dot(h, win_ref[:, 2 * D_MODEL + lo:2 * D_MODEL + lo + CONV_CC])
        u = c_gate * x_in
        u_prev = jnp.where(first, 0.0, pltpu.roll(u, 1, 0))
        u_next = jnp.where(last, 0.0, pltpu.roll(u, tm - 1, 0))
        conv = (ck_ref[0:1, lo:lo + CONV_CC] * u_prev + ck_ref[1:2, lo:lo + CONV_CC] * u
                + ck_ref[2:3, lo:lo + CONV_CC] * u_next)
        v_scr[:, lo:lo + CONV_CC] = (b_gate * conv).astype(BF16)
    o_ref[...] = x + ada_ref[2:3, :] * _dot(v_scr[...], wout_ref[...])


def _conv_layer(xs, ada, layer, j, g1, w_in, conv_k, w_out):
    tm = CONV_TM
    ctx_tiles = N_CTX // tm
    if len(xs) == 1:
        x_specs = [pl.BlockSpec((tm, D_MODEL), lambda i: (i, 0))]
    else:
        x_specs = [pl.BlockSpec((tm, D_MODEL), lambda i: (jnp.minimum(i, ctx_tiles - 1), 0)),
                   pl.BlockSpec((tm, D_MODEL), lambda i: (jnp.maximum(i - ctx_tiles, 0), 0))]
    return pl.pallas_call(
        functools.partial(_conv_kernel, split=len(xs) == 2),
        grid=(N_TOK // tm,),
        in_specs=x_specs + [
            _ada_spec(layer, tm),
            _resident((1, D_MODEL)),
            _resident_at(j, (D_MODEL, 3 * D_MODEL)),
            _resident_at(j, (CONV_WIDTH, D_MODEL)),
            _resident_at(j, (D_MODEL, D_MODEL)),
        ],
        out_specs=pl.BlockSpec((tm, D_MODEL), lambda i: (i, 0)),
        out_shape=jax.ShapeDtypeStruct((N_TOK, D_MODEL), F32),
        scratch_shapes=[pltpu.VMEM((tm, D_MODEL), BF16)],
        compiler_params=_params("arbitrary"),
        name=f"conv_mixer_{layer}",
    )(*xs, ada, g1, w_in, conv_k, w_out)


FFN_TM = 512
FFN_CHUNKS = ((0, 1536), (1536, D_FF))


def _ffn_kernel(x_ref, ada_ref, g2_ref, wg_ref, wu_ref, wd_ref, o_ref):
    x = x_ref[...]
    h = _modulate(x, g2_ref[...], ada_ref[3:4, :], ada_ref[4:5, :]).astype(BF16)
    f = None
    for lo, hi in FFN_CHUNKS:
        g = _dot(h, wg_ref[:, lo:hi])
        u = _dot(h, wu_ref[:, lo:hi])
        part = _dot((_silu(g) * u).astype(BF16), wd_ref[lo:hi, :])
        f = part if f is None else f + part
    o_ref[...] = x + ada_ref[5:6, :] * f


def _ffn_layer(x, ada, layer, m, g2, w_gate, w_up, w_down):
    tm = FFN_TM
    return pl.pallas_call(
        _ffn_kernel,
        grid=(N_TOK // tm,),
        in_specs=[
            pl.BlockSpec((tm, D_MODEL), lambda i: (i, 0)),
            _ada_spec(layer, tm),
            _resident((1, D_MODEL)),
            _resident_at(m, (D_MODEL, D_FF)),
            _resident_at(m, (D_MODEL, D_FF)),
            _resident_at(m, (D_FF, D_MODEL)),
        ],
        out_specs=pl.BlockSpec((tm, D_MODEL), lambda i: (i, 0)),
        out_shape=jax.ShapeDtypeStruct((N_TOK, D_MODEL), F32),
        compiler_params=_params("arbitrary"),
        name=f"dense_ffn_{layer}",
    )(x, ada, g2, w_gate, w_up, w_down)


def _attention(qk_parts, v_ref, heads, kv_of_head, dv, scale, o_scr, finish, keys_of, stage):
    def scores(rows, keys, h):
        s = None
        for q_ref, k_ref, width, k_shared in qk_parts:
            kb = 0 if k_shared else kv_of_head(h)
            part = _dot_t(q_ref[rows, h * width:(h + 1) * width],
                          k_ref[keys, kb * width:(kb + 1) * width])
            s = part if s is None else s + part
        return s * scale

    def q_block(rows):
        s_scr, p_scr = stage
        group = s_scr.shape[0]
        keys = keys_of(rows)
        for h0 in range(0, heads, group):
            for j in range(group):
                s_scr[j] = scores(rows, keys, h0 + j)
            for j in range(group):
                s = s_scr[j]
                e = jnp.exp(s - jnp.max(s, axis=-1, keepdims=True))
                p_scr[j] = (e * (1.0 / jnp.sum(e, axis=-1, keepdims=True))).astype(BF16)
            for j in range(group):
                h = h0 + j
                g = kv_of_head(h)
                o = _dot(p_scr[j], v_ref[keys, g * dv:(g + 1) * dv])
                o_scr[rows, h * dv:(h + 1) * dv] = o.astype(BF16)
        finish(rows)

    _for_row_blocks(o_scr.shape[0], q_block)


def _stage_scratch(group, keys):
    return [pltpu.VMEM((group, Q_TILE, keys), F32), pltpu.VMEM((group, Q_TILE, keys), BF16)]


LATENT_KEYS = DEC_SEQ + PAST_LEN
LATENT_STAGE_HEADS = 4


def _keys_of(latent):
    assert SEQ == Q_TILE
    return (lambda rows: slice(None)) if latent else (lambda rows: rows)


def _for_row_blocks(n_rows, fn):
    if n_rows <= 2 * Q_TILE:
        for b in range(n_rows // Q_TILE):
            fn(pl.ds(b * Q_TILE, Q_TILE))
    else:
        def body(b, carry):
            fn(pl.ds(pl.multiple_of(b * Q_TILE, Q_TILE), Q_TILE))
            return carry
        lax.fori_loop(0, n_rows // Q_TILE, body, 0)


MLA_QK = MLA_NOPE + MLA_ROPE
MLA_DOWN_COLS = MLA_Q_RANK + MLA_KV_RANK + 2 * MLA_ROPE
MLA_Q_NOPE_COLS = MLA_HEADS * MLA_NOPE
MLA_Q_ROPE_COLS = MLA_HEADS * MLA_ROPE


def _ctx_then_latent(body, n_in, refs):
    ins, lat_ref, rest = refs[:n_in], refs[n_in], refs[n_in + 1:]
    o_ref = rest[0]
    i = pl.program_id(0)

    @pl.when(i < CTX_TILES)
    def _():
        body(*ins, *rest, latent=False)

    @pl.when(i >= CTX_TILES)
    def _():
        o_ref[...] = lat_ref[...]


def _ctx_specs(out_tails):
    def tile(n_trailing):
        return lambda i: (jnp.minimum(i, CTX_TILES - 1),) + (0,) * n_trailing
    x_spec = pl.BlockSpec((CTX_TM, D_MODEL), tile(1))
    lat_spec = pl.BlockSpec((CTX_TM, D_MODEL), lambda i: (jnp.maximum(i - CTX_TILES, 0), 0))
    out_specs = [pl.BlockSpec((CTX_TM, D_MODEL), lambda i: (i, 0))]
    out_specs += [pl.BlockSpec((CTX_TM,) + tail, tile(len(tail))) for tail in out_tails]
    return x_spec, lat_spec, out_specs


CTX_TM = 2 * SEQ
CTX_TILES = N_CTX // CTX_TM
CTX_STEPS = CTX_TILES + N_LAT // CTX_TM


def _mla_kernel(*refs, latent):
    if latent:
        _mla_body(*refs, latent=True)
    else:
        _ctx_then_latent(_mla_body, 9, refs)


def _mla_body(*refs, latent):
    if latent:
        (x_ref, ada_ref, g1_ref, wdown_ref, gq_ref, gkv_ref, wuq_ref, wukv_ref, wo_ref,
         cosq_ref, sinq_ref, cosk_ref, sink_ref, cckv_ref, ckpe_ref,
         o_ref, qn_scr, qr_scr, kn_scr, kr_scr, v_scr, o_scr, *stage) = refs
    else:
        (x_ref, ada_ref, g1_ref, wdown_ref, gq_ref, gkv_ref, wuq_ref, wukv_ref, wo_ref,
         o_ref, ckv_out_ref, kpe_out_ref, qn_scr, qr_scr, kn_scr, kr_scr, v_scr, o_scr,
         *stage) = refs
    tm = x_ref.shape[0]
    kpe_lo = MLA_Q_RANK + MLA_KV_RANK

    def project(rows):
        h = _modulate(x_ref[rows, :], g1_ref[...], ada_ref[0:1, :], ada_ref[1:2, :]).astype(BF16)
        down = _dot(h, wdown_ref[...])
        c_q = down[:, :MLA_Q_RANK]
        c_kv = down[:, MLA_Q_RANK:kpe_lo]
        kpe = down[:, kpe_lo:kpe_lo + MLA_ROPE]
        q = _dot((_rms(c_q) * gq_ref[...]).astype(BF16), wuq_ref[...])
        ckv_n = _rms(c_kv) * gkv_ref[...]
        kv = _dot(ckv_n.astype(BF16), wukv_ref[...])
        qn_scr[rows, :] = q[:, :MLA_Q_NOPE_COLS].astype(BF16)
        q_rope = q[:, MLA_Q_NOPE_COLS:MLA_Q_NOPE_COLS + MLA_Q_ROPE_COLS]
        kn_scr[rows, :] = kv[:, :MLA_Q_NOPE_COLS].astype(BF16)
        v_scr[rows, :] = kv[:, MLA_Q_NOPE_COLS:].astype(BF16)
        if latent:
            q_swap = q[:, MLA_Q_NOPE_COLS + MLA_Q_ROPE_COLS:]
            q_rope = q_rope * cosq_ref[rows, :] + q_swap * sinq_ref[rows, :]
            kpe_swap = down[:, kpe_lo + MLA_ROPE:kpe_lo + 2 * MLA_ROPE]
            kpe = kpe * cosk_ref[rows, :] + kpe_swap * sink_ref[rows, :]
        else:
            ckv_out_ref[rows, :] = ckv_n
            kpe_out_ref[rows, :] = kpe
        qr_scr[rows, :] = q_rope.astype(BF16)
        kr_scr[rows, :] = kpe.astype(BF16)

    _for_row_blocks(tm, project)

    if latent:
        def expand_cache(rows):
            kv_c = _dot(cckv_ref[rows, :].astype(BF16), wukv_ref[...])
            dst = pl.ds(tm + rows.start, Q_TILE)
            kn_scr[dst, :] = kv_c[:, :MLA_Q_NOPE_COLS].astype(BF16)
            v_scr[dst, :] = kv_c[:, MLA_Q_NOPE_COLS:].astype(BF16)
            kr_scr[dst, :] = ckpe_ref[rows, :].astype(BF16)

        _for_row_blocks(PAST_LEN, expand_cache)

    def finish(rows):
        o_ref[rows, :] = x_ref[rows, :] + ada_ref[2:3, :] * _dot(o_scr[rows, :], wo_ref[...])

    _attention([(qn_scr, kn_scr, MLA_NOPE, False), (qr_scr, kr_scr, MLA_ROPE, True)],
               v_scr, MLA_HEADS, lambda hd: hd, MLA_V, MLA_QK ** -0.5, o_scr, finish,
               _keys_of(latent), stage)


def _mla_scratch(tm, sk):
    return [
        pltpu.VMEM((tm, MLA_Q_NOPE_COLS), BF16),
        pltpu.VMEM((tm, MLA_Q_ROPE_COLS), BF16),
        pltpu.VMEM((sk, MLA_Q_NOPE_COLS), BF16),
        pltpu.VMEM((sk, MLA_ROPE), BF16),
        pltpu.VMEM((sk, MLA_HEADS * MLA_V), BF16),
        pltpu.VMEM((tm, MLA_HEADS * MLA_V), BF16),
    ]


def _mla_layer(x, ada, layer, g1, w_down, g_q, g_kv, w_uq, w_ukv, w_o,
               cache_ckv, cache_kpe, rope):
    n_uq = w_uq.shape[1]
    weights = [
        _resident((1, D_MODEL)),
        _resident((D_MODEL, MLA_DOWN_COLS)),
        _resident((1, MLA_Q_RANK)),
        _resident((1, MLA_KV_RANK)),
        _resident((MLA_Q_RANK, n_uq)),
        _resident((MLA_KV_RANK, 2 * MLA_Q_NOPE_COLS)),
        _resident((MLA_HEADS * MLA_V, D_MODEL)),
    ]
    ctx_weights = list(weights)
    ctx_weights[4] = pl.BlockSpec((MLA_Q_RANK, MLA_Q_NOPE_COLS + MLA_Q_ROPE_COLS), lambda *_: (0, 0),
                                  pipeline_mode=pl.Buffered(1))
    weight_args = (g1, w_down, g_q, g_kv, w_uq, w_ukv, w_o)
    cos_q, sin_q, cos_k, sin_k = rope
    ctx_blocks = N_CTX // DEC_SEQ
    lat_out = pl.pallas_call(
        functools.partial(_mla_kernel, latent=True),
        grid=(DEC_BATCH,),
        in_specs=[pl.BlockSpec((DEC_SEQ, D_MODEL), lambda b: (ctx_blocks + b, 0)),
                  _ada_spec_at(layer, lambda b: 1 + b)] + weights + [
            _resident((DEC_SEQ, MLA_Q_ROPE_COLS)),
            _resident((DEC_SEQ, MLA_Q_ROPE_COLS)),
            _resident((DEC_SEQ, MLA_ROPE)),
            _resident((DEC_SEQ, MLA_ROPE)),
            pl.BlockSpec((None, PAST_LEN, MLA_KV_RANK), lambda b: (b, 0, 0)),
            pl.BlockSpec((None, PAST_LEN, MLA_ROPE), lambda b: (b, 0, 0)),
        ],
        out_specs=pl.BlockSpec((DEC_SEQ, D_MODEL), lambda b: (b, 0)),
        out_shape=jax.ShapeDtypeStruct((N_LAT, D_MODEL), F32),
        scratch_shapes=(_mla_scratch(DEC_SEQ, LATENT_KEYS)
                        + _stage_scratch(LATENT_STAGE_HEADS, LATENT_KEYS)),
        compiler_params=_params("arbitrary"),
        name=f"mla_latent_{layer}",
    )(x, ada, *weight_args, cos_q, sin_q, cos_k, sin_k, cache_ckv, cache_kpe)

    x_spec, lat_spec, out_specs = _ctx_specs(((MLA_KV_RANK,), (MLA_ROPE,)))
    return pl.pallas_call(
        functools.partial(_mla_kernel, latent=False),
        grid=(CTX_STEPS,),
        in_specs=[x_spec, _ada_spec_at(layer, lambda i: 0)] + ctx_weights + [lat_spec],
        out_specs=out_specs,
        out_shape=[
            jax.ShapeDtypeStruct((N_TOK, D_MODEL), F32),
            jax.ShapeDtypeStruct((N_CTX, MLA_KV_RANK), F32),
            jax.ShapeDtypeStruct((N_CTX, MLA_ROPE), F32),
        ],
        scratch_shapes=_mla_scratch(CTX_TM, CTX_TM) + _stage_scratch(MLA_HEADS, SEQ),
        compiler_params=_params("arbitrary"),
        name=f"mla_ctx_{layer}",
    )(x, ada, *weight_args, lat_out)


GQA_Q_COLS = GQA_HEADS * GQA_HEAD_DIM
GQA_KV_COLS = GQA_KV_HEADS * GQA_HEAD_DIM
GQA_REP = GQA_HEADS // GQA_KV_HEADS


def _head_rms(t, n_heads):
    cols = []
    for hd in range(n_heads):
        th = t[:, hd * GQA_HEAD_DIM:(hd + 1) * GQA_HEAD_DIM]
        r = lax.rsqrt(jnp.mean(th * th, axis=-1, keepdims=True) + EPS)
        cols.append(jnp.broadcast_to(r, th.shape))
    return jnp.concatenate(cols, axis=-1)


def _gqa_kernel(*refs, latent):
    if latent:
        _gqa_body(*refs, latent=True)
    else:
        _ctx_then_latent(_gqa_body, 7, refs)


def _gqa_body(*refs, latent):
    if latent:
        (x_ref, ada_ref, g1_ref, wqkv_ref, gq_ref, gk_ref, wo_ref,
         gqs_ref, gks_ref, cosq_ref, sinq_ref, cosk_ref, sink_ref, ck_ref, cv_ref,
         o_ref, q_scr, k_scr, v_scr, o_scr, *stage) = refs
    else:
        (x_ref, ada_ref, g1_ref, wqkv_ref, gq_ref, gk_ref, wo_ref,
         o_ref, k_out_ref, v_out_ref, q_scr, k_scr, v_scr, o_scr, *stage) = refs
    tm = x_ref.shape[0]

    def project(rows):
        h = _modulate(x_ref[rows, :], g1_ref[...], ada_ref[0:1, :], ada_ref[1:2, :]).astype(BF16)
        qkv = _dot(h, wqkv_ref[...])
        q_raw = qkv[:, :GQA_Q_COLS]
        k_raw = qkv[:, GQA_Q_COLS:GQA_Q_COLS + GQA_KV_COLS]
        v = qkv[:, GQA_Q_COLS + GQA_KV_COLS:GQA_Q_COLS + 2 * GQA_KV_COLS]
        q_r = _head_rms(q_raw, GQA_HEADS)
        k_r = _head_rms(k_raw, GQA_KV_HEADS)
        q = q_raw * q_r * gq_ref[...]
        k = k_raw * k_r * gk_ref[...]
        if latent:
            lo = GQA_Q_COLS + 2 * GQA_KV_COLS
            q_swap = qkv[:, lo:lo + GQA_Q_COLS] * q_r * gqs_ref[...]
            k_swap = qkv[:, lo + GQA_Q_COLS:lo + GQA_Q_COLS + GQA_KV_COLS] * k_r * gks_ref[...]
            q = q * cosq_ref[rows, :] + q_swap * sinq_ref[rows, :]
            k = k * cosk_ref[rows, :] + k_swap * sink_ref[rows, :]
        else:
            for g in range(GQA_KV_HEADS):
                cols = slice(g * GQA_HEAD_DIM, (g + 1) * GQA_HEAD_DIM)
                k_out_ref[rows, g, :] = k[:, cols]
                v_out_ref[rows, g, :] = v[:, cols]
        q_scr[rows, :] = q.astype(BF16)
        k_scr[rows, :] = k.astype(BF16)
        v_scr[rows, :] = v.astype(BF16)

    _for_row_blocks(tm, project)

    if latent:
        k_scr[tm:, :] = ck_ref[...].astype(BF16)
        v_scr[tm:, :] = cv_ref[...].astype(BF16)

    def finish(rows):
        o_ref[rows, :] = x_ref[rows, :] + ada_ref[2:3, :] * _dot(o_scr[rows, :], wo_ref[...])

    _attention([(q_scr, k_scr, GQA_HEAD_DIM, False)], v_scr, GQA_HEADS,
               lambda hd: hd // GQA_REP, GQA_HEAD_DIM, GQA_HEAD_DIM ** -0.5, o_scr, finish,
               _keys_of(latent), stage)


def _gqa_scratch(tm, sk):
    return [
        pltpu.VMEM((tm, GQA_Q_COLS), BF16),
        pltpu.VMEM((sk, GQA_KV_COLS), BF16),
        pltpu.VMEM((sk, GQA_KV_COLS), BF16),
        pltpu.VMEM((tm, GQA_Q_COLS), BF16),
    ]


def _gqa_layer(x, ada, layer, g1, w_qkv_ctx, w_qkv_lat, g_q, g_k, g_q_swap, g_k_swap, w_o,
               cache_k, cache_v, rope):
    cos_q, sin_q, cos_k, sin_k = rope
    ctx_blocks = N_CTX // DEC_SEQ
    lat_out = pl.pallas_call(
        functools.partial(_gqa_kernel, latent=True),
        grid=(DEC_BATCH,),
        in_specs=[pl.BlockSpec((DEC_SEQ, D_MODEL), lambda b: (ctx_blocks + b, 0)),
                  _ada_spec_at(layer, lambda b: 1 + b),
                  _resident((1, D_MODEL)),
                  _resident(w_qkv_lat.shape),
                  _resident((1, GQA_Q_COLS)),
                  _resident((1, GQA_KV_COLS)),
                  _resident((GQA_Q_COLS, D_MODEL)),
                  _resident((1, GQA_Q_COLS)),
                  _resident((1, GQA_KV_COLS)),
                  _resident((DEC_SEQ, GQA_Q_COLS)),
                  _resident((DEC_SEQ, GQA_Q_COLS)),
                  _resident((DEC_SEQ, GQA_KV_COLS)),
                  _resident((DEC_SEQ, GQA_KV_COLS)),
                  pl.BlockSpec((None, PAST_LEN, GQA_KV_COLS), lambda b: (b, 0, 0)),
                  pl.BlockSpec((None, PAST_LEN, GQA_KV_COLS), lambda b: (b, 0, 0))],
        out_specs=pl.BlockSpec((DEC_SEQ, D_MODEL), lambda b: (b, 0)),
        out_shape=jax.ShapeDtypeStruct((N_LAT, D_MODEL), F32),
        scratch_shapes=(_gqa_scratch(DEC_SEQ, LATENT_KEYS)
                        + _stage_scratch(LATENT_STAGE_HEADS, LATENT_KEYS)),
        compiler_params=_params("arbitrary"),
        name=f"gqa_latent_{layer}",
    )(x, ada, g1, w_qkv_lat, g_q, g_k, w_o, g_q_swap, g_k_swap,
      cos_q, sin_q, cos_k, sin_k, cache_k, cache_v)

    kv_tail = (GQA_KV_HEADS, GQA_HEAD_DIM)
    x_spec, lat_spec, out_specs = _ctx_specs((kv_tail, kv_tail))
    return pl.pallas_call(
        functools.partial(_gqa_kernel, latent=False),
        grid=(CTX_STEPS,),
        in_specs=[x_spec,
                  _ada_spec_at(layer, lambda i: 0),
                  _resident((1, D_MODEL)),
                  _resident(w_qkv_ctx.shape),
                  _resident((1, GQA_Q_COLS)),
                  _resident((1, GQA_KV_COLS)),
                  _resident((GQA_Q_COLS, D_MODEL)),
                  lat_spec],
        out_specs=out_specs,
        out_shape=[
            jax.ShapeDtypeStruct((N_TOK, D_MODEL), F32),
            jax.ShapeDtypeStruct((N_CTX,) + kv_tail, F32),
            jax.ShapeDtypeStruct((N_CTX,) + kv_tail, F32),
        ],
        scratch_shapes=_gqa_scratch(CTX_TM, CTX_TM) + _stage_scratch(GQA_HEADS, SEQ),
        compiler_params=_params("arbitrary"),
        name=f"gqa_ctx_{layer}",
    )(x, ada, g1, w_qkv_ctx, g_q, g_k, w_o, lat_out)


ROUTE_TM = 512
MOE_TG = 256
MOE_MIN_TILES = 2 * N_TOK // MOE_TG
MOE_MAX_TILES = MOE_MIN_TILES + N_EXPERTS
MOE_ROWS = MOE_MAX_TILES * MOE_TG
MOE_FILLS = 2 * N_EXPERTS
MOE_FC = 1024
COMBINE_TM = 256
ISSUE_UNROLL = 8


def _route_kernel(x_ref, ada_ref, g2_ref, wr_hi_ref, wr_lo_ref, h_ref, dest_ref, gatew_ref,
                  count_ref, carry_scr):
    tm = x_ref.shape[0]

    @pl.when(pl.program_id(0) == 0)
    def _():
        carry_scr[...] = jnp.zeros_like(carry_scr)

    h = _modulate(x_ref[...], g2_ref[...], ada_ref[3:4, :], ada_ref[4:5, :])
    h_ref[...] = h.reshape(tm, 1, D_MODEL)
    h_hi = h.astype(BF16)
    h_lo = (h - h_hi.astype(F32)).astype(BF16)
    logits = _dot(h_hi, wr_hi_ref[...]) + (_dot(h_lo, wr_hi_ref[...]) + _dot(h_hi, wr_lo_ref[...]))
    lane = lax.broadcasted_iota(jnp.int32, logits.shape, 1)
    neg = jnp.float32(-jnp.inf)
    l1 = jnp.where(lane < N_EXPERTS, logits, neg)
    m1 = jnp.max(l1, axis=-1, keepdims=True)
    i1 = jnp.min(jnp.where(l1 == m1, lane, LANES), axis=-1, keepdims=True)
    l2 = jnp.where(lane == i1, neg, l1)
    m2 = jnp.max(l2, axis=-1, keepdims=True)
    i2 = jnp.min(jnp.where(l2 == m2, lane, LANES), axis=-1, keepdims=True)
    e2 = jnp.exp(m2 - m1)
    w1 = 1.0 / (1.0 + e2)
    w2 = e2 / (1.0 + e2)
    member = jnp.where((lane == i1) | (lane == i2), 1.0, 0.0)
    r = lax.broadcasted_iota(jnp.int32, (tm, tm), 0)
    c = lax.broadcasted_iota(jnp.int32, (tm, tm), 1)
    tri = jnp.where(c < r, 1.0, 0.0).astype(BF16)
    before = _dot(tri, member.astype(BF16)) + carry_scr[0:1, :]
    rank1 = jnp.sum(jnp.where(lane == i1, before, 0.0), axis=-1, keepdims=True).astype(jnp.int32)
    rank2 = jnp.sum(jnp.where(lane == i2, before, 0.0), axis=-1, keepdims=True).astype(jnp.int32)
    table = jnp.where(lane == 0, i1, jnp.where(lane == 1, i2, jnp.where(
        lane == 2, rank1, jnp.where(lane == 3, rank2, 0))))
    dest_ref[...] = jnp.transpose(table)[0:8, :]
    gatew_ref[...] = jnp.where(lane == 0, w1, jnp.where(lane == 1, w2, 0.0))
    carry_scr[...] = carry_scr[...] + jnp.sum(member, axis=0, keepdims=True)
    count_ref[...] = carry_scr[...]


def _route(x, ada, layer, g2, w_router):
    tm = ROUTE_TM
    w_router_hi = w_router.astype(BF16)
    return pl.pallas_call(
        _route_kernel,
        grid=(N_TOK // tm,),
        in_specs=[
            pl.BlockSpec((tm, D_MODEL), lambda i: (i, 0)),
            _ada_spec(layer, tm),
            _resident((1, D_MODEL)),
            _resident((D_MODEL, LANES)),
            _resident((D_MODEL, LANES)),
        ],
        out_specs=[
            pl.BlockSpec((tm, 1, D_MODEL), lambda i: (i, 0, 0)),
            pl.BlockSpec((8, tm), lambda i: (0, i)),
            pl.BlockSpec((tm, LANES), lambda i: (i, 0)),
            pl.BlockSpec((8, LANES), lambda i: (0, 0)),
        ],
        out_shape=[
            jax.ShapeDtypeStruct((N_TOK, 1, D_MODEL), F32),
            jax.ShapeDtypeStruct((8, N_TOK), jnp.int32),
            jax.ShapeDtypeStruct((N_TOK, LANES), F32),
            jax.ShapeDtypeStruct((8, LANES), F32),
        ],
        scratch_shapes=[pltpu.VMEM((8, LANES), F32)],
        compiler_params=_params("arbitrary"),
        name=f"moe_route_{layer}",
    )(x, ada, g2, w_router_hi, (w_router - w_router_hi.astype(F32)).astype(BF16))


def _cast_specs(m, parts):
    last = N_EXPERTS * parts - 1

    def src(i, *_):
        c = jnp.minimum(i, last)
        return (m, c // parts, c % parts, 0)

    def dst(i, *_):
        c = jnp.minimum(i, last)
        return (c // parts, c % parts, 0)

    shapes = [(D_MODEL // parts, MOE_FF), (D_MODEL // parts, MOE_FF), (MOE_FF // parts, D_MODEL)]
    in_specs = [pl.BlockSpec((None, None) + s, src) for s in shapes]
    out_specs = [pl.BlockSpec((None,) + s, dst) for s in shapes]
    out_shape = [jax.ShapeDtypeStruct((N_EXPERTS, D_MODEL, MOE_FF), BF16),
                 jax.ShapeDtypeStruct((N_EXPERTS, D_MODEL, MOE_FF), BF16),
                 jax.ShapeDtypeStruct((N_EXPERTS, MOE_FF, D_MODEL), BF16)]
    return in_specs, out_specs, out_shape


def _cast_chunk(srcs, dsts):
    for s_ref, d_ref in zip(srcs, dsts):
        d_ref[...] = s_ref[...].astype(BF16)


DISPATCH_STEPS = 2 * N_EXPERTS
DISPATCH_CAST_PARTS = 2
EXPERTS_CAST_PARTS = 8


def _dispatch_kernel(*refs, cast):
    if cast:
        (d1_ref, d2_ref, fill_start_ref, fill_rows_ref, h_ref, wg_ref, wu_ref, wd_ref,
         hs_ref, wg_out_ref, wu_out_ref, wd_out_ref, zero_scr, sem) = refs
    else:
        d1_ref, d2_ref, fill_start_ref, fill_rows_ref, h_ref, hs_ref, zero_scr, sem = refs
    i = pl.program_id(0)
    tm = h_ref.shape[0]
    base = i * tm

    def fill_copy(k):
        n = fill_rows_ref[k]
        return pltpu.make_async_copy(zero_scr.at[pl.ds(0, n)],
                                     hs_ref.at[pl.ds(fill_start_ref[k], n)], sem.at[1])

    @pl.when(i == 0)
    def _():
        zero_scr[...] = jnp.zeros_like(zero_scr)
        for k in range(MOE_FILLS):
            @pl.when(fill_rows_ref[k] > 0)
            def _():
                fill_copy(k).start()
        for k in range(MOE_FILLS):
            @pl.when(fill_rows_ref[k] > 0)
            def _():
                fill_copy(k).wait()

    def row_copy(r, d_ref):
        return pltpu.make_async_copy(h_ref.at[r], hs_ref.at[d_ref[base + r]], sem.at[0])

    def issue(g, carry):
        for k in range(ISSUE_UNROLL):
            r = g * ISSUE_UNROLL + k
            row_copy(r, d1_ref).start(priority=0)
            row_copy(r, d2_ref).start(priority=1)
        return carry

    def drain():
        for _ in range(2):
            pltpu.make_async_copy(h_ref, hs_ref.at[pl.ds(0, tm)], sem.at[0]).wait()

    lax.fori_loop(0, tm // ISSUE_UNROLL, issue, 0)
    if cast:
        _cast_chunk((wg_ref, wu_ref, wd_ref), (wg_out_ref, wu_out_ref, wd_out_ref))
    drain()


def _dispatch(d1, d2, fill_start, fill_rows, h_rows, cast=None):
    tm = N_TOK // DISPATCH_STEPS
    in_specs = [pl.BlockSpec((tm, 1, D_MODEL), lambda i, *_: (i, 0, 0))]
    out_specs = [pl.BlockSpec(memory_space=pl.ANY)]
    out_shape = [jax.ShapeDtypeStruct((MOE_ROWS, 1, D_MODEL), F32)]
    weights = ()
    if cast is not None:
        assert DISPATCH_STEPS == N_EXPERTS * DISPATCH_CAST_PARTS
        cast_in, cast_out, cast_shape = _cast_specs(cast[0], DISPATCH_CAST_PARTS)
        in_specs += cast_in
        out_specs += cast_out
        out_shape += cast_shape
        weights = cast[1:]
    out = pl.pallas_call(
        functools.partial(_dispatch_kernel, cast=cast is not None),
        grid_spec=pltpu.PrefetchScalarGridSpec(
            num_scalar_prefetch=4,
            grid=(DISPATCH_STEPS,),
            in_specs=in_specs,
            out_specs=out_specs,
            scratch_shapes=[pltpu.VMEM((MOE_TG, 1, D_MODEL), F32),
                            pltpu.SemaphoreType.DMA((2,))],
        ),
        out_shape=out_shape,
        compiler_params=_params("arbitrary"),
        name="moe_dispatch",
    )(d1, d2, fill_start, fill_rows, h_rows, *weights)
    return out if cast is not None else out[0]


def _experts_kernel(*refs, cast):
    if cast:
        (te_ref, nt_ref, hs_ref, wg_ref, wu_ref, wd_ref, ng_ref, nu_ref, nd_ref,
         ys_ref, ng_out_ref, nu_out_ref, nd_out_ref, h2d_scr) = refs
    else:
        te_ref, nt_ref, hs_ref, wg_ref, wu_ref, wd_ref, ys_ref, h2d_scr = refs
    used = pl.program_id(0) < nt_ref[0]

    @pl.when(jnp.logical_not(used))
    def _():
        ys_ref[...] = jnp.zeros_like(ys_ref)

    @pl.when(used)
    def _():
        if cast:
            _cast_chunk((ng_ref, nu_ref, nd_ref), (ng_out_ref, nu_out_ref, nd_out_ref))
        h2d_scr[...] = hs_ref[...].reshape(MOE_TG, D_MODEL)
        h = h2d_scr[...].astype(BF16)
        y = None
        for lo in range(0, MOE_FF, MOE_FC):
            g = _dot(h, wg_ref[:, lo:lo + MOE_FC])
            u = _dot(h, wu_ref[:, lo:lo + MOE_FC])
            part = _dot((_silu(g) * u).astype(BF16), wd_ref[lo:lo + MOE_FC, :])
            y = part if y is None else y + part
        ys_ref[...] = y.reshape(MOE_TG, 1, D_MODEL)


def _experts(tile_expert, n_tiles, hs, w_gate, w_up, w_down, cast=None):
    rows = pl.BlockSpec((MOE_TG, 1, D_MODEL), lambda t, te, nt: (t, 0, 0))
    in_specs = [
        rows,
        pl.BlockSpec((None, D_MODEL, MOE_FF), lambda t, te, nt: (te[t], 0, 0)),
        pl.BlockSpec((None, D_MODEL, MOE_FF), lambda t, te, nt: (te[t], 0, 0)),
        pl.BlockSpec((None, MOE_FF, D_MODEL), lambda t, te, nt: (te[t], 0, 0)),
    ]
    out_specs = [rows]
    out_shape = [jax.ShapeDtypeStruct((MOE_ROWS, 1, D_MODEL), F32)]
    next_weights = ()
    if cast is not None:
        assert MOE_MIN_TILES >= N_EXPERTS * EXPERTS_CAST_PARTS
        cast_in, cast_out, cast_shape = _cast_specs(cast[0], EXPERTS_CAST_PARTS)
        in_specs += cast_in
        out_specs += cast_out
        out_shape += cast_shape
        next_weights = cast[1:]
    out = pl.pallas_call(
        functools.partial(_experts_kernel, cast=cast is not None),
        grid_spec=pltpu.PrefetchScalarGridSpec(
            num_scalar_prefetch=2,
            grid=(MOE_MAX_TILES,),
            in_specs=in_specs,
            out_specs=out_specs,
            scratch_shapes=[pltpu.VMEM((MOE_TG, D_MODEL), F32)],
        ),
        out_shape=out_shape,
        compiler_params=_params("arbitrary"),
        name="moe_experts",
    )(tile_expert, n_tiles, hs, w_gate, w_up, w_down, *next_weights)
    return out if cast is not None else out[0]


def _combine_kernel(*refs, final):
    if final:
        (d1_ref, d2_ref, x_ref, ada_ref, gw_ref, gf_ref, ys_ref, yp_ref, yl_ref,
         ya0, yb0, ya1, yb1, ya2d, yb2d, sem) = refs
    else:
        (d1_ref, d2_ref, x_ref, ada_ref, gw_ref, ys_ref, o_ref,
         ya0, yb0, ya1, yb1, ya2d, yb2d, sem) = refs
    i = pl.program_id(0)
    tm = x_ref.shape[0]

    def row_copies(step, r, buf_a, buf_b, slot):
        t = step * tm + r
        return (pltpu.make_async_copy(ys_ref.at[d1_ref[t]], buf_a.at[r], sem.at[slot]),
                pltpu.make_async_copy(ys_ref.at[d2_ref[t]], buf_b.at[r], sem.at[slot]))

    def start_tile(step, buf_a, buf_b, slot):
        def body(g, carry):
            for k in range(ISSUE_UNROLL):
                copies = row_copies(step, g * ISSUE_UNROLL + k, buf_a, buf_b, slot)
                for priority, cp in enumerate(copies):
                    cp.start(priority=priority)
            return carry
        lax.fori_loop(0, tm // ISSUE_UNROLL, body, 0)

    def wait_tile(buf_a, buf_b, slot):
        for buf in (buf_a, buf_b):
            pltpu.make_async_copy(ys_ref.at[pl.ds(0, tm)], buf, sem.at[slot]).wait()

    @pl.when(i == 0)
    def _():
        start_tile(0, ya0, yb0, 0)

    def run(cur, nxt):
        @pl.when(i + 1 < pl.num_programs(0))
        def _():
            start_tile(i + 1, *nxt)

        wait_tile(*cur)
        ya2d[...] = cur[0][...].reshape(tm, D_MODEL)
        yb2d[...] = cur[1][...].reshape(tm, D_MODEL)
        mix = gw_ref[:, 0:1] * ya2d[...] + gw_ref[:, 1:2] * yb2d[...]
        out = x_ref[...] + ada_ref[5:6, :] * mix
        if final:
            y = _rms(out) * gf_ref[...]

            @pl.when(i * tm < N_CTX)
            def _():
                yp_ref[...] = y

            @pl.when(i * tm >= N_CTX)
            def _():
                yl_ref[...] = y
        else:
            o_ref[...] = out

    @pl.when(i % 2 == 0)
    def _():
        run((ya0, yb0, 0), (ya1, yb1, 1))

    @pl.when(i % 2 == 1)
    def _():
        run((ya1, yb1, 1), (ya0, yb0, 0))


def _combine(d1, d2, x, ada, layer, gate_w, ys, final_g=None):
    tm = COMBINE_TM
    final = final_g is not None
    ctx_tiles = N_CTX // tm
    row_buf = pltpu.VMEM((tm, 1, D_MODEL), F32)
    in_specs = [
        pl.BlockSpec((tm, D_MODEL), lambda i, *_: (i, 0)),
        pl.BlockSpec((None, None, 6, D_MODEL),
                     lambda i, *_: (layer, _cond_of_tile(i, tm), 0, 0)),
        pl.BlockSpec((tm, LANES), lambda i, *_: (i, 0)),
    ]
    if final:
        in_specs.append(pl.BlockSpec((1, D_MODEL), lambda i, *_: (0, 0)))
        out_specs = [
            pl.BlockSpec((tm, D_MODEL), lambda i, *_: (jnp.minimum(i, ctx_tiles - 1), 0)),
            pl.BlockSpec((tm, D_MODEL), lambda i, *_: (jnp.maximum(i - ctx_tiles, 0), 0)),
        ]
        out_shape = [jax.ShapeDtypeStruct((N_CTX, D_MODEL), F32),
                     jax.ShapeDtypeStruct((N_LAT, D_MODEL), F32)]
        args = (d1, d2, x, ada, gate_w, final_g, ys)
    else:
        out_specs = pl.BlockSpec((tm, D_MODEL), lambda i, *_: (i, 0))
        out_shape = jax.ShapeDtypeStruct((N_TOK, D_MODEL), F32)
        args = (d1, d2, x, ada, gate_w, ys)
    in_specs.append(pl.BlockSpec(memory_space=pl.ANY))
    return pl.pallas_call(
        functools.partial(_combine_kernel, final=final),
        grid_spec=pltpu.PrefetchScalarGridSpec(
            num_scalar_prefetch=2,
            grid=(N_TOK // tm,),
            in_specs=in_specs,
            out_specs=out_specs,
            scratch_shapes=[row_buf, row_buf, row_buf, row_buf,
                            pltpu.VMEM((tm, D_MODEL), F32), pltpu.VMEM((tm, D_MODEL), F32),
                            pltpu.SemaphoreType.DMA((2,))],
        ),
        out_shape=out_shape,
        compiler_params=_params("arbitrary"),
        name="moe_combine",
    )(*args)


def _moe_layer(x, ada, layer, m, g2, w_router, w_gate, w_up, w_down, weights_bf, final_g):
    h_rows, route, gate_w, counts = _route(x, ada, layer, g2, w_router)
    cnt = counts[0, :N_EXPERTS].astype(jnp.int32)
    tiles_per_expert = (cnt + MOE_TG - 1) // MOE_TG
    tile_end = jnp.cumsum(tiles_per_expert)
    row_start = (tile_end - tiles_per_expert) * MOE_TG
    n_tiles = tile_end[-1:]
    t = jnp.minimum(jnp.arange(MOE_MAX_TILES, dtype=jnp.int32), n_tiles - 1)
    tile_expert = jnp.sum((t[:, None] >= tile_end[None, :]).astype(jnp.int32), axis=1)
    experts = jnp.arange(N_EXPERTS, dtype=jnp.int32)
    d1 = jnp.sum(jnp.where(route[0][:, None] == experts, row_start, 0), axis=1) + route[2]
    d2 = jnp.sum(jnp.where(route[1][:, None] == experts, row_start, 0), axis=1) + route[3]
    spare_tile = jnp.minimum(n_tiles + experts, MOE_MAX_TILES - 1)
    fill_start = jnp.concatenate([row_start + cnt, spare_tile * MOE_TG])
    fill_rows = jnp.concatenate([tiles_per_expert * MOE_TG - cnt,
                                 jnp.where(n_tiles + experts < MOE_MAX_TILES, MOE_TG, 0)])
    f32_weights = (w_gate, w_up, w_down)
    if weights_bf is None:
        hs, *weights_bf = _dispatch(d1, d2, fill_start, fill_rows, h_rows, (m,) + f32_weights)
    else:
        hs = _dispatch(d1, d2, fill_start, fill_rows, h_rows)
    next_bf = None
    if m + 1 < w_gate.shape[0]:
        ys, *next_bf = _experts(tile_expert, n_tiles, hs, *weights_bf, (m + 1,) + f32_weights)
    else:
        ys = _experts(tile_expert, n_tiles, hs, *weights_bf)
    return _combine(d1, d2, x, ada, layer, gate_w, ys, final_g), next_bf


assert DEPTH % 2 == 0


def _rope_tables(dim):
    half = dim // 2
    quarter = half // 2
    t = np.arange(DEC_SEQ)
    pos = np.stack([t // GRID_W, t % GRID_W], axis=1).astype(np.float32)
    freqs = (ROPE_THETA ** (-np.arange(quarter, dtype=np.float32) / quarter)).astype(np.float32)
    ang = pos[:, :, None] * freqs[None, None, :]
    cos = np.cos(ang.astype(np.float64))
    sin = np.sin(ang.astype(np.float64))
    cos_t = np.concatenate([cos, cos], axis=-1).reshape(DEC_SEQ, dim)
    sin_t = np.concatenate([-sin, sin], axis=-1).reshape(DEC_SEQ, dim)
    lane = np.arange(dim)
    swap = np.where(lane % half < quarter, lane + quarter, lane - quarter)
    return cos_t.astype(np.float32), sin_t.astype(np.float32), swap


def kernel(x_prompt, x_sample, cache_mla_ckv, cache_mla_kpe, cache_gqa_k, cache_gqa_v, c, c_ctx, w_ada, b_ada, norm1_g, norm2_g, conv_w_in, conv_k, conv_w_out, mla_w_down, mla_q_norm_g, mla_kv_norm_g, mla_w_uq, mla_w_ukv, mla_w_o, gqa_w_qkv, gqa_q_norm_g, gqa_k_norm_g, gqa_w_o, ffn_w_gate, ffn_w_up, ffn_w_down, moe_w_router, moe_w_gate, moe_w_up, moe_w_down, final_norm_g):
    xs = (x_prompt.reshape(N_CTX, D_MODEL), x_sample.reshape(N_LAT, D_MODEL))
    conv_w_in, conv_w_out = conv_w_in.astype(BF16), conv_w_out.astype(BF16)
    ffn_w_gate, ffn_w_up, ffn_w_down = (w.astype(BF16) for w in (ffn_w_gate, ffn_w_up, ffn_w_down))
    cond = jnp.concatenate(
        [c_ctx[None, :], c, jnp.zeros((N_COND - 1 - DEC_BATCH, D_MODEL), F32)], axis=0)
    ada = _adaln_table(cond, w_ada, b_ada)

    new_ckv, new_kpe, new_k, new_v = [], [], [], []
    moe_bf = None
    for i in range(DEPTH):
        j = i // N_MIXERS
        kind = i % N_MIXERS
        g1 = norm1_g[i][None, :]
        if kind == 0:
            x = _conv_layer(xs if i == 0 else (x,), ada, i, j, g1, conv_w_in, conv_k, conv_w_out)
        elif kind == 1:
            cos64, sin64, swap64 = _rope_tables(MLA_ROPE)
            wd = mla_w_down[j]
            kpe_cols = wd[:, MLA_Q_RANK + MLA_KV_RANK:]
            w_down = jnp.concatenate([wd, kpe_cols[:, swap64]], axis=1).astype(BF16)
            wq = mla_w_uq[j].reshape(MLA_Q_RANK, MLA_HEADS, MLA_QK)
            wq_nope = wq[:, :, :MLA_NOPE].reshape(MLA_Q_RANK, MLA_Q_NOPE_COLS)
            wq_rope = wq[:, :, MLA_NOPE:]
            w_uq = jnp.concatenate(
                [wq_nope, wq_rope.reshape(MLA_Q_RANK, MLA_Q_ROPE_COLS),
                 wq_rope[:, :, swap64].reshape(MLA_Q_RANK, MLA_Q_ROPE_COLS)], axis=1).astype(BF16)
            wkv = mla_w_ukv[j].reshape(MLA_KV_RANK, MLA_HEADS, MLA_NOPE + MLA_V)
            w_ukv = jnp.concatenate(
                [wkv[:, :, :MLA_NOPE].reshape(MLA_KV_RANK, MLA_Q_NOPE_COLS),
                 wkv[:, :, MLA_NOPE:].reshape(MLA_KV_RANK, MLA_HEADS * MLA_V)], axis=1).astype(BF16)
            rope = (jnp.asarray(np.tile(cos64, (1, MLA_HEADS))), jnp.asarray(np.tile(sin64, (1, MLA_HEADS))),
                    jnp.asarray(cos64), jnp.asarray(sin64))
            x, ckv_p, kpe_p = _mla_layer(
                x, ada, i, g1, w_down, mla_q_norm_g[j][None, :], mla_kv_norm_g[j][None, :],
                w_uq, w_ukv, mla_w_o[j].astype(BF16), cache_mla_ckv[:, j], cache_mla_kpe[:, j], rope)
            new_ckv.append(ckv_p.reshape(BATCH, SEQ, MLA_KV_RANK))
            new_kpe.append(kpe_p.reshape(BATCH, SEQ, MLA_ROPE))
        else:
            cos128, sin128, swap128 = _rope_tables(GQA_HEAD_DIM)
            wqkv = gqa_w_qkv[j]
            wq = wqkv[:, :GQA_Q_COLS].reshape(D_MODEL, GQA_HEADS, GQA_HEAD_DIM)
            wk = wqkv[:, GQA_Q_COLS:GQA_Q_COLS + GQA_KV_COLS].reshape(D_MODEL, GQA_KV_HEADS, GQA_HEAD_DIM)
            w_qkv_ctx = wqkv.astype(BF16)
            w_qkv_lat = jnp.concatenate(
                [wqkv, wq[:, :, swap128].reshape(D_MODEL, GQA_Q_COLS),
                 wk[:, :, swap128].reshape(D_MODEL, GQA_KV_COLS)], axis=1).astype(BF16)
            gq = gqa_q_norm_g[j]
            gk = gqa_k_norm_g[j]
            rope = (jnp.asarray(np.tile(cos128, (1, GQA_HEADS))), jnp.asarray(np.tile(sin128, (1, GQA_HEADS))),
                    jnp.asarray(np.tile(cos128, (1, GQA_KV_HEADS))), jnp.asarray(np.tile(sin128, (1, GQA_KV_HEADS))))
            x, k_p, v_p = _gqa_layer(
                x, ada, i, g1, w_qkv_ctx, w_qkv_lat,
                jnp.tile(gq, GQA_HEADS)[None, :], jnp.tile(gk, GQA_KV_HEADS)[None, :],
                jnp.tile(gq[swap128], GQA_HEADS)[None, :], jnp.tile(gk[swap128], GQA_KV_HEADS)[None, :],
                gqa_w_o[j].astype(BF16),
                cache_gqa_k[:, j].reshape(DEC_BATCH, PAST_LEN, GQA_KV_COLS),
                cache_gqa_v[:, j].reshape(DEC_BATCH, PAST_LEN, GQA_KV_COLS), rope)
            new_k.append(k_p.reshape(BATCH, SEQ, GQA_KV_HEADS, GQA_HEAD_DIM))
            new_v.append(v_p.reshape(BATCH, SEQ, GQA_KV_HEADS, GQA_HEAD_DIM))

        m = i // 2
        g2 = norm2_g[i][None, :]
        if i % 2 == 0:
            x = _ffn_layer(x, ada, i, m, g2, ffn_w_gate, ffn_w_up, ffn_w_down)
        else:
            w_router = jnp.pad(moe_w_router[m], ((0, 0), (0, LANES - N_EXPERTS)))
            final_g = final_norm_g[None, :] if i == DEPTH - 1 else None
            x, moe_bf = _moe_layer(x, ada, i, m, g2, w_router, moe_w_gate, moe_w_up, moe_w_down,
                                   moe_bf, final_g)

    y_prompt, y_sample = x
    y_prompt = y_prompt.reshape(BATCH, SEQ, D_MODEL)
    y_sample = y_sample.reshape(DEC_BATCH, DEC_SEQ, D_MODEL)
    return (y_prompt, y_sample,
            jnp.stack(new_ckv, axis=1), jnp.stack(new_kpe, axis=1),
            jnp.stack(new_k, axis=1), jnp.stack(new_v, axis=1))
```

```python
import functools

import numpy as np
import jax
import jax.numpy as jnp
from jax import lax
from jax.experimental import pallas as pl
from jax.experimental.pallas import tpu as pltpu

D_MODEL = 1024
BATCH = 32
SEQ = 256
DEPTH = 4
DEC_BATCH = 2
DEC_SEQ = 1024
PAST_LEN = 512
GRID_W = 64
N_MIXERS = 3
CONV_WIDTH = 3
MLA_HEADS = 8
MLA_NOPE = 128
MLA_ROPE = 64
MLA_V = 128
MLA_Q_RANK = 384
MLA_KV_RANK = 256
GQA_HEADS = 8
GQA_KV_HEADS = 2
GQA_HEAD_DIM = 128
D_FF = 2816
N_EXPERTS = 8
MOE_FF = 2048
ROPE_THETA = 10000.0
EPS = 1e-6

N_CTX = BATCH * SEQ
N_LAT = DEC_BATCH * DEC_SEQ
N_TOK = N_CTX + N_LAT
N_COND = 8
LANES = 128
Q_TILE = 256
VMEM_LIMIT = 56 * 1024 * 1024

F32 = jnp.float32
BF16 = jnp.bfloat16


def _dot(a, b):
    return jnp.dot(a, b, preferred_element_type=F32)


def _dot_t(a, b):
    return lax.dot_general(a, b, (((1,), (1,)), ((), ())), preferred_element_type=F32)


def _rms(x):
    return x * lax.rsqrt(jnp.mean(x * x, axis=-1, keepdims=True) + EPS)


def _modulate(x, g, shift, scale):
    return _rms(x) * g * (1.0 + scale) + shift


def _silu(x):
    return x * jax.nn.sigmoid(x)


def _cond_of_tile(i, tm):
    start = i * tm
    return jnp.where(start < N_CTX, 0, 1 + (start - N_CTX) // DEC_SEQ)


def _resident(shape):
    return pl.BlockSpec(shape, lambda *_: (0,) * len(shape), pipeline_mode=pl.Buffered(1))


def _resident_at(index, shape):
    return pl.BlockSpec((None,) + tuple(shape), lambda *_: (index,) + (0,) * len(shape),
                        pipeline_mode=pl.Buffered(1))


def _params(*sem):
    return pltpu.CompilerParams(dimension_semantics=sem, vmem_limit_bytes=VMEM_LIMIT)


def _adaln_kernel(cond_ref, w_ref, b_ref, o_ref):
    s = _silu(cond_ref[...]).astype(BF16)
    o_ref[...] = _dot(s, w_ref[...].astype(BF16)) + b_ref[...]


def _adaln_table(cond, w_ada, b_ada):
    tn = 1536
    out = pl.pallas_call(
        _adaln_kernel,
        grid=(DEPTH, 6 * D_MODEL // tn),
        in_specs=[
            pl.BlockSpec((N_COND, D_MODEL), lambda l, j: (0, 0)),
            pl.BlockSpec((None, D_MODEL, tn), lambda l, j: (l, 0, j)),
            pl.BlockSpec((None, 1, tn), lambda l, j: (l, 0, j)),
        ],
        out_specs=pl.BlockSpec((None, N_COND, tn), lambda l, j: (l, 0, j)),
        out_shape=jax.ShapeDtypeStruct((DEPTH, N_COND, 6 * D_MODEL), F32),
        compiler_params=_params("arbitrary", "arbitrary"),
        name="adaln_table",
    )(cond, w_ada, b_ada.reshape(DEPTH, 1, 6 * D_MODEL))
    return out.reshape(DEPTH, N_COND, 6, D_MODEL)


def _ada_spec(layer, tm):
    return pl.BlockSpec((None, None, 6, D_MODEL), lambda i: (layer, _cond_of_tile(i, tm), 0, 0))


def _ada_spec_at(layer, cond_fn):
    return pl.BlockSpec((None, None, 6, D_MODEL), lambda i: (layer, cond_fn(i), 0, 0))


CONV_TM = 1024
CONV_CC = 256


def _conv_kernel(*refs, split):
    if split:
        xp_ref, x_ref, ada_ref, g1_ref, win_ref, ck_ref, wout_ref, o_ref, v_scr = refs
    else:
        x_ref, ada_ref, g1_ref, win_ref, ck_ref, wout_ref, o_ref, v_scr = refs
    tm = x_ref.shape[0]
    i = pl.program_id(0)
    x = x_ref[...]
    if split:
        x = jnp.where(i * tm < N_CTX, xp_ref[...], x)
    h = _modulate(x, g1_ref[...], ada_ref[0:1, :], ada_ref[1:2, :]).astype(BF16)
    period = jnp.where(i * tm < N_CTX, SEQ, DEC_SEQ)
    pos = lax.broadcasted_iota(jnp.int32, (tm, 1), 0) & (period - 1)
    first = pos == 0
    last = pos == period - 1
    for j in range(D_MODEL // CONV_CC):
        lo = j * CONV_CC
        b_gate = _dot(h, win_ref[:, lo:lo + CONV_CC])
        c_gate = _dot(h, win_ref[:, D_MODEL + lo:D_MODEL + lo + CONV_CC])
        x_in = _dot(h, win_ref[:, 2 * D_MODEL + lo:2 * D_MODEL + lo + CONV_CC])
        u = c_gate * x_in
        u_prev = jnp.where(first, 0.0, pltpu.roll(u, 1, 0))
        u_next = jnp.where(last, 0.0, pltpu.roll(u, tm - 1, 0))
        conv = (ck_ref[0:1, lo:lo + CONV_CC] * u_prev + ck_ref[1:2, lo:lo + CONV_CC] * u
                + ck_ref[2:3, lo:lo + CONV_CC] * u_next)
        v_scr[:, lo:lo + CONV_CC] = (b_gate * conv).astype(BF16)
    o_ref[...] = x + ada_ref[2:3, :] * _dot(v_scr[...], wout_ref[...])


def _conv_layer(xs, ada, layer, j, g1, w_in, conv_k, w_out):
    tm = CONV_TM
    ctx_tiles = N_CTX // tm
    if len(xs) == 1:
        x_specs = [pl.BlockSpec((tm, D_MODEL), lambda i: (i, 0))]
    else:
        x_specs = [pl.BlockSpec((tm, D_MODEL), lambda i: (jnp.minimum(i, ctx_tiles - 1), 0)),
                   pl.BlockSpec((tm, D_MODEL), lambda i: (jnp.maximum(i - ctx_tiles, 0), 0))]
    return pl.pallas_call(
        functools.partial(_conv_kernel, split=len(xs) == 2),
        grid=(N_TOK // tm,),
        in_specs=x_specs + [
            _ada_spec(layer, tm),
            _resident((1, D_MODEL)),
            _resident_at(j, (D_MODEL, 3 * D_MODEL)),
            _resident_at(j, (CONV_WIDTH, D_MODEL)),
            _resident_at(j, (D_MODEL, D_MODEL)),
        ],
        out_specs=pl.BlockSpec((tm, D_MODEL), lambda i: (i, 0)),
        out_shape=jax.ShapeDtypeStruct((N_TOK, D_MODEL), F32),
        scratch_shapes=[pltpu.VMEM((tm, D_MODEL), BF16)],
        compiler_params=_params("arbitrary"),
        name=f"conv_mixer_{layer}",
    )(*xs, ada, g1, w_in, conv_k, w_out)


FFN_TM = 512
FFN_CAST_TM = 256
FFN_CAST_PARTS = 4
FFN_CHUNKS = ((0, 1536), (1536, D_FF))


def _ffn_kernel(*refs, cast):
    if cast:
        (x_ref, ada_ref, g2_ref, wg_ref, wu_ref, wd_ref, eg_ref, eu_ref, ed_ref,
         o_ref, eg_out_ref, eu_out_ref, ed_out_ref) = refs
        _cast_chunk((eg_ref, eu_ref, ed_ref), (eg_out_ref, eu_out_ref, ed_out_ref))
    else:
        x_ref, ada_ref, g2_ref, wg_ref, wu_ref, wd_ref, o_ref = refs
    x = x_ref[...]
    h = _modulate(x, g2_ref[...], ada_ref[3:4, :], ada_ref[4:5, :]).astype(BF16)
    f = None
    for lo, hi in FFN_CHUNKS:
        g = _dot(h, wg_ref[:, lo:hi])
        u = _dot(h, wu_ref[:, lo:hi])
        part = _dot((_silu(g) * u).astype(BF16), wd_ref[lo:hi, :])
        f = part if f is None else f + part
    o_ref[...] = x + ada_ref[5:6, :] * f


def _ffn_layer(x, ada, layer, m, g2, w_gate, w_up, w_down, cast=None):
    tm = FFN_TM if cast is None else FFN_CAST_TM
    in_specs = [
        pl.BlockSpec((tm, D_MODEL), lambda i: (i, 0)),
        _ada_spec(layer, tm),
        _resident((1, D_MODEL)),
        _resident_at(m, (D_MODEL, D_FF)),
        _resident_at(m, (D_MODEL, D_FF)),
        _resident_at(m, (D_FF, D_MODEL)),
    ]
    out_specs = [pl.BlockSpec((tm, D_MODEL), lambda i: (i, 0))]
    out_shape = [jax.ShapeDtypeStruct((N_TOK, D_MODEL), F32)]
    expert_weights = ()
    if cast is not None:
        assert N_TOK // tm >= N_EXPERTS * FFN_CAST_PARTS
        cast_in, cast_out, cast_shape = _cast_specs(cast[0], FFN_CAST_PARTS)
        in_specs += cast_in
        out_specs += cast_out
        out_shape += cast_shape
        expert_weights = cast[1:]
    out = pl.pallas_call(
        functools.partial(_ffn_kernel, cast=cast is not None),
        grid=(N_TOK // tm,),
        in_specs=in_specs,
        out_specs=out_specs,
        out_shape=out_shape,
        compiler_params=_params("arbitrary"),
        name=f"dense_ffn_{layer}",
    )(x, ada, g2, w_gate, w_up, w_down, *expert_weights)
    return (out[0], out[1:]) if cast is not None else (out[0], None)


def _attention(qk_parts, v_ref, heads, kv_of_head, dv, scale, o_scr, finish, keys_of, stage):
    def scores(rows, keys, h):
        s = None
        for q_ref, k_ref, width, k_shared in qk_parts:
            kb = 0 if k_shared else kv_of_head(h)
            part = _dot_t(q_ref[rows, h * width:(h + 1) * width],
                          k_ref[keys, kb * width:(kb + 1) * width])
            s = part if s is None else s + part
        return s * scale

    def q_block(rows):
        s_scr, p_scr = stage
        group = s_scr.shape[0]
        keys = keys_of(rows)
        for h0 in range(0, heads, group):
            for j in range(group):
                s_scr[j] = scores(rows, keys, h0 + j)
            for j in range(group):
                s = s_scr[j]
                e = jnp.exp(s - jnp.max(s, axis=-1, keepdims=True))
                p_scr[j] = (e * (1.0 / jnp.sum(e, axis=-1, keepdims=True))).astype(BF16)
            for j in range(group):
                h = h0 + j
                g = kv_of_head(h)
                o = _dot(p_scr[j], v_ref[keys, g * dv:(g + 1) * dv])
                o_scr[rows, h * dv:(h + 1) * dv] = o.astype(BF16)
        finish(rows)

    _for_row_blocks(o_scr.shape[0], q_block)


def _stage_scratch(group, keys):
    return [pltpu.VMEM((group, Q_TILE, keys), F32), pltpu.VMEM((group, Q_TILE, keys), BF16)]


LATENT_KEYS = DEC_SEQ + PAST_LEN
LATENT_STAGE_HEADS = 4


def _keys_of(latent):
    assert SEQ == Q_TILE
    return (lambda rows: slice(None)) if latent else (lambda rows: rows)


def _for_row_blocks(n_rows, fn):
    if n_rows <= 2 * Q_TILE:
        for b in range(n_rows // Q_TILE):
            fn(pl.ds(b * Q_TILE, Q_TILE))
    else:
        def body(b, carry):
            fn(pl.ds(pl.multiple_of(b * Q_TILE, Q_TILE), Q_TILE))
            return carry
        lax.fori_loop(0, n_rows // Q_TILE, body, 0)


MLA_QK = MLA_NOPE + MLA_ROPE
MLA_DOWN_COLS = MLA_Q_RANK + MLA_KV_RANK + 2 * MLA_ROPE
MLA_Q_NOPE_COLS = MLA_HEADS * MLA_NOPE
MLA_Q_ROPE_COLS = MLA_HEADS * MLA_ROPE


def _ctx_then_latent(body, n_in, refs):
    ins, lat_ref, rest = refs[:n_in], refs[n_in], refs[n_in + 1:]
    o_ref = rest[0]
    i = pl.program_id(0)

    @pl.when(i < CTX_TILES)
    def _():
        body(*ins, *rest, latent=False)

    @pl.when(i >= CTX_TILES)
    def _():
        o_ref[...] = lat_ref[...]


def _ctx_specs(out_tails):
    def tile(n_trailing):
        return lambda i: (jnp.minimum(i, CTX_TILES - 1),) + (0,) * n_trailing
    x_spec = pl.BlockSpec((CTX_TM, D_MODEL), tile(1))
    lat_spec = pl.BlockSpec((CTX_TM, D_MODEL), lambda i: (jnp.maximum(i - CTX_TILES, 0), 0))
    out_specs = [pl.BlockSpec((CTX_TM, D_MODEL), lambda i: (i, 0))]
    out_specs += [pl.BlockSpec((CTX_TM,) + tail, tile(len(tail))) for tail in out_tails]
    return x_spec, lat_spec, out_specs


CTX_TM = 2 * SEQ
CTX_TILES = N_CTX // CTX_TM
CTX_STEPS = CTX_TILES + N_LAT // CTX_TM


def _mla_kernel(*refs, latent):
    if latent:
        _mla_body(*refs, latent=True)
    else:
        _ctx_then_latent(_mla_body, 9, refs)


def _mla_body(*refs, latent):
    if latent:
        (x_ref, ada_ref, g1_ref, wdown_ref, gq_ref, gkv_ref, wuq_ref, wukv_ref, wo_ref,
         cosq_ref, sinq_ref, cosk_ref, sink_ref, cckv_ref, ckpe_ref,
         o_ref, qn_scr, qr_scr, kn_scr, kr_scr, v_scr, o_scr, *stage) = refs
    else:
        (x_ref, ada_ref, g1_ref, wdown_ref, gq_ref, gkv_ref, wuq_ref, wukv_ref, wo_ref,
         o_ref, ckv_out_ref, kpe_out_ref, qn_scr, qr_scr, kn_scr, kr_scr, v_scr, o_scr,
         *stage) = refs
    tm = x_ref.shape[0]
    kpe_lo = MLA_Q_RANK + MLA_KV_RANK

    def project(rows):
        h = _modulate(x_ref[rows, :], g1_ref[...], ada_ref[0:1, :], ada_ref[1:2, :]).astype(BF16)
        down = _dot(h, wdown_ref[...])
        c_q = down[:, :MLA_Q_RANK]
        c_kv = down[:, MLA_Q_RANK:kpe_lo]
        kpe = down[:, kpe_lo:kpe_lo + MLA_ROPE]
        q = _dot((_rms(c_q) * gq_ref[...]).astype(BF16), wuq_ref[...])
        ckv_n = _rms(c_kv) * gkv_ref[...]
        kv = _dot(ckv_n.astype(BF16), wukv_ref[...])
        qn_scr[rows, :] = q[:, :MLA_Q_NOPE_COLS].astype(BF16)
        q_rope = q[:, MLA_Q_NOPE_COLS:MLA_Q_NOPE_COLS + MLA_Q_ROPE_COLS]
        kn_scr[rows, :] = kv[:, :MLA_Q_NOPE_COLS].astype(BF16)
        v_scr[rows, :] = kv[:, MLA_Q_NOPE_COLS:].astype(BF16)
        if latent:
            q_swap = q[:, MLA_Q_NOPE_COLS + MLA_Q_ROPE_COLS:]
            q_rope = q_rope * cosq_ref[rows, :] + q_swap * sinq_ref[rows, :]
            kpe_swap = down[:, kpe_lo + MLA_ROPE:kpe_lo + 2 * MLA_ROPE]
            kpe = kpe * cosk_ref[rows, :] + kpe_swap * sink_ref[rows, :]
        else:
            ckv_out_ref[rows, :] = ckv_n
            kpe_out_ref[rows, :] = kpe
        qr_scr[rows, :] = q_rope.astype(BF16)
        kr_scr[rows, :] = kpe.astype(BF16)

    _for_row_blocks(tm, project)

    if latent:
        def expand_cache(rows):
            kv_c = _dot(cckv_ref[rows, :].astype(BF16), wukv_ref[...])
            dst = pl.ds(tm + rows.start, Q_TILE)
            kn_scr[dst, :] = kv_c[:, :MLA_Q_NOPE_COLS].astype(BF16)
            v_scr[dst, :] = kv_c[:, MLA_Q_NOPE_COLS:].astype(BF16)
            kr_scr[dst, :] = ckpe_ref[rows, :].astype(BF16)

        _for_row_blocks(PAST_LEN, expand_cache)

    def finish(rows):
        o_ref[rows, :] = x_ref[rows, :] + ada_ref[2:3, :] * _dot(o_scr[rows, :], wo_ref[...])

    _attention([(qn_scr, kn_scr, MLA_NOPE, False), (qr_scr, kr_scr, MLA_ROPE, True)],
               v_scr, MLA_HEADS, lambda hd: hd, MLA_V, MLA_QK ** -0.5, o_scr, finish,
               _keys_of(latent), stage)


def _mla_scratch(tm, sk):
    return [
        pltpu.VMEM((tm, MLA_Q_NOPE_COLS), BF16),
        pltpu.VMEM((tm, MLA_Q_ROPE_COLS), BF16),
        pltpu.VMEM((sk, MLA_Q_NOPE_COLS), BF16),
        pltpu.VMEM((sk, MLA_ROPE), BF16),
        pltpu.VMEM((sk, MLA_HEADS * MLA_V), BF16),
        pltpu.VMEM((tm, MLA_HEADS * MLA_V), BF16),
    ]


def _mla_layer(x, ada, layer, g1, w_down, g_q, g_kv, w_uq, w_ukv, w_o,
               cache_ckv, cache_kpe, rope):
    n_uq = w_uq.shape[1]
    weights = [
        _resident((1, D_MODEL)),
        _resident((D_MODEL, MLA_DOWN_COLS)),
        _resident((1, MLA_Q_RANK)),
        _resident((1, MLA_KV_RANK)),
        _resident((MLA_Q_RANK, n_uq)),
        _resident((MLA_KV_RANK, 2 * MLA_Q_NOPE_COLS)),
        _resident((MLA_HEADS * MLA_V, D_MODEL)),
    ]
    ctx_weights = list(weights)
    ctx_weights[4] = pl.BlockSpec((MLA_Q_RANK, MLA_Q_NOPE_COLS + MLA_Q_ROPE_COLS), lambda *_: (0, 0),
                                  pipeline_mode=pl.Buffered(1))
    weight_args = (g1, w_down, g_q, g_kv, w_uq, w_ukv, w_o)
    cos_q, sin_q, cos_k, sin_k = rope
    ctx_blocks = N_CTX // DEC_SEQ
    lat_out = pl.pallas_call(
        functools.partial(_mla_kernel, latent=True),
        grid=(DEC_BATCH,),
        in_specs=[pl.BlockSpec((DEC_SEQ, D_MODEL), lambda b: (ctx_blocks + b, 0)),
                  _ada_spec_at(layer, lambda b: 1 + b)] + weights + [
            _resident((DEC_SEQ, MLA_Q_ROPE_COLS)),
            _resident((DEC_SEQ, MLA_Q_ROPE_COLS)),
            _resident((DEC_SEQ, MLA_ROPE)),
            _resident((DEC_SEQ, MLA_ROPE)),
            pl.BlockSpec((None, PAST_LEN, MLA_KV_RANK), lambda b: (b, 0, 0)),
            pl.BlockSpec((None, PAST_LEN, MLA_ROPE), lambda b: (b, 0, 0)),
        ],
        out_specs=pl.BlockSpec((DEC_SEQ, D_MODEL), lambda b: (b, 0)),
        out_shape=jax.ShapeDtypeStruct((N_LAT, D_MODEL), F32),
        scratch_shapes=(_mla_scratch(DEC_SEQ, LATENT_KEYS)
                        + _stage_scratch(LATENT_STAGE_HEADS, LATENT_KEYS)),
        compiler_params=_params("arbitrary"),
        name=f"mla_latent_{layer}",
    )(x, ada, *weight_args, cos_q, sin_q, cos_k, sin_k, cache_ckv, cache_kpe)

    x_spec, lat_spec, out_specs = _ctx_specs(((MLA_KV_RANK,), (MLA_ROPE,)))
    return pl.pallas_call(
        functools.partial(_mla_kernel, latent=False),
        grid=(CTX_STEPS,),
        in_specs=[x_spec, _ada_spec_at(layer, lambda i: 0)] + ctx_weights + [lat_spec],
        out_specs=out_specs,
        out_shape=[
            jax.ShapeDtypeStruct((N_TOK, D_MODEL), F32),
            jax.ShapeDtypeStruct((N_CTX, MLA_KV_RANK), F32),
            jax.ShapeDtypeStruct((N_CTX, MLA_ROPE), F32),
        ],
        scratch_shapes=_mla_scratch(CTX_TM, CTX_TM) + _stage_scratch(MLA_HEADS, SEQ),
        compiler_params=_params("arbitrary"),
        name=f"mla_ctx_{layer}",
    )(x, ada, *weight_args, lat_out)


GQA_Q_COLS = GQA_HEADS * GQA_HEAD_DIM
GQA_KV_COLS = GQA_KV_HEADS * GQA_HEAD_DIM
GQA_REP = GQA_HEADS // GQA_KV_HEADS


def _head_rms(t, n_heads):
    cols = []
    for hd in range(n_heads):
        th = t[:, hd * GQA_HEAD_DIM:(hd + 1) * GQA_HEAD_DIM]
        r = lax.rsqrt(jnp.mean(th * th, axis=-1, keepdims=True) + EPS)
        cols.append(jnp.broadcast_to(r, th.shape))
    return jnp.concatenate(cols, axis=-1)


def _gqa_kernel(*refs, latent):
    if latent:
        _gqa_body(*refs, latent=True)
    else:
        _ctx_then_latent(_gqa_body, 7, refs)


def _gqa_body(*refs, latent):
    if latent:
        (x_ref, ada_ref, g1_ref, wqkv_ref, gq_ref, gk_ref, wo_ref,
         gqs_ref, gks_ref, cosq_ref, sinq_ref, cosk_ref, sink_ref, ck_ref, cv_ref,
         o_ref, q_scr, k_scr, v_scr, o_scr, *stage) = refs
    else:
        (x_ref, ada_ref, g1_ref, wqkv_ref, gq_ref, gk_ref, wo_ref,
         o_ref, k_out_ref, v_out_ref, q_scr, k_scr, v_scr, o_scr, *stage) = refs
    tm = x_ref.shape[0]

    def project(rows):
        h = _modulate(x_ref[rows, :], g1_ref[...], ada_ref[0:1, :], ada_ref[1:2, :]).astype(BF16)
        qkv = _dot(h, wqkv_ref[...])
        q_raw = qkv[:, :GQA_Q_COLS]
        k_raw = qkv[:, GQA_Q_COLS:GQA_Q_COLS + GQA_KV_COLS]
        v = qkv[:, GQA_Q_COLS + GQA_KV_COLS:GQA_Q_COLS + 2 * GQA_KV_COLS]
        q_r = _head_rms(q_raw, GQA_HEADS)
        k_r = _head_rms(k_raw, GQA_KV_HEADS)
        q = q_raw * q_r * gq_ref[...]
        k = k_raw * k_r * gk_ref[...]
        if latent:
            lo = GQA_Q_COLS + 2 * GQA_KV_COLS
            q_swap = qkv[:, lo:lo + GQA_Q_COLS] * q_r * gqs_ref[...]
            k_swap = qkv[:, lo + GQA_Q_COLS:lo + GQA_Q_COLS + GQA_KV_COLS] * k_r * gks_ref[...]
            q = q * cosq_ref[rows, :] + q_swap * sinq_ref[rows, :]
            k = k * cosk_ref[rows, :] + k_swap * sink_ref[rows, :]
        else:
            for g in range(GQA_KV_HEADS):
                cols = slice(g * GQA_HEAD_DIM, (g + 1) * GQA_HEAD_DIM)
                k_out_ref[rows, g, :] = k[:, cols]
                v_out_ref[rows, g, :] = v[:, cols]
        q_scr[rows, :] = q.astype(BF16)
        k_scr[rows, :] = k.astype(BF16)
        v_scr[rows, :] = v.astype(BF16)

    _for_row_blocks(tm, project)

    if latent:
        k_scr[tm:, :] = ck_ref[...].astype(BF16)
        v_scr[tm:, :] = cv_ref[...].astype(BF16)

    def finish(rows):
        o_ref[rows, :] = x_ref[rows, :] + ada_ref[2:3, :] * _dot(o_scr[rows, :], wo_ref[...])

    _attention([(q_scr, k_scr, GQA_HEAD_DIM, False)], v_scr, GQA_HEADS,
               lambda hd: hd // GQA_REP, GQA_HEAD_DIM, GQA_HEAD_DIM ** -0.5, o_scr, finish,
               _keys_of(latent), stage)


def _gqa_scratch(tm, sk):
    return [
        pltpu.VMEM((tm, GQA_Q_COLS), BF16),
        pltpu.VMEM((sk, GQA_KV_COLS), BF16),
        pltpu.VMEM((sk, GQA_KV_COLS), BF16),
        pltpu.VMEM((tm, GQA_Q_COLS), BF16),
    ]


def _gqa_layer(x, ada, layer, g1, w_qkv_ctx, w_qkv_lat, g_q, g_k, g_q_swap, g_k_swap, w_o,
               cache_k, cache_v, rope):
    cos_q, sin_q, cos_k, sin_k = rope
    ctx_blocks = N_CTX // DEC_SEQ
    lat_out = pl.pallas_call(
        functools.partial(_gqa_kernel, latent=True),
        grid=(DEC_BATCH,),
        in_specs=[pl.BlockSpec((DEC_SEQ, D_MODEL), lambda b: (ctx_blocks + b, 0)),
                  _ada_spec_at(layer, lambda b: 1 + b),
                  _resident((1, D_MODEL)),
                  _resident(w_qkv_lat.shape),
                  _resident((1, GQA_Q_COLS)),
                  _resident((1, GQA_KV_COLS)),
                  _resident((GQA_Q_COLS, D_MODEL)),
                  _resident((1, GQA_Q_COLS)),
                  _resident((1, GQA_KV_COLS)),
                  _resident((DEC_SEQ, GQA_Q_COLS)),
                  _resident((DEC_SEQ, GQA_Q_COLS)),
                  _resident((DEC_SEQ, GQA_KV_COLS)),
                  _resident((DEC_SEQ, GQA_KV_COLS)),
                  pl.BlockSpec((None, PAST_LEN, GQA_KV_COLS), lambda b: (b, 0, 0)),
                  pl.BlockSpec((None, PAST_LEN, GQA_KV_COLS), lambda b: (b, 0, 0))],
        out_specs=pl.BlockSpec((DEC_SEQ, D_MODEL), lambda b: (b, 0)),
        out_shape=jax.ShapeDtypeStruct((N_LAT, D_MODEL), F32),
        scratch_shapes=(_gqa_scratch(DEC_SEQ, LATENT_KEYS)
                        + _stage_scratch(LATENT_STAGE_HEADS, LATENT_KEYS)),
        compiler_params=_params("arbitrary"),
        name=f"gqa_latent_{layer}",
    )(x, ada, g1, w_qkv_lat, g_q, g_k, w_o, g_q_swap, g_k_swap,
      cos_q, sin_q, cos_k, sin_k, cache_k, cache_v)

    kv_tail = (GQA_KV_HEADS, GQA_HEAD_DIM)
    x_spec, lat_spec, out_specs = _ctx_specs((kv_tail, kv_tail))
    return pl.pallas_call(
        functools.partial(_gqa_kernel, latent=False),
        grid=(CTX_STEPS,),
        in_specs=[x_spec,
                  _ada_spec_at(layer, lambda i: 0),
                  _resident((1, D_MODEL)),
                  _resident(w_qkv_ctx.shape),
                  _resident((1, GQA_Q_COLS)),
                  _resident((1, GQA_KV_COLS)),
                  _resident((GQA_Q_COLS, D_MODEL)),
                  lat_spec],
        out_specs=out_specs,
        out_shape=[
            jax.ShapeDtypeStruct((N_TOK, D_MODEL), F32),
            jax.ShapeDtypeStruct((N_CTX,) + kv_tail, F32),
            jax.ShapeDtypeStruct((N_CTX,) + kv_tail, F32),
        ],
        scratch_shapes=_gqa_scratch(CTX_TM, CTX_TM) + _stage_scratch(GQA_HEADS, SEQ),
        compiler_params=_params("arbitrary"),
        name=f"gqa_ctx_{layer}",
    )(x, ada, g1, w_qkv_ctx, g_q, g_k, w_o, lat_out)


ROUTE_TM = 512
MOE_TG = 256
MOE_MIN_TILES = 2 * N_TOK // MOE_TG
MOE_MAX_TILES = MOE_MIN_TILES + N_EXPERTS
MOE_ROWS = MOE_MAX_TILES * MOE_TG
MOE_FILLS = 2 * N_EXPERTS
MOE_FC = 1024
COMBINE_TM = 256
ISSUE_UNROLL = 8


def _route_kernel(x_ref, ada_ref, g2_ref, wr_hi_ref, wr_lo_ref, h_ref, dest_ref, gatew_ref,
                  count_ref, carry_scr):
    tm = x_ref.shape[0]

    @pl.when(pl.program_id(0) == 0)
    def _():
        carry_scr[...] = jnp.zeros_like(carry_scr)

    h = _modulate(x_ref[...], g2_ref[...], ada_ref[3:4, :], ada_ref[4:5, :])
    h_ref[...] = h.reshape(tm, 1, D_MODEL)
    h_hi = h.astype(BF16)
    h_lo = (h - h_hi.astype(F32)).astype(BF16)
    logits = _dot(h_hi, wr_hi_ref[...]) + (_dot(h_lo, wr_hi_ref[...]) + _dot(h_hi, wr_lo_ref[...]))
    lane = lax.broadcasted_iota(jnp.int32, logits.shape, 1)
    neg = jnp.float32(-jnp.inf)
    l1 = jnp.where(lane < N_EXPERTS, logits, neg)
    m1 = jnp.max(l1, axis=-1, keepdims=True)
    i1 = jnp.min(jnp.where(l1 == m1, lane, LANES), axis=-1, keepdims=True)
    l2 = jnp.where(lane == i1, neg, l1)
    m2 = jnp.max(l2, axis=-1, keepdims=True)
    i2 = jnp.min(jnp.where(l2 == m2, lane, LANES), axis=-1, keepdims=True)
    e2 = jnp.exp(m2 - m1)
    w1 = 1.0 / (1.0 + e2)
    w2 = e2 / (1.0 + e2)
    member = jnp.where((lane == i1) | (lane == i2), 1.0, 0.0)
    r = lax.broadcasted_iota(jnp.int32, (tm, tm), 0)
    c = lax.broadcasted_iota(jnp.int32, (tm, tm), 1)
    tri = jnp.where(c < r, 1.0, 0.0).astype(BF16)
    before = _dot(tri, member.astype(BF16)) + carry_scr[0:1, :]
    rank1 = jnp.sum(jnp.where(lane == i1, before, 0.0), axis=-1, keepdims=True).astype(jnp.int32)
    rank2 = jnp.sum(jnp.where(lane == i2, before, 0.0), axis=-1, keepdims=True).astype(jnp.int32)
    table = jnp.where(lane == 0, i1, jnp.where(lane == 1, i2, jnp.where(
        lane == 2, rank1, jnp.where(lane == 3, rank2, 0))))
    dest_ref[...] = jnp.transpose(table)[0:8, :]
    gatew_ref[...] = jnp.where(lane == 0, w1, jnp.where(lane == 1, w2, 0.0))
    carry_scr[...] = carry_scr[...] + jnp.sum(member, axis=0, keepdims=True)
    count_ref[...] = carry_scr[...]


def _route(x, ada, layer, g2, w_router):
    tm = ROUTE_TM
    w_router_hi = w_router.astype(BF16)
    return pl.pallas_call(
        _route_kernel,
        grid=(N_TOK // tm,),
        in_specs=[
            pl.BlockSpec((tm, D_MODEL), lambda i: (i, 0)),
            _ada_spec(layer, tm),
            _resident((1, D_MODEL)),
            _resident((D_MODEL, LANES)),
            _resident((D_MODEL, LANES)),
        ],
        out_specs=[
            pl.BlockSpec((tm, 1, D_MODEL), lambda i: (i, 0, 0)),
            pl.BlockSpec((8, tm), lambda i: (0, i)),
            pl.BlockSpec((tm, LANES), lambda i: (i, 0)),
            pl.BlockSpec((8, LANES), lambda i: (0, 0)),
        ],
        out_shape=[
            jax.ShapeDtypeStruct((N_TOK, 1, D_MODEL), F32),
            jax.ShapeDtypeStruct((8, N_TOK), jnp.int32),
            jax.ShapeDtypeStruct((N_TOK, LANES), F32),
            jax.ShapeDtypeStruct((8, LANES), F32),
        ],
        scratch_shapes=[pltpu.VMEM((8, LANES), F32)],
        compiler_params=_params("arbitrary"),
        name=f"moe_route_{layer}",
    )(x, ada, g2, w_router_hi, (w_router - w_router_hi.astype(F32)).astype(BF16))


def _cast_specs(m, parts):
    last = N_EXPERTS * parts - 1

    def src(i, *_):
        c = jnp.minimum(i, last)
        return (m, c // parts, c % parts, 0)

    def dst(i, *_):
        c = jnp.minimum(i, last)
        return (c // parts, c % parts, 0)

    shapes = [(D_MODEL // parts, MOE_FF), (D_MODEL // parts, MOE_FF), (MOE_FF // parts, D_MODEL)]
    in_specs = [pl.BlockSpec((None, None) + s, src) for s in shapes]
    out_specs = [pl.BlockSpec((None,) + s, dst) for s in shapes]
    out_shape = [jax.ShapeDtypeStruct((N_EXPERTS, D_MODEL, MOE_FF), BF16),
                 jax.ShapeDtypeStruct((N_EXPERTS, D_MODEL, MOE_FF), BF16),
                 jax.ShapeDtypeStruct((N_EXPERTS, MOE_FF, D_MODEL), BF16)]
    return in_specs, out_specs, out_shape


def _cast_chunk(srcs, dsts):
    for s_ref, d_ref in zip(srcs, dsts):
        d_ref[...] = s_ref[...].astype(BF16)


DISPATCH_TM = 640
EXPERTS_CAST_PARTS = 8


def _dispatch_kernel(d1_ref, d2_ref, fill_start_ref, fill_rows_ref, h_ref, hs_ref, zero_scr, sem):
    i = pl.program_id(0)
    tm = h_ref.shape[0]
    base = i * tm

    def fill_copy(k):
        n = fill_rows_ref[k]
        return pltpu.make_async_copy(zero_scr.at[pl.ds(0, n)],
                                     hs_ref.at[pl.ds(fill_start_ref[k], n)], sem.at[1])

    @pl.when(i == 0)
    def _():
        zero_scr[...] = jnp.zeros_like(zero_scr)
        for k in range(MOE_FILLS):
            @pl.when(fill_rows_ref[k] > 0)
            def _():
                fill_copy(k).start()
        for k in range(MOE_FILLS):
            @pl.when(fill_rows_ref[k] > 0)
            def _():
                fill_copy(k).wait()

    def row_copy(r, d_ref):
        return pltpu.make_async_copy(h_ref.at[r], hs_ref.at[d_ref[base + r]], sem.at[0])

    def issue(g, carry):
        for k in range(ISSUE_UNROLL):
            r = g * ISSUE_UNROLL + k
            row_copy(r, d1_ref).start(priority=0)
            row_copy(r, d2_ref).start(priority=1)
        return carry

    def drain():
        for _ in range(2):
            pltpu.make_async_copy(h_ref, hs_ref.at[pl.ds(0, tm)], sem.at[0]).wait()

    lax.fori_loop(0, tm // ISSUE_UNROLL, issue, 0)
    drain()


def _dispatch(d1, d2, fill_start, fill_rows, h_rows):
    tm = DISPATCH_TM
    return pl.pallas_call(
        _dispatch_kernel,
        grid_spec=pltpu.PrefetchScalarGridSpec(
            num_scalar_prefetch=4,
            grid=(N_TOK // tm,),
            in_specs=[pl.BlockSpec((tm, 1, D_MODEL), lambda i, *_: (i, 0, 0))],
            out_specs=pl.BlockSpec(memory_space=pl.ANY),
            scratch_shapes=[pltpu.VMEM((MOE_TG, 1, D_MODEL), F32),
                            pltpu.SemaphoreType.DMA((2,))],
        ),
        out_shape=jax.ShapeDtypeStruct((MOE_ROWS, 1, D_MODEL), F32),
        compiler_params=_params("arbitrary"),
        name="moe_dispatch",
    )(d1, d2, fill_start, fill_rows, h_rows)


def _experts_kernel(*refs, cast):
    if cast:
        (te_ref, nt_ref, hs_ref, wg_ref, wu_ref, wd_ref, ng_ref, nu_ref, nd_ref,
         ys_ref, ng_out_ref, nu_out_ref, nd_out_ref, h2d_scr) = refs
    else:
        te_ref, nt_ref, hs_ref, wg_ref, wu_ref, wd_ref, ys_ref, h2d_scr = refs
    used = pl.program_id(0) < nt_ref[0]

    @pl.when(jnp.logical_not(used))
    def _():
        ys_ref[...] = jnp.zeros_like(ys_ref)

    @pl.when(used)
    def _():
        if cast:
            _cast_chunk((ng_ref, nu_ref, nd_ref), (ng_out_ref, nu_out_ref, nd_out_ref))
        h2d_scr[...] = hs_ref[...].reshape(MOE_TG, D_MODEL)
        h = h2d_scr[...].astype(BF16)
        y = None
        for lo in range(0, MOE_FF, MOE_FC):
            g = _dot(h, wg_ref[:, lo:lo + MOE_FC])
            u = _dot(h, wu_ref[:, lo:lo + MOE_FC])
            part = _dot((_silu(g) * u).astype(BF16), wd_ref[lo:lo + MOE_FC, :])
            y = part if y is None else y + part
        ys_ref[...] = y.reshape(MOE_TG, 1, D_MODEL)


def _experts(tile_expert, n_tiles, hs, w_gate, w_up, w_down, cast=None):
    rows = pl.BlockSpec((MOE_TG, 1, D_MODEL), lambda t, te, nt: (t, 0, 0))
    in_specs = [
        rows,
        pl.BlockSpec((None, D_MODEL, MOE_FF), lambda t, te, nt: (te[t], 0, 0)),
        pl.BlockSpec((None, D_MODEL, MOE_FF), lambda t, te, nt: (te[t], 0, 0)),
        pl.BlockSpec((None, MOE_FF, D_MODEL), lambda t, te, nt: (te[t], 0, 0)),
    ]
    out_specs = [rows]
    out_shape = [jax.ShapeDtypeStruct((MOE_ROWS, 1, D_MODEL), F32)]
    next_weights = ()
    if cast is not None:
        assert MOE_MIN_TILES >= N_EXPERTS * EXPERTS_CAST_PARTS
        cast_in, cast_out, cast_shape = _cast_specs(cast[0], EXPERTS_CAST_PARTS)
        in_specs += cast_in
        out_specs += cast_out
        out_shape += cast_shape
        next_weights = cast[1:]
    out = pl.pallas_call(
        functools.partial(_experts_kernel, cast=cast is not None),
        grid_spec=pltpu.PrefetchScalarGridSpec(
            num_scalar_prefetch=2,
            grid=(MOE_MAX_TILES,),
            in_specs=in_specs,
            out_specs=out_specs,
            scratch_shapes=[pltpu.VMEM((MOE_TG, D_MODEL), F32)],
        ),
        out_shape=out_shape,
        compiler_params=_params("arbitrary"),
        name="moe_experts",
    )(tile_expert, n_tiles, hs, w_gate, w_up, w_down, *next_weights)
    return out if cast is not None else out[0]


def _combine_kernel(*refs, final):
    if final:
        (d1_ref, d2_ref, x_ref, ada_ref, gw_ref, gf_ref, ys_ref, yp_ref, yl_ref,
         ya0, yb0, ya1, yb1, ya2d, yb2d, sem) = refs
    else:
        (d1_ref, d2_ref, x_ref, ada_ref, gw_ref, ys_ref, o_ref,
         ya0, yb0, ya1, yb1, ya2d, yb2d, sem) = refs
    i = pl.program_id(0)
    tm = x_ref.shape[0]

    def row_copies(step, r, buf_a, buf_b, slot):
        t = step * tm + r
        return (pltpu.make_async_copy(ys_ref.at[d1_ref[t]], buf_a.at[r], sem.at[slot]),
                pltpu.make_async_copy(ys_ref.at[d2_ref[t]], buf_b.at[r], sem.at[slot]))

    def start_tile(step, buf_a, buf_b, slot):
        def body(g, carry):
            for k in range(ISSUE_UNROLL):
                copies = row_copies(step, g * ISSUE_UNROLL + k, buf_a, buf_b, slot)
                for priority, cp in enumerate(copies):
                    cp.start(priority=priority)
            return carry
        lax.fori_loop(0, tm // ISSUE_UNROLL, body, 0)

    def wait_tile(buf_a, buf_b, slot):
        for buf in (buf_a, buf_b):
            pltpu.make_async_copy(ys_ref.at[pl.ds(0, tm)], buf, sem.at[slot]).wait()

    @pl.when(i == 0)
    def _():
        start_tile(0, ya0, yb0, 0)

    def run(cur, nxt):
        @pl.when(i + 1 < pl.num_programs(0))
        def _():
            start_tile(i + 1, *nxt)

        wait_tile(*cur)
        ya2d[...] = cur[0][...].reshape(tm, D_MODEL)
        yb2d[...] = cur[1][...].reshape(tm, D_MODEL)
        mix = gw_ref[:, 0:1] * ya2d[...] + gw_ref[:, 1:2] * yb2d[...]
        out = x_ref[...] + ada_ref[5:6, :] * mix
        if final:
            y = _rms(out) * gf_ref[...]

            @pl.when(i * tm < N_CTX)
            def _():
                yp_ref[...] = y

            @pl.when(i * tm >= N_CTX)
            def _():
                yl_ref[...] = y
        else:
            o_ref[...] = out

    @pl.when(i % 2 == 0)
    def _():
        run((ya0, yb0, 0), (ya1, yb1, 1))

    @pl.when(i % 2 == 1)
    def _():
        run((ya1, yb1, 1), (ya0, yb0, 0))


def _combine(d1, d2, x, ada, layer, gate_w, ys, final_g=None):
    tm = COMBINE_TM
    final = final_g is not None
    ctx_tiles = N_CTX // tm
    row_buf = pltpu.VMEM((tm, 1, D_MODEL), F32)
    in_specs = [
        pl.BlockSpec((tm, D_MODEL), lambda i, *_: (i, 0)),
        pl.BlockSpec((None, None, 6, D_MODEL),
                     lambda i, *_: (layer, _cond_of_tile(i, tm), 0, 0)),
        pl.BlockSpec((tm, LANES), lambda i, *_: (i, 0)),
    ]
    if final:
        in_specs.append(pl.BlockSpec((1, D_MODEL), lambda i, *_: (0, 0)))
        out_specs = [
            pl.BlockSpec((tm, D_MODEL), lambda i, *_: (jnp.minimum(i, ctx_tiles - 1), 0)),
            pl.BlockSpec((tm, D_MODEL), lambda i, *_: (jnp.maximum(i - ctx_tiles, 0), 0)),
        ]
        out_shape = [jax.ShapeDtypeStruct((N_CTX, D_MODEL), F32),
                     jax.ShapeDtypeStruct((N_LAT, D_MODEL), F32)]
        args = (d1, d2, x, ada, gate_w, final_g, ys)
    else:
        out_specs = pl.BlockSpec((tm, D_MODEL), lambda i, *_: (i, 0))
        out_shape = jax.ShapeDtypeStruct((N_TOK, D_MODEL), F32)
        args = (d1, d2, x, ada, gate_w, ys)
    in_specs.append(pl.BlockSpec(memory_space=pl.ANY))
    return pl.pallas_call(
        functools.partial(_combine_kernel, final=final),
        grid_spec=pltpu.PrefetchScalarGridSpec(
            num_scalar_prefetch=2,
            grid=(N_TOK // tm,),
            in_specs=in_specs,
            out_specs=out_specs,
            scratch_shapes=[row_buf, row_buf, row_buf, row_buf,
                            pltpu.VMEM((tm, D_MODEL), F32), pltpu.VMEM((tm, D_MODEL), F32),
                            pltpu.SemaphoreType.DMA((2,))],
        ),
        out_shape=out_shape,
        compiler_params=_params("arbitrary"),
        name="moe_combine",
    )(*args)


def _moe_layer(x, ada, layer, m, g2, w_router, w_gate, w_up, w_down, weights_bf, final_g):
    h_rows, route, gate_w, counts = _route(x, ada, layer, g2, w_router)
    cnt = counts[0, :N_EXPERTS].astype(jnp.int32)
    tiles_per_expert = (cnt + MOE_TG - 1) // MOE_TG
    tile_end = jnp.cumsum(tiles_per_expert)
    row_start = (tile_end - tiles_per_expert) * MOE_TG
    n_tiles = tile_end[-1:]
    t = jnp.minimum(jnp.arange(MOE_MAX_TILES, dtype=jnp.int32), n_tiles - 1)
    tile_expert = jnp.sum((t[:, None] >= tile_end[None, :]).astype(jnp.int32), axis=1)
    experts = jnp.arange(N_EXPERTS, dtype=jnp.int32)
    d1 = jnp.sum(jnp.where(route[0][:, None] == experts, row_start, 0), axis=1) + route[2]
    d2 = jnp.sum(jnp.where(route[1][:, None] == experts, row_start, 0), axis=1) + route[3]
    spare_tile = jnp.minimum(n_tiles + experts, MOE_MAX_TILES - 1)
    fill_start = jnp.concatenate([row_start + cnt, spare_tile * MOE_TG])
    fill_rows = jnp.concatenate([tiles_per_expert * MOE_TG - cnt,
                                 jnp.where(n_tiles + experts < MOE_MAX_TILES, MOE_TG, 0)])
    hs = _dispatch(d1, d2, fill_start, fill_rows, h_rows)
    next_bf = None
    if m + 1 < w_gate.shape[0]:
        ys, *next_bf = _experts(tile_expert, n_tiles, hs, *weights_bf,
                                (m + 1, w_gate, w_up, w_down))
    else:
        ys = _experts(tile_expert, n_tiles, hs, *weights_bf)
    return _combine(d1, d2, x, ada, layer, gate_w, ys, final_g), next_bf


assert DEPTH % 2 == 0


def _rope_tables(dim):
    half = dim // 2
    quarter = half // 2
    t = np.arange(DEC_SEQ)
    pos = np.stack([t // GRID_W, t % GRID_W], axis=1).astype(np.float32)
    freqs = (ROPE_THETA ** (-np.arange(quarter, dtype=np.float32) / quarter)).astype(np.float32)
    ang = pos[:, :, None] * freqs[None, None, :]
    cos = np.cos(ang.astype(np.float64))
    sin = np.sin(ang.astype(np.float64))
    cos_t = np.concatenate([cos, cos], axis=-1).reshape(DEC_SEQ, dim)
    sin_t = np.concatenate([-sin, sin], axis=-1).reshape(DEC_SEQ, dim)
    lane = np.arange(dim)
    swap = np.where(lane % half < quarter, lane + quarter, lane - quarter)
    return cos_t.astype(np.float32), sin_t.astype(np.float32), swap


def kernel(x_prompt, x_sample, cache_mla_ckv, cache_mla_kpe, cache_gqa_k, cache_gqa_v, c, c_ctx, w_ada, b_ada, norm1_g, norm2_g, conv_w_in, conv_k, conv_w_out, mla_w_down, mla_q_norm_g, mla_kv_norm_g, mla_w_uq, mla_w_ukv, mla_w_o, gqa_w_qkv, gqa_q_norm_g, gqa_k_norm_g, gqa_w_o, ffn_w_gate, ffn_w_up, ffn_w_down, moe_w_router, moe_w_gate, moe_w_up, moe_w_down, final_norm_g):
    xs = (x_prompt.reshape(N_CTX, D_MODEL), x_sample.reshape(N_LAT, D_MODEL))
    conv_w_in, conv_w_out = conv_w_in.astype(BF16), conv_w_out.astype(BF16)
    ffn_w_gate, ffn_w_up, ffn_w_down = (w.astype(BF16) for w in (ffn_w_gate, ffn_w_up, ffn_w_down))
    cond = jnp.concatenate(
        [c_ctx[None, :], c, jnp.zeros((N_COND - 1 - DEC_BATCH, D_MODEL), F32)], axis=0)
    ada = _adaln_table(cond, w_ada, b_ada)

    new_ckv, new_kpe, new_k, new_v = [], [], [], []
    moe_bf = None
    for i in range(DEPTH):
        j = i // N_MIXERS
        kind = i % N_MIXERS
        g1 = norm1_g[i][None, :]
        if kind == 0:
            x = _conv_layer(xs if i == 0 else (x,), ada, i, j, g1, conv_w_in, conv_k, conv_w_out)
        elif kind == 1:
            cos64, sin64, swap64 = _rope_tables(MLA_ROPE)
            wd = mla_w_down[j]
            kpe_cols = wd[:, MLA_Q_RANK + MLA_KV_RANK:]
            w_down = jnp.concatenate([wd, kpe_cols[:, swap64]], axis=1).astype(BF16)
            wq = mla_w_uq[j].reshape(MLA_Q_RANK, MLA_HEADS, MLA_QK)
            wq_nope = wq[:, :, :MLA_NOPE].reshape(MLA_Q_RANK, MLA_Q_NOPE_COLS)
            wq_rope = wq[:, :, MLA_NOPE:]
            w_uq = jnp.concatenate(
                [wq_nope, wq_rope.reshape(MLA_Q_RANK, MLA_Q_ROPE_COLS),
                 wq_rope[:, :, swap64].reshape(MLA_Q_RANK, MLA_Q_ROPE_COLS)], axis=1).astype(BF16)
            wkv = mla_w_ukv[j].reshape(MLA_KV_RANK, MLA_HEADS, MLA_NOPE + MLA_V)
            w_ukv = jnp.concatenate(
                [wkv[:, :, :MLA_NOPE].reshape(MLA_KV_RANK, MLA_Q_NOPE_COLS),
                 wkv[:, :, MLA_NOPE:].reshape(MLA_KV_RANK, MLA_HEADS * MLA_V)], axis=1).astype(BF16)
            rope = (jnp.asarray(np.tile(cos64, (1, MLA_HEADS))), jnp.asarray(np.tile(sin64, (1, MLA_HEADS))),
                    jnp.asarray(cos64), jnp.asarray(sin64))
            x, ckv_p, kpe_p = _mla_layer(
                x, ada, i, g1, w_down, mla_q_norm_g[j][None, :], mla_kv_norm_g[j][None, :],
                w_uq, w_ukv, mla_w_o[j].astype(BF16), cache_mla_ckv[:, j], cache_mla_kpe[:, j], rope)
            new_ckv.append(ckv_p.reshape(BATCH, SEQ, MLA_KV_RANK))
            new_kpe.append(kpe_p.reshape(BATCH, SEQ, MLA_ROPE))
        else:
            cos128, sin128, swap128 = _rope_tables(GQA_HEAD_DIM)
            wqkv = gqa_w_qkv[j]
            wq = wqkv[:, :GQA_Q_COLS].reshape(D_MODEL, GQA_HEADS, GQA_HEAD_DIM)
            wk = wqkv[:, GQA_Q_COLS:GQA_Q_COLS + GQA_KV_COLS].reshape(D_MODEL, GQA_KV_HEADS, GQA_HEAD_DIM)
            w_qkv_ctx = wqkv.astype(BF16)
            w_qkv_lat = jnp.concatenate(
                [wqkv, wq[:, :, swap128].reshape(D_MODEL, GQA_Q_COLS),
                 wk[:, :, swap128].reshape(D_MODEL, GQA_KV_COLS)], axis=1).astype(BF16)
            gq = gqa_q_norm_g[j]
            gk = gqa_k_norm_g[j]
            rope = (jnp.asarray(np.tile(cos128, (1, GQA_HEADS))), jnp.asarray(np.tile(sin128, (1, GQA_HEADS))),
                    jnp.asarray(np.tile(cos128, (1, GQA_KV_HEADS))), jnp.asarray(np.tile(sin128, (1, GQA_KV_HEADS))))
            x, k_p, v_p = _gqa_layer(
                x, ada, i, g1, w_qkv_ctx, w_qkv_lat,
                jnp.tile(gq, GQA_HEADS)[None, :], jnp.tile(gk, GQA_KV_HEADS)[None, :],
                jnp.tile(gq[swap128], GQA_HEADS)[None, :], jnp.tile(gk[swap128], GQA_KV_HEADS)[None, :],
                gqa_w_o[j].astype(BF16),
                cache_gqa_k[:, j].reshape(DEC_BATCH, PAST_LEN, GQA_KV_COLS),
                cache_gqa_v[:, j].reshape(DEC_BATCH, PAST_LEN, GQA_KV_COLS), rope)
            new_k.append(k_p.reshape(BATCH, SEQ, GQA_KV_HEADS, GQA_HEAD_DIM))
            new_v.append(v_p.reshape(BATCH, SEQ, GQA_KV_HEADS, GQA_HEAD_DIM))

        m = i // 2
        g2 = norm2_g[i][None, :]
        if i % 2 == 0:
            cast = (m, moe_w_gate, moe_w_up, moe_w_down) if moe_bf is None else None
            x, prepared = _ffn_layer(x, ada, i, m, g2, ffn_w_gate, ffn_w_up, ffn_w_down, cast)
            moe_bf = prepared if cast is not None else moe_bf
        else:
            w_router = jnp.pad(moe_w_router[m], ((0, 0), (0, LANES - N_EXPERTS)))
            final_g = final_norm_g[None, :] if i == DEPTH - 1 else None
            x, moe_bf = _moe_layer(x, ada, i, m, g2, w_router, moe_w_gate, moe_w_up, moe_w_down,
                                   moe_bf, final_g)

    y_prompt, y_sample = x
    y_prompt = y_prompt.reshape(BATCH, SEQ, D_MODEL)
    y_sample = y_sample.reshape(DEC_BATCH, DEC_SEQ, D_MODEL)
    return (y_prompt, y_sample,
            jnp.stack(new_ckv, axis=1), jnp.stack(new_kpe, axis=1),
            jnp.stack(new_k, axis=1), jnp.stack(new_v, axis=1))
```

```python
import functools

import numpy as np
import jax
import jax.numpy as jnp
from jax import lax
from jax.experimental import pallas as pl
from jax.experimental.pallas import tpu as pltpu

D_MODEL = 1024
BATCH = 32
SEQ = 256
DEPTH = 4
DEC_BATCH = 2
DEC_SEQ = 1024
PAST_LEN = 512
GRID_W = 64
N_MIXERS = 3
CONV_WIDTH = 3
MLA_HEADS = 8
MLA_NOPE = 128
MLA_ROPE = 64
MLA_V = 128
MLA_Q_RANK = 384
MLA_KV_RANK = 256
GQA_HEADS = 8
GQA_KV_HEADS = 2
GQA_HEAD_DIM = 128
D_FF = 2816
N_EXPERTS = 8
MOE_FF = 2048
ROPE_THETA = 10000.0
EPS = 1e-6

N_CTX = BATCH * SEQ
N_LAT = DEC_BATCH * DEC_SEQ
N_TOK = N_CTX + N_LAT
N_COND = 8
LANES = 128
SUBLANES = 8
ADALN_TN = 1536
Q_TILE = 256
VMEM_LIMIT = 56 * 1024 * 1024

F32 = jnp.float32
BF16 = jnp.bfloat16


def _dot(a, b):
    return jnp.dot(a, b, preferred_element_type=F32)


def _dot_t(a, b):
    return lax.dot_general(a, b, (((1,), (1,)), ((), ())), preferred_element_type=F32)


def _rms(x):
    return x * lax.rsqrt(jnp.mean(x * x, axis=-1, keepdims=True) + EPS)


def _modulate(x, g, shift, scale):
    return _rms(x) * g * (1.0 + scale) + shift


def _silu(x):
    return x * jax.nn.sigmoid(x)


def _cond_of_tile(i, tm):
    start = i * tm
    return jnp.where(start < N_CTX, 0, 1 + (start - N_CTX) // DEC_SEQ)


def _resident(shape):
    return pl.BlockSpec(shape, lambda *_: (0,) * len(shape), pipeline_mode=pl.Buffered(1))


def _resident_at(index, shape):
    return pl.BlockSpec((None,) + tuple(shape), lambda *_: (index,) + (0,) * len(shape),
                        pipeline_mode=pl.Buffered(1))


def _params(*sem):
    return pltpu.CompilerParams(dimension_semantics=sem, vmem_limit_bytes=VMEM_LIMIT)


def _adaln_kernel(cond_ref, w_ref, b_ref, o_ref):
    s = _silu(cond_ref[...]).astype(BF16)
    o_ref[...] = _dot(s, w_ref[...].astype(BF16)) + b_ref[...]


def _adaln_table(cond, w_ada, b_ada):
    tn = ADALN_TN
    out = pl.pallas_call(
        _adaln_kernel,
        grid=(DEPTH, 6 * D_MODEL // tn),
        in_specs=[
            pl.BlockSpec((N_COND, D_MODEL), lambda l, j: (0, 0)),
            pl.BlockSpec((None, D_MODEL, tn), lambda l, j: (l, 0, j)),
            pl.BlockSpec((None, 1, tn), lambda l, j: (l, 0, j)),
        ],
        out_specs=pl.BlockSpec((None, N_COND, tn), lambda l, j: (l, 0, j)),
        out_shape=jax.ShapeDtypeStruct((DEPTH, N_COND, 6 * D_MODEL), F32),
        compiler_params=_params("arbitrary", "arbitrary"),
        name="adaln_table",
    )(cond, w_ada, b_ada.reshape(DEPTH, 1, 6 * D_MODEL))
    return out.reshape(DEPTH, N_COND, 6, D_MODEL)


def _ada_spec(layer, tm):
    return pl.BlockSpec((None, None, 6, D_MODEL), lambda i: (layer, _cond_of_tile(i, tm), 0, 0))


def _ada_spec_at(layer, cond_fn):
    return pl.BlockSpec((None, None, 6, D_MODEL), lambda i: (layer, cond_fn(i), 0, 0))


CONV_TM = 1024
CONV_CC = 256


def _conv_kernel(*refs, split):
    if split:
        xp_ref, x_ref, ada_ref, g1_ref, win_ref, ck_ref, wout_ref, o_ref, v_scr = refs
    else:
        x_ref, ada_ref, g1_ref, win_ref, ck_ref, wout_ref, o_ref, v_scr = refs
    tm = x_ref.shape[0]
    i = pl.program_id(0)
    x = x_ref[...]
    if split:
        x = jnp.where(i * tm < N_CTX, xp_ref[...], x)
    h = _modulate(x, g1_ref[...], ada_ref[0:1, :], ada_ref[1:2, :]).astype(BF16)
    period = jnp.where(i * tm < N_CTX, SEQ, DEC_SEQ)
    pos = lax.broadcasted_iota(jnp.int32, (tm, 1), 0) & (period - 1)
    first = pos == 0
    last = pos == period - 1
    for j in range(D_MODEL // CONV_CC):
        lo = j * CONV_CC
        b_gate = _dot(h, win_ref[:, lo:lo + CONV_CC])
        c_gate = _dot(h, win_ref[:, D_MODEL + lo:D_MODEL + lo + CONV_CC])
        x_in = _dot(h, win_ref[:, 2 * D_MODEL + lo:2 * D_MODEL + lo + CONV_CC])
        u = c_gate * x_in
        u_prev = jnp.where(first, 0.0, pltpu.roll(u, 1, 0))
        u_next = jnp.where(last, 0.0, pltpu.roll(u, tm - 1, 0))
        conv = (ck_ref[0:1, lo:lo + CONV_CC] * u_prev + ck_ref[1:2, lo:lo + CONV_CC] * u
                + ck_ref[2:3, lo:lo + CONV_CC] * u_next)
        v_scr[:, lo:lo + CONV_CC] = (b_gate * conv).astype(BF16)
    o_ref[...] = x + ada_ref[2:3, :] * _dot(v_scr[...], wout_ref[...])


def _conv_layer(xs, ada, layer, j, g1, w_in, conv_k, w_out):
    tm = CONV_TM
    ctx_tiles = N_CTX // tm
    if len(xs) == 1:
        x_specs = [pl.BlockSpec((tm, D_MODEL), lambda i: (i, 0))]
    else:
        x_specs = [pl.BlockSpec((tm, D_MODEL), lambda i: (jnp.minimum(i, ctx_tiles - 1), 0)),
                   pl.BlockSpec((tm, D_MODEL), lambda i: (jnp.maximum(i - ctx_tiles, 0), 0))]
    return pl.pallas_call(
        functools.partial(_conv_kernel, split=len(xs) == 2),
        grid=(N_TOK // tm,),
        in_specs=x_specs + [
            _ada_spec(layer, tm),
            _resident((1, D_MODEL)),
            _resident_at(j, (D_MODEL, 3 * D_MODEL)),
            _resident_at(j, (CONV_WIDTH, D_MODEL)),
            _resident_at(j, (D_MODEL, D_MODEL)),
        ],
        out_specs=pl.BlockSpec((tm, D_MODEL), lambda i: (i, 0)),
        out_shape=jax.ShapeDtypeStruct((N_TOK, D_MODEL), F32),
        scratch_shapes=[pltpu.VMEM((tm, D_MODEL), BF16)],
        compiler_params=_params("arbitrary"),
        name=f"conv_mixer_{layer}",
    )(*xs, ada, g1, w_in, conv_k, w_out)


FFN_TM = 512
FFN_CAST_TM = 256
FFN_CAST_PARTS = 4
FFN_CHUNKS = ((0, 1536), (1536, D_FF))


def _ffn_kernel(*refs, cast):
    if cast:
        (x_ref, ada_ref, g2_ref, wg_ref, wu_ref, wd_ref, eg_ref, eu_ref, ed_ref,
         o_ref, eg_out_ref, eu_out_ref, ed_out_ref) = refs
        _cast_chunk((eg_ref, eu_ref, ed_ref), (eg_out_ref, eu_out_ref, ed_out_ref))
    else:
        x_ref, ada_ref, g2_ref, wg_ref, wu_ref, wd_ref, o_ref = refs
    x = x_ref[...]
    h = _modulate(x, g2_ref[...], ada_ref[3:4, :], ada_ref[4:5, :]).astype(BF16)
    f = None
    for lo, hi in FFN_CHUNKS:
        g = _dot(h, wg_ref[:, lo:hi])
        u = _dot(h, wu_ref[:, lo:hi])
        part = _dot((_silu(g) * u).astype(BF16), wd_ref[lo:hi, :])
        f = part if f is None else f + part
    o_ref[...] = x + ada_ref[5:6, :] * f


def _ffn_layer(x, ada, layer, m, g2, w_gate, w_up, w_down, cast=None):
    tm = FFN_TM if cast is None else FFN_CAST_TM
    in_specs = [
        pl.BlockSpec((tm, D_MODEL), lambda i: (i, 0)),
        _ada_spec(layer, tm),
        _resident((1, D_MODEL)),
        _resident_at(m, (D_MODEL, D_FF)),
        _resident_at(m, (D_MODEL, D_FF)),
        _resident_at(m, (D_FF, D_MODEL)),
    ]
    out_specs = [pl.BlockSpec((tm, D_MODEL), lambda i: (i, 0))]
    out_shape = [jax.ShapeDtypeStruct((N_TOK, D_MODEL), F32)]
    expert_weights = ()
    if cast is not None:
        assert N_TOK // tm >= N_EXPERTS * FFN_CAST_PARTS
        cast_in, cast_out, cast_shape = _cast_specs(cast[0], FFN_CAST_PARTS)
        in_specs += cast_in
        out_specs += cast_out
        out_shape += cast_shape
        expert_weights = cast[1:]
    out = pl.pallas_call(
        functools.partial(_ffn_kernel, cast=cast is not None),
        grid=(N_TOK // tm,),
        in_specs=in_specs,
        out_specs=out_specs,
        out_shape=out_shape,
        compiler_params=_params("arbitrary"),
        name=f"dense_ffn_{layer}",
    )(x, ada, g2, w_gate, w_up, w_down, *expert_weights)
    return (out[0], out[1:]) if cast is not None else (out[0], None)


def _attention(qk_parts, v_ref, heads, kv_of_head, dv, scale, o_scr, finish, keys_of, stage):
    def scores(rows, keys, h):
        s = None
        for q_ref, k_ref, width, k_shared in qk_parts:
            kb = 0 if k_shared else kv_of_head(h)
            part = _dot_t(q_ref[rows, h * width:(h + 1) * width],
                          k_ref[keys, kb * width:(kb + 1) * width])
            s = part if s is None else s + part
        return s * scale

    def q_block(rows):
        s_scr, p_scr = stage
        group = s_scr.shape[0]
        keys = keys_of(rows)
        for h0 in range(0, heads, group):
            for j in range(group):
                s_scr[j] = scores(rows, keys, h0 + j)
            for j in range(group):
                s = s_scr[j]
                e = jnp.exp(s - jnp.max(s, axis=-1, keepdims=True))
                p_scr[j] = (e * (1.0 / jnp.sum(e, axis=-1, keepdims=True))).astype(BF16)
            for j in range(group):
                h = h0 + j
                g = kv_of_head(h)
                o = _dot(p_scr[j], v_ref[keys, g * dv:(g + 1) * dv])
                o_scr[rows, h * dv:(h + 1) * dv] = o.astype(BF16)
        finish(rows)

    _for_row_blocks(o_scr.shape[0], q_block)


def _stage_scratch(group, keys):
    return [pltpu.VMEM((group, Q_TILE, keys), F32), pltpu.VMEM((group, Q_TILE, keys), BF16)]


LATENT_KEYS = DEC_SEQ + PAST_LEN
LATENT_STAGE_HEADS = 4


def _keys_of(latent):
    assert SEQ == Q_TILE
    return (lambda rows: slice(None)) if latent else (lambda rows: rows)


def _for_row_blocks(n_rows, fn):
    if n_rows <= 2 * Q_TILE:
        for b in range(n_rows // Q_TILE):
            fn(pl.ds(b * Q_TILE, Q_TILE))
    else:
        def body(b, carry):
            fn(pl.ds(pl.multiple_of(b * Q_TILE, Q_TILE), Q_TILE))
            return carry
        lax.fori_loop(0, n_rows // Q_TILE, body, 0)


MLA_QK = MLA_NOPE + MLA_ROPE
MLA_DOWN_COLS = MLA_Q_RANK + MLA_KV_RANK + 2 * MLA_ROPE
MLA_Q_NOPE_COLS = MLA_HEADS * MLA_NOPE
MLA_Q_ROPE_COLS = MLA_HEADS * MLA_ROPE


def _ctx_then_latent(body, n_in, refs):
    ins, lat_ref, rest = refs[:n_in], refs[n_in], refs[n_in + 1:]
    o_ref = rest[0]
    i = pl.program_id(0)

    @pl.when(i < CTX_TILES)
    def _():
        body(*ins, *rest, latent=False)

    @pl.when(i >= CTX_TILES)
    def _():
        o_ref[...] = lat_ref[...]


def _ctx_specs(out_tails):
    def tile(n_trailing):
        return lambda i: (jnp.minimum(i, CTX_TILES - 1),) + (0,) * n_trailing
    x_spec = pl.BlockSpec((CTX_TM, D_MODEL), tile(1))
    lat_spec = pl.BlockSpec((CTX_TM, D_MODEL), lambda i: (jnp.maximum(i - CTX_TILES, 0), 0))
    out_specs = [pl.BlockSpec((CTX_TM, D_MODEL), lambda i: (i, 0))]
    out_specs += [pl.BlockSpec((CTX_TM,) + tail, tile(len(tail))) for tail in out_tails]
    return x_spec, lat_spec, out_specs


CTX_TM = 2 * SEQ
CTX_TILES = N_CTX // CTX_TM
CTX_STEPS = CTX_TILES + N_LAT // CTX_TM


def _mla_kernel(*refs, latent):
    if latent:
        _mla_body(*refs, latent=True)
    else:
        _ctx_then_latent(_mla_body, 9, refs)


def _mla_body(*refs, latent):
    if latent:
        (x_ref, ada_ref, g1_ref, wdown_ref, gq_ref, gkv_ref, wuq_ref, wukv_ref, wo_ref,
         cosq_ref, sinq_ref, cosk_ref, sink_ref, cckv_ref, ckpe_ref,
         o_ref, qn_scr, qr_scr, kn_scr, kr_scr, v_scr, o_scr, *stage) = refs
    else:
        (x_ref, ada_ref, g1_ref, wdown_ref, gq_ref, gkv_ref, wuq_ref, wukv_ref, wo_ref,
         o_ref, ckv_out_ref, kpe_out_ref, qn_scr, qr_scr, kn_scr, kr_scr, v_scr, o_scr,
         *stage) = refs
    tm = x_ref.shape[0]
    kpe_lo = MLA_Q_RANK + MLA_KV_RANK

    def project(rows):
        h = _modulate(x_ref[rows, :], g1_ref[...], ada_ref[0:1, :], ada_ref[1:2, :]).astype(BF16)
        down = _dot(h, wdown_ref[...])
        c_q = down[:, :MLA_Q_RANK]
        c_kv = down[:, MLA_Q_RANK:kpe_lo]
        kpe = down[:, kpe_lo:kpe_lo + MLA_ROPE]
        q = _dot((_rms(c_q) * gq_ref[...]).astype(BF16), wuq_ref[...])
        ckv_n = _rms(c_kv) * gkv_ref[...]
        kv = _dot(ckv_n.astype(BF16), wukv_ref[...])
        qn_scr[rows, :] = q[:, :MLA_Q_NOPE_COLS].astype(BF16)
        q_rope = q[:, MLA_Q_NOPE_COLS:MLA_Q_NOPE_COLS + MLA_Q_ROPE_COLS]
        kn_scr[rows, :] = kv[:, :MLA_Q_NOPE_COLS].astype(BF16)
        v_scr[rows, :] = kv[:, MLA_Q_NOPE_COLS:].astype(BF16)
        if latent:
            q_swap = q[:, MLA_Q_NOPE_COLS + MLA_Q_ROPE_COLS:]
            q_rope = q_rope * cosq_ref[rows, :] + q_swap * sinq_ref[rows, :]
            kpe_swap = down[:, kpe_lo + MLA_ROPE:kpe_lo + 2 * MLA_ROPE]
            kpe = kpe * cosk_ref[rows, :] + kpe_swap * sink_ref[rows, :]
        else:
            ckv_out_ref[rows, :] = ckv_n
            kpe_out_ref[rows, :] = kpe
        qr_scr[rows, :] = q_rope.astype(BF16)
        kr_scr[rows, :] = kpe.astype(BF16)

    _for_row_blocks(tm, project)

    if latent:
        def expand_cache(rows):
            kv_c = _dot(cckv_ref[rows, :].astype(BF16), wukv_ref[...])
            dst = pl.ds(tm + rows.start, Q_TILE)
            kn_scr[dst, :] = kv_c[:, :MLA_Q_NOPE_COLS].astype(BF16)
            v_scr[dst, :] = kv_c[:, MLA_Q_NOPE_COLS:].astype(BF16)
            kr_scr[dst, :] = ckpe_ref[rows, :].astype(BF16)

        _for_row_blocks(PAST_LEN, expand_cache)

    def finish(rows):
        o_ref[rows, :] = x_ref[rows, :] + ada_ref[2:3, :] * _dot(o_scr[rows, :], wo_ref[...])

    _attention([(qn_scr, kn_scr, MLA_NOPE, False), (qr_scr, kr_scr, MLA_ROPE, True)],
               v_scr, MLA_HEADS, lambda hd: hd, MLA_V, MLA_QK ** -0.5, o_scr, finish,
               _keys_of(latent), stage)


def _mla_scratch(tm, sk):
    return [
        pltpu.VMEM((tm, MLA_Q_NOPE_COLS), BF16),
        pltpu.VMEM((tm, MLA_Q_ROPE_COLS), BF16),
        pltpu.VMEM((sk, MLA_Q_NOPE_COLS), BF16),
        pltpu.VMEM((sk, MLA_ROPE), BF16),
        pltpu.VMEM((sk, MLA_HEADS * MLA_V), BF16),
        pltpu.VMEM((tm, MLA_HEADS * MLA_V), BF16),
    ]


def _mla_layer(x, ada, layer, g1, w_down, g_q, g_kv, w_uq, w_ukv, w_o,
               cache_ckv, cache_kpe, rope):
    n_uq = w_uq.shape[1]
    weights = [
        _resident((1, D_MODEL)),
        _resident((D_MODEL, MLA_DOWN_COLS)),
        _resident((1, MLA_Q_RANK)),
        _resident((1, MLA_KV_RANK)),
        _resident((MLA_Q_RANK, n_uq)),
        _resident((MLA_KV_RANK, 2 * MLA_Q_NOPE_COLS)),
        _resident((MLA_HEADS * MLA_V, D_MODEL)),
    ]
    ctx_weights = list(weights)
    ctx_weights[4] = pl.BlockSpec((MLA_Q_RANK, MLA_Q_NOPE_COLS + MLA_Q_ROPE_COLS), lambda *_: (0, 0),
                                  pipeline_mode=pl.Buffered(1))
    weight_args = (g1, w_down, g_q, g_kv, w_uq, w_ukv, w_o)
    cos_q, sin_q, cos_k, sin_k = rope
    ctx_blocks = N_CTX // DEC_SEQ
    lat_out = pl.pallas_call(
        functools.partial(_mla_kernel, latent=True),
        grid=(DEC_BATCH,),
        in_specs=[pl.BlockSpec((DEC_SEQ, D_MODEL), lambda b: (ctx_blocks + b, 0)),
                  _ada_spec_at(layer, lambda b: 1 + b)] + weights + [
            _resident((DEC_SEQ, MLA_Q_ROPE_COLS)),
            _resident((DEC_SEQ, MLA_Q_ROPE_COLS)),
            _resident((DEC_SEQ, MLA_ROPE)),
            _resident((DEC_SEQ, MLA_ROPE)),
            pl.BlockSpec((None, PAST_LEN, MLA_KV_RANK), lambda b: (b, 0, 0)),
            pl.BlockSpec((None, PAST_LEN, MLA_ROPE), lambda b: (b, 0, 0)),
        ],
        out_specs=pl.BlockSpec((DEC_SEQ, D_MODEL), lambda b: (b, 0)),
        out_shape=jax.ShapeDtypeStruct((N_LAT, D_MODEL), F32),
        scratch_shapes=(_mla_scratch(DEC_SEQ, LATENT_KEYS)
                        + _stage_scratch(LATENT_STAGE_HEADS, LATENT_KEYS)),
        compiler_params=_params("arbitrary"),
        name=f"mla_latent_{layer}",
    )(x, ada, *weight_args, cos_q, sin_q, cos_k, sin_k, cache_ckv, cache_kpe)

    x_spec, lat_spec, out_specs = _ctx_specs(((MLA_KV_RANK,), (MLA_ROPE,)))
    return pl.pallas_call(
        functools.partial(_mla_kernel, latent=False),
        grid=(CTX_STEPS,),
        in_specs=[x_spec, _ada_spec_at(layer, lambda i: 0)] + ctx_weights + [lat_spec],
        out_specs=out_specs,
        out_shape=[
            jax.ShapeDtypeStruct((N_TOK, D_MODEL), F32),
            jax.ShapeDtypeStruct((N_CTX, MLA_KV_RANK), F32),
            jax.ShapeDtypeStruct((N_CTX, MLA_ROPE), F32),
        ],
        scratch_shapes=_mla_scratch(CTX_TM, CTX_TM) + _stage_scratch(MLA_HEADS, SEQ),
        compiler_params=_params("arbitrary"),
        name=f"mla_ctx_{layer}",
    )(x, ada, *weight_args, lat_out)


GQA_Q_COLS = GQA_HEADS * GQA_HEAD_DIM
GQA_KV_COLS = GQA_KV_HEADS * GQA_HEAD_DIM
GQA_REP = GQA_HEADS // GQA_KV_HEADS


def _head_rms(t, n_heads):
    cols = []
    for hd in range(n_heads):
        th = t[:, hd * GQA_HEAD_DIM:(hd + 1) * GQA_HEAD_DIM]
        r = lax.rsqrt(jnp.mean(th * th, axis=-1, keepdims=True) + EPS)
        cols.append(jnp.broadcast_to(r, th.shape))
    return jnp.concatenate(cols, axis=-1)


def _gqa_kernel(*refs, latent):
    if latent:
        _gqa_body(*refs, latent=True)
    else:
        _ctx_then_latent(_gqa_body, 7, refs)


def _gqa_body(*refs, latent):
    if latent:
        (x_ref, ada_ref, g1_ref, wqkv_ref, gq_ref, gk_ref, wo_ref,
         gqs_ref, gks_ref, cosq_ref, sinq_ref, cosk_ref, sink_ref, ck_ref, cv_ref,
         o_ref, q_scr, k_scr, v_scr, o_scr, *stage) = refs
    else:
        (x_ref, ada_ref, g1_ref, wqkv_ref, gq_ref, gk_ref, wo_ref,
         o_ref, k_out_ref, v_out_ref, q_scr, k_scr, v_scr, o_scr, *stage) = refs
    tm = x_ref.shape[0]

    def project(rows):
        h = _modulate(x_ref[rows, :], g1_ref[...], ada_ref[0:1, :], ada_ref[1:2, :]).astype(BF16)
        qkv = _dot(h, wqkv_ref[...])
        q_raw = qkv[:, :GQA_Q_COLS]
        k_raw = qkv[:, GQA_Q_COLS:GQA_Q_COLS + GQA_KV_COLS]
        v = qkv[:, GQA_Q_COLS + GQA_KV_COLS:GQA_Q_COLS + 2 * GQA_KV_COLS]
        q_r = _head_rms(q_raw, GQA_HEADS)
        k_r = _head_rms(k_raw, GQA_KV_HEADS)
        q = q_raw * q_r * gq_ref[...]
        k = k_raw * k_r * gk_ref[...]
        if latent:
            lo = GQA_Q_COLS + 2 * GQA_KV_COLS
            q_swap = qkv[:, lo:lo + GQA_Q_COLS] * q_r * gqs_ref[...]
            k_swap = qkv[:, lo + GQA_Q_COLS:lo + GQA_Q_COLS + GQA_KV_COLS] * k_r * gks_ref[...]
            q = q * cosq_ref[rows, :] + q_swap * sinq_ref[rows, :]
            k = k * cosk_ref[rows, :] + k_swap * sink_ref[rows, :]
        else:
            for g in range(GQA_KV_HEADS):
                cols = slice(g * GQA_HEAD_DIM, (g + 1) * GQA_HEAD_DIM)
                k_out_ref[rows, g, :] = k[:, cols]
                v_out_ref[rows, g, :] = v[:, cols]
        q_scr[rows, :] = q.astype(BF16)
        k_scr[rows, :] = k.astype(BF16)
        v_scr[rows, :] = v.astype(BF16)

    _for_row_blocks(tm, project)

    if latent:
        k_scr[tm:, :] = ck_ref[...].astype(BF16)
        v_scr[tm:, :] = cv_ref[...].astype(BF16)

    def finish(rows):
        o_ref[rows, :] = x_ref[rows, :] + ada_ref[2:3, :] * _dot(o_scr[rows, :], wo_ref[...])

    _attention([(q_scr, k_scr, GQA_HEAD_DIM, False)], v_scr, GQA_HEADS,
               lambda hd: hd // GQA_REP, GQA_HEAD_DIM, GQA_HEAD_DIM ** -0.5, o_scr, finish,
               _keys_of(latent), stage)


def _gqa_scratch(tm, sk):
    return [
        pltpu.VMEM((tm, GQA_Q_COLS), BF16),
        pltpu.VMEM((sk, GQA_KV_COLS), BF16),
        pltpu.VMEM((sk, GQA_KV_COLS), BF16),
        pltpu.VMEM((tm, GQA_Q_COLS), BF16),
    ]


def _gqa_layer(x, ada, layer, g1, w_qkv_ctx, w_qkv_lat, g_q, g_k, g_q_swap, g_k_swap, w_o,
               cache_k, cache_v, rope):
    cos_q, sin_q, cos_k, sin_k = rope
    ctx_blocks = N_CTX // DEC_SEQ
    lat_out = pl.pallas_call(
        functools.partial(_gqa_kernel, latent=True),
        grid=(DEC_BATCH,),
        in_specs=[pl.BlockSpec((DEC_SEQ, D_MODEL), lambda b: (ctx_blocks + b, 0)),
                  _ada_spec_at(layer, lambda b: 1 + b),
                  _resident((1, D_MODEL)),
                  _resident(w_qkv_lat.shape),
                  _resident((1, GQA_Q_COLS)),
                  _resident((1, GQA_KV_COLS)),
                  _resident((GQA_Q_COLS, D_MODEL)),
                  _resident((1, GQA_Q_COLS)),
                  _resident((1, GQA_KV_COLS)),
                  _resident((DEC_SEQ, GQA_Q_COLS)),
                  _resident((DEC_SEQ, GQA_Q_COLS)),
                  _resident((DEC_SEQ, GQA_KV_COLS)),
                  _resident((DEC_SEQ, GQA_KV_COLS)),
                  pl.BlockSpec((None, PAST_LEN, GQA_KV_COLS), lambda b: (b, 0, 0)),
                  pl.BlockSpec((None, PAST_LEN, GQA_KV_COLS), lambda b: (b, 0, 0))],
        out_specs=pl.BlockSpec((DEC_SEQ, D_MODEL), lambda b: (b, 0)),
        out_shape=jax.ShapeDtypeStruct((N_LAT, D_MODEL), F32),
        scratch_shapes=(_gqa_scratch(DEC_SEQ, LATENT_KEYS)
                        + _stage_scratch(LATENT_STAGE_HEADS, LATENT_KEYS)),
        compiler_params=_params("arbitrary"),
        name=f"gqa_latent_{layer}",
    )(x, ada, g1, w_qkv_lat, g_q, g_k, w_o, g_q_swap, g_k_swap,
      cos_q, sin_q, cos_k, sin_k, cache_k, cache_v)

    kv_tail = (GQA_KV_HEADS, GQA_HEAD_DIM)
    x_spec, lat_spec, out_specs = _ctx_specs((kv_tail, kv_tail))
    return pl.pallas_call(
        functools.partial(_gqa_kernel, latent=False),
        grid=(CTX_STEPS,),
        in_specs=[x_spec,
                  _ada_spec_at(layer, lambda i: 0),
                  _resident((1, D_MODEL)),
                  _resident(w_qkv_ctx.shape),
                  _resident((1, GQA_Q_COLS)),
                  _resident((1, GQA_KV_COLS)),
                  _resident((GQA_Q_COLS, D_MODEL)),
                  lat_spec],
        out_specs=out_specs,
        out_shape=[
            jax.ShapeDtypeStruct((N_TOK, D_MODEL), F32),
            jax.ShapeDtypeStruct((N_CTX,) + kv_tail, F32),
            jax.ShapeDtypeStruct((N_CTX,) + kv_tail, F32),
        ],
        scratch_shapes=_gqa_scratch(CTX_TM, CTX_TM) + _stage_scratch(GQA_HEADS, SEQ),
        compiler_params=_params("arbitrary"),
        name=f"gqa_ctx_{layer}",
    )(x, ada, g1, w_qkv_ctx, g_q, g_k, w_o, lat_out)


ROUTE_TM = 512
MOE_TG = 256
MOE_MIN_TILES = 2 * N_TOK // MOE_TG
MOE_MAX_TILES = MOE_MIN_TILES + N_EXPERTS
MOE_ROWS = MOE_MAX_TILES * MOE_TG
MOE_FILLS = 2 * N_EXPERTS
MOE_FC = 1024
COMBINE_TM = 256
ISSUE_UNROLL = 16


def _route_kernel(x_ref, ada_ref, g2_ref, wr_hi_ref, wr_lo_ref, h_ref, dest_ref, gatew_ref,
                  count_ref, carry_scr):
    tm = x_ref.shape[0]

    @pl.when(pl.program_id(0) == 0)
    def _():
        carry_scr[...] = jnp.zeros_like(carry_scr)

    h = _modulate(x_ref[...], g2_ref[...], ada_ref[3:4, :], ada_ref[4:5, :])
    h_ref[...] = h.reshape(tm, 1, D_MODEL)
    h_hi = h.astype(BF16)
    h_lo = (h - h_hi.astype(F32)).astype(BF16)
    logits = _dot(h_hi, wr_hi_ref[...]) + (_dot(h_lo, wr_hi_ref[...]) + _dot(h_hi, wr_lo_ref[...]))
    lane = lax.broadcasted_iota(jnp.int32, logits.shape, 1)
    neg = jnp.float32(-jnp.inf)
    l1 = jnp.where(lane < N_EXPERTS, logits, neg)
    m1 = jnp.max(l1, axis=-1, keepdims=True)
    i1 = jnp.min(jnp.where(l1 == m1, lane, LANES), axis=-1, keepdims=True)
    l2 = jnp.where(lane == i1, neg, l1)
    m2 = jnp.max(l2, axis=-1, keepdims=True)
    i2 = jnp.min(jnp.where(l2 == m2, lane, LANES), axis=-1, keepdims=True)
    e2 = jnp.exp(m2 - m1)
    w1 = 1.0 / (1.0 + e2)
    w2 = e2 / (1.0 + e2)
    member = jnp.where((lane == i1) | (lane == i2), 1.0, 0.0)
    r = lax.broadcasted_iota(jnp.int32, (tm, tm), 0)
    c = lax.broadcasted_iota(jnp.int32, (tm, tm), 1)
    tri = jnp.where(c < r, 1.0, 0.0).astype(BF16)
    before = _dot(tri, member.astype(BF16)) + carry_scr[0:1, :]
    rank1 = jnp.sum(jnp.where(lane == i1, before, 0.0), axis=-1, keepdims=True).astype(jnp.int32)
    rank2 = jnp.sum(jnp.where(lane == i2, before, 0.0), axis=-1, keepdims=True).astype(jnp.int32)
    table = jnp.where(lane == 0, i1, jnp.where(lane == 1, i2, jnp.where(
        lane == 2, rank1, jnp.where(lane == 3, rank2, 0))))
    dest_ref[...] = jnp.transpose(table)[0:SUBLANES, :]
    gatew_ref[...] = jnp.where(lane == 0, w1, jnp.where(lane == 1, w2, 0.0))
    carry_scr[...] = carry_scr[...] + jnp.sum(member, axis=0, keepdims=True)
    count_ref[...] = carry_scr[...]


def _route(x, ada, layer, g2, w_router):
    tm = ROUTE_TM
    w_router_hi = w_router.astype(BF16)
    return pl.pallas_call(
        _route_kernel,
        grid=(N_TOK // tm,),
        in_specs=[
            pl.BlockSpec((tm, D_MODEL), lambda i: (i, 0)),
            _ada_spec(layer, tm),
            _resident((1, D_MODEL)),
            _resident((D_MODEL, LANES)),
            _resident((D_MODEL, LANES)),
        ],
        out_specs=[
            pl.BlockSpec((tm, 1, D_MODEL), lambda i: (i, 0, 0)),
            pl.BlockSpec((SUBLANES, tm), lambda i: (0, i)),
            pl.BlockSpec((tm, LANES), lambda i: (i, 0)),
            pl.BlockSpec((SUBLANES, LANES), lambda i: (0, 0)),
        ],
        out_shape=[
            jax.ShapeDtypeStruct((N_TOK, 1, D_MODEL), F32),
            jax.ShapeDtypeStruct((SUBLANES, N_TOK), jnp.int32),
            jax.ShapeDtypeStruct((N_TOK, LANES), F32),
            jax.ShapeDtypeStruct((SUBLANES, LANES), F32),
        ],
        scratch_shapes=[pltpu.VMEM((SUBLANES, LANES), F32)],
        compiler_params=_params("arbitrary"),
        name=f"moe_route_{layer}",
    )(x, ada, g2, w_router_hi, (w_router - w_router_hi.astype(F32)).astype(BF16))


def _cast_specs(m, parts):
    last = N_EXPERTS * parts - 1

    def src(i, *_):
        c = jnp.minimum(i, last)
        return (m, c // parts, c % parts, 0)

    def dst(i, *_):
        c = jnp.minimum(i, last)
        return (c // parts, c % parts, 0)

    shapes = [(D_MODEL // parts, MOE_FF), (D_MODEL // parts, MOE_FF), (MOE_FF // parts, D_MODEL)]
    in_specs = [pl.BlockSpec((None, None) + s, src) for s in shapes]
    out_specs = [pl.BlockSpec((None,) + s, dst) for s in shapes]
    out_shape = [jax.ShapeDtypeStruct((N_EXPERTS, D_MODEL, MOE_FF), BF16),
                 jax.ShapeDtypeStruct((N_EXPERTS, D_MODEL, MOE_FF), BF16),
                 jax.ShapeDtypeStruct((N_EXPERTS, MOE_FF, D_MODEL), BF16)]
    return in_specs, out_specs, out_shape


def _cast_chunk(srcs, dsts):
    for s_ref, d_ref in zip(srcs, dsts):
        d_ref[...] = s_ref[...].astype(BF16)


DISPATCH_TM = 640
EXPERTS_CAST_PARTS = 8


def _dispatch_kernel(d1_ref, d2_ref, fill_start_ref, fill_rows_ref, h_ref, hs_ref, zero_scr, sem):
    i = pl.program_id(0)
    tm = h_ref.shape[0]
    base = i * tm

    def fill_copy(k):
        n = fill_rows_ref[k]
        return pltpu.make_async_copy(zero_scr.at[pl.ds(0, n)],
                                     hs_ref.at[pl.ds(fill_start_ref[k], n)], sem.at[1])

    @pl.when(i == 0)
    def _():
        zero_scr[...] = jnp.zeros_like(zero_scr)
        for k in range(MOE_FILLS):
            @pl.when(fill_rows_ref[k] > 0)
            def _():
                fill_copy(k).start()
        for k in range(MOE_FILLS):
            @pl.when(fill_rows_ref[k] > 0)
            def _():
                fill_copy(k).wait()

    def row_copy(r, d_ref):
        return pltpu.make_async_copy(h_ref.at[r], hs_ref.at[d_ref[base + r]], sem.at[0])

    def issue(g, carry):
        for k in range(ISSUE_UNROLL):
            r = g * ISSUE_UNROLL + k
            row_copy(r, d1_ref).start(priority=0)
            row_copy(r, d2_ref).start(priority=1)
        return carry

    def drain():
        for _ in range(2):
            pltpu.make_async_copy(h_ref, hs_ref.at[pl.ds(0, tm)], sem.at[0]).wait()

    lax.fori_loop(0, tm // ISSUE_UNROLL, issue, 0)
    drain()


def _dispatch(d1, d2, fill_start, fill_rows, h_rows):
    tm = DISPATCH_TM
    return pl.pallas_call(
        _dispatch_kernel,
        grid_spec=pltpu.PrefetchScalarGridSpec(
            num_scalar_prefetch=4,
            grid=(N_TOK // tm,),
            in_specs=[pl.BlockSpec((tm, 1, D_MODEL), lambda i, *_: (i, 0, 0))],
            out_specs=pl.BlockSpec(memory_space=pl.ANY),
            scratch_shapes=[pltpu.VMEM((MOE_TG, 1, D_MODEL), F32),
                            pltpu.SemaphoreType.DMA((2,))],
        ),
        out_shape=jax.ShapeDtypeStruct((MOE_ROWS, 1, D_MODEL), F32),
        compiler_params=_params("arbitrary"),
        name="moe_dispatch",
    )(d1, d2, fill_start, fill_rows, h_rows)


def _experts_kernel(*refs, cast):
    if cast:
        (te_ref, nt_ref, hs_ref, wg_ref, wu_ref, wd_ref, ng_ref, nu_ref, nd_ref,
         ys_ref, ng_out_ref, nu_out_ref, nd_out_ref, h2d_scr) = refs
    else:
        te_ref, nt_ref, hs_ref, wg_ref, wu_ref, wd_ref, ys_ref, h2d_scr = refs
    used = pl.program_id(0) < nt_ref[0]

    @pl.when(jnp.logical_not(used))
    def _():
        ys_ref[...] = jnp.zeros_like(ys_ref)

    @pl.when(used)
    def _():
        if cast:
            _cast_chunk((ng_ref, nu_ref, nd_ref), (ng_out_ref, nu_out_ref, nd_out_ref))
        h2d_scr[...] = hs_ref[...].reshape(MOE_TG, D_MODEL)
        h = h2d_scr[...].astype(BF16)
        y = None
        for lo in range(0, MOE_FF, MOE_FC):
            g = _dot(h, wg_ref[:, lo:lo + MOE_FC])
            u = _dot(h, wu_ref[:, lo:lo + MOE_FC])
            part = _dot((_silu(g) * u).astype(BF16), wd_ref[lo:lo + MOE_FC, :])
            y = part if y is None else y + part
        ys_ref[...] = y.reshape(MOE_TG, 1, D_MODEL)


def _experts(tile_expert, n_tiles, hs, w_gate, w_up, w_down, cast=None):
    rows = pl.BlockSpec((MOE_TG, 1, D_MODEL), lambda t, te, nt: (t, 0, 0))
    in_specs = [
        rows,
        pl.BlockSpec((None, D_MODEL, MOE_FF), lambda t, te, nt: (te[t], 0, 0)),
        pl.BlockSpec((None, D_MODEL, MOE_FF), lambda t, te, nt: (te[t], 0, 0)),
        pl.BlockSpec((None, MOE_FF, D_MODEL), lambda t, te, nt: (te[t], 0, 0)),
    ]
    out_specs = [rows]
    out_shape = [jax.ShapeDtypeStruct((MOE_ROWS, 1, D_MODEL), F32)]
    next_weights = ()
    if cast is not None:
        assert MOE_MIN_TILES >= N_EXPERTS * EXPERTS_CAST_PARTS
        cast_in, cast_out, cast_shape = _cast_specs(cast[0], EXPERTS_CAST_PARTS)
        in_specs += cast_in
        out_specs += cast_out
        out_shape += cast_shape
        next_weights = cast[1:]
    out = pl.pallas_call(
        functools.partial(_experts_kernel, cast=cast is not None),
        grid_spec=pltpu.PrefetchScalarGridSpec(
            num_scalar_prefetch=2,
            grid=(MOE_MAX_TILES,),
            in_specs=in_specs,
            out_specs=out_specs,
            scratch_shapes=[pltpu.VMEM((MOE_TG, D_MODEL), F32)],
        ),
        out_shape=out_shape,
        compiler_params=_params("arbitrary"),
        name="moe_experts",
    )(tile_expert, n_tiles, hs, w_gate, w_up, w_down, *next_weights)
    return out if cast is not None else out[0]


def _combine_kernel(*refs, final):
    if final:
        (d1_ref, d2_ref, x_ref, ada_ref, gw_ref, gf_ref, ys_ref, yp_ref, yl_ref,
         ya0, yb0, ya1, yb1, ya2d, yb2d, sem) = refs
    else:
        (d1_ref, d2_ref, x_ref, ada_ref, gw_ref, ys_ref, o_ref,
         ya0, yb0, ya1, yb1, ya2d, yb2d, sem) = refs
    i = pl.program_id(0)
    tm = x_ref.shape[0]

    def row_copies(step, r, buf_a, buf_b, slot):
        t = step * tm + r
        return (pltpu.make_async_copy(ys_ref.at[d1_ref[t]], buf_a.at[r], sem.at[slot]),
                pltpu.make_async_copy(ys_ref.at[d2_ref[t]], buf_b.at[r], sem.at[slot]))

    def start_tile(step, buf_a, buf_b, slot):
        def body(g, carry):
            for k in range(ISSUE_UNROLL):
                copies = row_copies(step, g * ISSUE_UNROLL + k, buf_a, buf_b, slot)
                for priority, cp in enumerate(copies):
                    cp.start(priority=priority)
            return carry
        lax.fori_loop(0, tm // ISSUE_UNROLL, body, 0)

    def wait_tile(buf_a, buf_b, slot):
        for buf in (buf_a, buf_b):
            pltpu.make_async_copy(ys_ref.at[pl.ds(0, tm)], buf, sem.at[slot]).wait()

    @pl.when(i == 0)
    def _():
        start_tile(0, ya0, yb0, 0)

    def run(cur, nxt):
        @pl.when(i + 1 < pl.num_programs(0))
        def _():
            start_tile(i + 1, *nxt)

        wait_tile(*cur)
        ya2d[...] = cur[0][...].reshape(tm, D_MODEL)
        yb2d[...] = cur[1][...].reshape(tm, D_MODEL)
        mix = gw_ref[:, 0:1] * ya2d[...] + gw_ref[:, 1:2] * yb2d[...]
        out = x_ref[...] + ada_ref[5:6, :] * mix
        if final:
            y = _rms(out) * gf_ref[...]

            @pl.when(i * tm < N_CTX)
            def _():
                yp_ref[...] = y

            @pl.when(i * tm >= N_CTX)
            def _():
                yl_ref[...] = y
        else:
            o_ref[...] = out

    @pl.when(i % 2 == 0)
    def _():
        run((ya0, yb0, 0), (ya1, yb1, 1))

    @pl.when(i % 2 == 1)
    def _():
        run((ya1, yb1, 1), (ya0, yb0, 0))


def _combine(d1, d2, x, ada, layer, gate_w, ys, final_g=None):
    tm = COMBINE_TM
    final = final_g is not None
    ctx_tiles = N_CTX // tm
    row_buf = pltpu.VMEM((tm, 1, D_MODEL), F32)
    in_specs = [
        pl.BlockSpec((tm, D_MODEL), lambda i, *_: (i, 0)),
        pl.BlockSpec((None, None, 6, D_MODEL),
                     lambda i, *_: (layer, _cond_of_tile(i, tm), 0, 0)),
        pl.BlockSpec((tm, LANES), lambda i, *_: (i, 0)),
    ]
    if final:
        in_specs.append(pl.BlockSpec((1, D_MODEL), lambda i, *_: (0, 0)))
        out_specs = [
            pl.BlockSpec((tm, D_MODEL), lambda i, *_: (jnp.minimum(i, ctx_tiles - 1), 0)),
            pl.BlockSpec((tm, D_MODEL), lambda i, *_: (jnp.maximum(i - ctx_tiles, 0), 0)),
        ]
        out_shape = [jax.ShapeDtypeStruct((N_CTX, D_MODEL), F32),
                     jax.ShapeDtypeStruct((N_LAT, D_MODEL), F32)]
        args = (d1, d2, x, ada, gate_w, final_g, ys)
    else:
        out_specs = pl.BlockSpec((tm, D_MODEL), lambda i, *_: (i, 0))
        out_shape = jax.ShapeDtypeStruct((N_TOK, D_MODEL), F32)
        args = (d1, d2, x, ada, gate_w, ys)
    in_specs.append(pl.BlockSpec(memory_space=pl.ANY))
    return pl.pallas_call(
        functools.partial(_combine_kernel, final=final),
        grid_spec=pltpu.PrefetchScalarGridSpec(
            num_scalar_prefetch=2,
            grid=(N_TOK // tm,),
            in_specs=in_specs,
            out_specs=out_specs,
            scratch_shapes=[row_buf, row_buf, row_buf, row_buf,
                            pltpu.VMEM((tm, D_MODEL), F32), pltpu.VMEM((tm, D_MODEL), F32),
                            pltpu.SemaphoreType.DMA((2,))],
        ),
        out_shape=out_shape,
        compiler_params=_params("arbitrary"),
        name="moe_combine",
    )(*args)


def _moe_layer(x, ada, layer, m, g2, w_router, w_gate, w_up, w_down, weights_bf, final_g):
    h_rows, route, gate_w, counts = _route(x, ada, layer, g2, w_router)
    cnt = counts[0, :N_EXPERTS].astype(jnp.int32)
    tiles_per_expert = (cnt + MOE_TG - 1) // MOE_TG
    tile_end = jnp.cumsum(tiles_per_expert)
    row_start = (tile_end - tiles_per_expert) * MOE_TG
    n_tiles = tile_end[-1:]
    t = jnp.minimum(jnp.arange(MOE_MAX_TILES, dtype=jnp.int32), n_tiles - 1)
    tile_expert = jnp.sum((t[:, None] >= tile_end[None, :]).astype(jnp.int32), axis=1)
    experts = jnp.arange(N_EXPERTS, dtype=jnp.int32)
    d1 = jnp.sum(jnp.where(route[0][:, None] == experts, row_start, 0), axis=1) + route[2]
    d2 = jnp.sum(jnp.where(route[1][:, None] == experts, row_start, 0), axis=1) + route[3]
    spare_tile = jnp.minimum(n_tiles + experts, MOE_MAX_TILES - 1)
    fill_start = jnp.concatenate([row_start + cnt, spare_tile * MOE_TG])
    fill_rows = jnp.concatenate([tiles_per_expert * MOE_TG - cnt,
                                 jnp.where(n_tiles + experts < MOE_MAX_TILES, MOE_TG, 0)])
    hs = _dispatch(d1, d2, fill_start, fill_rows, h_rows)
    next_bf = None
    if m + 1 < w_gate.shape[0]:
        ys, *next_bf = _experts(tile_expert, n_tiles, hs, *weights_bf,
                                (m + 1, w_gate, w_up, w_down))
    else:
        ys = _experts(tile_expert, n_tiles, hs, *weights_bf)
    return _combine(d1, d2, x, ada, layer, gate_w, ys, final_g), next_bf


assert DEPTH % 2 == 0


def _rope_tables(dim):
    half = dim // 2
    quarter = half // 2
    t = np.arange(DEC_SEQ)
    pos = np.stack([t // GRID_W, t % GRID_W], axis=1).astype(np.float32)
    freqs = (ROPE_THETA ** (-np.arange(quarter, dtype=np.float32) / quarter)).astype(np.float32)
    ang = pos[:, :, None] * freqs[None, None, :]
    cos = np.cos(ang.astype(np.float64))
    sin = np.sin(ang.astype(np.float64))
    cos_t = np.concatenate([cos, cos], axis=-1).reshape(DEC_SEQ, dim)
    sin_t = np.concatenate([-sin, sin], axis=-1).reshape(DEC_SEQ, dim)
    lane = np.arange(dim)
    swap = np.where(lane % half < quarter, lane + quarter, lane - quarter)
    return cos_t.astype(np.float32), sin_t.astype(np.float32), swap


def kernel(x_prompt, x_sample, cache_mla_ckv, cache_mla_kpe, cache_gqa_k, cache_gqa_v, c, c_ctx, w_ada, b_ada, norm1_g, norm2_g, conv_w_in, conv_k, conv_w_out, mla_w_down, mla_q_norm_g, mla_kv_norm_g, mla_w_uq, mla_w_ukv, mla_w_o, gqa_w_qkv, gqa_q_norm_g, gqa_k_norm_g, gqa_w_o, ffn_w_gate, ffn_w_up, ffn_w_down, moe_w_router, moe_w_gate, moe_w_up, moe_w_down, final_norm_g):
    xs = (x_prompt.reshape(N_CTX, D_MODEL), x_sample.reshape(N_LAT, D_MODEL))
    conv_w_in, conv_w_out = conv_w_in.astype(BF16), conv_w_out.astype(BF16)
    ffn_w_gate, ffn_w_up, ffn_w_down = (w.astype(BF16) for w in (ffn_w_gate, ffn_w_up, ffn_w_down))
    cond = jnp.concatenate(
        [c_ctx[None, :], c, jnp.zeros((N_COND - 1 - DEC_BATCH, D_MODEL), F32)], axis=0)
    ada = _adaln_table(cond, w_ada, b_ada)

    new_ckv, new_kpe, new_k, new_v = [], [], [], []
    moe_bf = None
    for i in range(DEPTH):
        j = i // N_MIXERS
        kind = i % N_MIXERS
        g1 = norm1_g[i][None, :]
        if kind == 0:
            x = _conv_layer(xs if i == 0 else (x,), ada, i, j, g1, conv_w_in, conv_k, conv_w_out)
        elif kind == 1:
            cos64, sin64, swap64 = _rope_tables(MLA_ROPE)
            wd = mla_w_down[j]
            kpe_cols = wd[:, MLA_Q_RANK + MLA_KV_RANK:]
            w_down = jnp.concatenate([wd, kpe_cols[:, swap64]], axis=1).astype(BF16)
            wq = mla_w_uq[j].reshape(MLA_Q_RANK, MLA_HEADS, MLA_QK)
            wq_nope = wq[:, :, :MLA_NOPE].reshape(MLA_Q_RANK, MLA_Q_NOPE_COLS)
            wq_rope = wq[:, :, MLA_NOPE:]
            w_uq = jnp.concatenate(
                [wq_nope, wq_rope.reshape(MLA_Q_RANK, MLA_Q_ROPE_COLS),
                 wq_rope[:, :, swap64].reshape(MLA_Q_RANK, MLA_Q_ROPE_COLS)], axis=1).astype(BF16)
            wkv = mla_w_ukv[j].reshape(MLA_KV_RANK, MLA_HEADS, MLA_NOPE + MLA_V)
            w_ukv = jnp.concatenate(
                [wkv[:, :, :MLA_NOPE].reshape(MLA_KV_RANK, MLA_Q_NOPE_COLS),
                 wkv[:, :, MLA_NOPE:].reshape(MLA_KV_RANK, MLA_HEADS * MLA_V)], axis=1).astype(BF16)
            rope = (jnp.asarray(np.tile(cos64, (1, MLA_HEADS))), jnp.asarray(np.tile(sin64, (1, MLA_HEADS))),
                    jnp.asarray(cos64), jnp.asarray(sin64))
            x, ckv_p, kpe_p = _mla_layer(
                x, ada, i, g1, w_down, mla_q_norm_g[j][None, :], mla_kv_norm_g[j][None, :],
                w_uq, w_ukv, mla_w_o[j].astype(BF16), cache_mla_ckv[:, j], cache_mla_kpe[:, j], rope)
            new_ckv.append(ckv_p.reshape(BATCH, SEQ, MLA_KV_RANK))
            new_kpe.append(kpe_p.reshape(BATCH, SEQ, MLA_ROPE))
        else:
            cos128, sin128, swap128 = _rope_tables(GQA_HEAD_DIM)
            wqkv = gqa_w_qkv[j]
            wq = wqkv[:, :GQA_Q_COLS].reshape(D_MODEL, GQA_HEADS, GQA_HEAD_DIM)
            wk = wqkv[:, GQA_Q_COLS:GQA_Q_COLS + GQA_KV_COLS].reshape(D_MODEL, GQA_KV_HEADS, GQA_HEAD_DIM)
            w_qkv_ctx = wqkv.astype(BF16)
            w_qkv_lat = jnp.concatenate(
                [wqkv, wq[:, :, swap128].reshape(D_MODEL, GQA_Q_COLS),
                 wk[:, :, swap128].reshape(D_MODEL, GQA_KV_COLS)], axis=1).astype(BF16)
            gq = gqa_q_norm_g[j]
            gk = gqa_k_norm_g[j]
            rope = (jnp.asarray(np.tile(cos128, (1, GQA_HEADS))), jnp.asarray(np.tile(sin128, (1, GQA_HEADS))),
                    jnp.asarray(np.tile(cos128, (1, GQA_KV_HEADS))), jnp.asarray(np.tile(sin128, (1, GQA_KV_HEADS))))
            x, k_p, v_p = _gqa_layer(
                x, ada, i, g1, w_qkv_ctx, w_qkv_lat,
                jnp.tile(gq, GQA_HEADS)[None, :], jnp.tile(gk, GQA_KV_HEADS)[None, :],
                jnp.tile(gq[swap128], GQA_HEADS)[None, :], jnp.tile(gk[swap128], GQA_KV_HEADS)[None, :],
                gqa_w_o[j].astype(BF16),
                cache_gqa_k[:, j].reshape(DEC_BATCH, PAST_LEN, GQA_KV_COLS),
                cache_gqa_v[:, j].reshape(DEC_BATCH, PAST_LEN, GQA_KV_COLS), rope)
            new_k.append(k_p.reshape(BATCH, SEQ, GQA_KV_HEADS, GQA_HEAD_DIM))
            new_v.append(v_p.reshape(BATCH, SEQ, GQA_KV_HEADS, GQA_HEAD_DIM))

        m = i // 2
        g2 = norm2_g[i][None, :]
        if i % 2 == 0:
            cast = (m, moe_w_gate, moe_w_up, moe_w_down) if moe_bf is None else None
            x, prepared = _ffn_layer(x, ada, i, m, g2, ffn_w_gate, ffn_w_up, ffn_w_down, cast)
            moe_bf = prepared if cast is not None else moe_bf
        else:
            w_router = jnp.pad(moe_w_router[m], ((0, 0), (0, LANES - N_EXPERTS)))
            final_g = final_norm_g[None, :] if i == DEPTH - 1 else None
            x, moe_bf = _moe_layer(x, ada, i, m, g2, w_router, moe_w_gate, moe_w_up, moe_w_down,
                                   moe_bf, final_g)

    y_prompt, y_sample = x
    y_prompt = y_prompt.reshape(BATCH, SEQ, D_MODEL)
    y_sample = y_sample.reshape(DEC_BATCH, DEC_SEQ, D_MODEL)
    return (y_prompt, y_sample,
            jnp.stack(new_ckv, axis=1), jnp.stack(new_kpe, axis=1),
            jnp.stack(new_k, axis=1), jnp.stack(new_v, axis=1))
```

```python
import functools

import numpy as np
import jax
import jax.numpy as jnp
from jax import lax
from jax.experimental import pallas as pl
from jax.experimental.pallas import tpu as pltpu

D_MODEL = 1024
BATCH = 32
SEQ = 256
DEPTH = 4
DEC_BATCH = 2
DEC_SEQ = 1024
PAST_LEN = 512
GRID_W = 64
N_MIXERS = 3
CONV_WIDTH = 3
MLA_HEADS = 8
MLA_NOPE = 128
MLA_ROPE = 64
MLA_V = 128
MLA_Q_RANK = 384
MLA_KV_RANK = 256
GQA_HEADS = 8
GQA_KV_HEADS = 2
GQA_HEAD_DIM = 128
D_FF = 2816
N_EXPERTS = 8
MOE_FF = 2048
ROPE_THETA = 10000.0
EPS = 1e-6

N_CTX = BATCH * SEQ
N_LAT = DEC_BATCH * DEC_SEQ
N_TOK = N_CTX + N_LAT
N_COND = 8
LANES = 128
SUBLANES = 8
ADALN_TN = 1536
Q_TILE = 256
VMEM_LIMIT = 56 * 1024 * 1024

F32 = jnp.float32
BF16 = jnp.bfloat16


def _dot(a, b):
    return jnp.dot(a, b, preferred_element_type=F32)


def _dot_t(a, b):
    return lax.dot_general(a, b, (((1,), (1,)), ((), ())), preferred_element_type=F32)


def _rms(x):
    return x * lax.rsqrt(jnp.mean(x * x, axis=-1, keepdims=True) + EPS)


def _modulate(x, g, shift, scale):
    return _rms(x) * g * (1.0 + scale) + shift


def _silu(x):
    return x * jax.nn.sigmoid(x)


def _cond_of_tile(i, tm):
    start = i * tm
    return jnp.where(start < N_CTX, 0, 1 + (start - N_CTX) // DEC_SEQ)


def _resident(shape):
    return pl.BlockSpec(shape, lambda *_: (0,) * len(shape), pipeline_mode=pl.Buffered(1))


def _resident_at(index, shape):
    return pl.BlockSpec((None,) + tuple(shape), lambda *_: (index,) + (0,) * len(shape),
                        pipeline_mode=pl.Buffered(1))


def _params(*sem):
    return pltpu.CompilerParams(dimension_semantics=sem, vmem_limit_bytes=VMEM_LIMIT)


def _adaln_kernel(cond_ref, w_ref, b_ref, o_ref):
    s = _silu(cond_ref[...]).astype(BF16)
    o_ref[...] = _dot(s, w_ref[...].astype(BF16)) + b_ref[...]


def _adaln_table(cond, w_ada, b_ada):
    tn = ADALN_TN
    out = pl.pallas_call(
        _adaln_kernel,
        grid=(DEPTH, 6 * D_MODEL // tn),
        in_specs=[
            pl.BlockSpec((N_COND, D_MODEL), lambda l, j: (0, 0)),
            pl.BlockSpec((None, D_MODEL, tn), lambda l, j: (l, 0, j)),
            pl.BlockSpec((None, 1, tn), lambda l, j: (l, 0, j)),
        ],
        out_specs=pl.BlockSpec((None, N_COND, tn), lambda l, j: (l, 0, j)),
        out_shape=jax.ShapeDtypeStruct((DEPTH, N_COND, 6 * D_MODEL), F32),
        compiler_params=_params("arbitrary", "arbitrary"),
        name="adaln_table",
    )(cond, w_ada, b_ada.reshape(DEPTH, 1, 6 * D_MODEL))
    return out.reshape(DEPTH, N_COND, 6, D_MODEL)


def _ada_spec(layer, tm):
    return pl.BlockSpec((None, None, 6, D_MODEL), lambda i: (layer, _cond_of_tile(i, tm), 0, 0))


def _ada_spec_at(layer, cond_fn):
    return pl.BlockSpec((None, None, 6, D_MODEL), lambda i: (layer, cond_fn(i), 0, 0))


CONV_TM = 1024
CONV_CC = 256


def _conv_kernel(*refs, split):
    if split:
        xp_ref, x_ref, ada_ref, g1_ref, win_ref, ck_ref, wout_ref, o_ref, v_scr = refs
    else:
        x_ref, ada_ref, g1_ref, win_ref, ck_ref, wout_ref, o_ref, v_scr = refs
    tm = x_ref.shape[0]
    i = pl.program_id(0)
    x = x_ref[...]
    if split:
        x = jnp.where(i * tm < N_CTX, xp_ref[...], x)
    h = _modulate(x, g1_ref[...], ada_ref[0:1, :], ada_ref[1:2, :]).astype(BF16)
    period = jnp.where(i * tm < N_CTX, SEQ, DEC_SEQ)
    pos = lax.broadcasted_iota(jnp.int32, (tm, 1), 0) & (period - 1)
    first = pos == 0
    last = pos == period - 1
    for j in range(D_MODEL // CONV_CC):
        lo = j * CONV_CC
        b_gate = _dot(h, win_ref[:, lo:lo + CONV_CC])
        c_gate = _dot(h, win_ref[:, D_MODEL + lo:D_MODEL + lo + CONV_CC])
        x_in = _dot(h, win_ref[:, 2 * D_MODEL + lo:2 * D_MODEL + lo + CONV_CC])
        u = c_gate * x_in
        u_prev = jnp.where(first, 0.0, pltpu.roll(u, 1, 0))
        u_next = jnp.where(last, 0.0, pltpu.roll(u, tm - 1, 0))
        conv = (ck_ref[0:1, lo:lo + CONV_CC] * u_prev + ck_ref[1:2, lo:lo + CONV_CC] * u
                + ck_ref[2:3, lo:lo + CONV_CC] * u_next)
        v_scr[:, lo:lo + CONV_CC] = (b_gate * conv).astype(BF16)
    o_ref[...] = x + ada_ref[2:3, :] * _dot(v_scr[...], wout_ref[...])


def _conv_layer(xs, ada, layer, j, g1, w_in, conv_k, w_out):
    tm = CONV_TM
    ctx_tiles = N_CTX // tm
    if len(xs) == 1:
        x_specs = [pl.BlockSpec((tm, D_MODEL), lambda i: (i, 0))]
    else:
        x_specs = [pl.BlockSpec((tm, D_MODEL), lambda i: (jnp.minimum(i, ctx_tiles - 1), 0)),
                   pl.BlockSpec((tm, D_MODEL), lambda i: (jnp.maximum(i - ctx_tiles, 0), 0))]
    return pl.pallas_call(
        functools.partial(_conv_kernel, split=len(xs) == 2),
        grid=(N_TOK // tm,),
        in_specs=x_specs + [
            _ada_spec(layer, tm),
            _resident((1, D_MODEL)),
            _resident_at(j, (D_MODEL, 3 * D_MODEL)),
            _resident_at(j, (CONV_WIDTH, D_MODEL)),
            _resident_at(j, (D_MODEL, D_MODEL)),
        ],
        out_specs=pl.BlockSpec((tm, D_MODEL), lambda i: (i, 0)),
        out_shape=jax.ShapeDtypeStruct((N_TOK, D_MODEL), F32),
        scratch_shapes=[pltpu.VMEM((tm, D_MODEL), BF16)],
        compiler_params=_params("arbitrary"),
        name=f"conv_mixer_{layer}",
    )(*xs, ada, g1, w_in, conv_k, w_out)


FFN_TM = 512
FFN_CAST_TM = 256
FFN_CAST_PARTS = 4
FFN_CHUNKS = ((0, 1536), (1536, D_FF))


def _ffn_kernel(*refs, cast):
    if cast:
        (x_ref, ada_ref, g2_ref, wg_ref, wu_ref, wd_ref, eg_ref, eu_ref, ed_ref,
         o_ref, eg_out_ref, eu_out_ref, ed_out_ref) = refs
        _cast_chunk((eg_ref, eu_ref, ed_ref), (eg_out_ref, eu_out_ref, ed_out_ref))
    else:
        x_ref, ada_ref, g2_ref, wg_ref, wu_ref, wd_ref, o_ref = refs
    x = x_ref[...]
    h = _modulate(x, g2_ref[...], ada_ref[3:4, :], ada_ref[4:5, :]).astype(BF16)
    f = None
    for lo, hi in FFN_CHUNKS:
        g = _dot(h, wg_ref[:, lo:hi])
        u = _dot(h, wu_ref[:, lo:hi])
        part = _dot((_silu(g) * u).astype(BF16), wd_ref[lo:hi, :])
        f = part if f is None else f + part
    o_ref[...] = x + ada_ref[5:6, :] * f


def _ffn_layer(x, ada, layer, m, g2, w_gate, w_up, w_down, cast=None):
    tm = FFN_TM if cast is None else FFN_CAST_TM
    in_specs = [
        pl.BlockSpec((tm, D_MODEL), lambda i: (i, 0)),
        _ada_spec(layer, tm),
        _resident((1, D_MODEL)),
        _resident_at(m, (D_MODEL, D_FF)),
        _resident_at(m, (D_MODEL, D_FF)),
        _resident_at(m, (D_FF, D_MODEL)),
    ]
    out_specs = [pl.BlockSpec((tm, D_MODEL), lambda i: (i, 0))]
    out_shape = [jax.ShapeDtypeStruct((N_TOK, D_MODEL), F32)]
    expert_weights = ()
    if cast is not None:
        assert N_TOK // tm >= N_EXPERTS * FFN_CAST_PARTS
        cast_in, cast_out, cast_shape = _cast_specs(cast[0], FFN_CAST_PARTS)
        in_specs += cast_in
        out_specs += cast_out
        out_shape += cast_shape
        expert_weights = cast[1:]
    out = pl.pallas_call(
        functools.partial(_ffn_kernel, cast=cast is not None),
        grid=(N_TOK // tm,),
        in_specs=in_specs,
        out_specs=out_specs,
        out_shape=out_shape,
        compiler_params=_params("arbitrary"),
        name=f"dense_ffn_{layer}",
    )(x, ada, g2, w_gate, w_up, w_down, *expert_weights)
    return (out[0], out[1:]) if cast is not None else (out[0], None)


def _attention(qk_parts, v_ref, heads, kv_of_head, dv, scale, o_scr, finish, keys_of, stage):
    def scores(rows, keys, h):
        s = None
        for q_ref, k_ref, width, k_shared in qk_parts:
            kb = 0 if k_shared else kv_of_head(h)
            part = _dot_t(q_ref[rows, h * width:(h + 1) * width],
                          k_ref[keys, kb * width:(kb + 1) * width])
            s = part if s is None else s + part
        return s * scale

    def q_block(rows):
        s_scr, p_scr = stage
        group = s_scr.shape[0]
        keys = keys_of(rows)
        for h0 in range(0, heads, group):
            for j in range(group):
                s_scr[j] = scores(rows, keys, h0 + j)
            for j in range(group):
                s = s_scr[j]
                e = jnp.exp(s - jnp.max(s, axis=-1, keepdims=True))
                p_scr[j] = (e * (1.0 / jnp.sum(e, axis=-1, keepdims=True))).astype(BF16)
            for j in range(group):
                h = h0 + j
                g = kv_of_head(h)
                o = _dot(p_scr[j], v_ref[keys, g * dv:(g + 1) * dv])
                o_scr[rows, h * dv:(h + 1) * dv] = o.astype(BF16)
        finish(rows)

    _for_row_blocks(o_scr.shape[0], q_block)


def _stage_scratch(group, keys):
    return [pltpu.VMEM((group, Q_TILE, keys), F32), pltpu.VMEM((group, Q_TILE, keys), BF16)]


LATENT_KEYS = DEC_SEQ + PAST_LEN
LATENT_STAGE_HEADS = 4


def _keys_of(latent):
    assert SEQ == Q_TILE
    return (lambda rows: slice(None)) if latent else (lambda rows: rows)


def _for_row_blocks(n_rows, fn):
    if n_rows <= 2 * Q_TILE:
        for b in range(n_rows // Q_TILE):
            fn(pl.ds(b * Q_TILE, Q_TILE))
    else:
        def body(b, carry):
            fn(pl.ds(pl.multiple_of(b * Q_TILE, Q_TILE), Q_TILE))
            return carry
        lax.fori_loop(0, n_rows // Q_TILE, body, 0)


MLA_QK = MLA_NOPE + MLA_ROPE
MLA_DOWN_COLS = MLA_Q_RANK + MLA_KV_RANK + 2 * MLA_ROPE
MLA_Q_NOPE_COLS = MLA_HEADS * MLA_NOPE
MLA_Q_ROPE_COLS = MLA_HEADS * MLA_ROPE


def _ctx_then_latent(body, n_in, refs):
    ins, lat_ref, rest = refs[:n_in], refs[n_in], refs[n_in + 1:]
    o_ref = rest[0]
    i = pl.program_id(0)

    @pl.when(i < CTX_TILES)
    def _():
        body(*ins, *rest, latent=False)

    @pl.when(i >= CTX_TILES)
    def _():
        o_ref[...] = lat_ref[...]


def _ctx_specs(out_tails):
    def tile(n_trailing):
        return lambda i: (jnp.minimum(i, CTX_TILES - 1),) + (0,) * n_trailing
    x_spec = pl.BlockSpec((CTX_TM, D_MODEL), tile(1))
    lat_spec = pl.BlockSpec((CTX_TM, D_MODEL), lambda i: (jnp.maximum(i - CTX_TILES, 0), 0))
    out_specs = [pl.BlockSpec((CTX_TM, D_MODEL), lambda i: (i, 0))]
    out_specs += [pl.BlockSpec((CTX_TM,) + tail, tile(len(tail))) for tail in out_tails]
    return x_spec, lat_spec, out_specs


CTX_TM = 2 * SEQ
CTX_TILES = N_CTX // CTX_TM
CTX_STEPS = CTX_TILES + N_LAT // CTX_TM


def _mla_kernel(*refs, latent):
    if latent:
        _mla_body(*refs, latent=True)
    else:
        _ctx_then_latent(_mla_body, 9, refs)


def _mla_body(*refs, latent):
    if latent:
        (x_ref, ada_ref, g1_ref, wdown_ref, gq_ref, gkv_ref, wuq_ref, wukv_ref, wo_ref,
         cosq_ref, sinq_ref, cosk_ref, sink_ref, cckv_ref, ckpe_ref,
         o_ref, qn_scr, qr_scr, kn_scr, kr_scr, v_scr, o_scr, *stage) = refs
    else:
        (x_ref, ada_ref, g1_ref, wdown_ref, gq_ref, gkv_ref, wuq_ref, wukv_ref, wo_ref,
         o_ref, ckv_out_ref, kpe_out_ref, qn_scr, qr_scr, kn_scr, kr_scr, v_scr, o_scr,
         *stage) = refs
    tm = x_ref.shape[0]
    kpe_lo = MLA_Q_RANK + MLA_KV_RANK

    def project(rows):
        h = _modulate(x_ref[rows, :], g1_ref[...], ada_ref[0:1, :], ada_ref[1:2, :]).astype(BF16)
        down = _dot(h, wdown_ref[...])
        c_q = down[:, :MLA_Q_RANK]
        c_kv = down[:, MLA_Q_RANK:kpe_lo]
        kpe = down[:, kpe_lo:kpe_lo + MLA_ROPE]
        q = _dot((_rms(c_q) * gq_ref[...]).astype(BF16), wuq_ref[...])
        ckv_n = _rms(c_kv) * gkv_ref[...]
        kv = _dot(ckv_n.astype(BF16), wukv_ref[...])
        qn_scr[rows, :] = q[:, :MLA_Q_NOPE_COLS].astype(BF16)
        q_rope = q[:, MLA_Q_NOPE_COLS:MLA_Q_NOPE_COLS + MLA_Q_ROPE_COLS]
        kn_scr[rows, :] = kv[:, :MLA_Q_NOPE_COLS].astype(BF16)
        v_scr[rows, :] = kv[:, MLA_Q_NOPE_COLS:].astype(BF16)
        if latent:
            q_swap = q[:, MLA_Q_NOPE_COLS + MLA_Q_ROPE_COLS:]
            q_rope = q_rope * cosq_ref[rows, :] + q_swap * sinq_ref[rows, :]
            kpe_swap = down[:, kpe_lo + MLA_ROPE:kpe_lo + 2 * MLA_ROPE]
            kpe = kpe * cosk_ref[rows, :] + kpe_swap * sink_ref[rows, :]
        else:
            ckv_out_ref[rows, :] = ckv_n
            kpe_out_ref[rows, :] = kpe
        qr_scr[rows, :] = q_rope.astype(BF16)
        kr_scr[rows, :] = kpe.astype(BF16)

    _for_row_blocks(tm, project)

    if latent:
        def expand_cache(rows):
            kv_c = _dot(cckv_ref[rows, :].astype(BF16), wukv_ref[...])
            dst = pl.ds(tm + rows.start, Q_TILE)
            kn_scr[dst, :] = kv_c[:, :MLA_Q_NOPE_COLS].astype(BF16)
            v_scr[dst, :] = kv_c[:, MLA_Q_NOPE_COLS:].astype(BF16)
            kr_scr[dst, :] = ckpe_ref[rows, :].astype(BF16)

        _for_row_blocks(PAST_LEN, expand_cache)

    def finish(rows):
        o_ref[rows, :] = x_ref[rows, :] + ada_ref[2:3, :] * _dot(o_scr[rows, :], wo_ref[...])

    _attention([(qn_scr, kn_scr, MLA_NOPE, False), (qr_scr, kr_scr, MLA_ROPE, True)],
               v_scr, MLA_HEADS, lambda hd: hd, MLA_V, MLA_QK ** -0.5, o_scr, finish,
               _keys_of(latent), stage)


def _mla_scratch(tm, sk):
    return [
        pltpu.VMEM((tm, MLA_Q_NOPE_COLS), BF16),
        pltpu.VMEM((tm, MLA_Q_ROPE_COLS), BF16),
        pltpu.VMEM((sk, MLA_Q_NOPE_COLS), BF16),
        pltpu.VMEM((sk, MLA_ROPE), BF16),
        pltpu.VMEM((sk, MLA_HEADS * MLA_V), BF16),
        pltpu.VMEM((tm, MLA_HEADS * MLA_V), BF16),
    ]


def _mla_layer(x, ada, layer, g1, w_down, g_q, g_kv, w_uq, w_ukv, w_o,
               cache_ckv, cache_kpe, rope):
    n_uq = w_uq.shape[1]
    weights = [
        _resident((1, D_MODEL)),
        _resident((D_MODEL, MLA_DOWN_COLS)),
        _resident((1, MLA_Q_RANK)),
        _resident((1, MLA_KV_RANK)),
        _resident((MLA_Q_RANK, n_uq)),
        _resident((MLA_KV_RANK, 2 * MLA_Q_NOPE_COLS)),
        _resident((MLA_HEADS * MLA_V, D_MODEL)),
    ]
    ctx_weights = list(weights)
    ctx_weights[4] = pl.BlockSpec((MLA_Q_RANK, MLA_Q_NOPE_COLS + MLA_Q_ROPE_COLS), lambda *_: (0, 0),
                                  pipeline_mode=pl.Buffered(1))
    weight_args = (g1, w_down, g_q, g_kv, w_uq, w_ukv, w_o)
    cos_q, sin_q, cos_k, sin_k = rope
    ctx_blocks = N_CTX // DEC_SEQ
    lat_out = pl.pallas_call(
        functools.partial(_mla_kernel, latent=True),
        grid=(DEC_BATCH,),
        in_specs=[pl.BlockSpec((DEC_SEQ, D_MODEL), lambda b: (ctx_blocks + b, 0)),
                  _ada_spec_at(layer, lambda b: 1 + b)] + weights + [
            _resident((DEC_SEQ, MLA_Q_ROPE_COLS)),
            _resident((DEC_SEQ, MLA_Q_ROPE_COLS)),
            _resident((DEC_SEQ, MLA_ROPE)),
            _resident((DEC_SEQ, MLA_ROPE)),
            pl.BlockSpec((None, PAST_LEN, MLA_KV_RANK), lambda b: (b, 0, 0)),
            pl.BlockSpec((None, PAST_LEN, MLA_ROPE), lambda b: (b, 0, 0)),
        ],
        out_specs=pl.BlockSpec((DEC_SEQ, D_MODEL), lambda b: (b, 0)),
        out_shape=jax.ShapeDtypeStruct((N_LAT, D_MODEL), F32),
        scratch_shapes=(_mla_scratch(DEC_SEQ, LATENT_KEYS)
                        + _stage_scratch(LATENT_STAGE_HEADS, LATENT_KEYS)),
        compiler_params=_params("arbitrary"),
        name=f"mla_latent_{layer}",
    )(x, ada, *weight_args, cos_q, sin_q, cos_k, sin_k, cache_ckv, cache_kpe)

    x_spec, lat_spec, out_specs = _ctx_specs(((MLA_KV_RANK,), (MLA_ROPE,)))
    return pl.pallas_call(
        functools.partial(_mla_kernel, latent=False),
        grid=(CTX_STEPS,),
        in_specs=[x_spec, _ada_spec_at(layer, lambda i: 0)] + ctx_weights + [lat_spec],
        out_specs=out_specs,
        out_shape=[
            jax.ShapeDtypeStruct((N_TOK, D_MODEL), F32),
            jax.ShapeDtypeStruct((N_CTX, MLA_KV_RANK), F32),
            jax.ShapeDtypeStruct((N_CTX, MLA_ROPE), F32),
        ],
        scratch_shapes=_mla_scratch(CTX_TM, CTX_TM) + _stage_scratch(MLA_HEADS, SEQ),
        compiler_params=_params("arbitrary"),
        name=f"mla_ctx_{layer}",
    )(x, ada, *weight_args, lat_out)


GQA_Q_COLS = GQA_HEADS * GQA_HEAD_DIM
GQA_KV_COLS = GQA_KV_HEADS * GQA_HEAD_DIM
GQA_REP = GQA_HEADS // GQA_KV_HEADS


def _head_rms(t, n_heads):
    cols = []
    for hd in range(n_heads):
        th = t[:, hd * GQA_HEAD_DIM:(hd + 1) * GQA_HEAD_DIM]
        r = lax.rsqrt(jnp.mean(th * th, axis=-1, keepdims=True) + EPS)
        cols.append(jnp.broadcast_to(r, th.shape))
    return jnp.concatenate(cols, axis=-1)


def _gqa_kernel(*refs, latent):
    if latent:
        _gqa_body(*refs, latent=True)
    else:
        _ctx_then_latent(_gqa_body, 7, refs)


def _gqa_body(*refs, latent):
    if latent:
        (x_ref, ada_ref, g1_ref, wqkv_ref, gq_ref, gk_ref, wo_ref,
         gqs_ref, gks_ref, cosq_ref, sinq_ref, cosk_ref, sink_ref, ck_ref, cv_ref,
         o_ref, q_scr, k_scr, v_scr, o_scr, *stage) = refs
    else:
        (x_ref, ada_ref, g1_ref, wqkv_ref, gq_ref, gk_ref, wo_ref,
         o_ref, k_out_ref, v_out_ref, q_scr, k_scr, v_scr, o_scr, *stage) = refs
    tm = x_ref.shape[0]

    def project(rows):
        h = _modulate(x_ref[rows, :], g1_ref[...], ada_ref[0:1, :], ada_ref[1:2, :]).astype(BF16)
        qkv = _dot(h, wqkv_ref[...])
        q_raw = qkv[:, :GQA_Q_COLS]
        k_raw = qkv[:, GQA_Q_COLS:GQA_Q_COLS + GQA_KV_COLS]
        v = qkv[:, GQA_Q_COLS + GQA_KV_COLS:GQA_Q_COLS + 2 * GQA_KV_COLS]
        q_r = _head_rms(q_raw, GQA_HEADS)
        k_r = _head_rms(k_raw, GQA_KV_HEADS)
        q = q_raw * q_r * gq_ref[...]
        k = k_raw * k_r * gk_ref[...]
        if latent:
            lo = GQA_Q_COLS + 2 * GQA_KV_COLS
            q_swap = qkv[:, lo:lo + GQA_Q_COLS] * q_r * gqs_ref[...]
            k_swap = qkv[:, lo + GQA_Q_COLS:lo + GQA_Q_COLS + GQA_KV_COLS] * k_r * gks_ref[...]
            q = q * cosq_ref[rows, :] + q_swap * sinq_ref[rows, :]
            k = k * cosk_ref[rows, :] + k_swap * sink_ref[rows, :]
        else:
            for g in range(GQA_KV_HEADS):
                cols = slice(g * GQA_HEAD_DIM, (g + 1) * GQA_HEAD_DIM)
                k_out_ref[rows, g, :] = k[:, cols]
                v_out_ref[rows, g, :] = v[:, cols]
        q_scr[rows, :] = q.astype(BF16)
        k_scr[rows, :] = k.astype(BF16)
        v_scr[rows, :] = v.astype(BF16)

    _for_row_blocks(tm, project)

    if latent:
        k_scr[tm:, :] = ck_ref[...].astype(BF16)
        v_scr[tm:, :] = cv_ref[...].astype(BF16)

    def finish(rows):
        o_ref[rows, :] = x_ref[rows, :] + ada_ref[2:3, :] * _dot(o_scr[rows, :], wo_ref[...])

    _attention([(q_scr, k_scr, GQA_HEAD_DIM, False)], v_scr, GQA_HEADS,
               lambda hd: hd // GQA_REP, GQA_HEAD_DIM, GQA_HEAD_DIM ** -0.5, o_scr, finish,
               _keys_of(latent), stage)


def _gqa_scratch(tm, sk):
    return [
        pltpu.VMEM((tm, GQA_Q_COLS), BF16),
        pltpu.VMEM((sk, GQA_KV_COLS), BF16),
        pltpu.VMEM((sk, GQA_KV_COLS), BF16),
        pltpu.VMEM((tm, GQA_Q_COLS), BF16),
    ]


def _gqa_layer(x, ada, layer, g1, w_qkv_ctx, w_qkv_lat, g_q, g_k, g_q_swap, g_k_swap, w_o,
               cache_k, cache_v, rope):
    cos_q, sin_q, cos_k, sin_k = rope
    ctx_blocks = N_CTX // DEC_SEQ
    lat_out = pl.pallas_call(
        functools.partial(_gqa_kernel, latent=True),
        grid=(DEC_BATCH,),
        in_specs=[pl.BlockSpec((DEC_SEQ, D_MODEL), lambda b: (ctx_blocks + b, 0)),
                  _ada_spec_at(layer, lambda b: 1 + b),
                  _resident((1, D_MODEL)),
                  _resident(w_qkv_lat.shape),
                  _resident((1, GQA_Q_COLS)),
                  _resident((1, GQA_KV_COLS)),
                  _resident((GQA_Q_COLS, D_MODEL)),
                  _resident((1, GQA_Q_COLS)),
                  _resident((1, GQA_KV_COLS)),
                  _resident((DEC_SEQ, GQA_Q_COLS)),
                  _resident((DEC_SEQ, GQA_Q_COLS)),
                  _resident((DEC_SEQ, GQA_KV_COLS)),
                  _resident((DEC_SEQ, GQA_KV_COLS)),
                  pl.BlockSpec((None, PAST_LEN, GQA_KV_COLS), lambda b: (b, 0, 0)),
                  pl.BlockSpec((None, PAST_LEN, GQA_KV_COLS), lambda b: (b, 0, 0))],
        out_specs=pl.BlockSpec((DEC_SEQ, D_MODEL), lambda b: (b, 0)),
        out_shape=jax.ShapeDtypeStruct((N_LAT, D_MODEL), F32),
        scratch_shapes=(_gqa_scratch(DEC_SEQ, LATENT_KEYS)
                        + _stage_scratch(LATENT_STAGE_HEADS, LATENT_KEYS)),
        compiler_params=_params("arbitrary"),
        name=f"gqa_latent_{layer}",
    )(x, ada, g1, w_qkv_lat, g_q, g_k, w_o, g_q_swap, g_k_swap,
      cos_q, sin_q, cos_k, sin_k, cache_k, cache_v)

    kv_tail = (GQA_KV_HEADS, GQA_HEAD_DIM)
    x_spec, lat_spec, out_specs = _ctx_specs((kv_tail, kv_tail))
    return pl.pallas_call(
        functools.partial(_gqa_kernel, latent=False),
        grid=(CTX_STEPS,),
        in_specs=[x_spec,
                  _ada_spec_at(layer, lambda i: 0),
                  _resident((1, D_MODEL)),
                  _resident(w_qkv_ctx.shape),
                  _resident((1, GQA_Q_COLS)),
                  _resident((1, GQA_KV_COLS)),
                  _resident((GQA_Q_COLS, D_MODEL)),
                  lat_spec],
        out_specs=out_specs,
        out_shape=[
            jax.ShapeDtypeStruct((N_TOK, D_MODEL), F32),
            jax.ShapeDtypeStruct((N_CTX,) + kv_tail, F32),
            jax.ShapeDtypeStruct((N_CTX,) + kv_tail, F32),
        ],
        scratch_shapes=_gqa_scratch(CTX_TM, CTX_TM) + _stage_scratch(GQA_HEADS, SEQ),
        compiler_params=_params("arbitrary"),
        name=f"gqa_ctx_{layer}",
    )(x, ada, g1, w_qkv_ctx, g_q, g_k, w_o, lat_out)


ROUTE_TM = 512
MOE_TG = 256
MOE_MIN_TILES = 2 * N_TOK // MOE_TG
MOE_MAX_TILES = MOE_MIN_TILES + N_EXPERTS
MOE_ROWS = MOE_MAX_TILES * MOE_TG
MOE_FILLS = 2 * N_EXPERTS
MOE_FC = 1024
COMBINE_TM = 256
COMBINE_CHUNK = 32
ISSUE_UNROLL = 16


def _route_kernel(x_ref, ada_ref, g2_ref, wr_hi_ref, wr_lo_ref, h_ref, dest_ref, gatew_ref,
                  count_ref, carry_scr):
    tm = x_ref.shape[0]

    @pl.when(pl.program_id(0) == 0)
    def _():
        carry_scr[...] = jnp.zeros_like(carry_scr)

    h = _modulate(x_ref[...], g2_ref[...], ada_ref[3:4, :], ada_ref[4:5, :])
    h_ref[...] = h.reshape(tm, 1, D_MODEL)
    h_hi = h.astype(BF16)
    h_lo = (h - h_hi.astype(F32)).astype(BF16)
    logits = _dot(h_hi, wr_hi_ref[...]) + (_dot(h_lo, wr_hi_ref[...]) + _dot(h_hi, wr_lo_ref[...]))
    lane = lax.broadcasted_iota(jnp.int32, logits.shape, 1)
    neg = jnp.float32(-jnp.inf)
    l1 = jnp.where(lane < N_EXPERTS, logits, neg)
    m1 = jnp.max(l1, axis=-1, keepdims=True)
    i1 = jnp.min(jnp.where(l1 == m1, lane, LANES), axis=-1, keepdims=True)
    l2 = jnp.where(lane == i1, neg, l1)
    m2 = jnp.max(l2, axis=-1, keepdims=True)
    i2 = jnp.min(jnp.where(l2 == m2, lane, LANES), axis=-1, keepdims=True)
    e2 = jnp.exp(m2 - m1)
    w1 = 1.0 / (1.0 + e2)
    w2 = e2 / (1.0 + e2)
    member = jnp.where((lane == i1) | (lane == i2), 1.0, 0.0)
    r = lax.broadcasted_iota(jnp.int32, (tm, tm), 0)
    c = lax.broadcasted_iota(jnp.int32, (tm, tm), 1)
    tri = jnp.where(c < r, 1.0, 0.0).astype(BF16)
    before = _dot(tri, member.astype(BF16)) + carry_scr[0:1, :]
    rank1 = jnp.sum(jnp.where(lane == i1, before, 0.0), axis=-1, keepdims=True).astype(jnp.int32)
    rank2 = jnp.sum(jnp.where(lane == i2, before, 0.0), axis=-1, keepdims=True).astype(jnp.int32)
    table = jnp.where(lane == 0, i1, jnp.where(lane == 1, i2, jnp.where(
        lane == 2, rank1, jnp.where(lane == 3, rank2, 0))))
    dest_ref[...] = jnp.transpose(table)[0:SUBLANES, :]
    gatew_ref[...] = jnp.where(lane == 0, w1, jnp.where(lane == 1, w2, 0.0))
    carry_scr[...] = carry_scr[...] + jnp.sum(member, axis=0, keepdims=True)
    count_ref[...] = carry_scr[...]


def _route(x, ada, layer, g2, w_router):
    tm = ROUTE_TM
    w_router_hi = w_router.astype(BF16)
    return pl.pallas_call(
        _route_kernel,
        grid=(N_TOK // tm,),
        in_specs=[
            pl.BlockSpec((tm, D_MODEL), lambda i: (i, 0)),
            _ada_spec(layer, tm),
            _resident((1, D_MODEL)),
            _resident((D_MODEL, LANES)),
            _resident((D_MODEL, LANES)),
        ],
        out_specs=[
            pl.BlockSpec((tm, 1, D_MODEL), lambda i: (i, 0, 0)),
            pl.BlockSpec((SUBLANES, tm), lambda i: (0, i)),
            pl.BlockSpec((tm, LANES), lambda i: (i, 0)),
            pl.BlockSpec((SUBLANES, LANES), lambda i: (0, 0)),
        ],
        out_shape=[
            jax.ShapeDtypeStruct((N_TOK, 1, D_MODEL), F32),
            jax.ShapeDtypeStruct((SUBLANES, N_TOK), jnp.int32),
            jax.ShapeDtypeStruct((N_TOK, LANES), F32),
            jax.ShapeDtypeStruct((SUBLANES, LANES), F32),
        ],
        scratch_shapes=[pltpu.VMEM((SUBLANES, LANES), F32)],
        compiler_params=_params("arbitrary"),
        name=f"moe_route_{layer}",
    )(x, ada, g2, w_router_hi, (w_router - w_router_hi.astype(F32)).astype(BF16))


def _cast_specs(m, parts):
    last = N_EXPERTS * parts - 1

    def src(i, *_):
        c = jnp.minimum(i, last)
        return (m, c // parts, c % parts, 0)

    def dst(i, *_):
        c = jnp.minimum(i, last)
        return (c // parts, c % parts, 0)

    shapes = [(D_MODEL // parts, MOE_FF), (D_MODEL // parts, MOE_FF), (MOE_FF // parts, D_MODEL)]
    in_specs = [pl.BlockSpec((None, None) + s, src) for s in shapes]
    out_specs = [pl.BlockSpec((None,) + s, dst) for s in shapes]
    out_shape = [jax.ShapeDtypeStruct((N_EXPERTS, D_MODEL, MOE_FF), BF16),
                 jax.ShapeDtypeStruct((N_EXPERTS, D_MODEL, MOE_FF), BF16),
                 jax.ShapeDtypeStruct((N_EXPERTS, MOE_FF, D_MODEL), BF16)]
    return in_specs, out_specs, out_shape


def _cast_chunk(srcs, dsts):
    for s_ref, d_ref in zip(srcs, dsts):
        d_ref[...] = s_ref[...].astype(BF16)


DISPATCH_TM = 640
EXPERTS_CAST_PARTS = 8


def _dispatch_kernel(d1_ref, d2_ref, fill_start_ref, fill_rows_ref, h_ref, hs_ref, zero_scr, sem):
    i = pl.program_id(0)
    tm = h_ref.shape[0]
    base = i * tm

    def fill_copy(k):
        n = fill_rows_ref[k]
        return pltpu.make_async_copy(zero_scr.at[pl.ds(0, n)],
                                     hs_ref.at[pl.ds(fill_start_ref[k], n)], sem.at[1])

    @pl.when(i == 0)
    def _():
        zero_scr[...] = jnp.zeros_like(zero_scr)
        for k in range(MOE_FILLS):
            @pl.when(fill_rows_ref[k] > 0)
            def _():
                fill_copy(k).start()
        for k in range(MOE_FILLS):
            @pl.when(fill_rows_ref[k] > 0)
            def _():
                fill_copy(k).wait()

    def row_copy(r, d_ref):
        return pltpu.make_async_copy(h_ref.at[r], hs_ref.at[d_ref[base + r]], sem.at[0])

    def issue(g, carry):
        for k in range(ISSUE_UNROLL):
            r = g * ISSUE_UNROLL + k
            row_copy(r, d1_ref).start(priority=0)
            row_copy(r, d2_ref).start(priority=1)
        return carry

    def drain():
        for _ in range(2):
            pltpu.make_async_copy(h_ref, hs_ref.at[pl.ds(0, tm)], sem.at[0]).wait()

    lax.fori_loop(0, tm // ISSUE_UNROLL, issue, 0)
    drain()


def _dispatch(d1, d2, fill_start, fill_rows, h_rows):
    tm = DISPATCH_TM
    return pl.pallas_call(
        _dispatch_kernel,
        grid_spec=pltpu.PrefetchScalarGridSpec(
            num_scalar_prefetch=4,
            grid=(N_TOK // tm,),
            in_specs=[pl.BlockSpec((tm, 1, D_MODEL), lambda i, *_: (i, 0, 0))],
            out_specs=pl.BlockSpec(memory_space=pl.ANY),
            scratch_shapes=[pltpu.VMEM((MOE_TG, 1, D_MODEL), F32),
                            pltpu.SemaphoreType.DMA((2,))],
        ),
        out_shape=jax.ShapeDtypeStruct((MOE_ROWS, 1, D_MODEL), F32),
        compiler_params=_params("arbitrary"),
        name="moe_dispatch",
    )(d1, d2, fill_start, fill_rows, h_rows)


def _experts_kernel(*refs, cast):
    if cast:
        (te_ref, nt_ref, hs_ref, wg_ref, wu_ref, wd_ref, ng_ref, nu_ref, nd_ref,
         ys_ref, ng_out_ref, nu_out_ref, nd_out_ref, h2d_scr) = refs
    else:
        te_ref, nt_ref, hs_ref, wg_ref, wu_ref, wd_ref, ys_ref, h2d_scr = refs
    used = pl.program_id(0) < nt_ref[0]

    @pl.when(jnp.logical_not(used))
    def _():
        ys_ref[...] = jnp.zeros_like(ys_ref)

    @pl.when(used)
    def _():
        if cast:
            _cast_chunk((ng_ref, nu_ref, nd_ref), (ng_out_ref, nu_out_ref, nd_out_ref))
        h2d_scr[...] = hs_ref[...].reshape(MOE_TG, D_MODEL)
        h = h2d_scr[...].astype(BF16)
        y = None
        for lo in range(0, MOE_FF, MOE_FC):
            g = _dot(h, wg_ref[:, lo:lo + MOE_FC])
            u = _dot(h, wu_ref[:, lo:lo + MOE_FC])
            part = _dot((_silu(g) * u).astype(BF16), wd_ref[lo:lo + MOE_FC, :])
            y = part if y is None else y + part
        ys_ref[...] = y.reshape(MOE_TG, 1, D_MODEL)


def _experts(tile_expert, n_tiles, hs, w_gate, w_up, w_down, cast=None):
    rows = pl.BlockSpec((MOE_TG, 1, D_MODEL), lambda t, te, nt: (t, 0, 0))
    in_specs = [
        rows,
        pl.BlockSpec((None, D_MODEL, MOE_FF), lambda t, te, nt: (te[t], 0, 0)),
        pl.BlockSpec((None, D_MODEL, MOE_FF), lambda t, te, nt: (te[t], 0, 0)),
        pl.BlockSpec((None, MOE_FF, D_MODEL), lambda t, te, nt: (te[t], 0, 0)),
    ]
    out_specs = [rows]
    out_shape = [jax.ShapeDtypeStruct((MOE_ROWS, 1, D_MODEL), F32)]
    next_weights = ()
    if cast is not None:
        assert MOE_MIN_TILES >= N_EXPERTS * EXPERTS_CAST_PARTS
        cast_in, cast_out, cast_shape = _cast_specs(cast[0], EXPERTS_CAST_PARTS)
        in_specs += cast_in
        out_specs += cast_out
        out_shape += cast_shape
        next_weights = cast[1:]
    out = pl.pallas_call(
        functools.partial(_experts_kernel, cast=cast is not None),
        grid_spec=pltpu.PrefetchScalarGridSpec(
            num_scalar_prefetch=2,
            grid=(MOE_MAX_TILES,),
            in_specs=in_specs,
            out_specs=out_specs,
            scratch_shapes=[pltpu.VMEM((MOE_TG, D_MODEL), F32)],
        ),
        out_shape=out_shape,
        compiler_params=_params("arbitrary"),
        name="moe_experts",
    )(tile_expert, n_tiles, hs, w_gate, w_up, w_down, *next_weights)
    return out if cast is not None else out[0]


def _combine_kernel(*refs, final):
    if final:
        (d1_ref, d2_ref, x_ref, ada_ref, gw_ref, gf_ref, ys_ref, yp_ref, yl_ref,
         ya0, yb0, ya1, yb1, ya2d, yb2d, sem) = refs
    else:
        (d1_ref, d2_ref, x_ref, ada_ref, gw_ref, ys_ref, o_ref,
         ya0, yb0, ya1, yb1, ya2d, yb2d, sem) = refs
    i = pl.program_id(0)
    tm = x_ref.shape[0]

    def row_copies(step, r, buf_a, buf_b, slot):
        t = step * tm + r
        return (pltpu.make_async_copy(ys_ref.at[d1_ref[t]], buf_a.at[r], sem.at[slot]),
                pltpu.make_async_copy(ys_ref.at[d2_ref[t]], buf_b.at[r], sem.at[slot]))

    def start_tile(step, buf_a, buf_b, slot):
        def body(g, carry):
            for k in range(ISSUE_UNROLL):
                copies = row_copies(step, g * ISSUE_UNROLL + k, buf_a, buf_b, slot)
                for priority, cp in enumerate(copies):
                    cp.start(priority=priority)
            return carry
        lax.fori_loop(0, tm // ISSUE_UNROLL, body, 0)

    def wait_tile(buf_a, buf_b, slot):
        for buf in (buf_a, buf_b):
            pltpu.make_async_copy(ys_ref.at[pl.ds(0, tm)], buf, sem.at[slot]).wait()

    @pl.when(i == 0)
    def _():
        start_tile(0, ya0, yb0, 0)

    def run(cur, nxt):
        n = pl.num_programs(0)
        nxt_step = jnp.where(i + 1 < n, i + 1, 0)
        wait_tile(*cur)
        for c in range(tm // COMBINE_CHUNK):
            rows = pl.ds(c * COMBINE_CHUNK, COMBINE_CHUNK)
            ya2d[rows, :] = cur[0][rows].reshape(COMBINE_CHUNK, D_MODEL)
            yb2d[rows, :] = cur[1][rows].reshape(COMBINE_CHUNK, D_MODEL)
            mix = gw_ref[rows, 0:1] * ya2d[rows, :] + gw_ref[rows, 1:2] * yb2d[rows, :]
            out = x_ref[rows, :] + ada_ref[5:6, :] * mix
            if final:
                ya2d[rows, :] = _rms(out) * gf_ref[...]
            else:
                o_ref[rows, :] = out
            for r in range(c * COMBINE_CHUNK, (c + 1) * COMBINE_CHUNK):
                for priority, cp in enumerate(row_copies(nxt_step, r, *nxt)):
                    cp.start(priority=priority)

        @pl.when(i == n - 1)
        def _():
            wait_tile(*nxt)

        if final:
            @pl.when(i * tm < N_CTX)
            def _():
                yp_ref[...] = ya2d[...]

            @pl.when(i * tm >= N_CTX)
            def _():
                yl_ref[...] = ya2d[...]

    @pl.when(i % 2 == 0)
    def _():
        run((ya0, yb0, 0), (ya1, yb1, 1))

    @pl.when(i % 2 == 1)
    def _():
        run((ya1, yb1, 1), (ya0, yb0, 0))


def _combine(d1, d2, x, ada, layer, gate_w, ys, final_g=None):
    tm = COMBINE_TM
    final = final_g is not None
    ctx_tiles = N_CTX // tm
    row_buf = pltpu.VMEM((tm, 1, D_MODEL), F32)
    in_specs = [
        pl.BlockSpec((tm, D_MODEL), lambda i, *_: (i, 0)),
        pl.BlockSpec((None, None, 6, D_MODEL),
                     lambda i, *_: (layer, _cond_of_tile(i, tm), 0, 0)),
        pl.BlockSpec((tm, LANES), lambda i, *_: (i, 0)),
    ]
    if final:
        in_specs.append(pl.BlockSpec((1, D_MODEL), lambda i, *_: (0, 0)))
        out_specs = [
            pl.BlockSpec((tm, D_MODEL), lambda i, *_: (jnp.minimum(i, ctx_tiles - 1), 0)),
            pl.BlockSpec((tm, D_MODEL), lambda i, *_: (jnp.maximum(i - ctx_tiles, 0), 0)),
        ]
        out_shape = [jax.ShapeDtypeStruct((N_CTX, D_MODEL), F32),
                     jax.ShapeDtypeStruct((N_LAT, D_MODEL), F32)]
        args = (d1, d2, x, ada, gate_w, final_g, ys)
    else:
        out_specs = pl.BlockSpec((tm, D_MODEL), lambda i, *_: (i, 0))
        out_shape = jax.ShapeDtypeStruct((N_TOK, D_MODEL), F32)
        args = (d1, d2, x, ada, gate_w, ys)
    in_specs.append(pl.BlockSpec(memory_space=pl.ANY))
    return pl.pallas_call(
        functools.partial(_combine_kernel, final=final),
        grid_spec=pltpu.PrefetchScalarGridSpec(
            num_scalar_prefetch=2,
            grid=(N_TOK // tm,),
            in_specs=in_specs,
            out_specs=out_specs,
            scratch_shapes=[row_buf, row_buf, row_buf, row_buf,
                            pltpu.VMEM((tm, D_MODEL), F32), pltpu.VMEM((tm, D_MODEL), F32),
                            pltpu.SemaphoreType.DMA((2,))],
        ),
        out_shape=out_shape,
        compiler_params=_params("arbitrary"),
        name="moe_combine",
    )(*args)


def _moe_layer(x, ada, layer, m, g2, w_router, w_gate, w_up, w_down, weights_bf, final_g):
    h_rows, route, gate_w, counts = _route(x, ada, layer, g2, w_router)
    cnt = counts[0, :N_EXPERTS].astype(jnp.int32)
    tiles_per_expert = (cnt + MOE_TG - 1) // MOE_TG
    tile_end = jnp.cumsum(tiles_per_expert)
    row_start = (tile_end - tiles_per_expert) * MOE_TG
    n_tiles = tile_end[-1:]
    t = jnp.minimum(jnp.arange(MOE_MAX_TILES, dtype=jnp.int32), n_tiles - 1)
    tile_expert = jnp.sum((t[:, None] >= tile_end[None, :]).astype(jnp.int32), axis=1)
    experts = jnp.arange(N_EXPERTS, dtype=jnp.int32)
    d1 = jnp.sum(jnp.where(route[0][:, None] == experts, row_start, 0), axis=1) + route[2]
    d2 = jnp.sum(jnp.where(route[1][:, None] == experts, row_start, 0), axis=1) + route[3]
    spare_tile = jnp.minimum(n_tiles + experts, MOE_MAX_TILES - 1)
    fill_start = jnp.concatenate([row_start + cnt, spare_tile * MOE_TG])
    fill_rows = jnp.concatenate([tiles_per_expert * MOE_TG - cnt,
                                 jnp.where(n_tiles + experts < MOE_MAX_TILES, MOE_TG, 0)])
    hs = _dispatch(d1, d2, fill_start, fill_rows, h_rows)
    next_bf = None
    if m + 1 < w_gate.shape[0]:
        ys, *next_bf = _experts(tile_expert, n_tiles, hs, *weights_bf,
                                (m + 1, w_gate, w_up, w_down))
    else:
        ys = _experts(tile_expert, n_tiles, hs, *weights_bf)
    return _combine(d1, d2, x, ada, layer, gate_w, ys, final_g), next_bf


assert DEPTH % 2 == 0


def _rope_tables(dim):
    half = dim // 2
    quarter = half // 2
    t = np.arange(DEC_SEQ)
    pos = np.stack([t // GRID_W, t % GRID_W], axis=1).astype(np.float32)
    freqs = (ROPE_THETA ** (-np.arange(quarter, dtype=np.float32) / quarter)).astype(np.float32)
    ang = pos[:, :, None] * freqs[None, None, :]
    cos = np.cos(ang.astype(np.float64))
    sin = np.sin(ang.astype(np.float64))
    cos_t = np.concatenate([cos, cos], axis=-1).reshape(DEC_SEQ, dim)
    sin_t = np.concatenate([-sin, sin], axis=-1).reshape(DEC_SEQ, dim)
    lane = np.arange(dim)
    swap = np.where(lane % half < quarter, lane + quarter, lane - quarter)
    return cos_t.astype(np.float32), sin_t.astype(np.float32), swap


def kernel(x_prompt, x_sample, cache_mla_ckv, cache_mla_kpe, cache_gqa_k, cache_gqa_v, c, c_ctx, w_ada, b_ada, norm1_g, norm2_g, conv_w_in, conv_k, conv_w_out, mla_w_down, mla_q_norm_g, mla_kv_norm_g, mla_w_uq, mla_w_ukv, mla_w_o, gqa_w_qkv, gqa_q_norm_g, gqa_k_norm_g, gqa_w_o, ffn_w_gate, ffn_w_up, ffn_w_down, moe_w_router, moe_w_gate, moe_w_up, moe_w_down, final_norm_g):
    xs = (x_prompt.reshape(N_CTX, D_MODEL), x_sample.reshape(N_LAT, D_MODEL))
    conv_w_in, conv_w_out = conv_w_in.astype(BF16), conv_w_out.astype(BF16)
    ffn_w_gate, ffn_w_up, ffn_w_down = (w.astype(BF16) for w in (ffn_w_gate, ffn_w_up, ffn_w_down))
    cond = jnp.concatenate(
        [c_ctx[None, :], c, jnp.zeros((N_COND - 1 - DEC_BATCH, D_MODEL), F32)], axis=0)
    ada = _adaln_table(cond, w_ada, b_ada)

    new_ckv, new_kpe, new_k, new_v = [], [], [], []
    moe_bf = None
    for i in range(DEPTH):
        j = i // N_MIXERS
        kind = i % N_MIXERS
        g1 = norm1_g[i][None, :]
        if kind == 0:
            x = _conv_layer(xs if i == 0 else (x,), ada, i, j, g1, conv_w_in, conv_k, conv_w_out)
        elif kind == 1:
            cos64, sin64, swap64 = _rope_tables(MLA_ROPE)
            wd = mla_w_down[j]
            kpe_cols = wd[:, MLA_Q_RANK + MLA_KV_RANK:]
            w_down = jnp.concatenate([wd, kpe_cols[:, swap64]], axis=1).astype(BF16)
            wq = mla_w_uq[j].reshape(MLA_Q_RANK, MLA_HEADS, MLA_QK)
            wq_nope = wq[:, :, :MLA_NOPE].reshape(MLA_Q_RANK, MLA_Q_NOPE_COLS)
            wq_rope = wq[:, :, MLA_NOPE:]
            w_uq = jnp.concatenate(
                [wq_nope, wq_rope.reshape(MLA_Q_RANK, MLA_Q_ROPE_COLS),
                 wq_rope[:, :, swap64].reshape(MLA_Q_RANK, MLA_Q_ROPE_COLS)], axis=1).astype(BF16)
            wkv = mla_w_ukv[j].reshape(MLA_KV_RANK, MLA_HEADS, MLA_NOPE + MLA_V)
            w_ukv = jnp.concatenate(
                [wkv[:, :, :MLA_NOPE].reshape(MLA_KV_RANK, MLA_Q_NOPE_COLS),
                 wkv[:, :, MLA_NOPE:].reshape(MLA_KV_RANK, MLA_HEADS * MLA_V)], axis=1).astype(BF16)
            rope = (jnp.asarray(np.tile(cos64, (1, MLA_HEADS))), jnp.asarray(np.tile(sin64, (1, MLA_HEADS))),
                    jnp.asarray(cos64), jnp.asarray(sin64))
            x, ckv_p, kpe_p = _mla_layer(
                x, ada, i, g1, w_down, mla_q_norm_g[j][None, :], mla_kv_norm_g[j][None, :],
                w_uq, w_ukv, mla_w_o[j].astype(BF16), cache_mla_ckv[:, j], cache_mla_kpe[:, j], rope)
            new_ckv.append(ckv_p.reshape(BATCH, SEQ, MLA_KV_RANK))
            new_kpe.append(kpe_p.reshape(BATCH, SEQ, MLA_ROPE))
        else:
            cos128, sin128, swap128 = _rope_tables(GQA_HEAD_DIM)
            wqkv = gqa_w_qkv[j]
            wq = wqkv[:, :GQA_Q_COLS].reshape(D_MODEL, GQA_HEADS, GQA_HEAD_DIM)
            wk = wqkv[:, GQA_Q_COLS:GQA_Q_COLS + GQA_KV_COLS].reshape(D_MODEL, GQA_KV_HEADS, GQA_HEAD_DIM)
            w_qkv_ctx = wqkv.astype(BF16)
            w_qkv_lat = jnp.concatenate(
                [wqkv, wq[:, :, swap128].reshape(D_MODEL, GQA_Q_COLS),
                 wk[:, :, swap128].reshape(D_MODEL, GQA_KV_COLS)], axis=1).astype(BF16)
            gq = gqa_q_norm_g[j]
            gk = gqa_k_norm_g[j]
            rope = (jnp.asarray(np.tile(cos128, (1, GQA_HEADS))), jnp.asarray(np.tile(sin128, (1, GQA_HEADS))),
                    jnp.asarray(np.tile(cos128, (1, GQA_KV_HEADS))), jnp.asarray(np.tile(sin128, (1, GQA_KV_HEADS))))
            x, k_p, v_p = _gqa_layer(
                x, ada, i, g1, w_qkv_ctx, w_qkv_lat,
                jnp.tile(gq, GQA_HEADS)[None, :], jnp.tile(gk, GQA_KV_HEADS)[None, :],
                jnp.tile(gq[swap128], GQA_HEADS)[None, :], jnp.tile(gk[swap128], GQA_KV_HEADS)[None, :],
                gqa_w_o[j].astype(BF16),
                cache_gqa_k[:, j].reshape(DEC_BATCH, PAST_LEN, GQA_KV_COLS),
                cache_gqa_v[:, j].reshape(DEC_BATCH, PAST_LEN, GQA_KV_COLS), rope)
            new_k.append(k_p.reshape(BATCH, SEQ, GQA_KV_HEADS, GQA_HEAD_DIM))
            new_v.append(v_p.reshape(BATCH, SEQ, GQA_KV_HEADS, GQA_HEAD_DIM))

        m = i // 2
        g2 = norm2_g[i][None, :]
        if i % 2 == 0:
            cast = (m, moe_w_gate, moe_w_up, moe_w_down) if moe_bf is None else None
            x, prepared = _ffn_layer(x, ada, i, m, g2, ffn_w_gate, ffn_w_up, ffn_w_down, cast)
            moe_bf = prepared if cast is not None else moe_bf
        else:
            w_router = jnp.pad(moe_w_router[m], ((0, 0), (0, LANES - N_EXPERTS)))
            final_g = final_norm_g[None, :] if i == DEPTH - 1 else None
            x, moe_bf = _moe_layer(x, ada, i, m, g2, w_router, moe_w_gate, moe_w_up, moe_w_down,
                                   moe_bf, final_g)

    y_prompt, y_sample = x
    y_prompt = y_prompt.reshape(BATCH, SEQ, D_MODEL)
    y_sample = y_sample.reshape(DEC_BATCH, DEC_SEQ, D_MODEL)
    return (y_prompt, y_sample,
            jnp.stack(new_ckv, axis=1), jnp.stack(new_kpe, axis=1),
            jnp.stack(new_k, axis=1), jnp.stack(new_v, axis=1))
```
